```python
import jax, jax.numpy as jnp
from jax import lax
import numpy as np

D_MODEL = 1024
BATCH = 16
SEQ = 256
DEPTH = 4
DEC_BATCH = 8
DEC_SEQ = 2048
PAST_LEN = 256

GRID_W = 64
HEAD_DIM = 64
N_GROUPS = 4
GROUP_W = D_MODEL // N_GROUPS
A_HEADS = GROUP_W // HEAD_DIM
A_KV = max(1, A_HEADS // 2)
B_HEADS = GROUP_W // HEAD_DIM
C_HEADS = GROUP_W // HEAD_DIM
D_HEADS = GROUP_W // HEAD_DIM
WINDOW = 128
QBLK = 128
NA_ROWS = 8
NA_COLS = 16
NA_QCOLS = 16
NA_KCOLS = 32
NCB = GRID_W // NA_QCOLS
CHUNK = 64
ROPE_BASE = 10000.0
EPS = 1e-6
NEG = -1e30
LB_FLOOR = 1e-30
IN_WIDTHS = (GROUP_W, A_KV * HEAD_DIM, A_KV * HEAD_DIM, GROUP_W,
             GROUP_W, GROUP_W, GROUP_W, GROUP_W,
             GROUP_W, GROUP_W, GROUP_W, GROUP_W,
             GROUP_W, GROUP_W, GROUP_W, GROUP_W, GROUP_W)
IN_WIDTH = sum(IN_WIDTHS)

kernel_name = 'hybrid_diffusion_parallel_heads_step'


def rmsnorm(x, g):
    x32 = x.astype(jnp.float32)
    y = x32 * lax.rsqrt(jnp.mean(x32 * x32, axis=-1, keepdims=True) + EPS)
    return (y * g.astype(jnp.float32)).astype(x.dtype)


def adaln(cvec, w, b):
    m = jax.nn.silu(cvec) @ w + b
    return jnp.split(m, 3, axis=-1)


def rope_2d(x):
    bsz, slen, nh, hd = x.shape
    t = jnp.arange(slen)
    half = hd // 2
    nf = half // 2
    freqs = ROPE_BASE ** (-jnp.arange(nf, dtype=jnp.float32) / nf)

    def rot(xa, pos):
        ang = pos.astype(jnp.float32)[:, None] * freqs[None, :]
        cos = jnp.cos(ang)[None, :, None, :]
        sin = jnp.sin(ang)[None, :, None, :]
        x1, x2 = xa[..., :nf], xa[..., nf:]
        return jnp.concatenate([x1 * cos - x2 * sin, x2 * cos + x1 * sin], axis=-1)

    x32 = x.astype(jnp.float32)
    out = jnp.concatenate([rot(x32[..., :half], t // GRID_W), rot(x32[..., half:], t % GRID_W)], axis=-1)
    return out.astype(x.dtype)


def split_branches(h, w_in):
    bsz, slen = h.shape[:2]
    parts = jnp.split(h @ w_in, np.cumsum(IN_WIDTHS)[:-1].tolist(), axis=-1)
    return [p.reshape(bsz, slen, -1, HEAD_DIM) for p in parts]


def finish(x, outs, gates, w_out, g_post, gate_mod):
    bsz, slen = x.shape[:2]
    mixed = jnp.concatenate(
        [(o.astype(x.dtype) * jax.nn.silu(g)).reshape(bsz, slen, GROUP_W) for o, g in zip(outs, gates)], axis=-1)
    return x + gate_mod * rmsnorm(mixed @ w_out, g_post)


def dense_attn(q, k, v, sink):
    bsz, slen, nh, hd = q.shape
    nkv = k.shape[2]
    grp = nh // nkv
    nb = slen // QBLK
    nk = k.shape[1]
    qb = jnp.moveaxis(q.reshape(bsz, nb, QBLK, nkv, grp, hd), 1, 0)

    def block(qi):
        s = jnp.einsum('bqkgd,bskd->bkgqs', qi, k, preferred_element_type=jnp.float32) * hd ** -0.5
        if sink is not None:
            sk = jnp.broadcast_to(sink.astype(jnp.float32).reshape(1, nkv, grp, 1, 1), s.shape[:-1] + (1,))
            s = jnp.concatenate([s, sk], axis=-1)
        p = jax.nn.softmax(s, axis=-1)[..., :nk].astype(v.dtype)
        return jnp.einsum('bkgqs,bskd->bqkgd', p, v)

    o = lax.map(block, qb)
    return jnp.moveaxis(o, 0, 1).reshape(bsz, slen, nh, hd)


def window_attn(q, k, v, k_ctx, v_ctx, sink):
    bsz, slen, nh, hd = q.shape
    nkv = k.shape[2]
    grp = nh // nkv
    nb = slen // QBLK
    qb = q.reshape(bsz, nb, QBLK, nkv, grp, hd)

    def band(t):
        tp = jnp.pad(t, ((0, 0), (QBLK, QBLK), (0, 0), (0, 0))).reshape(bsz, nb + 2, QBLK, nkv, hd)
        return jnp.concatenate([tp[:, :-2], tp[:, 1:-1], tp[:, 2:]], axis=2)

    kb, vb = band(k), band(v)
    blk = np.arange(nb)[:, None]
    qpos = blk * QBLK + np.arange(QBLK)[None, :]
    kpos = (blk - 1) * QBLK + np.arange(3 * QBLK)[None, :]
    mask = ((kpos[:, None, :] >= 0) & (kpos[:, None, :] < slen)
            & (np.abs(qpos[:, :, None] - kpos[:, None, :]) <= WINDOW))
    scale = hd ** -0.5
    s_loc = jnp.einsum('bnqkgd,bnskd->bnkgqs', qb, kb, preferred_element_type=jnp.float32) * scale
    s_loc = jnp.where(mask[None, :, None, None], s_loc, NEG)
    s_ctx = jnp.einsum('bnqkgd,bskd->bnkgqs', qb, k_ctx, preferred_element_type=jnp.float32) * scale
    sk = jnp.broadcast_to(sink.astype(jnp.float32).reshape(1, 1, nkv, grp, 1, 1), s_loc.shape[:-1] + (1,))
    p = jax.nn.softmax(jnp.concatenate([s_loc, s_ctx, sk], axis=-1), axis=-1).astype(v.dtype)
    nloc = 3 * QBLK
    nctx = k_ctx.shape[1]
    o = (jnp.einsum('bnkgqs,bnskd->bnqkgd', p[..., :nloc], vb)
         + jnp.einsum('bnkgqs,bskd->bnqkgd', p[..., nloc:nloc + nctx], v_ctx))
    return o.reshape(bsz, slen, nh, hd)


def neighborhood_attn(q, k, v, k_ctx, v_ctx, rpb):
    bsz, slen, nh, hd = q.shape
    rows = slen // GRID_W
    wr = min(NA_ROWS, rows)
    r = np.arange(rows)
    row_idx = np.clip(r - wr // 2, 0, rows - wr)[:, None] + np.arange(wr)[None, :]
    kcol0 = np.clip(np.arange(NCB) * NA_QCOLS - (NA_KCOLS - NA_QCOLS) // 2, 0, GRID_W - NA_KCOLS)
    col_idx = kcol0[:, None] + np.arange(NA_KCOLS)[None, :]
    qcol = np.arange(NCB)[:, None] * NA_QCOLS + np.arange(NA_QCOLS)[None, :]
    qws = np.clip(qcol - NA_COLS // 2, 0, GRID_W - NA_COLS)
    col_ok = ((col_idx[:, None, :] >= qws[:, :, None])
              & (col_idx[:, None, :] < qws[:, :, None] + NA_COLS))
    nkeys = wr * NA_KCOLS
    mask = np.broadcast_to(col_ok[:, :, None, :], (NCB, NA_QCOLS, wr, NA_KCOLS)).reshape(NCB, NA_QCOLS, nkeys)
    ri = (row_idx - r[:, None]) + NA_ROWS - 1
    ci = np.clip(col_idx[:, None, :] - qcol[:, :, None], -(NA_COLS - 1), NA_COLS - 1) + NA_COLS - 1
    bias = rpb.astype(jnp.float32)[:, ri[:, None, None, :, None], ci[None, :, :, None, :]]
    bias = jnp.transpose(bias.reshape(nh, rows, NCB, NA_QCOLS, nkeys), (1, 2, 0, 3, 4))

    def gather(t):
        t5 = t.reshape(bsz, rows, GRID_W, nh, hd)
        g = t5[:, row_idx[:, None, :, None], col_idx[None, :, None, :]]
        return g.reshape(bsz, rows, NCB, nkeys, nh, hd)

    kb, vb = gather(k), gather(v)
    qb = q.reshape(bsz, rows, NCB, NA_QCOLS, nh, hd)
    scale = hd ** -0.5
    s_loc = jnp.einsum('brcqhd,brckhd->brchqk', qb, kb, preferred_element_type=jnp.float32) * scale + bias[None]
    s_loc = jnp.where(mask[None, None, :, None], s_loc, NEG)
    s_ctx = jnp.einsum('brcqhd,bshd->brchqs', qb, k_ctx, preferred_element_type=jnp.float32) * scale
    p = jax.nn.softmax(jnp.concatenate([s_loc, s_ctx], axis=-1), axis=-1).astype(v.dtype)
    o = (jnp.einsum('brchqk,brckhd->brcqhd', p[..., :nkeys], vb)
         + jnp.einsum('brchqs,bshd->brcqhd', p[..., nkeys:], v_ctx))
    return o.reshape(bsz, slen, nh, hd)


def _chunks(t, nc):
    bsz, slen, nh, d = t.shape
    return jnp.moveaxis(t.astype(jnp.float32).reshape(bsz, nc, CHUNK, nh, d), 1, 0)


def retention_scan(q, k, v, log_gamma, s0):
    bsz, slen, nh, _ = q.shape
    nc = slen // CHUNK
    lg = log_gamma.astype(jnp.float32)
    idx = jnp.arange(CHUNK, dtype=jnp.float32)
    dist = idx[:, None] - idx[None, :]
    decay = jnp.where(dist >= 0, jnp.exp(jnp.maximum(dist, 0.0)[None] * lg[:, None, None]), 0.0)
    q_decay = jnp.exp((idx[:, None] + 1.0) * lg[None, :])
    k_decay = jnp.exp((CHUNK - 1.0 - idx)[:, None] * lg[None, :])
    chunk_decay = jnp.exp(CHUNK * lg)

    def step(state, inp):
        qc, kc, vc = inp
        scores = jnp.einsum('bqhd,bkhd->bhqk', qc, kc) * decay
        intra = jnp.einsum('bhqk,bkhe->bqhe', scores, vc)
        cross = jnp.einsum('bqhd,bhde->bqhe', qc, state) * q_decay[None, :, :, None]
        new = (state * chunk_decay[None, :, None, None]
               + jnp.einsum('bkhd,bkhe->bhde', kc * k_decay[None, :, :, None], vc))
        return new, intra + cross

    s_fin, o = lax.scan(step, s0.astype(jnp.float32), (_chunks(q, nc), _chunks(k, nc), _chunks(v, nc)))
    return jnp.moveaxis(o, 0, 1).reshape(bsz, slen, nh, v.shape[-1]), s_fin


def hgrn2_scan(q, log_f, k, v, s0):
    bsz, slen, nh, _ = q.shape
    nc = slen // CHUNK
    causal = jnp.tril(jnp.ones((CHUNK, CHUNK), dtype=bool))

    def step(state, inp):
        qc, lfc, kc, vc = inp
        b = jnp.cumsum(lfc, axis=1)
        diff = b[:, :, None] - b[:, None, :]
        w = jnp.where(causal[None, :, :, None, None], jnp.exp(jnp.minimum(diff, 0.0)), 0.0)
        attn = jnp.einsum('bihd,bjhd,bijhd->bhij', qc, kc, w)
        intra = jnp.einsum('bhij,bjhe->bihe', attn, vc)
        cross = jnp.einsum('bihd,bhde->bihe', qc * jnp.exp(b), state)
        b_last = b[:, -1]
        new = (state * jnp.exp(b_last)[..., None]
               + jnp.einsum('bjhd,bjhe->bhde', kc * jnp.exp(b_last[:, None] - b), vc))
        return new, intra + cross

    xs = (_chunks(q, nc), _chunks(log_f, nc), _chunks(k, nc), _chunks(v, nc))
    s_fin, o = lax.scan(step, s0.astype(jnp.float32), xs)
    return jnp.moveaxis(o, 0, 1).reshape(bsz, slen, nh, v.shape[-1]), s_fin


def retention_branch(q, k, v, log_gamma, s0_f, s0_b):
    k = k * HEAD_DIM ** -0.5
    o_f, s_f = retention_scan(q, k, v, log_gamma[0], s0_f)
    o_b, s_b = retention_scan(q[:, ::-1], k[:, ::-1], v[:, ::-1], log_gamma[1], s0_b)
    o = o_f + o_b[:, ::-1]
    mu = jnp.mean(o, axis=-1, keepdims=True)
    var = jnp.mean(jnp.square(o - mu), axis=-1, keepdims=True)
    o = (o - mu) * lax.rsqrt(var + EPS)
    return o, jnp.stack([s_f, s_b], axis=1)


def hgrn2_branch(q, z_f, z_b, v, lb, s0_f, s0_b):
    q = jax.nn.silu(q)

    def forget(z, lbd):
        lbd = jnp.maximum(lbd, LB_FLOOR)
        log_f = jnp.logaddexp(jnp.log(lbd), jnp.log1p(-lbd) + jax.nn.log_sigmoid(z.astype(jnp.float32)))
        return log_f, 1.0 - jnp.exp(log_f)

    lf_f, k_f = forget(z_f, lb[0])
    o_f, s_f = hgrn2_scan(q, lf_f, k_f, v, s0_f)
    lf_b, k_b = forget(z_b[:, ::-1], lb[1])
    o_b, s_b = hgrn2_scan(q[:, ::-1], lf_b, k_b, v[:, ::-1], s0_b)
    o = o_f + o_b[:, ::-1]
    o = o * lax.rsqrt(jnp.mean(o * o, axis=-1, keepdims=True) + EPS)
    return o, jnp.stack([s_f, s_b], axis=1)


def context_layer(x, c_ctx, w_ada_l, b_ada_l, g_pre_l, g_post_l, w_in_l, w_out_l, sink_l, log_gamma_l, lb_l):
    shift, scale, gate = adaln(c_ctx, w_ada_l, b_ada_l)
    h = rmsnorm(x, g_pre_l) * (1.0 + scale) + shift
    aq, ak, av, ag, bq, bk, bv, bg, cq, ck, cv, cg, dq, dzf, dzb, di, dg = split_branches(h, w_in_l)
    bsz = x.shape[0]
    zc = jnp.zeros((bsz, C_HEADS, HEAD_DIM, HEAD_DIM), jnp.float32)
    zd = jnp.zeros((bsz, D_HEADS, HEAD_DIM, HEAD_DIM), jnp.float32)
    o_a = dense_attn(aq, ak, av, sink_l)
    o_b = dense_attn(bq, bk, bv, None)
    o_c, s_c = retention_branch(cq, ck, cv, log_gamma_l, zc, zc)
    o_d, s_d = hgrn2_branch(dq, dzf, dzb, di, lb_l, zd, zd)
    x = finish(x, (o_a, o_b, o_c, o_d), (ag, bg, cg, dg), w_out_l, g_post_l, gate)
    return x, (ak, av, bk, bv, s_c, s_d)


def latent_layer(x, c, ca_k, ca_v, cb_k, cb_v, st_c, st_d, w_ada_l, b_ada_l, g_pre_l, g_post_l,
                 w_in_l, w_out_l, sink_l, rpb_l, log_gamma_l, lb_l):
    shift, scale, gate = [m[:, None, :] for m in adaln(c, w_ada_l, b_ada_l)]
    h = rmsnorm(x, g_pre_l) * (1.0 + scale) + shift
    aq, ak, av, ag, bq, bk, bv, bg, cq, ck, cv, cg, dq, dzf, dzb, di, dg = split_branches(h, w_in_l)
    o_a = window_attn(rope_2d(aq), rope_2d(ak), av, ca_k, ca_v, sink_l)
    o_b = neighborhood_attn(bq, bk, bv, cb_k, cb_v, rpb_l)
    o_c, _ = retention_branch(rope_2d(cq), rope_2d(ck), cv, log_gamma_l, st_c[:, 0], st_c[:, 1])
    o_d, _ = hgrn2_branch(dq, dzf, dzb, di, lb_l, st_d[:, 0], st_d[:, 1])
    return finish(x, (o_a, o_b, o_c, o_d), (ag, bg, cg, dg), w_out_l, g_post_l, gate)


def setup_inputs(seed: int = 0) -> dict:
    key = jax.random.key(seed)
    ks = jax.random.split(key, 20)
    nrm = jax.random.normal
    f32 = jnp.float32
    ret_init = jnp.log(2.0 ** (5.0 + jnp.arange(C_HEADS, dtype=f32)) - 1.0)
    return {
        'x_prompt': nrm(ks[0], (BATCH, SEQ, D_MODEL), f32),
        'x_sample': nrm(ks[1], (DEC_BATCH, DEC_SEQ, D_MODEL), f32),
        'c': nrm(ks[2], (DEC_BATCH, D_MODEL), f32),
        'cache_win_k': nrm(ks[3], (DEC_BATCH, DEPTH, PAST_LEN, A_KV, HEAD_DIM), f32),
        'cache_win_v': nrm(ks[4], (DEC_BATCH, DEPTH, PAST_LEN, A_KV, HEAD_DIM), f32),
        'cache_na_k': nrm(ks[5], (DEC_BATCH, DEPTH, PAST_LEN, B_HEADS, HEAD_DIM), f32),
        'cache_na_v': nrm(ks[6], (DEC_BATCH, DEPTH, PAST_LEN, B_HEADS, HEAD_DIM), f32),
        'state_ret': nrm(ks[7], (DEC_BATCH, DEPTH, 2, C_HEADS, HEAD_DIM, HEAD_DIM), f32),
        'state_hgrn': nrm(ks[8], (DEC_BATCH, DEPTH, 2, D_HEADS, HEAD_DIM, HEAD_DIM), f32),
        'c_ctx': nrm(ks[9], (D_MODEL,), f32),
        'w_ada': nrm(ks[10], (DEPTH, D_MODEL, 3 * D_MODEL), f32) * D_MODEL ** -0.5,
        'b_ada': nrm(ks[11], (DEPTH, 3 * D_MODEL), f32) * 0.02,
        'g_pre': 1.0 + 0.02 * nrm(ks[12], (DEPTH, D_MODEL), f32),
        'g_post': 1.0 + 0.02 * nrm(ks[13], (DEPTH, D_MODEL), f32),
        'w_in': nrm(ks[14], (DEPTH, D_MODEL, IN_WIDTH), f32) * D_MODEL ** -0.5,
        'w_out': nrm(ks[15], (DEPTH, D_MODEL, D_MODEL), f32) * D_MODEL ** -0.5,
        'attn_sink': nrm(ks[16], (DEPTH, A_HEADS), f32) * 0.5,
        'na_rpb': nrm(ks[17], (DEPTH, B_HEADS, 2 * NA_ROWS - 1, 2 * NA_COLS - 1), f32) * 0.1,
        'ret_decay_logit': ret_init[None, None, :] + 0.1 * nrm(ks[18], (DEPTH, 2, C_HEADS), f32),
        'hgrn_lb_logit': nrm(ks[19], (DEPTH, 2, D_HEADS * HEAD_DIM), f32) * 0.5,
    }


def reference(x_prompt, x_sample, c, cache_win_k, cache_win_v, cache_na_k, cache_na_v, state_ret, state_hgrn,
              c_ctx, w_ada, b_ada, g_pre, g_post, w_in, w_out, attn_sink, na_rpb, ret_decay_logit, hgrn_lb_logit):
    p_lb = jax.nn.softmax(hgrn_lb_logit.astype(jnp.float32), axis=0)
    lower_bounds = (jnp.cumsum(p_lb, axis=0) - p_lb[0:1]).reshape(DEPTH, 2, D_HEADS, HEAD_DIM)
    log_gammas = jax.nn.log_sigmoid(ret_decay_logit.astype(jnp.float32))

    x = x_prompt
    win_k, win_v, na_k, na_v, st_ret, st_hgrn = [], [], [], [], [], []
    for l in range(DEPTH):
        x, (ak, av, bk, bv, s_c, s_d) = context_layer(
            x, c_ctx, w_ada[l], b_ada[l], g_pre[l], g_post[l], w_in[l], w_out[l],
            attn_sink[l], log_gammas[l], lower_bounds[l])
        win_k.append(ak)
        win_v.append(av)
        na_k.append(bk)
        na_v.append(bv)
        st_ret.append(s_c)
        st_hgrn.append(s_d)
    y_prompt = x
    new_win_k = jnp.stack(win_k, axis=1)
    new_win_v = jnp.stack(win_v, axis=1)
    new_na_k = jnp.stack(na_k, axis=1)
    new_na_v = jnp.stack(na_v, axis=1)
    new_state_ret = jnp.stack(st_ret, axis=1)
    new_state_hgrn = jnp.stack(st_hgrn, axis=1)

    x = x_sample
    for l in range(DEPTH):
        x = latent_layer(
            x, c, cache_win_k[:, l], cache_win_v[:, l], cache_na_k[:, l], cache_na_v[:, l],
            state_ret[:, l], state_hgrn[:, l], w_ada[l], b_ada[l], g_pre[l], g_post[l], w_in[l], w_out[l],
            attn_sink[l], na_rpb[l], log_gammas[l], lower_bounds[l])
    y_sample = x
    return (y_prompt, y_sample, new_win_k, new_win_v, new_na_k, new_na_v, new_state_ret, new_state_hgrn)
```

```python
import functools

import numpy as np
import jax
import jax.numpy as jnp
from jax import lax
from jax.experimental import pallas as pl
from jax.experimental.pallas import tpu as pltpu

F32 = jnp.float32
BF16 = jnp.bfloat16

D_MODEL = 1024
DEPTH = 4
GRID_W = 64
HEAD_DIM = 64
N_HEADS = 4
GROUP_W = N_HEADS * HEAD_DIM
A_KV = 2
WINDOW = 128
NA_ROWS = 8
NA_COLS = 16
ROPE_BASE = 10000.0
EPS = 1e-6
NEG = -1e30
LB_FLOOR = 1e-30
W_A = 3 * GROUP_W
W_B = 4 * GROUP_W
W_C = 4 * GROUP_W
W_D = 5 * GROUP_W
IN_WIDTH = W_A + W_B + W_C + W_D

V7X_VMEM_LIMIT_BYTES = 56 * 1024 * 1024
ADA_ROWS = 16

NT_DIMS = (((1,), (1,)), ((), ()))
TN_DIMS = (((0,), (0,)), ((), ()))


def _cparams(sem):
    return pltpu.CompilerParams(dimension_semantics=sem, vmem_limit_bytes=V7X_VMEM_LIMIT_BYTES)


def _sigmoid(x):
    return 1.0 / (1.0 + jnp.exp(-x))


def _silu(x):
    return x * _sigmoid(x)


def _head_masks(rows):
    lane = lax.broadcasted_iota(jnp.int32, (rows, GROUP_W), 1)
    return [(lane >= h * HEAD_DIM) & (lane < (h + 1) * HEAD_DIM) for h in range(N_HEADS)]


def _stack_heads(x, masks):
    return jnp.concatenate([jnp.where(m, x, 0.0) for m in masks], axis=0)


def _unstack_heads(x, masks, t):
    out = jnp.where(masks[0], x[0:t], 0.0)
    for h in range(1, N_HEADS):
        out = out + jnp.where(masks[h], x[h * t:(h + 1) * t], 0.0)
    return out


def _block_ones():
    shift = HEAD_DIM.bit_length() - 1
    r = lax.broadcasted_iota(jnp.int32, (GROUP_W, GROUP_W), 0) >> shift
    c = lax.broadcasted_iota(jnp.int32, (GROUP_W, GROUP_W), 1) >> shift
    return r == c


def _head_sum(x, ones_bf16):
    hi = x.astype(BF16)
    lo = (x - hi.astype(F32)).astype(BF16)
    return (jnp.dot(hi, ones_bf16, preferred_element_type=F32)
            + jnp.dot(lo, ones_bf16, preferred_element_type=F32))


def _adaln_kernel(c_ref, w_ref, b_ref, o_ref):
    s = _silu(c_ref[...]).astype(BF16)
    o_ref[0] = jnp.dot(s, w_ref[0].astype(BF16), preferred_element_type=F32) + b_ref[0]


def _adaln(cvecs, w_ada, b_ada):
    depth, d, d3 = w_ada.shape
    tn = 512
    return pl.pallas_call(
        _adaln_kernel,
        grid=(depth, d3 // tn),
        in_specs=[pl.BlockSpec((ADA_ROWS, d), lambda l, j: (0, 0)),
                  pl.BlockSpec((1, d, tn), lambda l, j: (l, 0, j)),
                  pl.BlockSpec((1, 1, tn), lambda l, j: (l, 0, j))],
        out_specs=pl.BlockSpec((1, ADA_ROWS, tn), lambda l, j: (l, 0, j)),
        out_shape=jax.ShapeDtypeStruct((depth, ADA_ROWS, d3), F32),
        compiler_params=_cparams(("arbitrary", "arbitrary")),
        name="adaln",
    )(cvecs, w_ada, b_ada.reshape(depth, 1, d3))


def _rope(x, cos, sin):
    w = x.shape[-1]
    lane = lax.broadcasted_iota(jnp.int32, x.shape, 1)
    first = (lane & 31) < 16
    swapped = jnp.where(first, pltpu.roll(x, w - 16, 1), pltpu.roll(x, 16, 1))
    return x * cos[:, :w] + swapped * sin[:, :w]


def _inproj_kernel(*refs, rope):
    if rope:
        x_ref, shift_ref, scale_ref, g_ref, w_ref, cos_ref, sin_ref, pa_ref, pb_ref, pc_ref, pd_ref = refs
    else:
        x_ref, shift_ref, scale_ref, g_ref, w_ref, pa_ref, pb_ref, pc_ref, pd_ref = refs
    x = x_ref[0]
    ms = jnp.mean(x * x, axis=-1, keepdims=True)
    y = x * lax.rsqrt(ms + EPS) * g_ref[...]
    h = (y * (1.0 + scale_ref[0]) + shift_ref[0]).astype(BF16)

    def mm(c0, c1):
        return jnp.dot(h, w_ref[:, c0:c1], preferred_element_type=F32)

    if rope:
        cos = cos_ref[...]
        sin = sin_ref[...]
    g = GROUP_W
    aq = mm(0, g)
    akv = mm(g, 2 * g)
    if rope:
        aq = _rope(aq, cos, sin)
        ak = _rope(akv[:, :g // 2], cos, sin)
        akv = jnp.concatenate([ak, akv[:, g // 2:]], axis=1)
    pa_ref[0, :, 0:g] = aq.astype(pa_ref.dtype)
    pa_ref[0, :, g:2 * g] = akv.astype(pa_ref.dtype)
    pa_ref[0, :, 2 * g:3 * g] = mm(2 * g, 3 * g).astype(pa_ref.dtype)
    pb_ref[0] = mm(W_A, W_A + W_B).astype(pb_ref.dtype)
    c0 = W_A + W_B
    cq = mm(c0, c0 + g)
    ck = mm(c0 + g, c0 + 2 * g)
    if rope:
        cq = _rope(cq, cos, sin)
        ck = _rope(ck, cos, sin)
    pc_ref[0, :, 0:g] = cq.astype(pc_ref.dtype)
    pc_ref[0, :, g:2 * g] = ck.astype(pc_ref.dtype)
    pc_ref[0, :, 2 * g:4 * g] = mm(c0 + 2 * g, c0 + 4 * g).astype(pc_ref.dtype)
    d0 = c0 + W_C
    pd_ref[0] = mm(d0, d0 + W_D).astype(pd_ref.dtype)


def _rope_tables(slen):
    t = np.arange(slen)
    nf = HEAD_DIM // 4
    freqs = ROPE_BASE ** (-np.arange(nf, dtype=np.float64) / nf)
    d = np.arange(HEAD_DIM)
    pos = np.where(d[None, :] < HEAD_DIM // 2, (t // GRID_W)[:, None], (t % GRID_W)[:, None])
    ang = pos * freqs[d % nf][None, :]
    sign = np.where((d % (2 * nf)) < nf, -1.0, 1.0)[None, :]
    cos = np.tile(np.cos(ang), (1, N_HEADS))
    sin = np.tile(np.sin(ang) * sign, (1, N_HEADS))
    return jnp.asarray(cos, F32), jnp.asarray(sin, F32)


def _in_proj(x, shift, scale, g_pre, w_in_bf16, rope, tm=256):
    bsz, slen, d = x.shape
    shared = shift.shape[0] == 1
    mod_map = (lambda b, i: (0, 0, 0)) if shared else (lambda b, i: (b, 0, 0))
    in_specs = [pl.BlockSpec((1, tm, d), lambda b, i: (b, i, 0)),
                pl.BlockSpec((1, 1, d), mod_map),
                pl.BlockSpec((1, 1, d), mod_map),
                pl.BlockSpec((1, d), lambda b, i: (0, 0)),
                pl.BlockSpec((d, IN_WIDTH), lambda b, i: (0, 0))]
    args = [x, shift, scale, g_pre.reshape(1, d), w_in_bf16]
    if rope:
        cos, sin = _rope_tables(slen)
        in_specs += [pl.BlockSpec((tm, GROUP_W), lambda b, i: (i, 0))] * 2
        args += [cos, sin]
    widths = (W_A, W_B, W_C, W_D)
    return pl.pallas_call(
        functools.partial(_inproj_kernel, rope=rope),
        grid=(bsz, slen // tm),
        in_specs=in_specs,
        out_specs=[pl.BlockSpec((1, tm, w), lambda b, i: (b, i, 0)) for w in widths],
        out_shape=[jax.ShapeDtypeStruct((bsz, slen, w), F32) for w in widths],
        compiler_params=_cparams(("arbitrary", "arbitrary")),
        name="in_proj",
    )(*args)


def _attn_kernel(*refs, n_kv, qb, kw, back, slen, has_ctx, n_tab, tab_heads, has_sink):
    it = iter(refs)
    q_ref, k_ref, v_ref, g_ref = next(it), next(it), next(it), next(it)
    kc_ref = vc_ref = tab_ref = sink_ref = None
    if has_ctx:
        kc_ref, vc_ref = next(it), next(it)
    if n_tab:
        tab_ref = next(it)
    if has_sink:
        sink_ref = next(it)
    o_ref = next(it)

    n = pl.program_id(1)
    nblk = slen // qb
    ws = pl.multiple_of(jnp.clip(n * qb - back, 0, slen - kw), HEAD_DIM)
    if n_tab == 3:
        tix = jnp.where(n == 0, 0, jnp.where(n == nblk - 1, 2, 1))
    else:
        tix = 0
    grp = N_HEADS // n_kv
    scale = HEAD_DIM ** -0.5
    for h in range(N_HEADS):
        kh = h // grp
        hs = slice(h * HEAD_DIM, (h + 1) * HEAD_DIM)
        ks = slice(kh * HEAD_DIM, (kh + 1) * HEAD_DIM)
        q = q_ref[0, :, hs].astype(BF16)
        kl = k_ref[0, pl.ds(ws, kw), ks].astype(BF16)
        vl = v_ref[0, pl.ds(ws, kw), ks].astype(BF16)
        s = lax.dot_general(q, kl, NT_DIMS, preferred_element_type=F32) * scale
        if n_tab:
            s = s + tab_ref[tix, h if tab_heads == N_HEADS else 0]
        m = jnp.max(s, axis=-1, keepdims=True)
        if has_ctx:
            sc = lax.dot_general(q, kc_ref[0, :, ks].astype(BF16), NT_DIMS, preferred_element_type=F32) * scale
            m = jnp.maximum(m, jnp.max(sc, axis=-1, keepdims=True))
        if has_sink:
            m = jnp.maximum(m, sink_ref[h])
        p = jnp.exp(s - m)
        den = jnp.sum(p, axis=-1, keepdims=True)
        acc = jnp.dot(p.astype(BF16), vl, preferred_element_type=F32)
        if has_ctx:
            pc = jnp.exp(sc - m)
            den = den + jnp.sum(pc, axis=-1, keepdims=True)
            acc = acc + jnp.dot(pc.astype(BF16), vc_ref[0, :, ks].astype(BF16), preferred_element_type=F32)
        if has_sink:
            den = den + jnp.exp(sink_ref[h] - m)
        o = acc / den
        o_ref[0, :, hs] = (o * _silu(g_ref[0, :, hs])).astype(o_ref.dtype)


def _attention(p, q_col, k_col, v_col, g_col, n_kv, qb, kw, back, ctx_k=None, ctx_v=None,
               table=None, sink=None):
    bsz, slen, _ = p.shape
    kvw = n_kv * HEAD_DIM
    in_specs = [pl.BlockSpec((1, qb, GROUP_W), lambda b, n: (b, n, q_col)),
                pl.BlockSpec((1, slen, kvw), lambda b, n: (b, 0, k_col)),
                pl.BlockSpec((1, slen, kvw), lambda b, n: (b, 0, v_col)),
                pl.BlockSpec((1, qb, GROUP_W), lambda b, n: (b, n, g_col))]
    args = [p, p, p, p]
    has_ctx = ctx_k is not None
    if has_ctx:
        past = ctx_k.shape[1]
        in_specs += [pl.BlockSpec((1, past, kvw), lambda b, n: (b, 0, 0))] * 2
        args += [ctx_k, ctx_v]
    n_tab = tab_heads = 0
    if table is not None:
        n_tab, tab_heads = table.shape[0], table.shape[1]
        in_specs.append(pl.BlockSpec(table.shape, lambda b, n: (0, 0, 0, 0)))
        args.append(table)
    if sink is not None:
        in_specs.append(pl.BlockSpec(memory_space=pltpu.SMEM))
        args.append(sink)
    kern = functools.partial(_attn_kernel, n_kv=n_kv, qb=qb, kw=kw, back=back, slen=slen, has_ctx=has_ctx,
                             n_tab=n_tab, tab_heads=tab_heads, has_sink=sink is not None)
    return pl.pallas_call(
        kern,
        grid=(bsz, slen // qb),
        in_specs=in_specs,
        out_specs=pl.BlockSpec((1, qb, GROUP_W), lambda b, n: (b, n, 0)),
        out_shape=jax.ShapeDtypeStruct((bsz, slen, GROUP_W), BF16),
        compiler_params=_cparams(("arbitrary", "arbitrary")),
        name="attn",
    )(*args)


WIN_QB = 128
WIN_KW = 3 * WIN_QB
NA_QROWS = 4
NA_KROWS = 12
NA_QB = NA_QROWS * GRID_W
NA_KW = NA_KROWS * GRID_W


def _window_table(slen):
    nblk = slen // WIN_QB
    tabs = []
    for n in (0, 1, nblk - 1):
        ws = int(np.clip(n * WIN_QB - WIN_QB, 0, slen - WIN_KW))
        qpos = n * WIN_QB + np.arange(WIN_QB)[:, None]
        kpos = ws + np.arange(WIN_KW)[None, :]
        tabs.append(np.where(np.abs(qpos - kpos) <= WINDOW, 0.0, NEG))
    return jnp.asarray(np.stack(tabs)[:, None], F32)


def _na_table(rpb, slen):
    rows = slen // GRID_W
    wr = min(NA_ROWS, rows)
    nblk = rows // NA_QROWS
    ri_l, ci_l, ok_l = [], [], []
    for g in (0, 1, nblk - 1):
        ws_row = int(np.clip(g * NA_QROWS - NA_QROWS, 0, rows - NA_KROWS))
        r = (g * NA_QROWS + np.arange(NA_QROWS))[:, None, None, None]
        qc = np.arange(GRID_W)[None, :, None, None]
        kr = (ws_row + np.arange(NA_KROWS))[None, None, :, None]
        kc = np.arange(GRID_W)[None, None, None, :]
        rs = np.clip(r - wr // 2, 0, rows - wr)
        qws = np.clip(qc - NA_COLS // 2, 0, GRID_W - NA_COLS)
        ok = (kr >= rs) & (kr < rs + wr) & (kc >= qws) & (kc < qws + NA_COLS)
        ri = np.clip(kr - r + NA_ROWS - 1, 0, 2 * NA_ROWS - 2)
        ci = np.clip(kc - qc + NA_COLS - 1, 0, 2 * NA_COLS - 2)
        shp = (NA_QROWS, GRID_W, NA_KROWS, GRID_W)
        ri_l.append(np.broadcast_to(ri, shp).reshape(NA_QB, NA_KW))
        ci_l.append(np.broadcast_to(ci, shp).reshape(NA_QB, NA_KW))
        ok_l.append(np.broadcast_to(ok, shp).reshape(NA_QB, NA_KW))
    ri, ci, ok = np.stack(ri_l), np.stack(ci_l), np.stack(ok_l)
    bias = rpb.astype(F32)[:, ri, ci]
    return jnp.transpose(jnp.where(ok[None], bias, NEG), (1, 0, 2, 3))


def _ret_kernel(q_ref, k_ref, v_ref, g_ref, lgl_ref, lgr_ref, s0_ref, o_ref, sfin_ref, of_scr, st_scr,
                *, slen, t):
    nc = slen // t
    masks = _head_masks(t)
    bmask = _block_ones()
    ones_bf16 = jnp.where(bmask, 1.0, 0.0).astype(BF16)
    ii = lax.broadcasted_iota(jnp.int32, (N_HEADS * t, t), 0) & (t - 1)
    jj = lax.broadcasted_iota(jnp.int32, (N_HEADS * t, t), 1)
    dist = (ii - jj).astype(F32)
    idx = lax.broadcasted_iota(jnp.int32, (t, GROUP_W), 0).astype(F32)

    def run(direction, emit):
        fwd = direction == 0
        lgl = lgl_ref[direction]
        lgr = lgr_ref[direction]
        if fwd:
            dmat = jnp.where(dist >= 0, jnp.exp(dist * lgr), 0.0)
            qdec = jnp.exp((idx + 1.0) * lgl)
            kdec = jnp.exp((t - 1.0 - idx) * lgl)
        else:
            dmat = jnp.where(dist <= 0, jnp.exp(-dist * lgr), 0.0)
            qdec = jnp.exp((t - idx) * lgl)
            kdec = jnp.exp(idx * lgl)
        cdec = jnp.exp(float(t) * lgl)
        st_scr[...] = s0_ref[0, direction]

        def body(ci, carry):
            c = ci if fwd else nc - 1 - ci
            rows = pl.ds(pl.multiple_of(c * t, t), t)
            q = q_ref[0, rows, :]
            k = k_ref[0, rows, :] * (HEAD_DIM ** -0.5)
            v = v_ref[0, rows, :].astype(BF16)
            qs = _stack_heads(q, masks).astype(BF16)
            sc = lax.dot_general(qs, k.astype(BF16), NT_DIMS, preferred_element_type=F32) * dmat
            oall = jnp.dot(sc.astype(BF16), v, preferred_element_type=F32)
            o = _unstack_heads(oall, masks, t)
            st = st_scr[...]
            o = o + lax.dot_general((q * qdec).astype(BF16), st.astype(BF16), NT_DIMS,
                                    preferred_element_type=F32)
            upd = lax.dot_general(v, (k * kdec).astype(BF16), TN_DIMS, preferred_element_type=F32)
            st_scr[...] = st * cdec + jnp.where(bmask, upd, 0.0)
            emit(rows, o)
            return carry

        lax.fori_loop(0, nc, body, 0)
        sfin_ref[0, direction] = st_scr[...]

    def emit_fwd(rows, o):
        of_scr[rows, :] = o

    def emit_bwd(rows, o):
        o = o + of_scr[rows, :]
        mu = _head_sum(o, ones_bf16) * (1.0 / HEAD_DIM)
        d = o - mu
        var = _head_sum(d * d, ones_bf16) * (1.0 / HEAD_DIM)
        y = d * lax.rsqrt(var + EPS)
        o_ref[0, rows, :] = (y * _silu(g_ref[0, rows, :])).astype(o_ref.dtype)

    run(0, emit_fwd)
    run(1, emit_bwd)


def _blockdiag_t(s0):
    bsz = s0.shape[0]
    eye = jnp.eye(N_HEADS, dtype=s0.dtype)
    out = jnp.einsum('bxhde,hg->bxhegd', s0, eye)
    return out.reshape(bsz, 2, GROUP_W, GROUP_W)


def _unblockdiag_t(st):
    bsz = st.shape[0]
    s5 = st.reshape(bsz, 2, N_HEADS, HEAD_DIM, N_HEADS, HEAD_DIM)
    blocks = jnp.stack([s5[:, :, h, :, h, :] for h in range(N_HEADS)], axis=2)
    return jnp.swapaxes(blocks, -1, -2)


def _retention(pc, log_gamma, s0_bd, t=64):
    bsz, slen, _ = pc.shape
    lgl = jnp.repeat(log_gamma, HEAD_DIM, axis=1).reshape(2, 1, GROUP_W)
    lgr = jnp.repeat(log_gamma, t, axis=1).reshape(2, N_HEADS * t, 1)
    col = lambda j: pl.BlockSpec((1, slen, GROUP_W), lambda b: (b, 0, j))
    st_spec = pl.BlockSpec((1, 2, GROUP_W, GROUP_W), lambda b: (b, 0, 0, 0))
    return pl.pallas_call(
        functools.partial(_ret_kernel, slen=slen, t=t),
        grid=(bsz,),
        in_specs=[col(0), col(1), col(2), col(3),
                  pl.BlockSpec((2, 1, GROUP_W), lambda b: (0, 0, 0)),
                  pl.BlockSpec((2, N_HEADS * t, 1), lambda b: (0, 0, 0)),
                  st_spec],
        out_specs=[pl.BlockSpec((1, slen, GROUP_W), lambda b: (b, 0, 0)), st_spec],
        out_shape=[jax.ShapeDtypeStruct((bsz, slen, GROUP_W), BF16),
                   jax.ShapeDtypeStruct((bsz, 2, GROUP_W, GROUP_W), F32)],
        scratch_shapes=[pltpu.VMEM((slen, GROUP_W), F32), pltpu.VMEM((GROUP_W, GROUP_W), F32)],
        compiler_params=_cparams(("arbitrary",)),
        name="retention",
    )(pc, pc, pc, pc, lgl, lgr, s0_bd)


HGRN_DIRECT = 8


def _split3(x):
    h1 = x.astype(BF16)
    r1 = x - h1.astype(F32)
    h2 = r1.astype(BF16)
    h3 = (r1 - h2.astype(F32)).astype(BF16)
    return h1, h2, h3


def _hgrn_kernel(q_ref, zf_ref, zb_ref, v_ref, g_ref, lb_ref, s0_ref, o_ref, sfin_ref, of_scr, st_scr,
                 *, slen, t):
    nc = slen // t
    masks = _head_masks(t)
    bmask = _block_ones()
    ones_bf16 = jnp.where(bmask, 1.0, 0.0).astype(BF16)
    r_i = lax.broadcasted_iota(jnp.int32, (t, t), 0)
    c_i = lax.broadcasted_iota(jnp.int32, (t, t), 1)
    sub = lax.broadcasted_iota(jnp.int32, (t, GROUP_W), 0) & (HGRN_DIRECT - 1)
    levels = []
    s = t // 2
    while s >= HGRN_DIRECT:
        levels.append(s)
        s //= 2
    lvl_masks = []
    for s in levels:
        shift = (2 * s).bit_length() - 1
        m = jnp.where((r_i >> shift) == (c_i >> shift), 1.0, 0.0)
        lvl_masks.append(jnp.concatenate([m] * N_HEADS, axis=0))

    def run(direction, z_ref, emit):
        fwd = direction == 0
        lbd = jnp.maximum(lb_ref[direction], LB_FLOOR)
        tri = jnp.where((r_i >= c_i) if fwd else (r_i <= c_i), 1.0, 0.0).astype(BF16)
        st_scr[...] = s0_ref[0, direction]

        def body(ci, carry):
            c = ci if fwd else nc - 1 - ci
            rows = pl.ds(pl.multiple_of(c * t, t), t)
            q = _silu(q_ref[0, rows, :])
            z = z_ref[0, rows, :]
            v = v_ref[0, rows, :]
            vb = v.astype(BF16)
            a = jnp.exp(-jnp.abs(z))
            inv = 1.0 / (1.0 + a)
            pos = z >= 0
            sig = jnp.where(pos, inv, a * inv)
            nsig = jnp.where(pos, a * inv, inv)
            lf = jnp.log(lbd + (1.0 - lbd) * sig)
            k = (1.0 - lbd) * nsig
            h1, h2, h3 = _split3(lf)
            b = (jnp.dot(tri, h1, preferred_element_type=F32) + jnp.dot(tri, h2, preferred_element_type=F32)
                 + jnp.dot(tri, h3, preferred_element_type=F32))

            sc = None
            for s, lm in zip(levels, lvl_masks):
                pa, pb = [], []
                zero = jnp.zeros((s, GROUP_W), F32)
                for gi in range(t // (2 * s)):
                    lo = b[gi * 2 * s:gi * 2 * s + s]
                    hi = b[gi * 2 * s + s:(gi + 1) * 2 * s]
                    if fwd:
                        anc = hi[0:1]
                        pa += [zero, jnp.exp(hi - anc)]
                        pb += [jnp.exp(anc - lo), zero]
                    else:
                        anc = lo[s - 1:s]
                        pa += [jnp.exp(lo - anc), zero]
                        pb += [zero, jnp.exp(anc - hi)]
                qa = _stack_heads(q * jnp.concatenate(pa, axis=0), masks).astype(BF16)
                kb = (k * jnp.concatenate(pb, axis=0)).astype(BF16)
                term = lax.dot_general(qa, kb, NT_DIMS, preferred_element_type=F32) * lm
                sc = term if sc is None else sc + term
            o = None
            if sc is not None:
                o = _unstack_heads(jnp.dot(sc.astype(BF16), vb, preferred_element_type=F32), masks, t)

            for dlt in range(HGRN_DIRECT):
                if dlt == 0:
                    term = _head_sum(q * k, ones_bf16) * v
                else:
                    sh = dlt if fwd else t - dlt
                    ok = (sub >= dlt) if fwd else (sub + dlt <= HGRN_DIRECT - 1)
                    w = jnp.exp(jnp.minimum(b - pltpu.roll(b, sh, 0), 0.0))
                    tt = jnp.where(ok, q * pltpu.roll(k, sh, 0) * w, 0.0)
                    term = _head_sum(tt, ones_bf16) * pltpu.roll(v, sh, 0)
                o = term if o is None else o + term

            st = st_scr[...]
            o = o + lax.dot_general((q * jnp.exp(b)).astype(BF16), st.astype(BF16), NT_DIMS,
                                    preferred_element_type=F32)
            bl = b[t - 1:t] if fwd else b[0:1]
            upd = lax.dot_general(vb, (k * jnp.exp(bl - b)).astype(BF16), TN_DIMS,
                                  preferred_element_type=F32)
            st_scr[...] = st * jnp.exp(bl) + jnp.where(bmask, upd, 0.0)
            emit(rows, o)
            return carry

        lax.fori_loop(0, nc, body, 0)
        sfin_ref[0, direction] = st_scr[...]

    def emit_fwd(rows, o):
        of_scr[rows, :] = o

    def emit_bwd(rows, o):
        o = o + of_scr[rows, :]
        ms = _head_sum(o * o, ones_bf16) * (1.0 / HEAD_DIM)
        y = o * lax.rsqrt(ms + EPS)
        o_ref[0, rows, :] = (y * _silu(g_ref[0, rows, :])).astype(o_ref.dtype)

    run(0, zf_ref, emit_fwd)
    run(1, zb_ref, emit_bwd)


def _hgrn(pd, lb, s0_bd, t=64):
    bsz, slen, _ = pd.shape
    col = lambda j: pl.BlockSpec((1, slen, GROUP_W), lambda b: (b, 0, j))
    st_spec = pl.BlockSpec((1, 2, GROUP_W, GROUP_W), lambda b: (b, 0, 0, 0))
    return pl.pallas_call(
        functools.partial(_hgrn_kernel, slen=slen, t=t),
        grid=(bsz,),
        in_specs=[col(0), col(1), col(2), col(3), col(4),
                  pl.BlockSpec((2, 1, GROUP_W), lambda b: (0, 0, 0)),
                  st_spec],
        out_specs=[pl.BlockSpec((1, slen, GROUP_W), lambda b: (b, 0, 0)), st_spec],
        out_shape=[jax.ShapeDtypeStruct((bsz, slen, GROUP_W), BF16),
                   jax.ShapeDtypeStruct((bsz, 2, GROUP_W, GROUP_W), F32)],
        scratch_shapes=[pltpu.VMEM((slen, GROUP_W), F32), pltpu.VMEM((GROUP_W, GROUP_W), F32)],
        compiler_params=_cparams(("arbitrary",)),
        name="hgrn2",
    )(pd, pd, pd, pd, pd, lb.reshape(2, 1, GROUP_W), s0_bd)


def _outproj_kernel(ma_ref, mb_ref, mc_ref, md_ref, w_ref, x_ref, gate_ref, g_ref, o_ref):
    g = GROUP_W
    y = jnp.dot(ma_ref[0], w_ref[0:g, :], preferred_element_type=F32)
    y = y + jnp.dot(mb_ref[0], w_ref[g:2 * g, :], preferred_element_type=F32)
    y = y + jnp.dot(mc_ref[0], w_ref[2 * g:3 * g, :], preferred_element_type=F32)
    y = y + jnp.dot(md_ref[0], w_ref[3 * g:4 * g, :], preferred_element_type=F32)
    ms = jnp.mean(y * y, axis=-1, keepdims=True)
    r = y * lax.rsqrt(ms + EPS) * g_ref[...]
    o_ref[0] = x_ref[0] + gate_ref[0] * r


def _out_proj(mixed, w_out_bf16, x, gate_mod, g_post, tm=256):
    bsz, slen, d = x.shape
    shared = gate_mod.shape[0] == 1
    mod_map = (lambda b, i: (0, 0, 0)) if shared else (lambda b, i: (b, 0, 0))
    mspec = pl.BlockSpec((1, tm, GROUP_W), lambda b, i: (b, i, 0))
    return pl.pallas_call(
        _outproj_kernel,
        grid=(bsz, slen // tm),
        in_specs=[mspec, mspec, mspec, mspec,
                  pl.BlockSpec((d, d), lambda b, i: (0, 0)),
                  pl.BlockSpec((1, tm, d), lambda b, i: (b, i, 0)),
                  pl.BlockSpec((1, 1, d), mod_map),
                  pl.BlockSpec((1, d), lambda b, i: (0, 0))],
        out_specs=pl.BlockSpec((1, tm, d), lambda b, i: (b, i, 0)),
        out_shape=jax.ShapeDtypeStruct((bsz, slen, d), F32),
        compiler_params=_cparams(("arbitrary", "arbitrary")),
        name="out_proj",
    )(*mixed, w_out_bf16, x, gate_mod, g_post.reshape(1, d))


def _context_layer(x, mod, g_pre, g_post, w_in, w_out, sink, log_gamma, lb):
    shift, scale, gate = mod
    bsz, slen, _ = x.shape
    pa, pb, pc, pd = _in_proj(x, shift, scale, g_pre, w_in, rope=False)
    o_a = _attention(pa, 0, 2, 3, 2, n_kv=A_KV, qb=slen, kw=slen, back=0, sink=sink)
    o_b = _attention(pb, 0, 1, 2, 3, n_kv=N_HEADS, qb=slen, kw=slen, back=0)
    zeros = jnp.zeros((bsz, 2, GROUP_W, GROUP_W), F32)
    o_c, s_c = _retention(pc, log_gamma, zeros)
    o_d, s_d = _hgrn(pd, lb, zeros)
    x = _out_proj((o_a, o_b, o_c, o_d), w_out, x, gate, g_post)
    g = GROUP_W
    ak = pa[:, :, g:g + g // 2].reshape(bsz, slen, A_KV, HEAD_DIM)
    av = pa[:, :, g + g // 2:2 * g].reshape(bsz, slen, A_KV, HEAD_DIM)
    bk = pb[:, :, g:2 * g].reshape(bsz, slen, N_HEADS, HEAD_DIM)
    bv = pb[:, :, 2 * g:3 * g].reshape(bsz, slen, N_HEADS, HEAD_DIM)
    return x, (ak, av, bk, bv, _unblockdiag_t(s_c), _unblockdiag_t(s_d))


def _latent_layer(x, mod, ca_k, ca_v, cb_k, cb_v, st_c, st_d, g_pre, g_post, w_in, w_out, sink, win_tab,
                  na_tab, log_gamma, lb):
    shift, scale, gate = mod
    bsz, slen, _ = x.shape
    past = ca_k.shape[1]
    pa, pb, pc, pd = _in_proj(x, shift, scale, g_pre, w_in, rope=True)
    o_a = _attention(pa, 0, 2, 3, 2, n_kv=A_KV, qb=WIN_QB, kw=WIN_KW, back=WIN_QB,
                     ctx_k=ca_k.reshape(bsz, past, A_KV * HEAD_DIM), ctx_v=ca_v.reshape(bsz, past, A_KV * HEAD_DIM),
                     table=win_tab, sink=sink)
    o_b = _attention(pb, 0, 1, 2, 3, n_kv=N_HEADS, qb=NA_QB, kw=NA_KW, back=NA_QB,
                     ctx_k=cb_k.reshape(bsz, past, GROUP_W), ctx_v=cb_v.reshape(bsz, past, GROUP_W),
                     table=na_tab)
    o_c, _ = _retention(pc, log_gamma, _blockdiag_t(st_c))
    o_d, _ = _hgrn(pd, lb, _blockdiag_t(st_d))
    return _out_proj((o_a, o_b, o_c, o_d), w_out, x, gate, g_post)


def kernel(x_prompt, x_sample, c, cache_win_k, cache_win_v, cache_na_k, cache_na_v, state_ret, state_hgrn,
           c_ctx, w_ada, b_ada, g_pre, g_post, w_in, w_out, attn_sink, na_rpb, ret_decay_logit, hgrn_lb_logit):
    depth = w_ada.shape[0]
    dec_b, dec_s, d = x_sample.shape
    p_lb = jax.nn.softmax(hgrn_lb_logit.astype(F32), axis=0)
    lower_bounds = jnp.cumsum(p_lb, axis=0) - p_lb[0:1]
    log_gammas = jax.nn.log_sigmoid(ret_decay_logit.astype(F32))
    w_in_b = w_in.astype(BF16)
    w_out_b = w_out.astype(BF16)

    cvecs = jnp.zeros((ADA_ROWS, d), F32).at[:dec_b].set(c).at[dec_b].set(c_ctx)
    mods = _adaln(cvecs, w_ada, b_ada)
    win_tab = _window_table(dec_s)

    def mod_of(l, lo, hi):
        m = mods[l, lo:hi].reshape(hi - lo, 1, 3, d)
        return m[:, :, 0], m[:, :, 1], m[:, :, 2]

    x = x_prompt
    outs = [[] for _ in range(6)]
    for l in range(depth):
        x, extra = _context_layer(x, mod_of(l, dec_b, dec_b + 1), g_pre[l], g_post[l], w_in_b[l], w_out_b[l],
                                  attn_sink[l], log_gammas[l], lower_bounds[l])
        for acc, e in zip(outs, extra):
            acc.append(e)
    y_prompt = x
    stacked = [jnp.stack(o, axis=1) for o in outs]

    x = x_sample
    for l in range(depth):
        x = _latent_layer(x, mod_of(l, 0, dec_b), cache_win_k[:, l], cache_win_v[:, l], cache_na_k[:, l],
                          cache_na_v[:, l], state_ret[:, l], state_hgrn[:, l], g_pre[l], g_post[l],
                          w_in_b[l], w_out_b[l], attn_sink[l], win_tab, _na_table(na_rpb[l], dec_s),
                          log_gammas[l], lower_bounds[l])
    return (y_prompt, x, *stacked)
```

```python
import functools

import numpy as np
import jax
import jax.numpy as jnp
from jax import lax
from jax.experimental import pallas as pl
from jax.experimental.pallas import tpu as pltpu

F32 = jnp.float32
BF16 = jnp.bfloat16

D_MODEL = 1024
DEPTH = 4
GRID_W = 64
HEAD_DIM = 64
N_HEADS = 4
GROUP_W = N_HEADS * HEAD_DIM
A_KV = 2
WINDOW = 128
NA_ROWS = 8
NA_COLS = 16
ROPE_BASE = 10000.0
EPS = 1e-6
NEG = -1e30
LB_FLOOR = 1e-30
W_A = 3 * GROUP_W
W_B = 4 * GROUP_W
W_C = 4 * GROUP_W
W_D = 5 * GROUP_W
IN_WIDTH = W_A + W_B + W_C + W_D

V7X_VMEM_LIMIT_BYTES = 56 * 1024 * 1024
ADA_ROWS = 16

NT_DIMS = (((1,), (1,)), ((), ()))
TN_DIMS = (((0,), (0,)), ((), ()))


def _cparams(sem):
    return pltpu.CompilerParams(dimension_semantics=sem, vmem_limit_bytes=V7X_VMEM_LIMIT_BYTES)


def _sigmoid(x):
    return 1.0 / (1.0 + jnp.exp(-x))


def _silu(x):
    return x * _sigmoid(x)


def _head_masks(rows):
    lane = lax.broadcasted_iota(jnp.int32, (rows, GROUP_W), 1)
    return [(lane >= h * HEAD_DIM) & (lane < (h + 1) * HEAD_DIM) for h in range(N_HEADS)]


def _stack_heads(x, masks):
    return jnp.concatenate([jnp.where(m, x, 0.0) for m in masks], axis=0)


def _unstack_heads(x, masks, t):
    out = jnp.where(masks[0], x[0:t], 0.0)
    for h in range(1, N_HEADS):
        out = out + jnp.where(masks[h], x[h * t:(h + 1) * t], 0.0)
    return out


def _block_ones():
    shift = HEAD_DIM.bit_length() - 1
    r = lax.broadcasted_iota(jnp.int32, (GROUP_W, GROUP_W), 0) >> shift
    c = lax.broadcasted_iota(jnp.int32, (GROUP_W, GROUP_W), 1) >> shift
    return r == c


def _head_sum(x, ones_bf16):
    hi = x.astype(BF16)
    lo = (x - hi.astype(F32)).astype(BF16)
    return (jnp.dot(hi, ones_bf16, preferred_element_type=F32)
            + jnp.dot(lo, ones_bf16, preferred_element_type=F32))


def _adaln_kernel(c_ref, w_ref, b_ref, o_ref):
    s = _silu(c_ref[...]).astype(BF16)
    o_ref[0] = jnp.dot(s, w_ref[0].astype(BF16), preferred_element_type=F32) + b_ref[0]


def _adaln(cvecs, w_ada, b_ada):
    depth, d, d3 = w_ada.shape
    tn = 512
    return pl.pallas_call(
        _adaln_kernel,
        grid=(depth, d3 // tn),
        in_specs=[pl.BlockSpec((ADA_ROWS, d), lambda l, j: (0, 0)),
                  pl.BlockSpec((1, d, tn), lambda l, j: (l, 0, j)),
                  pl.BlockSpec((1, 1, tn), lambda l, j: (l, 0, j))],
        out_specs=pl.BlockSpec((1, ADA_ROWS, tn), lambda l, j: (l, 0, j)),
        out_shape=jax.ShapeDtypeStruct((depth, ADA_ROWS, d3), F32),
        compiler_params=_cparams(("arbitrary", "arbitrary")),
        name="adaln",
    )(cvecs, w_ada, b_ada.reshape(depth, 1, d3))


def _rope(x, cos, sin):
    w = x.shape[-1]
    lane = lax.broadcasted_iota(jnp.int32, x.shape, 1)
    first = (lane & 31) < 16
    swapped = jnp.where(first, pltpu.roll(x, w - 16, 1), pltpu.roll(x, 16, 1))
    return x * cos[:, :w] + swapped * sin[:, :w]


def _inproj_kernel(*refs, rope):
    if rope:
        x_ref, shift_ref, scale_ref, g_ref, w_ref, cos_ref, sin_ref, pa_ref, pb_ref, pc_ref, pd_ref = refs
    else:
        x_ref, shift_ref, scale_ref, g_ref, w_ref, pa_ref, pb_ref, pc_ref, pd_ref = refs
    x = x_ref[0]
    ms = jnp.mean(x * x, axis=-1, keepdims=True)
    y = x * lax.rsqrt(ms + EPS) * g_ref[...]
    h = (y * (1.0 + scale_ref[0]) + shift_ref[0]).astype(BF16)

    def mm(c0, c1):
        return jnp.dot(h, w_ref[:, c0:c1], preferred_element_type=F32)

    if rope:
        cos = cos_ref[...]
        sin = sin_ref[...]
    g = GROUP_W
    aq = mm(0, g)
    akv = mm(g, 2 * g)
    if rope:
        aq = _rope(aq, cos, sin)
        ak = _rope(akv[:, :g // 2], cos, sin)
        akv = jnp.concatenate([ak, akv[:, g // 2:]], axis=1)
    pa_ref[0, :, 0:g] = aq.astype(pa_ref.dtype)
    pa_ref[0, :, g:2 * g] = akv.astype(pa_ref.dtype)
    pa_ref[0, :, 2 * g:3 * g] = mm(2 * g, 3 * g).astype(pa_ref.dtype)
    pb_ref[0] = mm(W_A, W_A + W_B).astype(pb_ref.dtype)
    c0 = W_A + W_B
    cq = mm(c0, c0 + g)
    ck = mm(c0 + g, c0 + 2 * g)
    if rope:
        cq = _rope(cq, cos, sin)
        ck = _rope(ck, cos, sin)
    pc_ref[0, :, 0:g] = cq.astype(pc_ref.dtype)
    pc_ref[0, :, g:2 * g] = ck.astype(pc_ref.dtype)
    pc_ref[0, :, 2 * g:4 * g] = mm(c0 + 2 * g, c0 + 4 * g).astype(pc_ref.dtype)
    d0 = c0 + W_C
    pd_ref[0] = mm(d0, d0 + W_D).astype(pd_ref.dtype)


def _rope_tables(slen):
    t = np.arange(slen)
    nf = HEAD_DIM // 4
    freqs = ROPE_BASE ** (-np.arange(nf, dtype=np.float64) / nf)
    d = np.arange(HEAD_DIM)
    pos = np.where(d[None, :] < HEAD_DIM // 2, (t // GRID_W)[:, None], (t % GRID_W)[:, None])
    ang = pos * freqs[d % nf][None, :]
    sign = np.where((d % (2 * nf)) < nf, -1.0, 1.0)[None, :]
    cos = np.tile(np.cos(ang), (1, N_HEADS))
    sin = np.tile(np.sin(ang) * sign, (1, N_HEADS))
    return jnp.asarray(cos, F32), jnp.asarray(sin, F32)


def _in_proj(x, shift, scale, g_pre, w_in_bf16, rope, tm=256):
    bsz, slen, d = x.shape
    shared = shift.shape[0] == 1
    mod_map = (lambda b, i: (0, 0, 0)) if shared else (lambda b, i: (b, 0, 0))
    in_specs = [pl.BlockSpec((1, tm, d), lambda b, i: (b, i, 0)),
                pl.BlockSpec((1, 1, d), mod_map),
                pl.BlockSpec((1, 1, d), mod_map),
                pl.BlockSpec((1, d), lambda b, i: (0, 0)),
                pl.BlockSpec((d, IN_WIDTH), lambda b, i: (0, 0))]
    args = [x, shift, scale, g_pre.reshape(1, d), w_in_bf16]
    if rope:
        cos, sin = _rope_tables(slen)
        in_specs += [pl.BlockSpec((tm, GROUP_W), lambda b, i: (i, 0))] * 2
        args += [cos, sin]
    widths = (W_A, W_B, W_C, W_D)
    return pl.pallas_call(
        functools.partial(_inproj_kernel, rope=rope),
        grid=(bsz, slen // tm),
        in_specs=in_specs,
        out_specs=[pl.BlockSpec((1, tm, w), lambda b, i: (b, i, 0)) for w in widths],
        out_shape=[jax.ShapeDtypeStruct((bsz, slen, w), F32) for w in widths],
        compiler_params=_cparams(("arbitrary", "arbitrary")),
        name="in_proj",
    )(*args)


def _attn_kernel(*refs, n_kv, qb, kw, back, slen, has_ctx, n_tab, tab_heads, has_sink):
    it = iter(refs)
    q_ref, k_ref, v_ref, g_ref = next(it), next(it), next(it), next(it)
    kc_ref = vc_ref = tab_ref = sink_ref = None
    if has_ctx:
        kc_ref, vc_ref = next(it), next(it)
    if n_tab:
        tab_ref = next(it)
    if has_sink:
        sink_ref = next(it)
    o_ref = next(it)

    n = pl.program_id(1)
    nblk = slen // qb
    ws = pl.multiple_of(jnp.clip(n * qb - back, 0, slen - kw), HEAD_DIM)
    if n_tab == 3:
        tix = jnp.where(n == 0, 0, jnp.where(n == nblk - 1, 2, 1))
    else:
        tix = 0
    grp = N_HEADS // n_kv
    scale = HEAD_DIM ** -0.5
    for h in range(N_HEADS):
        kh = h // grp
        hs = slice(h * HEAD_DIM, (h + 1) * HEAD_DIM)
        ks = slice(kh * HEAD_DIM, (kh + 1) * HEAD_DIM)
        q = q_ref[0, :, hs].astype(BF16)
        kl = k_ref[0, pl.ds(ws, kw), ks].astype(BF16)
        vl = v_ref[0, pl.ds(ws, kw), ks].astype(BF16)
        s = lax.dot_general(q, kl, NT_DIMS, preferred_element_type=F32) * scale
        if n_tab:
            s = s + tab_ref[tix, h if tab_heads == N_HEADS else 0]
        m = jnp.max(s, axis=-1, keepdims=True)
        if has_ctx:
            sc = lax.dot_general(q, kc_ref[0, :, ks].astype(BF16), NT_DIMS, preferred_element_type=F32) * scale
            m = jnp.maximum(m, jnp.max(sc, axis=-1, keepdims=True))
        if has_sink:
            m = jnp.maximum(m, sink_ref[h])
        p = jnp.exp(s - m)
        den = jnp.sum(p, axis=-1, keepdims=True)
        acc = jnp.dot(p.astype(BF16), vl, preferred_element_type=F32)
        if has_ctx:
            pc = jnp.exp(sc - m)
            den = den + jnp.sum(pc, axis=-1, keepdims=True)
            acc = acc + jnp.dot(pc.astype(BF16), vc_ref[0, :, ks].astype(BF16), preferred_element_type=F32)
        if has_sink:
            den = den + jnp.exp(sink_ref[h] - m)
        o = acc / den
        o_ref[0, :, hs] = (o * _silu(g_ref[0, :, hs])).astype(o_ref.dtype)


def _attention(p, q_col, k_col, v_col, g_col, n_kv, qb, kw, back, ctx_k=None, ctx_v=None,
               table=None, sink=None):
    bsz, slen, _ = p.shape
    kvw = n_kv * HEAD_DIM
    in_specs = [pl.BlockSpec((1, qb, GROUP_W), lambda b, n: (b, n, q_col)),
                pl.BlockSpec((1, slen, kvw), lambda b, n: (b, 0, k_col)),
                pl.BlockSpec((1, slen, kvw), lambda b, n: (b, 0, v_col)),
                pl.BlockSpec((1, qb, GROUP_W), lambda b, n: (b, n, g_col))]
    args = [p, p, p, p]
    has_ctx = ctx_k is not None
    if has_ctx:
        past = ctx_k.shape[1]
        in_specs += [pl.BlockSpec((1, past, kvw), lambda b, n: (b, 0, 0))] * 2
        args += [ctx_k, ctx_v]
    n_tab = tab_heads = 0
    if table is not None:
        n_tab, tab_heads = table.shape[0], table.shape[1]
        in_specs.append(pl.BlockSpec(table.shape, lambda b, n: (0, 0, 0, 0)))
        args.append(table)
    if sink is not None:
        in_specs.append(pl.BlockSpec(memory_space=pltpu.SMEM))
        args.append(sink)
    kern = functools.partial(_attn_kernel, n_kv=n_kv, qb=qb, kw=kw, back=back, slen=slen, has_ctx=has_ctx,
                             n_tab=n_tab, tab_heads=tab_heads, has_sink=sink is not None)
    return pl.pallas_call(
        kern,
        grid=(bsz, slen // qb),
        in_specs=in_specs,
        out_specs=pl.BlockSpec((1, qb, GROUP_W), lambda b, n: (b, n, 0)),
        out_shape=jax.ShapeDtypeStruct((bsz, slen, GROUP_W), BF16),
        compiler_params=_cparams(("arbitrary", "arbitrary")),
        name="attn",
    )(*args)


WIN_QB = 128
WIN_KW = 3 * WIN_QB
NA_QROWS = 4
NA_KROWS = 12
NA_QB = NA_QROWS * GRID_W
NA_KW = NA_KROWS * GRID_W


def _window_table(slen):
    nblk = slen // WIN_QB
    tabs = []
    for n in (0, 1, nblk - 1):
        ws = int(np.clip(n * WIN_QB - WIN_QB, 0, slen - WIN_KW))
        qpos = n * WIN_QB + np.arange(WIN_QB)[:, None]
        kpos = ws + np.arange(WIN_KW)[None, :]
        tabs.append(np.where(np.abs(qpos - kpos) <= WINDOW, 0.0, NEG))
    return jnp.asarray(np.stack(tabs)[:, None], F32)


N_RPB_R = 2 * NA_ROWS - 1
N_RPB_C = 2 * NA_COLS - 1


def _na_table_kernel(rpb_ref, o_ref, tz_scr, *, rows):
    base = (pl.program_id(0) * N_HEADS + pl.program_id(1)) * (N_RPB_R * N_RPB_C)
    qc = lax.broadcasted_iota(jnp.int32, (GRID_W, 2 * GRID_W), 0)
    kk = lax.broadcasted_iota(jnp.int32, (GRID_W, 2 * GRID_W), 1)
    kc = kk & (GRID_W - 1)
    diff = kc - qc
    qws = jnp.clip(qc - NA_COLS // 2, 0, GRID_W - NA_COLS)
    col_ok = (kc >= qws) & (kc < qws + NA_COLS)
    neg = jnp.full((GRID_W, 2 * GRID_W), NEG, F32)
    for dr in range(N_RPB_R):
        acc = neg
        for m in range(N_RPB_C):
            acc = jnp.where(diff == m - (NA_COLS - 1), rpb_ref[base + dr * N_RPB_C + m], acc)
        tz_scr[dr] = jnp.where(col_ok, acc, NEG)
    wr = min(NA_ROWS, rows)
    nblk = rows // NA_QROWS
    for ti, g in enumerate((0, 1, nblk - 1)):
        ws_row = min(max(g * NA_QROWS - NA_QROWS, 0), rows - NA_KROWS)
        for qr in range(NA_QROWS):
            r = g * NA_QROWS + qr
            rs = min(max(r - wr // 2, 0), rows - wr)
            for p in range(NA_KROWS // 2):
                halves = []
                for kr in (ws_row + 2 * p, ws_row + 2 * p + 1):
                    halves.append(tz_scr[kr - r + NA_ROWS - 1] if rs <= kr < rs + wr else neg)
                o_ref[0, ti, 0, qr * GRID_W:(qr + 1) * GRID_W, p * 2 * GRID_W:(p + 1) * 2 * GRID_W] = (
                    jnp.where(kk < GRID_W, halves[0], halves[1]))


def _na_tables(na_rpb, slen):
    depth = na_rpb.shape[0]
    return pl.pallas_call(
        functools.partial(_na_table_kernel, rows=slen // GRID_W),
        grid=(depth, N_HEADS),
        in_specs=[pl.BlockSpec(memory_space=pltpu.SMEM)],
        out_specs=pl.BlockSpec((1, 3, 1, NA_QB, NA_KW), lambda l, h: (l, 0, h, 0, 0)),
        out_shape=jax.ShapeDtypeStruct((depth, 3, N_HEADS, NA_QB, NA_KW), F32),
        scratch_shapes=[pltpu.VMEM((N_RPB_R, GRID_W, 2 * GRID_W), F32)],
        compiler_params=_cparams(("arbitrary", "arbitrary")),
        name="na_table",
    )(na_rpb.astype(F32).reshape(-1))


def _ret_kernel(q_ref, k_ref, v_ref, g_ref, lgl_ref, lgr_ref, s0_ref, o_ref, sfin_ref, of_scr, st_scr,
                *, slen, t):
    nc = slen // t
    masks = _head_masks(t)
    bmask = _block_ones()
    ones_bf16 = jnp.where(bmask, 1.0, 0.0).astype(BF16)
    ii = lax.broadcasted_iota(jnp.int32, (N_HEADS * t, t), 0) & (t - 1)
    jj = lax.broadcasted_iota(jnp.int32, (N_HEADS * t, t), 1)
    dist = (ii - jj).astype(F32)
    idx = lax.broadcasted_iota(jnp.int32, (t, GROUP_W), 0).astype(F32)

    def run(direction, emit):
        fwd = direction == 0
        lgl = lgl_ref[direction]
        lgr = lgr_ref[direction]
        if fwd:
            dmat = jnp.where(dist >= 0, jnp.exp(dist * lgr), 0.0)
            qdec = jnp.exp((idx + 1.0) * lgl)
            kdec = jnp.exp((t - 1.0 - idx) * lgl)
        else:
            dmat = jnp.where(dist <= 0, jnp.exp(-dist * lgr), 0.0)
            qdec = jnp.exp((t - idx) * lgl)
            kdec = jnp.exp(idx * lgl)
        cdec = jnp.exp(float(t) * lgl)
        st_scr[...] = s0_ref[0, direction]

        def body(ci, carry):
            c = ci if fwd else nc - 1 - ci
            rows = pl.ds(pl.multiple_of(c * t, t), t)
            q = q_ref[0, rows, :]
            k = k_ref[0, rows, :] * (HEAD_DIM ** -0.5)
            v = v_ref[0, rows, :].astype(BF16)
            qs = _stack_heads(q, masks).astype(BF16)
            sc = lax.dot_general(qs, k.astype(BF16), NT_DIMS, preferred_element_type=F32) * dmat
            oall = jnp.dot(sc.astype(BF16), v, preferred_element_type=F32)
            o = _unstack_heads(oall, masks, t)
            st = st_scr[...]
            o = o + lax.dot_general((q * qdec).astype(BF16), st.astype(BF16), NT_DIMS,
                                    preferred_element_type=F32)
            upd = lax.dot_general(v, (k * kdec).astype(BF16), TN_DIMS, preferred_element_type=F32)
            st_scr[...] = st * cdec + jnp.where(bmask, upd, 0.0)
            emit(rows, o)
            return carry

        lax.fori_loop(0, nc, body, 0)
        sfin_ref[0, direction] = st_scr[...]

    def emit_fwd(rows, o):
        of_scr[rows, :] = o

    def emit_bwd(rows, o):
        o = o + of_scr[rows, :]
        mu = _head_sum(o, ones_bf16) * (1.0 / HEAD_DIM)
        d = o - mu
        var = _head_sum(d * d, ones_bf16) * (1.0 / HEAD_DIM)
        y = d * lax.rsqrt(var + EPS)
        o_ref[0, rows, :] = (y * _silu(g_ref[0, rows, :])).astype(o_ref.dtype)

    run(0, emit_fwd)
    run(1, emit_bwd)


def _blockdiag_t(s0):
    bsz = s0.shape[0]
    eye = jnp.eye(N_HEADS, dtype=s0.dtype)
    out = jnp.einsum('bxhde,hg->bxhegd', s0, eye)
    return out.reshape(bsz, 2, GROUP_W, GROUP_W)


def _unblockdiag_t(st):
    bsz = st.shape[0]
    s5 = st.reshape(bsz, 2, N_HEADS, HEAD_DIM, N_HEADS, HEAD_DIM)
    blocks = jnp.stack([s5[:, :, h, :, h, :] for h in range(N_HEADS)], axis=2)
    return jnp.swapaxes(blocks, -1, -2)


def _retention(pc, log_gamma, s0_bd, t=64):
    bsz, slen, _ = pc.shape
    lgl = jnp.repeat(log_gamma, HEAD_DIM, axis=1).reshape(2, 1, GROUP_W)
    lgr = jnp.repeat(log_gamma, t, axis=1).reshape(2, N_HEADS * t, 1)
    col = lambda j: pl.BlockSpec((1, slen, GROUP_W), lambda b: (b, 0, j))
    st_spec = pl.BlockSpec((1, 2, GROUP_W, GROUP_W), lambda b: (b, 0, 0, 0))
    return pl.pallas_call(
        functools.partial(_ret_kernel, slen=slen, t=t),
        grid=(bsz,),
        in_specs=[col(0), col(1), col(2), col(3),
                  pl.BlockSpec((2, 1, GROUP_W), lambda b: (0, 0, 0)),
                  pl.BlockSpec((2, N_HEADS * t, 1), lambda b: (0, 0, 0)),
                  st_spec],
        out_specs=[pl.BlockSpec((1, slen, GROUP_W), lambda b: (b, 0, 0)), st_spec],
        out_shape=[jax.ShapeDtypeStruct((bsz, slen, GROUP_W), BF16),
                   jax.ShapeDtypeStruct((bsz, 2, GROUP_W, GROUP_W), F32)],
        scratch_shapes=[pltpu.VMEM((slen, GROUP_W), F32), pltpu.VMEM((GROUP_W, GROUP_W), F32)],
        compiler_params=_cparams(("arbitrary",)),
        name="retention",
    )(pc, pc, pc, pc, lgl, lgr, s0_bd)


HGRN_DIRECT = 8


def _split3(x):
    h1 = x.astype(BF16)
    r1 = x - h1.astype(F32)
    h2 = r1.astype(BF16)
    h3 = (r1 - h2.astype(F32)).astype(BF16)
    return h1, h2, h3


def _hgrn_kernel(q_ref, zf_ref, zb_ref, v_ref, g_ref, lb_ref, s0_ref, o_ref, sfin_ref, of_scr, st_scr,
                 *, slen, t):
    nc = slen // t
    masks = _head_masks(t)
    bmask = _block_ones()
    ones_bf16 = jnp.where(bmask, 1.0, 0.0).astype(BF16)
    r_i = lax.broadcasted_iota(jnp.int32, (t, t), 0)
    c_i = lax.broadcasted_iota(jnp.int32, (t, t), 1)
    sub = lax.broadcasted_iota(jnp.int32, (t, GROUP_W), 0) & (HGRN_DIRECT - 1)
    levels = []
    s = t // 2
    while s >= HGRN_DIRECT:
        levels.append(s)
        s //= 2
    lvl_masks = []
    for s in levels:
        shift = (2 * s).bit_length() - 1
        m = jnp.where((r_i >> shift) == (c_i >> shift), 1.0, 0.0)
        lvl_masks.append(jnp.concatenate([m] * N_HEADS, axis=0))

    def run(direction, z_ref, emit):
        fwd = direction == 0
        lbd = jnp.maximum(lb_ref[direction], LB_FLOOR)
        tri = jnp.where((r_i >= c_i) if fwd else (r_i <= c_i), 1.0, 0.0).astype(BF16)
        st_scr[...] = s0_ref[0, direction]

        def body(ci, carry):
            c = ci if fwd else nc - 1 - ci
            rows = pl.ds(pl.multiple_of(c * t, t), t)
            q = _silu(q_ref[0, rows, :])
            z = z_ref[0, rows, :]
            v = v_ref[0, rows, :]
            vb = v.astype(BF16)
            a = jnp.exp(-jnp.abs(z))
            inv = 1.0 / (1.0 + a)
            pos = z >= 0
            sig = jnp.where(pos, inv, a * inv)
            nsig = jnp.where(pos, a * inv, inv)
            lf = jnp.log(lbd + (1.0 - lbd) * sig)
            k = (1.0 - lbd) * nsig
            h1, h2, h3 = _split3(lf)
            b = (jnp.dot(tri, h1, preferred_element_type=F32) + jnp.dot(tri, h2, preferred_element_type=F32)
                 + jnp.dot(tri, h3, preferred_element_type=F32))

            sc = None
            for s, lm in zip(levels, lvl_masks):
                pa, pb = [], []
                zero = jnp.zeros((s, GROUP_W), F32)
                for gi in range(t // (2 * s)):
                    lo = b[gi * 2 * s:gi * 2 * s + s]
                    hi = b[gi * 2 * s + s:(gi + 1) * 2 * s]
                    if fwd:
                        anc = hi[0:1]
                        pa += [zero, jnp.exp(hi - anc)]
                        pb += [jnp.exp(anc - lo), zero]
                    else:
                        anc = lo[s - 1:s]
                        pa += [jnp.exp(lo - anc), zero]
                        pb += [zero, jnp.exp(anc - hi)]
                qa = _stack_heads(q * jnp.concatenate(pa, axis=0), masks).astype(BF16)
                kb = (k * jnp.concatenate(pb, axis=0)).astype(BF16)
                term = lax.dot_general(qa, kb, NT_DIMS, preferred_element_type=F32) * lm
                sc = term if sc is None else sc + term
            o = None
            if sc is not None:
                o = _unstack_heads(jnp.dot(sc.astype(BF16), vb, preferred_element_type=F32), masks, t)

            for dlt in range(HGRN_DIRECT):
                if dlt == 0:
                    term = _head_sum(q * k, ones_bf16) * v
                else:
                    sh = dlt if fwd else t - dlt
                    ok = (sub >= dlt) if fwd else (sub + dlt <= HGRN_DIRECT - 1)
                    w = jnp.exp(jnp.minimum(b - pltpu.roll(b, sh, 0), 0.0))
                    tt = jnp.where(ok, q * pltpu.roll(k, sh, 0) * w, 0.0)
                    term = _head_sum(tt, ones_bf16) * pltpu.roll(v, sh, 0)
                o = term if o is None else o + term

            st = st_scr[...]
            o = o + lax.dot_general((q * jnp.exp(b)).astype(BF16), st.astype(BF16), NT_DIMS,
                                    preferred_element_type=F32)
            bl = b[t - 1:t] if fwd else b[0:1]
            upd = lax.dot_general(vb, (k * jnp.exp(bl - b)).astype(BF16), TN_DIMS,
                                  preferred_element_type=F32)
            st_scr[...] = st * jnp.exp(bl) + jnp.where(bmask, upd, 0.0)
            emit(rows, o)
            return carry

        lax.fori_loop(0, nc, body, 0)
        sfin_ref[0, direction] = st_scr[...]

    def emit_fwd(rows, o):
        of_scr[rows, :] = o

    def emit_bwd(rows, o):
        o = o + of_scr[rows, :]
        ms = _head_sum(o * o, ones_bf16) * (1.0 / HEAD_DIM)
        y = o * lax.rsqrt(ms + EPS)
        o_ref[0, rows, :] = (y * _silu(g_ref[0, rows, :])).astype(o_ref.dtype)

    run(0, zf_ref, emit_fwd)
    run(1, zb_ref, emit_bwd)


def _hgrn(pd, lb, s0_bd, t=64):
    bsz, slen, _ = pd.shape
    col = lambda j: pl.BlockSpec((1, slen, GROUP_W), lambda b: (b, 0, j))
    st_spec = pl.BlockSpec((1, 2, GROUP_W, GROUP_W), lambda b: (b, 0, 0, 0))
    return pl.pallas_call(
        functools.partial(_hgrn_kernel, slen=slen, t=t),
        grid=(bsz,),
        in_specs=[col(0), col(1), col(2), col(3), col(4),
                  pl.BlockSpec((2, 1, GROUP_W), lambda b: (0, 0, 0)),
                  st_spec],
        out_specs=[pl.BlockSpec((1, slen, GROUP_W), lambda b: (b, 0, 0)), st_spec],
        out_shape=[jax.ShapeDtypeStruct((bsz, slen, GROUP_W), BF16),
                   jax.ShapeDtypeStruct((bsz, 2, GROUP_W, GROUP_W), F32)],
        scratch_shapes=[pltpu.VMEM((slen, GROUP_W), F32), pltpu.VMEM((GROUP_W, GROUP_W), F32)],
        compiler_params=_cparams(("arbitrary",)),
        name="hgrn2",
    )(pd, pd, pd, pd, pd, lb.reshape(2, 1, GROUP_W), s0_bd)


def _outproj_kernel(ma_ref, mb_ref, mc_ref, md_ref, w_ref, x_ref, gate_ref, g_ref, o_ref):
    g = GROUP_W
    y = jnp.dot(ma_ref[0], w_ref[0:g, :], preferred_element_type=F32)
    y = y + jnp.dot(mb_ref[0], w_ref[g:2 * g, :], preferred_element_type=F32)
    y = y + jnp.dot(mc_ref[0], w_ref[2 * g:3 * g, :], preferred_element_type=F32)
    y = y + jnp.dot(md_ref[0], w_ref[3 * g:4 * g, :], preferred_element_type=F32)
    ms = jnp.mean(y * y, axis=-1, keepdims=True)
    r = y * lax.rsqrt(ms + EPS) * g_ref[...]
    o_ref[0] = x_ref[0] + gate_ref[0] * r


def _out_proj(mixed, w_out_bf16, x, gate_mod, g_post, tm=256):
    bsz, slen, d = x.shape
    shared = gate_mod.shape[0] == 1
    mod_map = (lambda b, i: (0, 0, 0)) if shared else (lambda b, i: (b, 0, 0))
    mspec = pl.BlockSpec((1, tm, GROUP_W), lambda b, i: (b, i, 0))
    return pl.pallas_call(
        _outproj_kernel,
        grid=(bsz, slen // tm),
        in_specs=[mspec, mspec, mspec, mspec,
                  pl.BlockSpec((d, d), lambda b, i: (0, 0)),
                  pl.BlockSpec((1, tm, d), lambda b, i: (b, i, 0)),
                  pl.BlockSpec((1, 1, d), mod_map),
                  pl.BlockSpec((1, d), lambda b, i: (0, 0))],
        out_specs=pl.BlockSpec((1, tm, d), lambda b, i: (b, i, 0)),
        out_shape=jax.ShapeDtypeStruct((bsz, slen, d), F32),
        compiler_params=_cparams(("arbitrary", "arbitrary")),
        name="out_proj",
    )(*mixed, w_out_bf16, x, gate_mod, g_post.reshape(1, d))


def _context_layer(x, mod, g_pre, g_post, w_in, w_out, sink, log_gamma, lb):
    shift, scale, gate = mod
    bsz, slen, _ = x.shape
    pa, pb, pc, pd = _in_proj(x, shift, scale, g_pre, w_in, rope=False)
    o_a = _attention(pa, 0, 2, 3, 2, n_kv=A_KV, qb=slen, kw=slen, back=0, sink=sink)
    o_b = _attention(pb, 0, 1, 2, 3, n_kv=N_HEADS, qb=slen, kw=slen, back=0)
    zeros = jnp.zeros((bsz, 2, GROUP_W, GROUP_W), F32)
    o_c, s_c = _retention(pc, log_gamma, zeros)
    o_d, s_d = _hgrn(pd, lb, zeros)
    x = _out_proj((o_a, o_b, o_c, o_d), w_out, x, gate, g_post)
    g = GROUP_W
    ak = pa[:, :, g:g + g // 2].reshape(bsz, slen, A_KV, HEAD_DIM)
    av = pa[:, :, g + g // 2:2 * g].reshape(bsz, slen, A_KV, HEAD_DIM)
    bk = pb[:, :, g:2 * g].reshape(bsz, slen, N_HEADS, HEAD_DIM)
    bv = pb[:, :, 2 * g:3 * g].reshape(bsz, slen, N_HEADS, HEAD_DIM)
    return x, (ak, av, bk, bv, _unblockdiag_t(s_c), _unblockdiag_t(s_d))


def _latent_layer(x, mod, ca_k, ca_v, cb_k, cb_v, st_c, st_d, g_pre, g_post, w_in, w_out, sink, win_tab,
                  na_tab, log_gamma, lb):
    shift, scale, gate = mod
    bsz, slen, _ = x.shape
    past = ca_k.shape[1]
    pa, pb, pc, pd = _in_proj(x, shift, scale, g_pre, w_in, rope=True)
    o_a = _attention(pa, 0, 2, 3, 2, n_kv=A_KV, qb=WIN_QB, kw=WIN_KW, back=WIN_QB,
                     ctx_k=ca_k.reshape(bsz, past, A_KV * HEAD_DIM), ctx_v=ca_v.reshape(bsz, past, A_KV * HEAD_DIM),
                     table=win_tab, sink=sink)
    o_b = _attention(pb, 0, 1, 2, 3, n_kv=N_HEADS, qb=NA_QB, kw=NA_KW, back=NA_QB,
                     ctx_k=cb_k.reshape(bsz, past, GROUP_W), ctx_v=cb_v.reshape(bsz, past, GROUP_W),
                     table=na_tab)
    o_c, _ = _retention(pc, log_gamma, _blockdiag_t(st_c))
    o_d, _ = _hgrn(pd, lb, _blockdiag_t(st_d))
    return _out_proj((o_a, o_b, o_c, o_d), w_out, x, gate, g_post)


def kernel(x_prompt, x_sample, c, cache_win_k, cache_win_v, cache_na_k, cache_na_v, state_ret, state_hgrn,
           c_ctx, w_ada, b_ada, g_pre, g_post, w_in, w_out, attn_sink, na_rpb, ret_decay_logit, hgrn_lb_logit):
    depth = w_ada.shape[0]
    dec_b, dec_s, d = x_sample.shape
    p_lb = jax.nn.softmax(hgrn_lb_logit.astype(F32), axis=0)
    lower_bounds = jnp.cumsum(p_lb, axis=0) - p_lb[0:1]
    log_gammas = jax.nn.log_sigmoid(ret_decay_logit.astype(F32))
    w_in_b = w_in.astype(BF16)
    w_out_b = w_out.astype(BF16)

    cvecs = jnp.zeros((ADA_ROWS, d), F32).at[:dec_b].set(c).at[dec_b].set(c_ctx)
    mods = _adaln(cvecs, w_ada, b_ada)
    win_tab = _window_table(dec_s)
    na_tabs = _na_tables(na_rpb, dec_s)

    def mod_of(l, lo, hi):
        m = mods[l, lo:hi].reshape(hi - lo, 1, 3, d)
        return m[:, :, 0], m[:, :, 1], m[:, :, 2]

    x = x_prompt
    outs = [[] for _ in range(6)]
    for l in range(depth):
        x, extra = _context_layer(x, mod_of(l, dec_b, dec_b + 1), g_pre[l], g_post[l], w_in_b[l], w_out_b[l],
                                  attn_sink[l], log_gammas[l], lower_bounds[l])
        for acc, e in zip(outs, extra):
            acc.append(e)
    y_prompt = x
    stacked = [jnp.stack(o, axis=1) for o in outs]

    x = x_sample
    for l in range(depth):
        x = _latent_layer(x, mod_of(l, 0, dec_b), cache_win_k[:, l], cache_win_v[:, l], cache_na_k[:, l],
                          cache_na_v[:, l], state_ret[:, l], state_hgrn[:, l], g_pre[l], g_post[l],
                          w_in_b[l], w_out_b[l], attn_sink[l], win_tab, na_tabs[l],
                          log_gammas[l], lower_bounds[l])
    return (y_prompt, x, *stacked)
```

```python
import functools

import numpy as np
import jax
import jax.numpy as jnp
from jax import lax
from jax.experimental import pallas as pl
from jax.experimental.pallas import tpu as pltpu

F32 = jnp.float32
BF16 = jnp.bfloat16

D_MODEL = 1024
DEPTH = 4
GRID_W = 64
HEAD_DIM = 64
N_HEADS = 4
GROUP_W = N_HEADS * HEAD_DIM
A_KV = 2
WINDOW = 128
NA_ROWS = 8
NA_COLS = 16
ROPE_BASE = 10000.0
EPS = 1e-6
NEG = -1e30
LB_FLOOR = 1e-30
LOG2E = 1.4426950408889634
W_A = 3 * GROUP_W
W_B = 4 * GROUP_W
W_C = 4 * GROUP_W
W_D = 5 * GROUP_W
IN_WIDTH = W_A + W_B + W_C + W_D

V7X_VMEM_LIMIT_BYTES = 56 * 1024 * 1024
ADA_ROWS = 16

NT_DIMS = (((1,), (1,)), ((), ()))
TN_DIMS = (((0,), (0,)), ((), ()))


def _cparams(sem):
    return pltpu.CompilerParams(dimension_semantics=sem, vmem_limit_bytes=V7X_VMEM_LIMIT_BYTES)


def _sigmoid(x):
    return 1.0 / (1.0 + jnp.exp(-x))


def _silu(x):
    return x * _sigmoid(x)


def _head_masks(rows):
    lane = lax.broadcasted_iota(jnp.int32, (rows, GROUP_W), 1)
    return [(lane >= h * HEAD_DIM) & (lane < (h + 1) * HEAD_DIM) for h in range(N_HEADS)]


def _stack_heads(x, masks):
    return jnp.concatenate([jnp.where(m, x, 0.0) for m in masks], axis=0)


def _unstack_heads(x, masks, t):
    out = jnp.where(masks[0], x[0:t], 0.0)
    for h in range(1, N_HEADS):
        out = out + jnp.where(masks[h], x[h * t:(h + 1) * t], 0.0)
    return out


def _block_ones():
    shift = HEAD_DIM.bit_length() - 1
    r = lax.broadcasted_iota(jnp.int32, (GROUP_W, GROUP_W), 0) >> shift
    c = lax.broadcasted_iota(jnp.int32, (GROUP_W, GROUP_W), 1) >> shift
    return r == c


def _head_sum(x, ones_bf16):
    hi = x.astype(BF16)
    lo = (x - hi.astype(F32)).astype(BF16)
    return (jnp.dot(hi, ones_bf16, preferred_element_type=F32)
            + jnp.dot(lo, ones_bf16, preferred_element_type=F32))


def _adaln_kernel(c_ref, w_ref, b_ref, o_ref):
    s = _silu(c_ref[...]).astype(BF16)
    o_ref[0] = jnp.dot(s, w_ref[0].astype(BF16), preferred_element_type=F32) + b_ref[0]


def _adaln(cvecs, w_ada, b_ada):
    depth, d, d3 = w_ada.shape
    tn = 512
    return pl.pallas_call(
        _adaln_kernel,
        grid=(depth, d3 // tn),
        in_specs=[pl.BlockSpec((ADA_ROWS, d), lambda l, j: (0, 0)),
                  pl.BlockSpec((1, d, tn), lambda l, j: (l, 0, j)),
                  pl.BlockSpec((1, 1, tn), lambda l, j: (l, 0, j))],
        out_specs=pl.BlockSpec((1, ADA_ROWS, tn), lambda l, j: (l, 0, j)),
        out_shape=jax.ShapeDtypeStruct((depth, ADA_ROWS, d3), F32),
        compiler_params=_cparams(("arbitrary", "arbitrary")),
        name="adaln",
    )(cvecs, w_ada, b_ada.reshape(depth, 1, d3))


def _rope(x, cos, sin):
    w = x.shape[-1]
    lane = lax.broadcasted_iota(jnp.int32, x.shape, 1)
    first = (lane & 31) < 16
    swapped = jnp.where(first, pltpu.roll(x, w - 16, 1), pltpu.roll(x, 16, 1))
    return x * cos[:, :w] + swapped * sin[:, :w]


def _inproj_kernel(*refs, rope):
    if rope:
        x_ref, shift_ref, scale_ref, g_ref, w_ref, cos_ref, sin_ref, pa_ref, pb_ref, pc_ref, pd_ref = refs
    else:
        x_ref, shift_ref, scale_ref, g_ref, w_ref, pa_ref, pb_ref, pc_ref, pd_ref = refs
    x = x_ref[0]
    ms = jnp.mean(x * x, axis=-1, keepdims=True)
    y = x * lax.rsqrt(ms + EPS) * g_ref[...]
    h = (y * (1.0 + scale_ref[0]) + shift_ref[0]).astype(BF16)

    def mm(c0, c1):
        return jnp.dot(h, w_ref[:, c0:c1], preferred_element_type=F32)

    if rope:
        cos = cos_ref[...]
        sin = sin_ref[...]
    g = GROUP_W
    aq = mm(0, g)
    akv = mm(g, 2 * g)
    if rope:
        aq = _rope(aq, cos, sin)
        ak = _rope(akv[:, :g // 2], cos, sin)
        akv = jnp.concatenate([ak, akv[:, g // 2:]], axis=1)
    pa_ref[0, :, 0:g] = aq.astype(pa_ref.dtype)
    pa_ref[0, :, g:2 * g] = akv.astype(pa_ref.dtype)
    pa_ref[0, :, 2 * g:3 * g] = mm(2 * g, 3 * g).astype(pa_ref.dtype)
    pb_ref[0] = mm(W_A, W_A + W_B).astype(pb_ref.dtype)
    c0 = W_A + W_B
    cq = mm(c0, c0 + g)
    ck = mm(c0 + g, c0 + 2 * g)
    if rope:
        cq = _rope(cq, cos, sin)
        ck = _rope(ck, cos, sin)
    pc_ref[0, :, 0:g] = cq.astype(pc_ref.dtype)
    pc_ref[0, :, g:2 * g] = ck.astype(pc_ref.dtype)
    pc_ref[0, :, 2 * g:4 * g] = mm(c0 + 2 * g, c0 + 4 * g).astype(pc_ref.dtype)
    d0 = c0 + W_C
    pd_ref[0] = mm(d0, d0 + W_D).astype(pd_ref.dtype)


def _rope_tables(slen):
    t = np.arange(slen)
    nf = HEAD_DIM // 4
    freqs = ROPE_BASE ** (-np.arange(nf, dtype=np.float64) / nf)
    d = np.arange(HEAD_DIM)
    pos = np.where(d[None, :] < HEAD_DIM // 2, (t // GRID_W)[:, None], (t % GRID_W)[:, None])
    ang = pos * freqs[d % nf][None, :]
    sign = np.where((d % (2 * nf)) < nf, -1.0, 1.0)[None, :]
    cos = np.tile(np.cos(ang), (1, N_HEADS))
    sin = np.tile(np.sin(ang) * sign, (1, N_HEADS))
    return jnp.asarray(cos, F32), jnp.asarray(sin, F32)


def _in_proj(x, shift, scale, g_pre, w_in_bf16, rope, tm=256):
    bsz, slen, d = x.shape
    shared = shift.shape[0] == 1
    mod_map = (lambda b, i: (0, 0, 0)) if shared else (lambda b, i: (b, 0, 0))
    in_specs = [pl.BlockSpec((1, tm, d), lambda b, i: (b, i, 0)),
                pl.BlockSpec((1, 1, d), mod_map),
                pl.BlockSpec((1, 1, d), mod_map),
                pl.BlockSpec((1, d), lambda b, i: (0, 0)),
                pl.BlockSpec((d, IN_WIDTH), lambda b, i: (0, 0))]
    args = [x, shift, scale, g_pre.reshape(1, d), w_in_bf16]
    if rope:
        cos, sin = _rope_tables(slen)
        in_specs += [pl.BlockSpec((tm, GROUP_W), lambda b, i: (i, 0))] * 2
        args += [cos, sin]
    widths = (W_A, W_B, W_C, W_D)
    return pl.pallas_call(
        functools.partial(_inproj_kernel, rope=rope),
        grid=(bsz, slen // tm),
        in_specs=in_specs,
        out_specs=[pl.BlockSpec((1, tm, w), lambda b, i: (b, i, 0)) for w in widths],
        out_shape=[jax.ShapeDtypeStruct((bsz, slen, w), F32) for w in widths],
        compiler_params=_cparams(("arbitrary", "arbitrary")),
        name="in_proj",
    )(*args)


def _attn_kernel(*refs, n_kv, qb, kw, back, slen, has_ctx, n_tab, tab_heads, has_sink):
    it = iter(refs)
    q_ref, k_ref, v_ref, g_ref = next(it), next(it), next(it), next(it)
    kc_ref = vc_ref = tab_ref = sink_ref = None
    if has_ctx:
        kc_ref, vc_ref = next(it), next(it)
    if n_tab:
        tab_ref = next(it)
    if has_sink:
        sink_ref = next(it)
    o_ref = next(it)

    n = pl.program_id(1)
    nblk = slen // qb
    ws = pl.multiple_of(jnp.clip(n * qb - back, 0, slen - kw), HEAD_DIM)
    if n_tab == 3:
        tix = jnp.where(n == 0, 0, jnp.where(n == nblk - 1, 2, 1))
    else:
        tix = 0
    grp = N_HEADS // n_kv
    scale = HEAD_DIM ** -0.5
    for kh in range(n_kv):
        heads = range(kh * grp, (kh + 1) * grp)
        hsl = [slice(h * HEAD_DIM, (h + 1) * HEAD_DIM) for h in heads]
        ks = slice(kh * HEAD_DIM, (kh + 1) * HEAD_DIM)
        q = jnp.concatenate([q_ref[0, :, sl] for sl in hsl], axis=0).astype(BF16)
        kl = k_ref[0, pl.ds(ws, kw), ks].astype(BF16)
        vl = v_ref[0, pl.ds(ws, kw), ks].astype(BF16)
        s = lax.dot_general(q, kl, NT_DIMS, preferred_element_type=F32) * scale
        if n_tab:
            s = s + jnp.concatenate([tab_ref[tix, h if tab_heads == N_HEADS else 0] for h in heads], axis=0)
        m = jnp.max(s, axis=-1, keepdims=True)
        if has_ctx:
            sc = lax.dot_general(q, kc_ref[0, :, ks].astype(BF16), NT_DIMS, preferred_element_type=F32) * scale
            m = jnp.maximum(m, jnp.max(sc, axis=-1, keepdims=True))
        if has_sink:
            sink = jnp.concatenate([jnp.full((qb, 1), sink_ref[h], F32) for h in heads], axis=0)
            m = jnp.maximum(m, sink)
        p = jnp.exp(s - m)
        den = jnp.sum(p, axis=-1, keepdims=True)
        acc = jnp.dot(p.astype(BF16), vl, preferred_element_type=F32)
        if has_ctx:
            pc = jnp.exp(sc - m)
            den = den + jnp.sum(pc, axis=-1, keepdims=True)
            acc = acc + jnp.dot(pc.astype(BF16), vc_ref[0, :, ks].astype(BF16), preferred_element_type=F32)
        if has_sink:
            den = den + jnp.exp(sink - m)
        o = acc / den
        for gi, sl in enumerate(hsl):
            o_ref[0, :, sl] = (o[gi * qb:(gi + 1) * qb] * _silu(g_ref[0, :, sl])).astype(o_ref.dtype)


def _attention(p, q_col, k_col, v_col, g_col, n_kv, qb, kw, back, ctx_k=None, ctx_v=None,
               table=None, sink=None):
    bsz, slen, _ = p.shape
    kvw = n_kv * HEAD_DIM
    in_specs = [pl.BlockSpec((1, qb, GROUP_W), lambda b, n: (b, n, q_col)),
                pl.BlockSpec((1, slen, kvw), lambda b, n: (b, 0, k_col)),
                pl.BlockSpec((1, slen, kvw), lambda b, n: (b, 0, v_col)),
                pl.BlockSpec((1, qb, GROUP_W), lambda b, n: (b, n, g_col))]
    args = [p, p, p, p]
    has_ctx = ctx_k is not None
    if has_ctx:
        past = ctx_k.shape[1]
        in_specs += [pl.BlockSpec((1, past, kvw), lambda b, n: (b, 0, 0))] * 2
        args += [ctx_k, ctx_v]
    n_tab = tab_heads = 0
    if table is not None:
        n_tab, tab_heads = table.shape[0], table.shape[1]
        in_specs.append(pl.BlockSpec(table.shape, lambda b, n: (0, 0, 0, 0)))
        args.append(table)
    if sink is not None:
        in_specs.append(pl.BlockSpec(memory_space=pltpu.SMEM))
        args.append(sink)
    kern = functools.partial(_attn_kernel, n_kv=n_kv, qb=qb, kw=kw, back=back, slen=slen, has_ctx=has_ctx,
                             n_tab=n_tab, tab_heads=tab_heads, has_sink=sink is not None)
    return pl.pallas_call(
        kern,
        grid=(bsz, slen // qb),
        in_specs=in_specs,
        out_specs=pl.BlockSpec((1, qb, GROUP_W), lambda b, n: (b, n, 0)),
        out_shape=jax.ShapeDtypeStruct((bsz, slen, GROUP_W), BF16),
        compiler_params=_cparams(("arbitrary", "arbitrary")),
        name="attn",
    )(*args)


WIN_QB = 256
WIN_BACK = WINDOW
WIN_KW = WIN_QB + 2 * WINDOW
NA_QROWS = 4
NA_KROWS = 12
NA_QB = NA_QROWS * GRID_W
NA_KW = NA_KROWS * GRID_W


def _window_table(slen):
    nblk = slen // WIN_QB
    tabs = []
    for n in (0, 1, nblk - 1):
        ws = int(np.clip(n * WIN_QB - WIN_BACK, 0, slen - WIN_KW))
        qpos = n * WIN_QB + np.arange(WIN_QB)[:, None]
        kpos = ws + np.arange(WIN_KW)[None, :]
        tabs.append(np.where(np.abs(qpos - kpos) <= WINDOW, 0.0, NEG))
    return jnp.asarray(np.stack(tabs)[:, None], F32)


N_RPB_R = 2 * NA_ROWS - 1
N_RPB_C = 2 * NA_COLS - 1


def _na_table_kernel(rpb_ref, o_ref, tz_scr, *, rows):
    base = (pl.program_id(0) * N_HEADS + pl.program_id(1)) * (N_RPB_R * N_RPB_C)
    qc = lax.broadcasted_iota(jnp.int32, (GRID_W, 2 * GRID_W), 0)
    kk = lax.broadcasted_iota(jnp.int32, (GRID_W, 2 * GRID_W), 1)
    kc = kk & (GRID_W - 1)
    diff = kc - qc
    qws = jnp.clip(qc - NA_COLS // 2, 0, GRID_W - NA_COLS)
    col_ok = (kc >= qws) & (kc < qws + NA_COLS)
    neg = jnp.full((GRID_W, 2 * GRID_W), NEG, F32)
    for dr in range(N_RPB_R):
        acc = neg
        for m in range(N_RPB_C):
            acc = jnp.where(diff == m - (NA_COLS - 1), rpb_ref[base + dr * N_RPB_C + m], acc)
        tz_scr[dr] = jnp.where(col_ok, acc, NEG)
    wr = min(NA_ROWS, rows)
    nblk = rows // NA_QROWS
    for ti, g in enumerate((0, 1, nblk - 1)):
        ws_row = min(max(g * NA_QROWS - NA_QROWS, 0), rows - NA_KROWS)
        for qr in range(NA_QROWS):
            r = g * NA_QROWS + qr
            rs = min(max(r - wr // 2, 0), rows - wr)
            for p in range(NA_KROWS // 2):
                halves = []
                for kr in (ws_row + 2 * p, ws_row + 2 * p + 1):
                    halves.append(tz_scr[kr - r + NA_ROWS - 1] if rs <= kr < rs + wr else neg)
                o_ref[0, ti, 0, qr * GRID_W:(qr + 1) * GRID_W, p * 2 * GRID_W:(p + 1) * 2 * GRID_W] = (
                    jnp.where(kk < GRID_W, halves[0], halves[1]))


def _na_tables(na_rpb, slen):
    depth = na_rpb.shape[0]
    return pl.pallas_call(
        functools.partial(_na_table_kernel, rows=slen // GRID_W),
        grid=(depth, N_HEADS),
        in_specs=[pl.BlockSpec(memory_space=pltpu.SMEM)],
        out_specs=pl.BlockSpec((1, 3, 1, NA_QB, NA_KW), lambda l, h: (l, 0, h, 0, 0)),
        out_shape=jax.ShapeDtypeStruct((depth, 3, N_HEADS, NA_QB, NA_KW), F32),
        scratch_shapes=[pltpu.VMEM((N_RPB_R, GRID_W, 2 * GRID_W), F32)],
        compiler_params=_cparams(("arbitrary", "arbitrary")),
        name="na_table",
    )(na_rpb.astype(F32).reshape(-1))


def _ret_kernel(q_ref, k_ref, v_ref, g_ref, lgl_ref, lgc_ref, s0_ref, o_ref, sfin_ref, sb_scr, st_scr,
                *, slen, t):
    nc = slen // t
    masks = _head_masks(t)
    bmask = _block_ones()
    ones_bf16 = jnp.where(bmask, 1.0, 0.0).astype(BF16)
    ii = lax.broadcasted_iota(jnp.int32, (t, N_HEADS * t), 0)
    jj = lax.broadcasted_iota(jnp.int32, (t, N_HEADS * t), 1) & (t - 1)
    dist = (ii - jj).astype(F32)
    idx = lax.broadcasted_iota(jnp.int32, (t, GROUP_W), 0).astype(F32)
    lgf, lgb = lgl_ref[0], lgl_ref[1]
    dmat = (jnp.where(dist >= 0, jnp.exp(dist * lgc_ref[0]), 0.0)
            + jnp.where(dist <= 0, jnp.exp(-dist * lgc_ref[1]), 0.0))
    qdec_f = jnp.exp((idx + 1.0) * lgf)
    kdec_f = jnp.exp((t - 1.0 - idx) * lgf)
    qdec_b = jnp.exp((t - idx) * lgb)
    kdec_b = jnp.exp(idx * lgb)
    cdec_f = jnp.exp(float(t) * lgf)
    cdec_b = jnp.exp(float(t) * lgb)
    kscale = HEAD_DIM ** -0.5

    def state_update(st, k, v, kdec, cdec):
        upd = lax.dot_general(v, (k * kdec).astype(BF16), TN_DIMS, preferred_element_type=F32)
        return st * cdec + jnp.where(bmask, upd, 0.0)

    st_scr[...] = s0_ref[0, 1]

    def sweep_bwd(ci, carry):
        c = nc - 1 - ci
        rows = pl.ds(pl.multiple_of(c * t, t), t)
        st = st_scr[...]
        sb_scr[c] = st.astype(BF16)
        k = k_ref[0, rows, :] * kscale
        st_scr[...] = state_update(st, k, v_ref[0, rows, :].astype(BF16), kdec_b, cdec_b)
        return carry

    lax.fori_loop(0, nc, sweep_bwd, 0)
    sfin_ref[0, 1] = st_scr[...]
    st_scr[...] = s0_ref[0, 0]

    def sweep_fwd(c, carry):
        rows = pl.ds(pl.multiple_of(c * t, t), t)
        q = q_ref[0, rows, :]
        k = k_ref[0, rows, :] * kscale
        v = v_ref[0, rows, :]
        ks = _stack_heads(k, masks).astype(BF16)
        vs = _stack_heads(v, masks).astype(BF16)
        sc = lax.dot_general(q.astype(BF16), ks, NT_DIMS, preferred_element_type=F32) * dmat
        o = jnp.dot(sc.astype(BF16), vs, preferred_element_type=F32)
        st = st_scr[...]
        o = o + lax.dot_general((q * qdec_f).astype(BF16), st.astype(BF16), NT_DIMS, preferred_element_type=F32)
        o = o + lax.dot_general((q * qdec_b).astype(BF16), sb_scr[c], NT_DIMS, preferred_element_type=F32)
        st_scr[...] = state_update(st, k, v.astype(BF16), kdec_f, cdec_f)
        mu = _head_sum(o, ones_bf16) * (1.0 / HEAD_DIM)
        d = o - mu
        var = _head_sum(d * d, ones_bf16) * (1.0 / HEAD_DIM)
        y = d * lax.rsqrt(var + EPS)
        o_ref[0, rows, :] = (y * _silu(g_ref[0, rows, :])).astype(o_ref.dtype)
        return carry

    lax.fori_loop(0, nc, sweep_fwd, 0)
    sfin_ref[0, 0] = st_scr[...]


def _blockdiag_t(s0):
    bsz = s0.shape[0]
    eye = jnp.eye(N_HEADS, dtype=s0.dtype)
    out = jnp.einsum('bxhde,hg->bxhegd', s0, eye)
    return out.reshape(bsz, 2, GROUP_W, GROUP_W)


def _unblockdiag_t(st):
    bsz = st.shape[0]
    s5 = st.reshape(bsz, 2, N_HEADS, HEAD_DIM, N_HEADS, HEAD_DIM)
    blocks = jnp.stack([s5[:, :, h, :, h, :] for h in range(N_HEADS)], axis=2)
    return jnp.swapaxes(blocks, -1, -2)


RET_CHUNK = 128


def _retention(pc, log_gamma, s0_bd, t=RET_CHUNK):
    bsz, slen, _ = pc.shape
    lgl = jnp.repeat(log_gamma, HEAD_DIM, axis=1).reshape(2, 1, GROUP_W)
    lgc = jnp.repeat(log_gamma, t, axis=1).reshape(2, 1, N_HEADS * t)
    col = lambda j: pl.BlockSpec((1, slen, GROUP_W), lambda b: (b, 0, j))
    st_spec = pl.BlockSpec((1, 2, GROUP_W, GROUP_W), lambda b: (b, 0, 0, 0))
    return pl.pallas_call(
        functools.partial(_ret_kernel, slen=slen, t=t),
        grid=(bsz,),
        in_specs=[col(0), col(1), col(2), col(3),
                  pl.BlockSpec((2, 1, GROUP_W), lambda b: (0, 0, 0)),
                  pl.BlockSpec((2, 1, N_HEADS * t), lambda b: (0, 0, 0)),
                  st_spec],
        out_specs=[pl.BlockSpec((1, slen, GROUP_W), lambda b: (b, 0, 0)), st_spec],
        out_shape=[jax.ShapeDtypeStruct((bsz, slen, GROUP_W), BF16),
                   jax.ShapeDtypeStruct((bsz, 2, GROUP_W, GROUP_W), F32)],
        scratch_shapes=[pltpu.VMEM((slen // t, GROUP_W, GROUP_W), BF16), pltpu.VMEM((GROUP_W, GROUP_W), F32)],
        compiler_params=_cparams(("arbitrary",)),
        name="retention",
    )(pc, pc, pc, pc, lgl, lgc, s0_bd)


HGRN_DIRECT = 8


def _split2(x):
    hi = x.astype(BF16)
    return hi, (x - hi.astype(F32)).astype(BF16)


def _hgrn_kernel(q_ref, zf_ref, zb_ref, v_ref, g_ref, lb_ref, s0_ref, o_ref, sfin_ref, of_scr, st_scr,
                 *, slen, t):
    nc = slen // t
    masks = _head_masks(t)
    bmask = _block_ones()
    ones_bf16 = jnp.where(bmask, 1.0, 0.0).astype(BF16)
    r_i = lax.broadcasted_iota(jnp.int32, (t, t), 0)
    c_i = lax.broadcasted_iota(jnp.int32, (t, t), 1)
    sub = lax.broadcasted_iota(jnp.int32, (t, GROUP_W), 0) & (HGRN_DIRECT - 1)
    levels = []
    s = t // 2
    while s >= HGRN_DIRECT:
        levels.append(s)
        s //= 2
    r_w = lax.broadcasted_iota(jnp.int32, (t, N_HEADS * t), 0)
    c_w = lax.broadcasted_iota(jnp.int32, (t, N_HEADS * t), 1) & (t - 1)
    lvl_masks = []
    for s in levels:
        shift = (2 * s).bit_length() - 1
        lvl_masks.append(jnp.where((r_w >> shift) == (c_w >> shift), 1.0, 0.0))

    def block_roll(x, shift):
        x3 = x.reshape(t // HGRN_DIRECT, HGRN_DIRECT, GROUP_W)
        return pltpu.roll(x3, shift, 1).reshape(t, GROUP_W)

    def run(direction, z_ref, emit):
        fwd = direction == 0
        lbd = jnp.maximum(lb_ref[direction], LB_FLOOR)
        tri = jnp.where((r_i >= c_i) if fwd else (r_i <= c_i), 1.0, 0.0).astype(BF16)
        st_scr[...] = s0_ref[0, direction]

        def body(ci, carry):
            c = ci if fwd else nc - 1 - ci
            rows = pl.ds(pl.multiple_of(c * t, t), t)
            q = _silu(q_ref[0, rows, :])
            z = z_ref[0, rows, :]
            v = v_ref[0, rows, :]
            vb = v.astype(BF16)
            a = jnp.exp(-jnp.abs(z))
            inv = 1.0 / (1.0 + a)
            pos = z >= 0
            sig = jnp.where(pos, inv, a * inv)
            nsig = jnp.where(pos, a * inv, inv)
            lf = jnp.log(lbd + (1.0 - lbd) * sig)
            k = (1.0 - lbd) * nsig
            h1, h2 = _split2(lf)
            b = jnp.dot(tri, h1, preferred_element_type=F32) + jnp.dot(tri, h2, preferred_element_type=F32)
            b = b * LOG2E
            vs = _stack_heads(v, masks).astype(BF16)

            sc = None
            for s, lm in zip(levels, lvl_masks):
                pa, pb = [], []
                zero = jnp.zeros((s, GROUP_W), F32)
                for gi in range(t // (2 * s)):
                    lo = b[gi * 2 * s:gi * 2 * s + s]
                    hi = b[gi * 2 * s + s:(gi + 1) * 2 * s]
                    if fwd:
                        anc = hi[0:1]
                        pa += [zero, jnp.exp2(hi - anc)]
                        pb += [jnp.exp2(anc - lo), zero]
                    else:
                        anc = lo[s - 1:s]
                        pa += [jnp.exp2(lo - anc), zero]
                        pb += [zero, jnp.exp2(anc - hi)]
                qa = (q * jnp.concatenate(pa, axis=0)).astype(BF16)
                kb = _stack_heads(k * jnp.concatenate(pb, axis=0), masks).astype(BF16)
                term = lax.dot_general(qa, kb, NT_DIMS, preferred_element_type=F32) * lm
                sc = term if sc is None else sc + term
            o = None
            if sc is not None:
                o = jnp.dot(sc.astype(BF16), vs, preferred_element_type=F32)

            prods, vals = [(q * k).astype(BF16)], [v]
            for dlt in range(1, HGRN_DIRECT):
                sh = dlt if fwd else HGRN_DIRECT - dlt
                ok = (sub >= dlt) if fwd else (sub + dlt <= HGRN_DIRECT - 1)
                w = jnp.exp2(jnp.minimum(b - block_roll(b, sh), 0.0))
                prods.append(jnp.where(ok, q * block_roll(k, sh) * w, 0.0).astype(BF16))
                vals.append(block_roll(v, sh))
            hs = jnp.dot(jnp.concatenate(prods, axis=0), ones_bf16, preferred_element_type=F32)
            for dlt in range(HGRN_DIRECT):
                term = hs[dlt * t:(dlt + 1) * t] * vals[dlt]
                o = term if o is None else o + term

            st = st_scr[...]
            o = o + lax.dot_general((q * jnp.exp2(b)).astype(BF16), st.astype(BF16), NT_DIMS,
                                    preferred_element_type=F32)
            bl = b[t - 1:t] if fwd else b[0:1]
            upd = lax.dot_general(vb, (k * jnp.exp2(bl - b)).astype(BF16), TN_DIMS,
                                  preferred_element_type=F32)
            st_scr[...] = st * jnp.exp2(bl) + jnp.where(bmask, upd, 0.0)
            emit(rows, o)
            return carry

        lax.fori_loop(0, nc, body, 0)
        sfin_ref[0, direction] = st_scr[...]

    def emit_fwd(rows, o):
        of_scr[rows, :] = o

    def emit_bwd(rows, o):
        o = o + of_scr[rows, :]
        ms = _head_sum(o * o, ones_bf16) * (1.0 / HEAD_DIM)
        y = o * lax.rsqrt(ms + EPS)
        o_ref[0, rows, :] = (y * _silu(g_ref[0, rows, :])).astype(o_ref.dtype)

    run(0, zf_ref, emit_fwd)
    run(1, zb_ref, emit_bwd)


HGRN_CHUNK = 128


def _hgrn(pd, lb, s0_bd, t=HGRN_CHUNK):
    bsz, slen, _ = pd.shape
    col = lambda j: pl.BlockSpec((1, slen, GROUP_W), lambda b: (b, 0, j))
    st_spec = pl.BlockSpec((1, 2, GROUP_W, GROUP_W), lambda b: (b, 0, 0, 0))
    return pl.pallas_call(
        functools.partial(_hgrn_kernel, slen=slen, t=t),
        grid=(bsz,),
        in_specs=[col(0), col(1), col(2), col(3), col(4),
                  pl.BlockSpec((2, 1, GROUP_W), lambda b: (0, 0, 0)),
                  st_spec],
        out_specs=[pl.BlockSpec((1, slen, GROUP_W), lambda b: (b, 0, 0)), st_spec],
        out_shape=[jax.ShapeDtypeStruct((bsz, slen, GROUP_W), BF16),
                   jax.ShapeDtypeStruct((bsz, 2, GROUP_W, GROUP_W), F32)],
        scratch_shapes=[pltpu.VMEM((slen, GROUP_W), F32), pltpu.VMEM((GROUP_W, GROUP_W), F32)],
        compiler_params=_cparams(("arbitrary",)),
        name="hgrn2",
    )(pd, pd, pd, pd, pd, lb.reshape(2, 1, GROUP_W), s0_bd)


def _outproj_kernel(ma_ref, mb_ref, mc_ref, md_ref, w_ref, x_ref, gate_ref, g_ref, o_ref):
    g = GROUP_W
    y = jnp.dot(ma_ref[0], w_ref[0:g, :], preferred_element_type=F32)
    y = y + jnp.dot(mb_ref[0], w_ref[g:2 * g, :], preferred_element_type=F32)
    y = y + jnp.dot(mc_ref[0], w_ref[2 * g:3 * g, :], preferred_element_type=F32)
    y = y + jnp.dot(md_ref[0], w_ref[3 * g:4 * g, :], preferred_element_type=F32)
    ms = jnp.mean(y * y, axis=-1, keepdims=True)
    r = y * lax.rsqrt(ms + EPS) * g_ref[...]
    o_ref[0] = x_ref[0] + gate_ref[0] * r


def _out_proj(mixed, w_out_bf16, x, gate_mod, g_post, tm=256):
    bsz, slen, d = x.shape
    shared = gate_mod.shape[0] == 1
    mod_map = (lambda b, i: (0, 0, 0)) if shared else (lambda b, i: (b, 0, 0))
    mspec = pl.BlockSpec((1, tm, GROUP_W), lambda b, i: (b, i, 0))
    return pl.pallas_call(
        _outproj_kernel,
        grid=(bsz, slen // tm),
        in_specs=[mspec, mspec, mspec, mspec,
                  pl.BlockSpec((d, d), lambda b, i: (0, 0)),
                  pl.BlockSpec((1, tm, d), lambda b, i: (b, i, 0)),
                  pl.BlockSpec((1, 1, d), mod_map),
                  pl.BlockSpec((1, d), lambda b, i: (0, 0))],
        out_specs=pl.BlockSpec((1, tm, d), lambda b, i: (b, i, 0)),
        out_shape=jax.ShapeDtypeStruct((bsz, slen, d), F32),
        compiler_params=_cparams(("arbitrary", "arbitrary")),
        name="out_proj",
    )(*mixed, w_out_bf16, x, gate_mod, g_post.reshape(1, d))


def _context_layer(x, mod, g_pre, g_post, w_in, w_out, sink, log_gamma, lb):
    shift, scale, gate = mod
    bsz, slen, _ = x.shape
    pa, pb, pc, pd = _in_proj(x, shift, scale, g_pre, w_in, rope=False)
    o_a = _attention(pa, 0, 2, 3, 2, n_kv=A_KV, qb=slen, kw=slen, back=0, sink=sink)
    o_b = _attention(pb, 0, 1, 2, 3, n_kv=N_HEADS, qb=slen, kw=slen, back=0)
    zeros = jnp.zeros((bsz, 2, GROUP_W, GROUP_W), F32)
    o_c, s_c = _retention(pc, log_gamma, zeros)
    o_d, s_d = _hgrn(pd, lb, zeros)
    x = _out_proj((o_a, o_b, o_c, o_d), w_out, x, gate, g_post)
    g = GROUP_W
    ak = pa[:, :, g:g + g // 2].reshape(bsz, slen, A_KV, HEAD_DIM)
    av = pa[:, :, g + g // 2:2 * g].reshape(bsz, slen, A_KV, HEAD_DIM)
    bk = pb[:, :, g:2 * g].reshape(bsz, slen, N_HEADS, HEAD_DIM)
    bv = pb[:, :, 2 * g:3 * g].reshape(bsz, slen, N_HEADS, HEAD_DIM)
    return x, (ak, av, bk, bv, _unblockdiag_t(s_c), _unblockdiag_t(s_d))


def _latent_layer(x, mod, ca_k, ca_v, cb_k, cb_v, st_c, st_d, g_pre, g_post, w_in, w_out, sink, win_tab,
                  na_tab, log_gamma, lb):
    shift, scale, gate = mod
    bsz, slen, _ = x.shape
    past = ca_k.shape[1]
    pa, pb, pc, pd = _in_proj(x, shift, scale, g_pre, w_in, rope=True)
    o_a = _attention(pa, 0, 2, 3, 2, n_kv=A_KV, qb=WIN_QB, kw=WIN_KW, back=WIN_BACK,
                     ctx_k=ca_k.reshape(bsz, past, A_KV * HEAD_DIM), ctx_v=ca_v.reshape(bsz, past, A_KV * HEAD_DIM),
                     table=win_tab, sink=sink)
    o_b = _attention(pb, 0, 1, 2, 3, n_kv=N_HEADS, qb=NA_QB, kw=NA_KW, back=NA_QB,
                     ctx_k=cb_k.reshape(bsz, past, GROUP_W), ctx_v=cb_v.reshape(bsz, past, GROUP_W),
                     table=na_tab)
    o_c, _ = _retention(pc, log_gamma, _blockdiag_t(st_c))
    o_d, _ = _hgrn(pd, lb, _blockdiag_t(st_d))
    return _out_proj((o_a, o_b, o_c, o_d), w_out, x, gate, g_post)


def kernel(x_prompt, x_sample, c, cache_win_k, cache_win_v, cache_na_k, cache_na_v, state_ret, state_hgrn,
           c_ctx, w_ada, b_ada, g_pre, g_post, w_in, w_out, attn_sink, na_rpb, ret_decay_logit, hgrn_lb_logit):
    depth = w_ada.shape[0]
    dec_b, dec_s, d = x_sample.shape
    p_lb = jax.nn.softmax(hgrn_lb_logit.astype(F32), axis=0)
    lower_bounds = jnp.cumsum(p_lb, axis=0) - p_lb[0:1]
    log_gammas = jax.nn.log_sigmoid(ret_decay_logit.astype(F32))
    w_in_b = w_in.astype(BF16)
    w_out_b = w_out.astype(BF16)

    cvecs = jnp.zeros((ADA_ROWS, d), F32).at[:dec_b].set(c).at[dec_b].set(c_ctx)
    mods = _adaln(cvecs, w_ada, b_ada)
    win_tab = _window_table(dec_s)
    na_tabs = _na_tables(na_rpb, dec_s)

    def mod_of(l, lo, hi):
        m = mods[l, lo:hi].reshape(hi - lo, 1, 3, d)
        return m[:, :, 0], m[:, :, 1], m[:, :, 2]

    x = x_prompt
    outs = [[] for _ in range(6)]
    for l in range(depth):
        x, extra = _context_layer(x, mod_of(l, dec_b, dec_b + 1), g_pre[l], g_post[l], w_in_b[l], w_out_b[l],
                                  attn_sink[l], log_gammas[l], lower_bounds[l])
        for acc, e in zip(outs, extra):
            acc.append(e)
    y_prompt = x
    stacked = [jnp.stack(o, axis=1) for o in outs]

    x = x_sample
    for l in range(depth):
        x = _latent_layer(x, mod_of(l, 0, dec_b), cache_win_k[:, l], cache_win_v[:, l], cache_na_k[:, l],
                          cache_na_v[:, l], state_ret[:, l], state_hgrn[:, l], g_pre[l], g_post[l],
                          w_in_b[l], w_out_b[l], attn_sink[l], win_tab, na_tabs[l],
                          log_gammas[l], lower_bounds[l])
    return (y_prompt, x, *stacked)
```

```python
import functools

import numpy as np
import jax
import jax.numpy as jnp
from jax import lax
from jax.experimental import pallas as pl
from jax.experimental.pallas import tpu as pltpu

F32 = jnp.float32
BF16 = jnp.bfloat16

D_MODEL = 1024
DEPTH = 4
GRID_W = 64
HEAD_DIM = 64
N_HEADS = 4
GROUP_W = N_HEADS * HEAD_DIM
A_KV = 2
WINDOW = 128
NA_ROWS = 8
NA_COLS = 16
ROPE_BASE = 10000.0
EPS = 1e-6
NEG = -1e30
LB_FLOOR = 1e-30
LOG2E = 1.4426950408889634
W_A = 3 * GROUP_W
W_B = 4 * GROUP_W
W_C = 4 * GROUP_W
W_D = 5 * GROUP_W
IN_WIDTH = W_A + W_B + W_C + W_D

V7X_VMEM_LIMIT_BYTES = 56 * 1024 * 1024
ADA_ROWS = 16

NT_DIMS = (((1,), (1,)), ((), ()))
TN_DIMS = (((0,), (0,)), ((), ()))


def _cparams(sem):
    return pltpu.CompilerParams(dimension_semantics=sem, vmem_limit_bytes=V7X_VMEM_LIMIT_BYTES)


def _sigmoid(x):
    return 1.0 / (1.0 + jnp.exp(-x))


def _silu(x):
    return x * _sigmoid(x)


def _head_masks(rows):
    lane = lax.broadcasted_iota(jnp.int32, (rows, GROUP_W), 1)
    return [(lane >= h * HEAD_DIM) & (lane < (h + 1) * HEAD_DIM) for h in range(N_HEADS)]


def _stack_heads(x, masks):
    return jnp.concatenate([jnp.where(m, x, 0.0) for m in masks], axis=0)


def _unstack_heads(x, masks, t):
    out = jnp.where(masks[0], x[0:t], 0.0)
    for h in range(1, N_HEADS):
        out = out + jnp.where(masks[h], x[h * t:(h + 1) * t], 0.0)
    return out


def _block_ones():
    shift = HEAD_DIM.bit_length() - 1
    r = lax.broadcasted_iota(jnp.int32, (GROUP_W, GROUP_W), 0) >> shift
    c = lax.broadcasted_iota(jnp.int32, (GROUP_W, GROUP_W), 1) >> shift
    return r == c


def _head_sum(x, ones_bf16):
    hi = x.astype(BF16)
    lo = (x - hi.astype(F32)).astype(BF16)
    return (jnp.dot(hi, ones_bf16, preferred_element_type=F32)
            + jnp.dot(lo, ones_bf16, preferred_element_type=F32))


def _adaln_kernel(c_ref, w_ref, b_ref, o_ref):
    s = _silu(c_ref[...]).astype(BF16)
    o_ref[0, 0] = jnp.dot(s, w_ref[0].astype(BF16), preferred_element_type=F32) + b_ref[0]


def _adaln(cvecs, w_ada, b_ada):
    depth, d, d3 = w_ada.shape
    tn = 512
    per = d // tn
    return pl.pallas_call(
        _adaln_kernel,
        grid=(depth, d3 // tn),
        in_specs=[pl.BlockSpec((ADA_ROWS, d), lambda l, j: (0, 0)),
                  pl.BlockSpec((1, d, tn), lambda l, j: (l, 0, j)),
                  pl.BlockSpec((1, 1, tn), lambda l, j: (l, 0, j))],
        out_specs=pl.BlockSpec((1, 1, ADA_ROWS, tn), lambda l, j: (l, j // per, 0, j % per)),
        out_shape=jax.ShapeDtypeStruct((depth, 3, ADA_ROWS, d), F32),
        compiler_params=_cparams(("arbitrary", "arbitrary")),
        name="adaln",
    )(cvecs, w_ada, b_ada.reshape(depth, 1, d3))


def _rope(x, cos, sin):
    w = x.shape[-1]
    lane = lax.broadcasted_iota(jnp.int32, x.shape, 1)
    first = (lane & 31) < 16
    swapped = jnp.where(first, pltpu.roll(x, w - 16, 1), pltpu.roll(x, 16, 1))
    return x * cos[:, :w] + swapped * sin[:, :w]


def _mod_row(mod_ref, mod_row):
    row = pl.program_id(0) if mod_row is None else mod_row
    return mod_ref[0, 0, pl.ds(row, 1), :]


def _inproj_kernel(*refs, rope, layer, mod_row):
    if rope:
        x_ref, shift_ref, scale_ref, g_ref, w_ref, cos_ref, sin_ref, pa_ref, pb_ref, pc_ref, pd_ref = refs
    else:
        x_ref, shift_ref, scale_ref, g_ref, w_ref, pa_ref, pb_ref, pc_ref, pd_ref = refs
    x = x_ref[0]
    ms = jnp.mean(x * x, axis=-1, keepdims=True)
    y = x * lax.rsqrt(ms + EPS) * g_ref[layer:layer + 1, :]
    h = (y * (1.0 + _mod_row(scale_ref, mod_row)) + _mod_row(shift_ref, mod_row)).astype(BF16)

    def mm(c0, c1):
        return jnp.dot(h, w_ref[0, :, c0:c1], preferred_element_type=F32)

    if rope:
        cos = cos_ref[...]
        sin = sin_ref[...]
    g = GROUP_W
    aq = mm(0, g)
    akv = mm(g, 2 * g)
    if rope:
        aq = _rope(aq, cos, sin)
        ak = _rope(akv[:, :g // 2], cos, sin)
        akv = jnp.concatenate([ak, akv[:, g // 2:]], axis=1)
    pa_ref[0, :, 0:g] = aq.astype(pa_ref.dtype)
    pa_ref[0, :, g:2 * g] = akv.astype(pa_ref.dtype)
    pa_ref[0, :, 2 * g:3 * g] = mm(2 * g, 3 * g).astype(pa_ref.dtype)
    pb_ref[0] = mm(W_A, W_A + W_B).astype(pb_ref.dtype)
    c0 = W_A + W_B
    cq = mm(c0, c0 + g)
    ck = mm(c0 + g, c0 + 2 * g)
    if rope:
        cq = _rope(cq, cos, sin)
        ck = _rope(ck, cos, sin)
    pc_ref[0, :, 0:g] = cq.astype(pc_ref.dtype)
    pc_ref[0, :, g:2 * g] = ck.astype(pc_ref.dtype)
    pc_ref[0, :, 2 * g:4 * g] = mm(c0 + 2 * g, c0 + 4 * g).astype(pc_ref.dtype)
    d0 = c0 + W_C
    pd_ref[0] = mm(d0, d0 + W_D).astype(pd_ref.dtype)


def _rope_tables(slen):
    t = np.arange(slen)
    nf = HEAD_DIM // 4
    freqs = ROPE_BASE ** (-np.arange(nf, dtype=np.float64) / nf)
    d = np.arange(HEAD_DIM)
    pos = np.where(d[None, :] < HEAD_DIM // 2, (t // GRID_W)[:, None], (t % GRID_W)[:, None])
    ang = pos * freqs[d % nf][None, :]
    sign = np.where((d % (2 * nf)) < nf, -1.0, 1.0)[None, :]
    cos = np.tile(np.cos(ang), (1, N_HEADS))
    sin = np.tile(np.sin(ang) * sign, (1, N_HEADS))
    return jnp.asarray(cos, F32), jnp.asarray(sin, F32)


def _mod_spec(which, layer):
    return pl.BlockSpec((1, 1, ADA_ROWS, D_MODEL), lambda b, i: (layer, which, 0, 0))


def _in_proj(x, mods, layer, mod_row, g_pre, w_in_bf16, rope, tm=256):
    bsz, slen, d = x.shape
    depth = g_pre.shape[0]
    in_specs = [pl.BlockSpec((1, tm, d), lambda b, i: (b, i, 0)),
                _mod_spec(0, layer),
                _mod_spec(1, layer),
                pl.BlockSpec((depth, d), lambda b, i: (0, 0)),
                pl.BlockSpec((1, d, IN_WIDTH), lambda b, i: (layer, 0, 0))]
    args = [x, mods, mods, g_pre, w_in_bf16]
    if rope:
        cos, sin = _rope_tables(slen)
        in_specs += [pl.BlockSpec((tm, GROUP_W), lambda b, i: (i, 0))] * 2
        args += [cos, sin]
    widths = (W_A, W_B, W_C, W_D)
    return pl.pallas_call(
        functools.partial(_inproj_kernel, rope=rope, layer=layer, mod_row=mod_row),
        grid=(bsz, slen // tm),
        in_specs=in_specs,
        out_specs=[pl.BlockSpec((1, tm, w), lambda b, i: (b, i, 0)) for w in widths],
        out_shape=[jax.ShapeDtypeStruct((bsz, slen, w), F32) for w in widths],
        compiler_params=_cparams(("arbitrary", "arbitrary")),
        name="in_proj",
    )(*args)


def _attn_kernel(*refs, n_kv, qb, kw, back, slen, has_ctx, n_tab, tab_heads, has_sink):
    it = iter(refs)
    q_ref, k_ref, v_ref, g_ref = next(it), next(it), next(it), next(it)
    kc_ref = vc_ref = tab_ref = sink_ref = None
    if has_ctx:
        kc_ref, vc_ref = next(it), next(it)
    if n_tab:
        tab_ref = next(it)
    if has_sink:
        sink_ref = next(it)
    o_ref = next(it)

    n = pl.program_id(1)
    nblk = slen // qb
    ws = pl.multiple_of(jnp.clip(n * qb - back, 0, slen - kw), HEAD_DIM)
    if n_tab == 3:
        tix = jnp.where(n == 0, 0, jnp.where(n == nblk - 1, 2, 1))
    else:
        tix = 0
    grp = N_HEADS // n_kv
    scale = HEAD_DIM ** -0.5
    for kh in range(n_kv):
        heads = range(kh * grp, (kh + 1) * grp)
        hsl = [slice(h * HEAD_DIM, (h + 1) * HEAD_DIM) for h in heads]
        ks = slice(kh * HEAD_DIM, (kh + 1) * HEAD_DIM)
        q = jnp.concatenate([q_ref[0, :, sl] for sl in hsl], axis=0).astype(BF16)
        kl = k_ref[0, pl.ds(ws, kw), ks].astype(BF16)
        vl = v_ref[0, pl.ds(ws, kw), ks].astype(BF16)
        s = lax.dot_general(q, kl, NT_DIMS, preferred_element_type=F32) * scale
        if n_tab:
            s = s + jnp.concatenate([tab_ref[0, tix, h if tab_heads == N_HEADS else 0] for h in heads], axis=0)
        m = jnp.max(s, axis=-1, keepdims=True)
        if has_ctx:
            sc = lax.dot_general(q, kc_ref[0, 0, :, ks].astype(BF16), NT_DIMS, preferred_element_type=F32) * scale
            m = jnp.maximum(m, jnp.max(sc, axis=-1, keepdims=True))
        if has_sink:
            sink = jnp.concatenate([jnp.full((qb, 1), sink_ref[h], F32) for h in heads], axis=0)
            m = jnp.maximum(m, sink)
        p = jnp.exp(s - m)
        den = jnp.sum(p, axis=-1, keepdims=True)
        acc = jnp.dot(p.astype(BF16), vl, preferred_element_type=F32)
        if has_ctx:
            pc = jnp.exp(sc - m)
            den = den + jnp.sum(pc, axis=-1, keepdims=True)
            acc = acc + jnp.dot(pc.astype(BF16), vc_ref[0, 0, :, ks].astype(BF16), preferred_element_type=F32)
        if has_sink:
            den = den + jnp.exp(sink - m)
        o = acc / den
        for gi, sl in enumerate(hsl):
            o_ref[0, :, sl] = (o[gi * qb:(gi + 1) * qb] * _silu(g_ref[0, :, sl])).astype(o_ref.dtype)


def _attention(p, q_col, k_col, v_col, g_col, n_kv, qb, kw, back, ctx_k=None, ctx_v=None,
               table=None, sink=None, layer=0):
    bsz, slen, _ = p.shape
    kvw = n_kv * HEAD_DIM
    in_specs = [pl.BlockSpec((1, qb, GROUP_W), lambda b, n: (b, n, q_col)),
                pl.BlockSpec((1, slen, kvw), lambda b, n: (b, 0, k_col)),
                pl.BlockSpec((1, slen, kvw), lambda b, n: (b, 0, v_col)),
                pl.BlockSpec((1, qb, GROUP_W), lambda b, n: (b, n, g_col))]
    args = [p, p, p, p]
    has_ctx = ctx_k is not None
    if has_ctx:
        past = ctx_k.shape[2]
        in_specs += [pl.BlockSpec((1, 1, past, kvw), lambda b, n: (b, layer, 0, 0))] * 2
        args += [ctx_k, ctx_v]
    n_tab = tab_heads = 0
    if table is not None:
        n_tab, tab_heads = table.shape[1], table.shape[2]
        tab_layer = layer if table.shape[0] > 1 else 0
        in_specs.append(pl.BlockSpec((1,) + table.shape[1:], lambda b, n: (tab_layer, 0, 0, 0, 0)))
        args.append(table)
    if sink is not None:
        in_specs.append(pl.BlockSpec(memory_space=pltpu.SMEM))
        args.append(sink)
    kern = functools.partial(_attn_kernel, n_kv=n_kv, qb=qb, kw=kw, back=back, slen=slen, has_ctx=has_ctx,
                             n_tab=n_tab, tab_heads=tab_heads, has_sink=sink is not None)
    return pl.pallas_call(
        kern,
        grid=(bsz, slen // qb),
        in_specs=in_specs,
        out_specs=pl.BlockSpec((1, qb, GROUP_W), lambda b, n: (b, n, 0)),
        out_shape=jax.ShapeDtypeStruct((bsz, slen, GROUP_W), BF16),
        compiler_params=_cparams(("arbitrary", "arbitrary")),
        name="attn",
    )(*args)


WIN_QB = 256
WIN_BACK = WINDOW
WIN_KW = WIN_QB + 2 * WINDOW
NA_QROWS = 4
NA_KROWS = 12
NA_QB = NA_QROWS * GRID_W
NA_KW = NA_KROWS * GRID_W


def _window_table(slen):
    nblk = slen // WIN_QB
    tabs = []
    for n in (0, 1, nblk - 1):
        ws = int(np.clip(n * WIN_QB - WIN_BACK, 0, slen - WIN_KW))
        qpos = n * WIN_QB + np.arange(WIN_QB)[:, None]
        kpos = ws + np.arange(WIN_KW)[None, :]
        tabs.append(np.where(np.abs(qpos - kpos) <= WINDOW, 0.0, NEG))
    return jnp.asarray(np.stack(tabs)[None, :, None], F32)


N_RPB_R = 2 * NA_ROWS - 1
N_RPB_C = 2 * NA_COLS - 1


def _na_table_kernel(rpb_ref, o_ref, tz_scr, *, rows):
    base = (pl.program_id(0) * N_HEADS + pl.program_id(1)) * (N_RPB_R * N_RPB_C)
    qc = lax.broadcasted_iota(jnp.int32, (GRID_W, 2 * GRID_W), 0)
    kk = lax.broadcasted_iota(jnp.int32, (GRID_W, 2 * GRID_W), 1)
    kc = kk & (GRID_W - 1)
    diff = kc - qc
    qws = jnp.clip(qc - NA_COLS // 2, 0, GRID_W - NA_COLS)
    col_ok = (kc >= qws) & (kc < qws + NA_COLS)
    neg = jnp.full((GRID_W, 2 * GRID_W), NEG, F32)
    for dr in range(N_RPB_R):
        acc = neg
        for m in range(N_RPB_C):
            acc = jnp.where(diff == m - (NA_COLS - 1), rpb_ref[base + dr * N_RPB_C + m], acc)
        tz_scr[dr] = jnp.where(col_ok, acc, NEG)
    wr = min(NA_ROWS, rows)
    nblk = rows // NA_QROWS
    for ti, g in enumerate((0, 1, nblk - 1)):
        ws_row = min(max(g * NA_QROWS - NA_QROWS, 0), rows - NA_KROWS)
        for qr in range(NA_QROWS):
            r = g * NA_QROWS + qr
            rs = min(max(r - wr // 2, 0), rows - wr)
            for p in range(NA_KROWS // 2):
                halves = []
                for kr in (ws_row + 2 * p, ws_row + 2 * p + 1):
                    halves.append(tz_scr[kr - r + NA_ROWS - 1] if rs <= kr < rs + wr else neg)
                o_ref[0, ti, 0, qr * GRID_W:(qr + 1) * GRID_W, p * 2 * GRID_W:(p + 1) * 2 * GRID_W] = (
                    jnp.where(kk < GRID_W, halves[0], halves[1]))


def _na_tables(na_rpb, slen):
    depth = na_rpb.shape[0]
    return pl.pallas_call(
        functools.partial(_na_table_kernel, rows=slen // GRID_W),
        grid=(depth, N_HEADS),
        in_specs=[pl.BlockSpec(memory_space=pltpu.SMEM)],
        out_specs=pl.BlockSpec((1, 3, 1, NA_QB, NA_KW), lambda l, h: (l, 0, h, 0, 0)),
        out_shape=jax.ShapeDtypeStruct((depth, 3, N_HEADS, NA_QB, NA_KW), F32),
        scratch_shapes=[pltpu.VMEM((N_RPB_R, GRID_W, 2 * GRID_W), F32)],
        compiler_params=_cparams(("arbitrary", "arbitrary")),
        name="na_table",
    )(na_rpb.astype(F32).reshape(-1))


def _ret_kernel(q_ref, k_ref, v_ref, g_ref, lgl_ref, lgc_ref, *rest, slen, t, has_s0):
    if has_s0:
        s0_ref, o_ref, sb_scr, st_scr = rest
    else:
        o_ref, sfin_ref, sb_scr, st_scr = rest
    nc = slen // t
    masks = _head_masks(t)
    bmask = _block_ones()
    ones_bf16 = jnp.where(bmask, 1.0, 0.0).astype(BF16)
    ii = lax.broadcasted_iota(jnp.int32, (t, N_HEADS * t), 0)
    jj = lax.broadcasted_iota(jnp.int32, (t, N_HEADS * t), 1) & (t - 1)
    dist = (ii - jj).astype(F32)
    idx = lax.broadcasted_iota(jnp.int32, (t, GROUP_W), 0).astype(F32)
    lgf, lgb = lgl_ref[0], lgl_ref[1]
    dmat = (jnp.where(dist >= 0, jnp.exp(dist * lgc_ref[0]), 0.0)
            + jnp.where(dist <= 0, jnp.exp(-dist * lgc_ref[1]), 0.0))
    qdec_f = jnp.exp((idx + 1.0) * lgf)
    kdec_f = jnp.exp((t - 1.0 - idx) * lgf)
    qdec_b = jnp.exp((t - idx) * lgb)
    kdec_b = jnp.exp(idx * lgb)
    cdec_f = jnp.exp(float(t) * lgf)
    cdec_b = jnp.exp(float(t) * lgb)
    kscale = HEAD_DIM ** -0.5

    def state_update(st, k, v, kdec, cdec):
        upd = lax.dot_general(v, (k * kdec).astype(BF16), TN_DIMS, preferred_element_type=F32)
        return st * cdec + jnp.where(bmask, upd, 0.0)

    def init_state(direction):
        if has_s0:
            st_scr[...] = _expand_state(s0_ref[0, direction], bmask)
        else:
            st_scr[...] = jnp.zeros((GROUP_W, GROUP_W), F32)

    def emit_final(direction):
        if not has_s0:
            sfin_ref[0, direction] = _compact_state(st_scr[...], bmask)

    init_state(1)

    def sweep_bwd(ci, carry):
        c = nc - 1 - ci
        rows = pl.ds(pl.multiple_of(c * t, t), t)
        st = st_scr[...]
        sb_scr[c] = st.astype(BF16)
        k = k_ref[0, rows, :] * kscale
        st_scr[...] = state_update(st, k, v_ref[0, rows, :].astype(BF16), kdec_b, cdec_b)
        return carry

    lax.fori_loop(0, nc, sweep_bwd, 0)
    emit_final(1)
    init_state(0)

    def sweep_fwd(c, carry):
        rows = pl.ds(pl.multiple_of(c * t, t), t)
        q = q_ref[0, rows, :]
        k = k_ref[0, rows, :] * kscale
        v = v_ref[0, rows, :]
        ks = _stack_heads(k, masks).astype(BF16)
        vs = _stack_heads(v, masks).astype(BF16)
        sc = lax.dot_general(q.astype(BF16), ks, NT_DIMS, preferred_element_type=F32) * dmat
        o = jnp.dot(sc.astype(BF16), vs, preferred_element_type=F32)
        st = st_scr[...]
        o = o + lax.dot_general((q * qdec_f).astype(BF16), st.astype(BF16), NT_DIMS, preferred_element_type=F32)
        o = o + lax.dot_general((q * qdec_b).astype(BF16), sb_scr[c], NT_DIMS, preferred_element_type=F32)
        st_scr[...] = state_update(st, k, v.astype(BF16), kdec_f, cdec_f)
        mu = _head_sum(o, ones_bf16) * (1.0 / HEAD_DIM)
        d = o - mu
        var = _head_sum(d * d, ones_bf16) * (1.0 / HEAD_DIM)
        y = d * lax.rsqrt(var + EPS)
        o_ref[0, rows, :] = (y * _silu(g_ref[0, rows, :])).astype(o_ref.dtype)
        return carry

    lax.fori_loop(0, nc, sweep_fwd, 0)
    emit_final(0)


def _compact_states(s0):
    return jnp.swapaxes(s0, -1, -2).reshape(s0.shape[:-3] + (GROUP_W, HEAD_DIM))


def _uncompact_states(sc):
    bsz = sc.shape[0]
    return jnp.swapaxes(sc.reshape(bsz, 2, N_HEADS, HEAD_DIM, HEAD_DIM), -1, -2)


def _split3(x):
    h1 = x.astype(BF16)
    r1 = x - h1.astype(F32)
    h2 = r1.astype(BF16)
    return h1, h2, (r1 - h2.astype(F32)).astype(BF16)


def _expand_state(x, bmask):
    r = lax.broadcasted_iota(jnp.int32, (HEAD_DIM, GROUP_W), 0)
    c = lax.broadcasted_iota(jnp.int32, (HEAD_DIM, GROUP_W), 1) & (HEAD_DIM - 1)
    rep = jnp.where(r == c, 1.0, 0.0).astype(BF16)
    h1, h2, h3 = _split3(x)
    tiled = (jnp.dot(h1, rep, preferred_element_type=F32) + jnp.dot(h2, rep, preferred_element_type=F32)
             + jnp.dot(h3, rep, preferred_element_type=F32))
    return jnp.where(bmask, tiled, 0.0)


def _compact_state(st, bmask):
    r = lax.broadcasted_iota(jnp.int32, (GROUP_W, HEAD_DIM), 0) & (HEAD_DIM - 1)
    c = lax.broadcasted_iota(jnp.int32, (GROUP_W, HEAD_DIM), 1)
    fold = jnp.where(r == c, 1.0, 0.0).astype(BF16)
    h1, h2, h3 = _split3(jnp.where(bmask, st, 0.0))
    return (jnp.dot(h1, fold, preferred_element_type=F32) + jnp.dot(h2, fold, preferred_element_type=F32)
            + jnp.dot(h3, fold, preferred_element_type=F32))


def _state_specs(bsz, s0):
    spec = pl.BlockSpec((1, 2, GROUP_W, HEAD_DIM), lambda b: (b, 0, 0, 0))
    shape = jax.ShapeDtypeStruct((bsz, 2, GROUP_W, HEAD_DIM), F32)
    if s0 is None:
        return [], [], [spec], [shape]
    return [spec], [s0], [], []


RET_CHUNK = 128


def _retention(pc, log_gamma, s0=None, t=RET_CHUNK):
    bsz, slen, _ = pc.shape
    lgl = jnp.repeat(log_gamma, HEAD_DIM, axis=1).reshape(2, 1, GROUP_W)
    lgc = jnp.repeat(log_gamma, t, axis=1).reshape(2, 1, N_HEADS * t)
    col = lambda j: pl.BlockSpec((1, slen, GROUP_W), lambda b: (b, 0, j))
    s_in_specs, s_args, s_out_specs, s_out_shape = _state_specs(bsz, s0)
    return pl.pallas_call(
        functools.partial(_ret_kernel, slen=slen, t=t, has_s0=s0 is not None),
        grid=(bsz,),
        in_specs=[col(0), col(1), col(2), col(3),
                  pl.BlockSpec((2, 1, GROUP_W), lambda b: (0, 0, 0)),
                  pl.BlockSpec((2, 1, N_HEADS * t), lambda b: (0, 0, 0))] + s_in_specs,
        out_specs=[pl.BlockSpec((1, slen, GROUP_W), lambda b: (b, 0, 0))] + s_out_specs,
        out_shape=[jax.ShapeDtypeStruct((bsz, slen, GROUP_W), BF16)] + s_out_shape,
        scratch_shapes=[pltpu.VMEM((slen // t, GROUP_W, GROUP_W), BF16), pltpu.VMEM((GROUP_W, GROUP_W), F32)],
        compiler_params=_cparams(("arbitrary",)),
        name="retention",
    )(pc, pc, pc, pc, lgl, lgc, *s_args)


HGRN_DIRECT = 8


def _split2(x):
    hi = x.astype(BF16)
    return hi, (x - hi.astype(F32)).astype(BF16)


def _hgrn_kernel(q_ref, zf_ref, zb_ref, v_ref, g_ref, lb_ref, *rest, slen, t, has_s0):
    if has_s0:
        s0_ref, o_ref, of_scr, st_scr = rest
    else:
        o_ref, sfin_ref, of_scr, st_scr = rest
    nc = slen // t
    masks = _head_masks(t)
    bmask = _block_ones()
    ones_bf16 = jnp.where(bmask, 1.0, 0.0).astype(BF16)
    r_i = lax.broadcasted_iota(jnp.int32, (t, t), 0)
    c_i = lax.broadcasted_iota(jnp.int32, (t, t), 1)
    sub = lax.broadcasted_iota(jnp.int32, (t, GROUP_W), 0) & (HGRN_DIRECT - 1)
    levels = []
    s = t // 2
    while s >= HGRN_DIRECT:
        levels.append(s)
        s //= 2
    r_w = lax.broadcasted_iota(jnp.int32, (t, N_HEADS * t), 0)
    c_w = lax.broadcasted_iota(jnp.int32, (t, N_HEADS * t), 1) & (t - 1)
    lvl_masks = []
    for s in levels:
        shift = (2 * s).bit_length() - 1
        lvl_masks.append(jnp.where((r_w >> shift) == (c_w >> shift), 1.0, 0.0))

    def block_roll(x, shift):
        x3 = x.reshape(t // HGRN_DIRECT, HGRN_DIRECT, GROUP_W)
        return pltpu.roll(x3, shift, 1).reshape(t, GROUP_W)

    def run(direction, z_ref, emit):
        fwd = direction == 0
        lbd = jnp.maximum(lb_ref[direction], LB_FLOOR)
        tri = jnp.where((r_i >= c_i) if fwd else (r_i <= c_i), 1.0, 0.0).astype(BF16)
        if has_s0:
            st_scr[...] = _expand_state(s0_ref[0, direction], bmask)
        else:
            st_scr[...] = jnp.zeros((GROUP_W, GROUP_W), F32)

        def body(ci, carry):
            c = ci if fwd else nc - 1 - ci
            rows = pl.ds(pl.multiple_of(c * t, t), t)
            q = _silu(q_ref[0, rows, :])
            z = z_ref[0, rows, :]
            v = v_ref[0, rows, :]
            vb = v.astype(BF16)
            a = jnp.exp(-jnp.abs(z))
            inv = 1.0 / (1.0 + a)
            pos = z >= 0
            sig = jnp.where(pos, inv, a * inv)
            nsig = jnp.where(pos, a * inv, inv)
            lf = jnp.log(lbd + (1.0 - lbd) * sig)
            k = (1.0 - lbd) * nsig
            h1, h2 = _split2(lf)
            b = jnp.dot(tri, h1, preferred_element_type=F32) + jnp.dot(tri, h2, preferred_element_type=F32)
            b = b * LOG2E
            vs = _stack_heads(v, masks).astype(BF16)

            sc = None
            for s, lm in zip(levels, lvl_masks):
                pa, pb = [], []
                zero = jnp.zeros((s, GROUP_W), F32)
                for gi in range(t // (2 * s)):
                    lo = b[gi * 2 * s:gi * 2 * s + s]
                    hi = b[gi * 2 * s + s:(gi + 1) * 2 * s]
                    if fwd:
                        anc = hi[0:1]
                        pa += [zero, jnp.exp2(hi - anc)]
                        pb += [jnp.exp2(anc - lo), zero]
                    else:
                        anc = lo[s - 1:s]
                        pa += [jnp.exp2(lo - anc), zero]
                        pb += [zero, jnp.exp2(anc - hi)]
                qa = (q * jnp.concatenate(pa, axis=0)).astype(BF16)
                kb = _stack_heads(k * jnp.concatenate(pb, axis=0), masks).astype(BF16)
                term = lax.dot_general(qa, kb, NT_DIMS, preferred_element_type=F32) * lm
                sc = term if sc is None else sc + term
            o = None
            if sc is not None:
                o = jnp.dot(sc.astype(BF16), vs, preferred_element_type=F32)

            prods, vals = [(q * k).astype(BF16)], [v]
            for dlt in range(1, HGRN_DIRECT):
                sh = dlt if fwd else HGRN_DIRECT - dlt
                ok = (sub >= dlt) if fwd else (sub + dlt <= HGRN_DIRECT - 1)
                w = jnp.exp2(jnp.minimum(b - block_roll(b, sh), 0.0))
                prods.append(jnp.where(ok, q * block_roll(k, sh) * w, 0.0).astype(BF16))
                vals.append(block_roll(v, sh))
            hs = jnp.dot(jnp.concatenate(prods, axis=0), ones_bf16, preferred_element_type=F32)
            for dlt in range(HGRN_DIRECT):
                term = hs[dlt * t:(dlt + 1) * t] * vals[dlt]
                o = term if o is None else o + term

            st = st_scr[...]
            o = o + lax.dot_general((q * jnp.exp2(b)).astype(BF16), st.astype(BF16), NT_DIMS,
                                    preferred_element_type=F32)
            bl = b[t - 1:t] if fwd else b[0:1]
            upd = lax.dot_general(vb, (k * jnp.exp2(bl - b)).astype(BF16), TN_DIMS,
                                  preferred_element_type=F32)
            st_scr[...] = st * jnp.exp2(bl) + jnp.where(bmask, upd, 0.0)
            emit(rows, o)
            return carry

        lax.fori_loop(0, nc, body, 0)
        if not has_s0:
            sfin_ref[0, direction] = _compact_state(st_scr[...], bmask)

    def emit_fwd(rows, o):
        of_scr[rows, :] = o

    def emit_bwd(rows, o):
        o = o + of_scr[rows, :]
        ms = _head_sum(o * o, ones_bf16) * (1.0 / HEAD_DIM)
        y = o * lax.rsqrt(ms + EPS)
        o_ref[0, rows, :] = (y * _silu(g_ref[0, rows, :])).astype(o_ref.dtype)

    run(0, zf_ref, emit_fwd)
    run(1, zb_ref, emit_bwd)


HGRN_CHUNK = 128


def _hgrn(pd, lb, s0=None, t=HGRN_CHUNK):
    bsz, slen, _ = pd.shape
    col = lambda j: pl.BlockSpec((1, slen, GROUP_W), lambda b: (b, 0, j))
    s_in_specs, s_args, s_out_specs, s_out_shape = _state_specs(bsz, s0)
    return pl.pallas_call(
        functools.partial(_hgrn_kernel, slen=slen, t=t, has_s0=s0 is not None),
        grid=(bsz,),
        in_specs=[col(0), col(1), col(2), col(3), col(4),
                  pl.BlockSpec((2, 1, GROUP_W), lambda b: (0, 0, 0))] + s_in_specs,
        out_specs=[pl.BlockSpec((1, slen, GROUP_W), lambda b: (b, 0, 0))] + s_out_specs,
        out_shape=[jax.ShapeDtypeStruct((bsz, slen, GROUP_W), BF16)] + s_out_shape,
        scratch_shapes=[pltpu.VMEM((slen, GROUP_W), F32), pltpu.VMEM((GROUP_W, GROUP_W), F32)],
        compiler_params=_cparams(("arbitrary",)),
        name="hgrn2",
    )(pd, pd, pd, pd, pd, lb.reshape(2, 1, GROUP_W), *s_args)


def _outproj_kernel(ma_ref, mb_ref, mc_ref, md_ref, w_ref, x_ref, gate_ref, g_ref, o_ref, *, layer, mod_row):
    g = GROUP_W
    y = jnp.dot(ma_ref[0], w_ref[0, 0:g, :], preferred_element_type=F32)
    y = y + jnp.dot(mb_ref[0], w_ref[0, g:2 * g, :], preferred_element_type=F32)
    y = y + jnp.dot(mc_ref[0], w_ref[0, 2 * g:3 * g, :], preferred_element_type=F32)
    y = y + jnp.dot(md_ref[0], w_ref[0, 3 * g:4 * g, :], preferred_element_type=F32)
    ms = jnp.mean(y * y, axis=-1, keepdims=True)
    r = y * lax.rsqrt(ms + EPS) * g_ref[layer:layer + 1, :]
    o_ref[0] = x_ref[0] + _mod_row(gate_ref, mod_row) * r


OUTPROJ_TM = 512


def _out_proj(mixed, w_out_bf16, x, mods, layer, mod_row, g_post):
    bsz, slen, d = x.shape
    depth = g_post.shape[0]
    tm = min(OUTPROJ_TM, slen)
    mspec = pl.BlockSpec((1, tm, GROUP_W), lambda b, i: (b, i, 0))
    return pl.pallas_call(
        functools.partial(_outproj_kernel, layer=layer, mod_row=mod_row),
        grid=(bsz, slen // tm),
        in_specs=[mspec, mspec, mspec, mspec,
                  pl.BlockSpec((1, d, d), lambda b, i: (layer, 0, 0)),
                  pl.BlockSpec((1, tm, d), lambda b, i: (b, i, 0)),
                  _mod_spec(2, layer),
                  pl.BlockSpec((depth, d), lambda b, i: (0, 0))],
        out_specs=pl.BlockSpec((1, tm, d), lambda b, i: (b, i, 0)),
        out_shape=jax.ShapeDtypeStruct((bsz, slen, d), F32),
        compiler_params=_cparams(("arbitrary", "arbitrary")),
        name="out_proj",
    )(*mixed, w_out_bf16, x, mods, g_post)


def _context_layer(x, mods, layer, mod_row, g_pre, g_post, w_in, w_out, sink, log_gamma, lb):
    bsz, slen, _ = x.shape
    pa, pb, pc, pd = _in_proj(x, mods, layer, mod_row, g_pre, w_in, rope=False)
    o_a = _attention(pa, 0, 2, 3, 2, n_kv=A_KV, qb=slen, kw=slen, back=0, sink=sink)
    o_b = _attention(pb, 0, 1, 2, 3, n_kv=N_HEADS, qb=slen, kw=slen, back=0)
    o_c, s_c = _retention(pc, log_gamma)
    o_d, s_d = _hgrn(pd, lb)
    x = _out_proj((o_a, o_b, o_c, o_d), w_out, x, mods, layer, mod_row, g_post)
    g = GROUP_W
    ak = pa[:, :, g:g + g // 2].reshape(bsz, slen, A_KV, HEAD_DIM)
    av = pa[:, :, g + g // 2:2 * g].reshape(bsz, slen, A_KV, HEAD_DIM)
    bk = pb[:, :, g:2 * g].reshape(bsz, slen, N_HEADS, HEAD_DIM)
    bv = pb[:, :, 2 * g:3 * g].reshape(bsz, slen, N_HEADS, HEAD_DIM)
    return x, (ak, av, bk, bv, _uncompact_states(s_c), _uncompact_states(s_d))


def _latent_layer(x, mods, layer, ca_k, ca_v, cb_k, cb_v, st_c, st_d, g_pre, g_post, w_in, w_out, sink, win_tab,
                  na_tabs, log_gamma, lb):
    pa, pb, pc, pd = _in_proj(x, mods, layer, None, g_pre, w_in, rope=True)
    o_a = _attention(pa, 0, 2, 3, 2, n_kv=A_KV, qb=WIN_QB, kw=WIN_KW, back=WIN_BACK,
                     ctx_k=ca_k, ctx_v=ca_v, table=win_tab, sink=sink, layer=layer)
    o_b = _attention(pb, 0, 1, 2, 3, n_kv=N_HEADS, qb=NA_QB, kw=NA_KW, back=NA_QB,
                     ctx_k=cb_k, ctx_v=cb_v, table=na_tabs, layer=layer)
    o_c, = _retention(pc, log_gamma, st_c)
    o_d, = _hgrn(pd, lb, st_d)
    return _out_proj((o_a, o_b, o_c, o_d), w_out, x, mods, layer, None, g_post)


def kernel(x_prompt, x_sample, c, cache_win_k, cache_win_v, cache_na_k, cache_na_v, state_ret, state_hgrn,
           c_ctx, w_ada, b_ada, g_pre, g_post, w_in, w_out, attn_sink, na_rpb, ret_decay_logit, hgrn_lb_logit):
    depth = w_ada.shape[0]
    dec_b, dec_s, d = x_sample.shape
    p_lb = jax.nn.softmax(hgrn_lb_logit.astype(F32), axis=0)
    lower_bounds = jnp.cumsum(p_lb, axis=0) - p_lb[0:1]
    log_gammas = jax.nn.log_sigmoid(ret_decay_logit.astype(F32))
    w_in_b = w_in.astype(BF16)
    w_out_b = w_out.astype(BF16)

    cvecs = jnp.zeros((ADA_ROWS, d), F32).at[:dec_b].set(c).at[dec_b].set(c_ctx)
    mods = _adaln(cvecs, w_ada, b_ada)
    win_tab = _window_table(dec_s)
    na_tabs = _na_tables(na_rpb, dec_s)

    x = x_prompt
    outs = [[] for _ in range(6)]
    for l in range(depth):
        x, extra = _context_layer(x, mods, l, dec_b, g_pre, g_post, w_in_b, w_out_b,
                                  attn_sink[l], log_gammas[l], lower_bounds[l])
        for acc, e in zip(outs, extra):
            acc.append(e)
    y_prompt = x
    stacked = [jnp.stack(o, axis=1) for o in outs]

    past = cache_win_k.shape[2]
    ca_k = cache_win_k.reshape(dec_b, depth, past, A_KV * HEAD_DIM)
    ca_v = cache_win_v.reshape(dec_b, depth, past, A_KV * HEAD_DIM)
    cb_k = cache_na_k.reshape(dec_b, depth, past, GROUP_W)
    cb_v = cache_na_v.reshape(dec_b, depth, past, GROUP_W)
    st_c = _compact_states(state_ret)
    st_d = _compact_states(state_hgrn)
    x = x_sample
    for l in range(depth):
        x = _latent_layer(x, mods, l, ca_k, ca_v, cb_k, cb_v, st_c[:, l], st_d[:, l], g_pre, g_post,
                          w_in_b, w_out_b, attn_sink[l], win_tab, na_tabs, log_gammas[l], lower_bounds[l])
    return (y_prompt, x, *stacked)
```

```python
import functools

import numpy as np
import jax
import jax.numpy as jnp
from jax import lax
from jax.experimental import pallas as pl
from jax.experimental.pallas import tpu as pltpu

F32 = jnp.float32
BF16 = jnp.bfloat16

D_MODEL = 1024
DEPTH = 4
GRID_W = 64
HEAD_DIM = 64
N_HEADS = 4
GROUP_W = N_HEADS * HEAD_DIM
A_KV = 2
WINDOW = 128
NA_ROWS = 8
NA_COLS = 16
ROPE_BASE = 10000.0
EPS = 1e-6
NEG = -1e30
LB_FLOOR = 1e-30
LOG2E = 1.4426950408889634
W_A = 3 * GROUP_W
W_B = 4 * GROUP_W
W_C = 4 * GROUP_W
W_D = 5 * GROUP_W
IN_WIDTH = W_A + W_B + W_C + W_D

V7X_VMEM_LIMIT_BYTES = 56 * 1024 * 1024
ADA_ROWS = 16

NT_DIMS = (((1,), (1,)), ((), ()))
TN_DIMS = (((0,), (0,)), ((), ()))


def _cparams(sem, flags=None):
    return pltpu.CompilerParams(dimension_semantics=sem, vmem_limit_bytes=V7X_VMEM_LIMIT_BYTES, flags=flags)


def _sigmoid(x):
    return 1.0 / (1.0 + jnp.exp(-x))


def _silu(x):
    return x * _sigmoid(x)


def _head_masks(rows):
    lane = lax.broadcasted_iota(jnp.int32, (rows, GROUP_W), 1)
    return [(lane >= h * HEAD_DIM) & (lane < (h + 1) * HEAD_DIM) for h in range(N_HEADS)]


def _stack_heads(x, masks):
    return jnp.concatenate([jnp.where(m, x, 0.0) for m in masks], axis=0)


def _unstack_heads(x, masks, t):
    out = jnp.where(masks[0], x[0:t], 0.0)
    for h in range(1, N_HEADS):
        out = out + jnp.where(masks[h], x[h * t:(h + 1) * t], 0.0)
    return out


def _block_ones():
    shift = HEAD_DIM.bit_length() - 1
    r = lax.broadcasted_iota(jnp.int32, (GROUP_W, GROUP_W), 0) >> shift
    c = lax.broadcasted_iota(jnp.int32, (GROUP_W, GROUP_W), 1) >> shift
    return r == c


def _head_sum(x, ones_bf16):
    hi = x.astype(BF16)
    lo = (x - hi.astype(F32)).astype(BF16)
    return (jnp.dot(hi, ones_bf16, preferred_element_type=F32)
            + jnp.dot(lo, ones_bf16, preferred_element_type=F32))


def _adaln_kernel(c_ref, w_ref, b_ref, o_ref):
    s = _silu(c_ref[...]).astype(BF16)
    o_ref[0, 0] = jnp.dot(s, w_ref[0].astype(BF16), preferred_element_type=F32) + b_ref[0]


def _adaln(cvecs, w_ada, b_ada):
    depth, d, d3 = w_ada.shape
    tn = 512
    per = d // tn
    return pl.pallas_call(
        _adaln_kernel,
        grid=(depth, d3 // tn),
        in_specs=[pl.BlockSpec((ADA_ROWS, d), lambda l, j: (0, 0)),
                  pl.BlockSpec((1, d, tn), lambda l, j: (l, 0, j)),
                  pl.BlockSpec((1, 1, tn), lambda l, j: (l, 0, j))],
        out_specs=pl.BlockSpec((1, 1, ADA_ROWS, tn), lambda l, j: (l, j // per, 0, j % per)),
        out_shape=jax.ShapeDtypeStruct((depth, 3, ADA_ROWS, d), F32),
        compiler_params=_cparams(("arbitrary", "arbitrary")),
        name="adaln",
    )(cvecs, w_ada, b_ada.reshape(depth, 1, d3))


def _rope(x, cos, sin):
    w = x.shape[-1]
    lane = lax.broadcasted_iota(jnp.int32, x.shape, 1)
    first = (lane & 31) < 16
    swapped = jnp.where(first, pltpu.roll(x, w - 16, 1), pltpu.roll(x, 16, 1))
    return x * cos[:, :w] + swapped * sin[:, :w]


def _mod_row(mod_ref, mod_row):
    row = pl.program_id(0) if mod_row is None else mod_row
    return mod_ref[0, 0, pl.ds(row, 1), :]


def _inproj_kernel(*refs, rope, layer, mod_row):
    if rope:
        x_ref, shift_ref, scale_ref, g_ref, w_ref, cos_ref, sin_ref, pa_ref, pb_ref, pc_ref, pd_ref = refs
    else:
        x_ref, shift_ref, scale_ref, g_ref, w_ref, pa_ref, pb_ref, pc_ref, pd_ref = refs
    x = x_ref[0]
    ms = jnp.mean(x * x, axis=-1, keepdims=True)
    y = x * lax.rsqrt(ms + EPS) * g_ref[layer:layer + 1, :]
    h = (y * (1.0 + _mod_row(scale_ref, mod_row)) + _mod_row(shift_ref, mod_row)).astype(BF16)

    def mm(c0, c1):
        return jnp.dot(h, w_ref[0, :, c0:c1], preferred_element_type=F32)

    if rope:
        cos = cos_ref[...]
        sin = sin_ref[...]
    g = GROUP_W
    aq = mm(0, g)
    akv = mm(g, 2 * g)
    if rope:
        aq = _rope(aq, cos, sin)
        ak = _rope(akv[:, :g // 2], cos, sin)
        akv = jnp.concatenate([ak, akv[:, g // 2:]], axis=1)
    pa_ref[0, :, 0:g] = aq.astype(pa_ref.dtype)
    pa_ref[0, :, g:2 * g] = akv.astype(pa_ref.dtype)
    pa_ref[0, :, 2 * g:3 * g] = mm(2 * g, 3 * g).astype(pa_ref.dtype)
    pb_ref[0] = mm(W_A, W_A + W_B).astype(pb_ref.dtype)
    c0 = W_A + W_B
    cq = mm(c0, c0 + g)
    ck = mm(c0 + g, c0 + 2 * g)
    if rope:
        cq = _rope(cq, cos, sin)
        ck = _rope(ck, cos, sin)
    pc_ref[0, :, 0:g] = cq.astype(pc_ref.dtype)
    pc_ref[0, :, g:2 * g] = ck.astype(pc_ref.dtype)
    pc_ref[0, :, 2 * g:4 * g] = mm(c0 + 2 * g, c0 + 4 * g).astype(pc_ref.dtype)
    d0 = c0 + W_C
    pd_ref[0] = mm(d0, d0 + W_D).astype(pd_ref.dtype)


def _rope_tables(slen):
    t = np.arange(slen)
    nf = HEAD_DIM // 4
    freqs = ROPE_BASE ** (-np.arange(nf, dtype=np.float64) / nf)
    d = np.arange(HEAD_DIM)
    pos = np.where(d[None, :] < HEAD_DIM // 2, (t // GRID_W)[:, None], (t % GRID_W)[:, None])
    ang = pos * freqs[d % nf][None, :]
    sign = np.where((d % (2 * nf)) < nf, -1.0, 1.0)[None, :]
    cos = np.tile(np.cos(ang), (1, N_HEADS))
    sin = np.tile(np.sin(ang) * sign, (1, N_HEADS))
    return jnp.asarray(cos, F32), jnp.asarray(sin, F32)


def _mod_spec(which, layer):
    return pl.BlockSpec((1, 1, ADA_ROWS, D_MODEL), lambda b, i: (layer, which, 0, 0))


def _in_proj(x, mods, layer, mod_row, g_pre, w_in_bf16, rope, tm=256):
    bsz, slen, d = x.shape
    depth = g_pre.shape[0]
    in_specs = [pl.BlockSpec((1, tm, d), lambda b, i: (b, i, 0)),
                _mod_spec(0, layer),
                _mod_spec(1, layer),
                pl.BlockSpec((depth, d), lambda b, i: (0, 0)),
                pl.BlockSpec((1, d, IN_WIDTH), lambda b, i: (layer, 0, 0))]
    args = [x, mods, mods, g_pre, w_in_bf16]
    if rope:
        cos, sin = _rope_tables(slen)
        in_specs += [pl.BlockSpec((tm, GROUP_W), lambda b, i: (i, 0))] * 2
        args += [cos, sin]
    widths = (W_A, W_B, W_C, W_D)
    return pl.pallas_call(
        functools.partial(_inproj_kernel, rope=rope, layer=layer, mod_row=mod_row),
        grid=(bsz, slen // tm),
        in_specs=in_specs,
        out_specs=[pl.BlockSpec((1, tm, w), lambda b, i: (b, i, 0)) for w in widths],
        out_shape=[jax.ShapeDtypeStruct((bsz, slen, w), F32) for w in widths],
        compiler_params=_cparams(("arbitrary", "arbitrary")),
        name="in_proj",
    )(*args)


ATTN_ROWS = 256


def _attn_kernel(*refs, n_kv, qb, kw, back, slen, has_ctx, n_tab, tab_heads, has_sink):
    it = iter(refs)
    q_ref, k_ref, v_ref, g_ref = next(it), next(it), next(it), next(it)
    kc_ref = vc_ref = tab_ref = sink_ref = None
    if has_ctx:
        kc_ref, vc_ref = next(it), next(it)
    if n_tab:
        tab_ref = next(it)
    if has_sink:
        sink_ref = next(it)
    o_ref = next(it)

    n = pl.program_id(1)
    nblk = slen // qb
    ws = pl.multiple_of(jnp.clip(n * qb - back, 0, slen - kw), HEAD_DIM)
    if n_tab == 3:
        tix = jnp.where(n == 0, 0, jnp.where(n == nblk - 1, 2, 1))
    else:
        tix = 0
    grp = N_HEADS // n_kv
    qscale = HEAD_DIM ** -0.5 * LOG2E
    rb = min(qb, ATTN_ROWS // grp)
    for kh in range(n_kv):
        heads = range(kh * grp, (kh + 1) * grp)
        hsl = [slice(h * HEAD_DIM, (h + 1) * HEAD_DIM) for h in heads]
        ks = slice(kh * HEAD_DIM, (kh + 1) * HEAD_DIM)
        kl = k_ref[0, pl.ds(ws, kw), ks].astype(BF16)
        vl = v_ref[0, pl.ds(ws, kw), ks].astype(BF16)
        if has_ctx:
            kc = kc_ref[0, 0, :, ks].astype(BF16)
            vc = vc_ref[0, 0, :, ks].astype(BF16)
        for r0 in range(0, qb, rb):
            rs = slice(r0, r0 + rb)
            q = (jnp.concatenate([q_ref[0, rs, sl] for sl in hsl], axis=0) * qscale).astype(BF16)
            s = lax.dot_general(q, kl, NT_DIMS, preferred_element_type=F32)
            if n_tab:
                s = s + jnp.concatenate(
                    [tab_ref[0, tix, h if tab_heads == N_HEADS else 0, rs, :] for h in heads], axis=0)
            m = jnp.max(s, axis=-1, keepdims=True)
            if has_ctx:
                sc = lax.dot_general(q, kc, NT_DIMS, preferred_element_type=F32)
                m = jnp.maximum(m, jnp.max(sc, axis=-1, keepdims=True))
            if has_sink:
                sink = jnp.concatenate([jnp.full((rb, 1), sink_ref[h] * LOG2E, F32) for h in heads], axis=0)
                m = jnp.maximum(m, sink)
            p = jnp.exp2(s - m)
            den = jnp.sum(p, axis=-1, keepdims=True)
            acc = jnp.dot(p.astype(BF16), vl, preferred_element_type=F32)
            if has_ctx:
                pc = jnp.exp2(sc - m)
                den = den + jnp.sum(pc, axis=-1, keepdims=True)
                acc = acc + jnp.dot(pc.astype(BF16), vc, preferred_element_type=F32)
            if has_sink:
                den = den + jnp.exp2(sink - m)
            o = acc / den
            for gi, sl in enumerate(hsl):
                o_ref[0, rs, sl] = (o[gi * rb:(gi + 1) * rb] * _silu(g_ref[0, rs, sl])).astype(o_ref.dtype)


def _attention(p, q_col, k_col, v_col, g_col, n_kv, qb, kw, back, ctx_k=None, ctx_v=None,
               table=None, sink=None, layer=0):
    bsz, slen, _ = p.shape
    kvw = n_kv * HEAD_DIM
    in_specs = [pl.BlockSpec((1, qb, GROUP_W), lambda b, n: (b, n, q_col)),
                pl.BlockSpec((1, slen, kvw), lambda b, n: (b, 0, k_col)),
                pl.BlockSpec((1, slen, kvw), lambda b, n: (b, 0, v_col)),
                pl.BlockSpec((1, qb, GROUP_W), lambda b, n: (b, n, g_col))]
    args = [p, p, p, p]
    has_ctx = ctx_k is not None
    if has_ctx:
        past = ctx_k.shape[2]
        in_specs += [pl.BlockSpec((1, 1, past, kvw), lambda b, n: (b, layer, 0, 0))] * 2
        args += [ctx_k, ctx_v]
    n_tab = tab_heads = 0
    if table is not None:
        n_tab, tab_heads = table.shape[1], table.shape[2]
        tab_layer = layer if table.shape[0] > 1 else 0
        in_specs.append(pl.BlockSpec((1,) + table.shape[1:], lambda b, n: (tab_layer, 0, 0, 0, 0)))
        args.append(table)
    if sink is not None:
        in_specs.append(pl.BlockSpec(memory_space=pltpu.SMEM))
        args.append(sink)
    kern = functools.partial(_attn_kernel, n_kv=n_kv, qb=qb, kw=kw, back=back, slen=slen, has_ctx=has_ctx,
                             n_tab=n_tab, tab_heads=tab_heads, has_sink=sink is not None)
    return pl.pallas_call(
        kern,
        grid=(bsz, slen // qb),
        in_specs=in_specs,
        out_specs=pl.BlockSpec((1, qb, GROUP_W), lambda b, n: (b, n, 0)),
        out_shape=jax.ShapeDtypeStruct((bsz, slen, GROUP_W), BF16),
        compiler_params=_cparams(("arbitrary", "arbitrary")),
        name="attn",
    )(*args)


WIN_QB = 256
WIN_BACK = WINDOW
WIN_KW = WIN_QB + 2 * WINDOW
NA_QROWS = 4
NA_KROWS = 12
NA_QB = NA_QROWS * GRID_W
NA_KW = NA_KROWS * GRID_W


def _window_table(slen):
    nblk = slen // WIN_QB
    tabs = []
    for n in (0, 1, nblk - 1):
        ws = int(np.clip(n * WIN_QB - WIN_BACK, 0, slen - WIN_KW))
        qpos = n * WIN_QB + np.arange(WIN_QB)[:, None]
        kpos = ws + np.arange(WIN_KW)[None, :]
        tabs.append(np.where(np.abs(qpos - kpos) <= WINDOW, 0.0, NEG))
    return jnp.asarray(np.stack(tabs)[None, :, None], F32)


N_RPB_R = 2 * NA_ROWS - 1
N_RPB_C = 2 * NA_COLS - 1


def _na_table_kernel(rpb_ref, o_ref, tz_scr, *, rows):
    base = (pl.program_id(0) * N_HEADS + pl.program_id(1)) * (N_RPB_R * N_RPB_C)
    qc = lax.broadcasted_iota(jnp.int32, (GRID_W, 2 * GRID_W), 0)
    kk = lax.broadcasted_iota(jnp.int32, (GRID_W, 2 * GRID_W), 1)
    kc = kk & (GRID_W - 1)
    diff = kc - qc
    qws = jnp.clip(qc - NA_COLS // 2, 0, GRID_W - NA_COLS)
    col_ok = (kc >= qws) & (kc < qws + NA_COLS)
    neg = jnp.full((GRID_W, 2 * GRID_W), NEG, F32)
    for dr in range(N_RPB_R):
        acc = neg
        for m in range(N_RPB_C):
            acc = jnp.where(diff == m - (NA_COLS - 1), rpb_ref[base + dr * N_RPB_C + m] * LOG2E, acc)
        tz_scr[dr] = jnp.where(col_ok, acc, NEG)
    wr = min(NA_ROWS, rows)
    nblk = rows // NA_QROWS
    for ti, g in enumerate((0, 1, nblk - 1)):
        ws_row = min(max(g * NA_QROWS - NA_QROWS, 0), rows - NA_KROWS)
        for qr in range(NA_QROWS):
            r = g * NA_QROWS + qr
            rs = min(max(r - wr // 2, 0), rows - wr)
            for p in range(NA_KROWS // 2):
                halves = []
                for kr in (ws_row + 2 * p, ws_row + 2 * p + 1):
                    halves.append(tz_scr[kr - r + NA_ROWS - 1] if rs <= kr < rs + wr else neg)
                o_ref[0, ti, 0, qr * GRID_W:(qr + 1) * GRID_W, p * 2 * GRID_W:(p + 1) * 2 * GRID_W] = (
                    jnp.where(kk < GRID_W, halves[0], halves[1]))


def _na_tables(na_rpb, slen):
    depth = na_rpb.shape[0]
    return pl.pallas_call(
        functools.partial(_na_table_kernel, rows=slen // GRID_W),
        grid=(depth, N_HEADS),
        in_specs=[pl.BlockSpec(memory_space=pltpu.SMEM)],
        out_specs=pl.BlockSpec((1, 3, 1, NA_QB, NA_KW), lambda l, h: (l, 0, h, 0, 0)),
        out_shape=jax.ShapeDtypeStruct((depth, 3, N_HEADS, NA_QB, NA_KW), F32),
        scratch_shapes=[pltpu.VMEM((N_RPB_R, GRID_W, 2 * GRID_W), F32)],
        compiler_params=_cparams(("arbitrary", "arbitrary")),
        name="na_table",
    )(na_rpb.astype(F32).reshape(-1))


def _ret_kernel(q_ref, k_ref, v_ref, g_ref, lgl_ref, lgc_ref, *rest, slen, t, has_s0):
    if has_s0:
        s0_ref, o_ref, sb_scr, st_scr = rest
    else:
        o_ref, sfin_ref, sb_scr, st_scr = rest
    nc = slen // t
    masks = _head_masks(t)
    bmask = _block_ones()
    ones_bf16 = jnp.where(bmask, 1.0, 0.0).astype(BF16)
    ii = lax.broadcasted_iota(jnp.int32, (t, N_HEADS * t), 0)
    jj = lax.broadcasted_iota(jnp.int32, (t, N_HEADS * t), 1) & (t - 1)
    dist = (ii - jj).astype(F32)
    idx = lax.broadcasted_iota(jnp.int32, (t, GROUP_W), 0).astype(F32)
    lgf, lgb = lgl_ref[0], lgl_ref[1]
    dmat = (jnp.where(dist >= 0, jnp.exp(dist * lgc_ref[0]), 0.0)
            + jnp.where(dist <= 0, jnp.exp(-dist * lgc_ref[1]), 0.0))
    qdec_f = jnp.exp((idx + 1.0) * lgf)
    kdec_f = jnp.exp((t - 1.0 - idx) * lgf)
    qdec_b = jnp.exp((t - idx) * lgb)
    kdec_b = jnp.exp(idx * lgb)
    cdec_f = jnp.exp(float(t) * lgf)
    cdec_b = jnp.exp(float(t) * lgb)
    kscale = HEAD_DIM ** -0.5

    def state_update(st, k, v, kdec, cdec):
        upd = lax.dot_general(v, (k * kdec).astype(BF16), TN_DIMS, preferred_element_type=F32)
        return st * cdec + jnp.where(bmask, upd, 0.0)

    def init_state(direction):
        if has_s0:
            st_scr[...] = _expand_state(s0_ref[0, direction], bmask)
        else:
            st_scr[...] = jnp.zeros((GROUP_W, GROUP_W), F32)

    def emit_final(direction):
        if not has_s0:
            sfin_ref[0, direction] = _compact_state(st_scr[...], bmask)

    init_state(1)

    def sweep_bwd(ci, carry):
        c = nc - 1 - ci
        rows = pl.ds(pl.multiple_of(c * t, t), t)
        st = st_scr[...]
        sb_scr[c] = st.astype(BF16)
        k = k_ref[0, rows, :] * kscale
        st_scr[...] = state_update(st, k, v_ref[0, rows, :].astype(BF16), kdec_b, cdec_b)
        return carry

    lax.fori_loop(0, nc, sweep_bwd, 0)
    emit_final(1)
    init_state(0)

    def sweep_fwd(c, carry):
        rows = pl.ds(pl.multiple_of(c * t, t), t)
        q = q_ref[0, rows, :]
        k = k_ref[0, rows, :] * kscale
        v = v_ref[0, rows, :]
        ks = _stack_heads(k, masks).astype(BF16)
        vs = _stack_heads(v, masks).astype(BF16)
        sc = lax.dot_general(q.astype(BF16), ks, NT_DIMS, preferred_element_type=F32) * dmat
        o = jnp.dot(sc.astype(BF16), vs, preferred_element_type=F32)
        st = st_scr[...]
        o = o + lax.dot_general((q * qdec_f).astype(BF16), st.astype(BF16), NT_DIMS, preferred_element_type=F32)
        o = o + lax.dot_general((q * qdec_b).astype(BF16), sb_scr[c], NT_DIMS, preferred_element_type=F32)
        st_scr[...] = state_update(st, k, v.astype(BF16), kdec_f, cdec_f)
        mu = _head_sum(o, ones_bf16) * (1.0 / HEAD_DIM)
        d = o - mu
        var = _head_sum(d * d, ones_bf16) * (1.0 / HEAD_DIM)
        y = d * lax.rsqrt(var + EPS)
        o_ref[0, rows, :] = (y * _silu(g_ref[0, rows, :])).astype(o_ref.dtype)
        return carry

    lax.fori_loop(0, nc, sweep_fwd, 0, unroll=2)
    emit_final(0)


def _compact_states(s0):
    return jnp.swapaxes(s0, -1, -2).reshape(s0.shape[:-3] + (GROUP_W, HEAD_DIM))


def _uncompact_states(sc):
    bsz = sc.shape[0]
    return jnp.swapaxes(sc.reshape(bsz, 2, N_HEADS, HEAD_DIM, HEAD_DIM), -1, -2)


def _split3(x):
    h1 = x.astype(BF16)
    r1 = x - h1.astype(F32)
    h2 = r1.astype(BF16)
    return h1, h2, (r1 - h2.astype(F32)).astype(BF16)


def _expand_state(x, bmask):
    r = lax.broadcasted_iota(jnp.int32, (HEAD_DIM, GROUP_W), 0)
    c = lax.broadcasted_iota(jnp.int32, (HEAD_DIM, GROUP_W), 1) & (HEAD_DIM - 1)
    rep = jnp.where(r == c, 1.0, 0.0).astype(BF16)
    h1, h2, h3 = _split3(x)
    tiled = (jnp.dot(h1, rep, preferred_element_type=F32) + jnp.dot(h2, rep, preferred_element_type=F32)
             + jnp.dot(h3, rep, preferred_element_type=F32))
    return jnp.where(bmask, tiled, 0.0)


def _compact_state(st, bmask):
    r = lax.broadcasted_iota(jnp.int32, (GROUP_W, HEAD_DIM), 0) & (HEAD_DIM - 1)
    c = lax.broadcasted_iota(jnp.int32, (GROUP_W, HEAD_DIM), 1)
    fold = jnp.where(r == c, 1.0, 0.0).astype(BF16)
    h1, h2, h3 = _split3(jnp.where(bmask, st, 0.0))
    return (jnp.dot(h1, fold, preferred_element_type=F32) + jnp.dot(h2, fold, preferred_element_type=F32)
            + jnp.dot(h3, fold, preferred_element_type=F32))


def _state_specs(bsz, s0):
    spec = pl.BlockSpec((1, 2, GROUP_W, HEAD_DIM), lambda b: (b, 0, 0, 0))
    shape = jax.ShapeDtypeStruct((bsz, 2, GROUP_W, HEAD_DIM), F32)
    if s0 is None:
        return [], [], [spec], [shape]
    return [spec], [s0], [], []


RET_CHUNK = 128


def _retention(pc, log_gamma, s0=None, t=RET_CHUNK):
    bsz, slen, _ = pc.shape
    lgl = jnp.repeat(log_gamma, HEAD_DIM, axis=1).reshape(2, 1, GROUP_W)
    lgc = jnp.repeat(log_gamma, t, axis=1).reshape(2, 1, N_HEADS * t)
    col = lambda j: pl.BlockSpec((1, slen, GROUP_W), lambda b: (b, 0, j))
    s_in_specs, s_args, s_out_specs, s_out_shape = _state_specs(bsz, s0)
    return pl.pallas_call(
        functools.partial(_ret_kernel, slen=slen, t=t, has_s0=s0 is not None),
        grid=(bsz,),
        in_specs=[col(0), col(1), col(2), col(3),
                  pl.BlockSpec((2, 1, GROUP_W), lambda b: (0, 0, 0)),
                  pl.BlockSpec((2, 1, N_HEADS * t), lambda b: (0, 0, 0))] + s_in_specs,
        out_specs=[pl.BlockSpec((1, slen, GROUP_W), lambda b: (b, 0, 0))] + s_out_specs,
        out_shape=[jax.ShapeDtypeStruct((bsz, slen, GROUP_W), BF16)] + s_out_shape,
        scratch_shapes=[pltpu.VMEM((slen // t, GROUP_W, GROUP_W), BF16), pltpu.VMEM((GROUP_W, GROUP_W), F32)],
        compiler_params=_cparams(("arbitrary",)),
        name="retention",
    )(pc, pc, pc, pc, lgl, lgc, *s_args)


HGRN_DIRECT = 8


def _split2(x):
    hi = x.astype(BF16)
    return hi, (x - hi.astype(F32)).astype(BF16)


def _hgrn_kernel(q_ref, zf_ref, zb_ref, v_ref, g_ref, lb_ref, *rest, slen, t, has_s0):
    if has_s0:
        s0_ref, o_ref, of_scr, st_scr = rest
    else:
        o_ref, sfin_ref, of_scr, st_scr = rest
    nc = slen // t
    masks = _head_masks(t)
    bmask = _block_ones()
    ones_bf16 = jnp.where(bmask, 1.0, 0.0).astype(BF16)
    r_i = lax.broadcasted_iota(jnp.int32, (t, t), 0)
    c_i = lax.broadcasted_iota(jnp.int32, (t, t), 1)
    sub = lax.broadcasted_iota(jnp.int32, (t, GROUP_W), 0) & (HGRN_DIRECT - 1)
    levels = []
    s = t // 2
    while s >= HGRN_DIRECT:
        levels.append(s)
        s //= 2
    r_w = lax.broadcasted_iota(jnp.int32, (t, N_HEADS * t), 0)
    c_w = lax.broadcasted_iota(jnp.int32, (t, N_HEADS * t), 1) & (t - 1)
    lvl_masks = []
    for s in levels:
        shift = (2 * s).bit_length() - 1
        lvl_masks.append(jnp.where((r_w >> shift) == (c_w >> shift), 1.0, 0.0))

    def block_roll(x, shift):
        x3 = x.reshape(t // HGRN_DIRECT, HGRN_DIRECT, GROUP_W)
        return pltpu.roll(x3, shift, 1).reshape(t, GROUP_W)

    def run(direction, z_ref, emit):
        fwd = direction == 0
        lbd = jnp.maximum(lb_ref[direction], LB_FLOOR)
        tri = jnp.where((r_i >= c_i) if fwd else (r_i <= c_i), 1.0, 0.0).astype(BF16)
        if has_s0:
            st_scr[...] = _expand_state(s0_ref[0, direction], bmask)
        else:
            st_scr[...] = jnp.zeros((GROUP_W, GROUP_W), F32)

        def body(ci, carry):
            c = ci if fwd else nc - 1 - ci
            rows = pl.ds(pl.multiple_of(c * t, t), t)
            q = _silu(q_ref[0, rows, :])
            z = z_ref[0, rows, :]
            v = v_ref[0, rows, :]
            vb = v.astype(BF16)
            a = jnp.exp(-jnp.abs(z))
            inv = 1.0 / (1.0 + a)
            pos = z >= 0
            sig = jnp.where(pos, inv, a * inv)
            nsig = jnp.where(pos, a * inv, inv)
            lf = jnp.log(lbd + (1.0 - lbd) * sig)
            k = (1.0 - lbd) * nsig
            h1, h2 = _split2(lf)
            b = jnp.dot(tri, h1, preferred_element_type=F32) + jnp.dot(tri, h2, preferred_element_type=F32)
            b = b * LOG2E
            vs = _stack_heads(v, masks).astype(BF16)

            sc = None
            for s, lm in zip(levels, lvl_masks):
                pa, pb = [], []
                zero = jnp.zeros((s, GROUP_W), F32)
                for gi in range(t // (2 * s)):
                    b_lo = b[gi * 2 * s:gi * 2 * s + s]
                    b_hi = b[gi * 2 * s + s:(gi + 1) * 2 * s]
                    if fwd:
                        anc = b_hi[0:1]
                        pa += [zero, jnp.exp2(b_hi - anc)]
                        pb += [jnp.exp2(anc - b_lo), zero]
                    else:
                        anc = b_lo[s - 1:s]
                        pa += [jnp.exp2(b_lo - anc), zero]
                        pb += [zero, jnp.exp2(anc - b_hi)]
                qa = (q * jnp.concatenate(pa, axis=0)).astype(BF16)
                kb = _stack_heads(k * jnp.concatenate(pb, axis=0), masks).astype(BF16)
                term = lax.dot_general(qa, kb, NT_DIMS, preferred_element_type=F32) * lm
                sc = term if sc is None else sc + term
            o = None
            if sc is not None:
                o = jnp.dot(sc.astype(BF16), vs, preferred_element_type=F32)

            prods, vals = [(q * k).astype(BF16)], [v]
            for dlt in range(1, HGRN_DIRECT):
                sh = dlt if fwd else HGRN_DIRECT - dlt
                ok = (sub >= dlt) if fwd else (sub + dlt <= HGRN_DIRECT - 1)
                w = jnp.exp2(jnp.minimum(b - block_roll(b, sh), 0.0))
                prods.append(jnp.where(ok, q * block_roll(k, sh) * w, 0.0).astype(BF16))
                vals.append(block_roll(v, sh))
            hs = jnp.dot(jnp.concatenate(prods, axis=0), ones_bf16, preferred_element_type=F32)
            for dlt in range(HGRN_DIRECT):
                term = hs[dlt * t:(dlt + 1) * t] * vals[dlt]
                o = term if o is None else o + term

            st = st_scr[...]
            o = o + lax.dot_general((q * jnp.exp2(b)).astype(BF16), st.astype(BF16), NT_DIMS,
                                    preferred_element_type=F32)
            bl = b[t - 1:t] if fwd else b[0:1]
            upd = lax.dot_general(vb, (k * jnp.exp2(bl - b)).astype(BF16), TN_DIMS,
                                  preferred_element_type=F32)
            st_scr[...] = st * jnp.exp2(bl) + jnp.where(bmask, upd, 0.0)
            emit(rows, o)
            return carry

        lax.fori_loop(0, nc, body, 0, unroll=2)
        if not has_s0:
            sfin_ref[0, direction] = _compact_state(st_scr[...], bmask)

    def emit_fwd(rows, o):
        of_scr[rows, :] = o

    def emit_bwd(rows, o):
        o = o + of_scr[rows, :]
        ms = _head_sum(o * o, ones_bf16) * (1.0 / HEAD_DIM)
        y = o * lax.rsqrt(ms + EPS)
        o_ref[0, rows, :] = (y * _silu(g_ref[0, rows, :])).astype(o_ref.dtype)

    run(0, zf_ref, emit_fwd)
    run(1, zb_ref, emit_bwd)


HGRN_CHUNK = 128


def _hgrn(pd, lb, s0=None, t=HGRN_CHUNK):
    bsz, slen, _ = pd.shape
    col = lambda j: pl.BlockSpec((1, slen, GROUP_W), lambda b: (b, 0, j))
    s_in_specs, s_args, s_out_specs, s_out_shape = _state_specs(bsz, s0)
    return pl.pallas_call(
        functools.partial(_hgrn_kernel, slen=slen, t=t, has_s0=s0 is not None),
        grid=(bsz,),
        in_specs=[col(0), col(1), col(2), col(3), col(4),
                  pl.BlockSpec((2, 1, GROUP_W), lambda b: (0, 0, 0))] + s_in_specs,
        out_specs=[pl.BlockSpec((1, slen, GROUP_W), lambda b: (b, 0, 0))] + s_out_specs,
        out_shape=[jax.ShapeDtypeStruct((bsz, slen, GROUP_W), BF16)] + s_out_shape,
        scratch_shapes=[pltpu.VMEM((slen, GROUP_W), F32), pltpu.VMEM((GROUP_W, GROUP_W), F32)],
        compiler_params=_cparams(("arbitrary",)),
        name="hgrn2",
    )(pd, pd, pd, pd, pd, lb.reshape(2, 1, GROUP_W), *s_args)


def _outproj_kernel(ma_ref, mb_ref, mc_ref, md_ref, w_ref, x_ref, gate_ref, g_ref, o_ref, *, layer, mod_row):
    g = GROUP_W
    y = jnp.dot(ma_ref[0], w_ref[0, 0:g, :], preferred_element_type=F32)
    y = y + jnp.dot(mb_ref[0], w_ref[0, g:2 * g, :], preferred_element_type=F32)
    y = y + jnp.dot(mc_ref[0], w_ref[0, 2 * g:3 * g, :], preferred_element_type=F32)
    y = y + jnp.dot(md_ref[0], w_ref[0, 3 * g:4 * g, :], preferred_element_type=F32)
    ms = jnp.mean(y * y, axis=-1, keepdims=True)
    r = y * lax.rsqrt(ms + EPS) * g_ref[layer:layer + 1, :]
    o_ref[0] = x_ref[0] + _mod_row(gate_ref, mod_row) * r


OUTPROJ_TM = 1024


def _out_proj(mixed, w_out_bf16, x, mods, layer, mod_row, g_post):
    bsz, slen, d = x.shape
    depth = g_post.shape[0]
    tm = min(OUTPROJ_TM, slen)
    mspec = pl.BlockSpec((1, tm, GROUP_W), lambda b, i: (b, i, 0))
    return pl.pallas_call(
        functools.partial(_outproj_kernel, layer=layer, mod_row=mod_row),
        grid=(bsz, slen // tm),
        in_specs=[mspec, mspec, mspec, mspec,
                  pl.BlockSpec((1, d, d), lambda b, i: (layer, 0, 0)),
                  pl.BlockSpec((1, tm, d), lambda b, i: (b, i, 0)),
                  _mod_spec(2, layer),
                  pl.BlockSpec((depth, d), lambda b, i: (0, 0))],
        out_specs=pl.BlockSpec((1, tm, d), lambda b, i: (b, i, 0)),
        out_shape=jax.ShapeDtypeStruct((bsz, slen, d), F32),
        compiler_params=_cparams(("arbitrary", "arbitrary")),
        name="out_proj",
    )(*mixed, w_out_bf16, x, mods, g_post)


def _context_layer(x, mods, layer, mod_row, g_pre, g_post, w_in, w_out, sink, log_gamma, lb):
    bsz, slen, _ = x.shape
    pa, pb, pc, pd = _in_proj(x, mods, layer, mod_row, g_pre, w_in, rope=False)
    o_a = _attention(pa, 0, 2, 3, 2, n_kv=A_KV, qb=slen, kw=slen, back=0, sink=sink)
    o_b = _attention(pb, 0, 1, 2, 3, n_kv=N_HEADS, qb=slen, kw=slen, back=0)
    o_c, s_c = _retention(pc, log_gamma)
    o_d, s_d = _hgrn(pd, lb)
    x = _out_proj((o_a, o_b, o_c, o_d), w_out, x, mods, layer, mod_row, g_post)
    g = GROUP_W
    ak = pa[:, :, g:g + g // 2].reshape(bsz, slen, A_KV, HEAD_DIM)
    av = pa[:, :, g + g // 2:2 * g].reshape(bsz, slen, A_KV, HEAD_DIM)
    bk = pb[:, :, g:2 * g].reshape(bsz, slen, N_HEADS, HEAD_DIM)
    bv = pb[:, :, 2 * g:3 * g].reshape(bsz, slen, N_HEADS, HEAD_DIM)
    return x, (ak, av, bk, bv, _uncompact_states(s_c), _uncompact_states(s_d))


def _latent_layer(x, mods, layer, ca_k, ca_v, cb_k, cb_v, st_c, st_d, g_pre, g_post, w_in, w_out, sink, win_tab,
                  na_tabs, log_gamma, lb):
    pa, pb, pc, pd = _in_proj(x, mods, layer, None, g_pre, w_in, rope=True)
    o_a = _attention(pa, 0, 2, 3, 2, n_kv=A_KV, qb=WIN_QB, kw=WIN_KW, back=WIN_BACK,
                     ctx_k=ca_k, ctx_v=ca_v, table=win_tab, sink=sink, layer=layer)
    o_b = _attention(pb, 0, 1, 2, 3, n_kv=N_HEADS, qb=NA_QB, kw=NA_KW, back=NA_QB,
                     ctx_k=cb_k, ctx_v=cb_v, table=na_tabs, layer=layer)
    o_c, = _retention(pc, log_gamma, st_c)
    o_d, = _hgrn(pd, lb, st_d)
    return _out_proj((o_a, o_b, o_c, o_d), w_out, x, mods, layer, None, g_post)


def kernel(x_prompt, x_sample, c, cache_win_k, cache_win_v, cache_na_k, cache_na_v, state_ret, state_hgrn,
           c_ctx, w_ada, b_ada, g_pre, g_post, w_in, w_out, attn_sink, na_rpb, ret_decay_logit, hgrn_lb_logit):
    depth = w_ada.shape[0]
    dec_b, dec_s, d = x_sample.shape
    p_lb = jax.nn.softmax(hgrn_lb_logit.astype(F32), axis=0)
    lower_bounds = jnp.cumsum(p_lb, axis=0) - p_lb[0:1]
    log_gammas = jax.nn.log_sigmoid(ret_decay_logit.astype(F32))
    w_in_b = w_in.astype(BF16)
    w_out_b = w_out.astype(BF16)

    cvecs = jnp.zeros((ADA_ROWS, d), F32).at[:dec_b].set(c).at[dec_b].set(c_ctx)
    mods = _adaln(cvecs, w_ada, b_ada)
    win_tab = _window_table(dec_s)
    na_tabs = _na_tables(na_rpb, dec_s)

    x = x_prompt
    outs = [[] for _ in range(6)]
    for l in range(depth):
        x, extra = _context_layer(x, mods, l, dec_b, g_pre, g_post, w_in_b, w_out_b,
                                  attn_sink[l], log_gammas[l], lower_bounds[l])
        for acc, e in zip(outs, extra):
            acc.append(e)
    y_prompt = x
    stacked = [jnp.stack(o, axis=1) for o in outs]

    past = cache_win_k.shape[2]
    ca_k = cache_win_k.reshape(dec_b, depth, past, A_KV * HEAD_DIM)
    ca_v = cache_win_v.reshape(dec_b, depth, past, A_KV * HEAD_DIM)
    cb_k = cache_na_k.reshape(dec_b, depth, past, GROUP_W)
    cb_v = cache_na_v.reshape(dec_b, depth, past, GROUP_W)
    st_c = _compact_states(state_ret)
    st_d = _compact_states(state_hgrn)
    x = x_sample
    for l in range(depth):
        x = _latent_layer(x, mods, l, ca_k, ca_v, cb_k, cb_v, st_c[:, l], st_d[:, l], g_pre, g_post,
                          w_in_b, w_out_b, attn_sink[l], win_tab, na_tabs, log_gammas[l], lower_bounds[l])
    return (y_prompt, x, *stacked)
```

```python
import functools

import numpy as np
import jax
import jax.numpy as jnp
from jax import lax
from jax.experimental import pallas as pl
from jax.experimental.pallas import tpu as pltpu

F32 = jnp.float32
BF16 = jnp.bfloat16

D_MODEL = 1024
DEPTH = 4
GRID_W = 64
HEAD_DIM = 64
N_HEADS = 4
GROUP_W = N_HEADS * HEAD_DIM
A_KV = 2
WINDOW = 128
NA_ROWS = 8
NA_COLS = 16
ROPE_BASE = 10000.0
EPS = 1e-6
NEG = -1e30
LB_FLOOR = 1e-30
LOG2E = 1.4426950408889634
W_A = 3 * GROUP_W
W_B = 4 * GROUP_W
W_C = 4 * GROUP_W
W_D = 5 * GROUP_W
IN_WIDTH = W_A + W_B + W_C + W_D

V7X_VMEM_LIMIT_BYTES = 56 * 1024 * 1024
ADA_ROWS = 16

NT_DIMS = (((1,), (1,)), ((), ()))
TN_DIMS = (((0,), (0,)), ((), ()))


def _cparams(sem, flags=None):
    return pltpu.CompilerParams(dimension_semantics=sem, vmem_limit_bytes=V7X_VMEM_LIMIT_BYTES, flags=flags)


def _sigmoid(x):
    return 1.0 / (1.0 + jnp.exp(-x))


def _silu(x):
    return x * _sigmoid(x)


def _head_masks(rows):
    lane = lax.broadcasted_iota(jnp.int32, (rows, GROUP_W), 1)
    return [(lane >= h * HEAD_DIM) & (lane < (h + 1) * HEAD_DIM) for h in range(N_HEADS)]


def _stack_heads(x, masks):
    return jnp.concatenate([jnp.where(m, x, 0.0) for m in masks], axis=0)


def _unstack_heads(x, masks, t):
    out = jnp.where(masks[0], x[0:t], 0.0)
    for h in range(1, N_HEADS):
        out = out + jnp.where(masks[h], x[h * t:(h + 1) * t], 0.0)
    return out


def _block_ones():
    shift = HEAD_DIM.bit_length() - 1
    r = lax.broadcasted_iota(jnp.int32, (GROUP_W, GROUP_W), 0) >> shift
    c = lax.broadcasted_iota(jnp.int32, (GROUP_W, GROUP_W), 1) >> shift
    return r == c


def _head_sum(x, ones_bf16):
    hi = x.astype(BF16)
    lo = (x - hi.astype(F32)).astype(BF16)
    return (jnp.dot(hi, ones_bf16, preferred_element_type=F32)
            + jnp.dot(lo, ones_bf16, preferred_element_type=F32))


def _adaln_kernel(c_ref, w_ref, b_ref, o_ref):
    s = _silu(c_ref[...]).astype(BF16)
    o_ref[0, 0] = jnp.dot(s, w_ref[0].astype(BF16), preferred_element_type=F32) + b_ref[0]


def _adaln(cvecs, w_ada, b_ada):
    depth, d, d3 = w_ada.shape
    tn = 512
    per = d // tn
    return pl.pallas_call(
        _adaln_kernel,
        grid=(depth, d3 // tn),
        in_specs=[pl.BlockSpec((ADA_ROWS, d), lambda l, j: (0, 0)),
                  pl.BlockSpec((1, d, tn), lambda l, j: (l, 0, j)),
                  pl.BlockSpec((1, 1, tn), lambda l, j: (l, 0, j))],
        out_specs=pl.BlockSpec((1, 1, ADA_ROWS, tn), lambda l, j: (l, j // per, 0, j % per)),
        out_shape=jax.ShapeDtypeStruct((depth, 3, ADA_ROWS, d), F32),
        compiler_params=_cparams(("arbitrary", "arbitrary")),
        name="adaln",
    )(cvecs, w_ada, b_ada.reshape(depth, 1, d3))


def _rope(x, cos, sin):
    w = x.shape[-1]
    lane = lax.broadcasted_iota(jnp.int32, x.shape, 1)
    first = (lane & 31) < 16
    swapped = jnp.where(first, pltpu.roll(x, w - 16, 1), pltpu.roll(x, 16, 1))
    return x * cos[:, :w] + swapped * sin[:, :w]


def _mod_row(mod_ref, mod_row):
    row = pl.program_id(0) if mod_row is None else mod_row
    return mod_ref[0, 0, pl.ds(row, 1), :]


def _inproj_kernel(*refs, rope, emit_kv, layer, mod_row):
    refs = list(refs)
    x_ref, shift_ref, scale_ref, g_ref, w_ref = refs[:5]
    if rope:
        cos_ref, sin_ref = refs[5:7]
    n_out = 8 if emit_kv else 4
    pa_ref, pb_ref, pc_ref, pd_ref = refs[-n_out:][:4]
    x = x_ref[0]
    ms = jnp.mean(x * x, axis=-1, keepdims=True)
    y = x * lax.rsqrt(ms + EPS) * g_ref[layer:layer + 1, :]
    h = (y * (1.0 + _mod_row(scale_ref, mod_row)) + _mod_row(shift_ref, mod_row)).astype(BF16)

    def mm(c0, c1):
        return jnp.dot(h, w_ref[0, :, c0:c1], preferred_element_type=F32)

    if rope:
        cos = cos_ref[...]
        sin = sin_ref[...]
    g = GROUP_W
    aq = mm(0, g)
    akv = mm(g, 2 * g)
    if rope:
        aq = _rope(aq, cos, sin)
        ak = _rope(akv[:, :g // 2], cos, sin)
        akv = jnp.concatenate([ak, akv[:, g // 2:]], axis=1)
    pa_ref[0, :, 0:g] = aq.astype(pa_ref.dtype)
    pa_ref[0, :, g:2 * g] = akv.astype(pa_ref.dtype)
    pa_ref[0, :, 2 * g:3 * g] = mm(2 * g, 3 * g).astype(pa_ref.dtype)
    pbv = mm(W_A, W_A + W_B)
    pb_ref[0] = pbv.astype(pb_ref.dtype)
    if emit_kv:
        ak_ref, av_ref, bk_ref, bv_ref = refs[-4:]
        ak_ref[0] = akv[:, :g // 2]
        av_ref[0] = akv[:, g // 2:]
        bk_ref[0] = pbv[:, g:2 * g]
        bv_ref[0] = pbv[:, 2 * g:3 * g]
    c0 = W_A + W_B
    cq = mm(c0, c0 + g)
    ck = mm(c0 + g, c0 + 2 * g)
    if rope:
        cq = _rope(cq, cos, sin)
        ck = _rope(ck, cos, sin)
    pc_ref[0, :, 0:g] = cq.astype(pc_ref.dtype)
    pc_ref[0, :, g:2 * g] = ck.astype(pc_ref.dtype)
    pc_ref[0, :, 2 * g:4 * g] = mm(c0 + 2 * g, c0 + 4 * g).astype(pc_ref.dtype)
    d0 = c0 + W_C
    pd_ref[0] = mm(d0, d0 + W_D).astype(pd_ref.dtype)


def _rope_tables(slen):
    t = np.arange(slen)
    nf = HEAD_DIM // 4
    freqs = ROPE_BASE ** (-np.arange(nf, dtype=np.float64) / nf)
    d = np.arange(HEAD_DIM)
    pos = np.where(d[None, :] < HEAD_DIM // 2, (t // GRID_W)[:, None], (t % GRID_W)[:, None])
    ang = pos * freqs[d % nf][None, :]
    sign = np.where((d % (2 * nf)) < nf, -1.0, 1.0)[None, :]
    cos = np.tile(np.cos(ang), (1, N_HEADS))
    sin = np.tile(np.sin(ang) * sign, (1, N_HEADS))
    return jnp.asarray(cos, F32), jnp.asarray(sin, F32)


def _mod_spec(which, layer):
    return pl.BlockSpec((1, 1, ADA_ROWS, D_MODEL), lambda b, i: (layer, which, 0, 0))


def _in_proj(x, mods, layer, mod_row, g_pre, w_in_bf16, rope, emit_kv=False, tm=256):
    bsz, slen, d = x.shape
    depth = g_pre.shape[0]
    in_specs = [pl.BlockSpec((1, tm, d), lambda b, i: (b, i, 0)),
                _mod_spec(0, layer),
                _mod_spec(1, layer),
                pl.BlockSpec((depth, d), lambda b, i: (0, 0)),
                pl.BlockSpec((1, d, IN_WIDTH), lambda b, i: (layer, 0, 0))]
    args = [x, mods, mods, g_pre, w_in_bf16]
    if rope:
        cos, sin = _rope_tables(slen)
        in_specs += [pl.BlockSpec((tm, GROUP_W), lambda b, i: (i, 0))] * 2
        args += [cos, sin]
    widths = (W_A, W_B, W_C, W_D)
    if emit_kv:
        widths += (A_KV * HEAD_DIM, A_KV * HEAD_DIM, GROUP_W, GROUP_W)
    return pl.pallas_call(
        functools.partial(_inproj_kernel, rope=rope, emit_kv=emit_kv, layer=layer, mod_row=mod_row),
        grid=(bsz, slen // tm),
        in_specs=in_specs,
        out_specs=[pl.BlockSpec((1, tm, w), lambda b, i: (b, i, 0)) for w in widths],
        out_shape=[jax.ShapeDtypeStruct((bsz, slen, w), F32) for w in widths],
        compiler_params=_cparams(("arbitrary", "arbitrary")),
        name="in_proj",
    )(*args)


ATTN_ROWS = 256


def _attn_kernel(*refs, n_kv, qb, kw, back, slen, has_ctx, n_tab, tab_heads, has_sink):
    it = iter(refs)
    q_ref, k_ref, v_ref, g_ref = next(it), next(it), next(it), next(it)
    kc_ref = vc_ref = tab_ref = sink_ref = None
    if has_ctx:
        kc_ref, vc_ref = next(it), next(it)
    if n_tab:
        tab_ref = next(it)
    if has_sink:
        sink_ref = next(it)
    o_ref = next(it)

    n = pl.program_id(1)
    nblk = slen // qb
    ws = pl.multiple_of(jnp.clip(n * qb - back, 0, slen - kw), HEAD_DIM)
    if n_tab == 3:
        tix = jnp.where(n == 0, 0, jnp.where(n == nblk - 1, 2, 1))
    else:
        tix = 0
    grp = N_HEADS // n_kv
    qscale = HEAD_DIM ** -0.5 * LOG2E
    stack = grp > 1 and has_ctx
    groups = ([list(range(kh * grp, (kh + 1) * grp)) for kh in range(n_kv)] if stack
              else [[h] for h in range(N_HEADS)])
    rb = min(qb, ATTN_ROWS // len(groups[0]))
    for heads in groups:
        kh = heads[0] // grp
        hsl = [slice(h * HEAD_DIM, (h + 1) * HEAD_DIM) for h in heads]
        ks = slice(kh * HEAD_DIM, (kh + 1) * HEAD_DIM)
        kall = k_ref[0, pl.ds(ws, kw), ks].astype(BF16)
        vall = v_ref[0, pl.ds(ws, kw), ks].astype(BF16)
        if has_ctx:
            kall = jnp.concatenate([kall, kc_ref[0, 0, :, ks].astype(BF16)], axis=0)
            vall = jnp.concatenate([vall, vc_ref[0, 0, :, ks].astype(BF16)], axis=0)
        for r0 in range(0, qb, rb):
            rs = slice(r0, r0 + rb)
            q = (jnp.concatenate([q_ref[0, rs, sl] for sl in hsl], axis=0) * qscale).astype(BF16)
            s = lax.dot_general(q, kall, NT_DIMS, preferred_element_type=F32)
            if n_tab:
                tab = jnp.concatenate(
                    [tab_ref[0, tix, h if tab_heads == N_HEADS else 0, rs, :] for h in heads], axis=0)
                s = jnp.concatenate([s[:, :kw] + tab, s[:, kw:]], axis=1) if has_ctx else s + tab
            m = jnp.max(s, axis=-1, keepdims=True)
            if has_sink:
                sink = jnp.concatenate([jnp.full((rb, 1), sink_ref[h] * LOG2E, F32) for h in heads], axis=0)
                m = jnp.maximum(m, sink)
            p = jnp.exp2(s - m)
            den = jnp.sum(p, axis=-1, keepdims=True)
            acc = jnp.dot(p.astype(BF16), vall, preferred_element_type=F32)
            if has_sink:
                den = den + jnp.exp2(sink - m)
            o = acc / den
            for gi, sl in enumerate(hsl):
                o_ref[0, rs, sl] = (o[gi * rb:(gi + 1) * rb] * _silu(g_ref[0, rs, sl])).astype(o_ref.dtype)


def _attention(p, q_col, k_col, v_col, g_col, n_kv, qb, kw, back, ctx_k=None, ctx_v=None,
               table=None, sink=None, layer=0):
    bsz, slen, _ = p.shape
    kvw = n_kv * HEAD_DIM
    in_specs = [pl.BlockSpec((1, qb, GROUP_W), lambda b, n: (b, n, q_col)),
                pl.BlockSpec((1, slen, kvw), lambda b, n: (b, 0, k_col)),
                pl.BlockSpec((1, slen, kvw), lambda b, n: (b, 0, v_col)),
                pl.BlockSpec((1, qb, GROUP_W), lambda b, n: (b, n, g_col))]
    args = [p, p, p, p]
    has_ctx = ctx_k is not None
    if has_ctx:
        past = ctx_k.shape[2]
        in_specs += [pl.BlockSpec((1, 1, past, kvw), lambda b, n: (b, layer, 0, 0))] * 2
        args += [ctx_k, ctx_v]
    n_tab = tab_heads = 0
    if table is not None:
        n_tab, tab_heads = table.shape[1], table.shape[2]
        tab_layer = layer if table.shape[0] > 1 else 0
        in_specs.append(pl.BlockSpec((1,) + table.shape[1:], lambda b, n: (tab_layer, 0, 0, 0, 0)))
        args.append(table)
    if sink is not None:
        in_specs.append(pl.BlockSpec(memory_space=pltpu.SMEM))
        args.append(sink)
    kern = functools.partial(_attn_kernel, n_kv=n_kv, qb=qb, kw=kw, back=back, slen=slen, has_ctx=has_ctx,
                             n_tab=n_tab, tab_heads=tab_heads, has_sink=sink is not None)
    return pl.pallas_call(
        kern,
        grid=(bsz, slen // qb),
        in_specs=in_specs,
        out_specs=pl.BlockSpec((1, qb, GROUP_W), lambda b, n: (b, n, 0)),
        out_shape=jax.ShapeDtypeStruct((bsz, slen, GROUP_W), BF16),
        compiler_params=_cparams(("arbitrary", "arbitrary")),
        name="attn",
    )(*args)


WIN_QB = 256
WIN_BACK = WINDOW
WIN_KW = WIN_QB + 2 * WINDOW
NA_QROWS = 4
NA_KROWS = 12
NA_QB = NA_QROWS * GRID_W
NA_KW = NA_KROWS * GRID_W


def _window_table(slen):
    nblk = slen // WIN_QB
    tabs = []
    for n in (0, 1, nblk - 1):
        ws = int(np.clip(n * WIN_QB - WIN_BACK, 0, slen - WIN_KW))
        qpos = n * WIN_QB + np.arange(WIN_QB)[:, None]
        kpos = ws + np.arange(WIN_KW)[None, :]
        tabs.append(np.where(np.abs(qpos - kpos) <= WINDOW, 0.0, NEG))
    return jnp.asarray(np.stack(tabs)[None, :, None], F32)


N_RPB_R = 2 * NA_ROWS - 1
N_RPB_C = 2 * NA_COLS - 1


def _na_table_kernel(rpb_ref, o_ref, tz_scr, *, rows):
    base = (pl.program_id(0) * N_HEADS + pl.program_id(1)) * (N_RPB_R * N_RPB_C)
    qc = lax.broadcasted_iota(jnp.int32, (GRID_W, 2 * GRID_W), 0)
    kk = lax.broadcasted_iota(jnp.int32, (GRID_W, 2 * GRID_W), 1)
    kc = kk & (GRID_W - 1)
    diff = kc - qc
    qws = jnp.clip(qc - NA_COLS // 2, 0, GRID_W - NA_COLS)
    col_ok = (kc >= qws) & (kc < qws + NA_COLS)
    neg = jnp.full((GRID_W, 2 * GRID_W), NEG, F32)
    for dr in range(N_RPB_R):
        acc = neg
        for m in range(N_RPB_C):
            acc = jnp.where(diff == m - (NA_COLS - 1), rpb_ref[base + dr * N_RPB_C + m] * LOG2E, acc)
        tz_scr[dr] = jnp.where(col_ok, acc, NEG)
    wr = min(NA_ROWS, rows)
    nblk = rows // NA_QROWS
    for ti, g in enumerate((0, 1, nblk - 1)):
        ws_row = min(max(g * NA_QROWS - NA_QROWS, 0), rows - NA_KROWS)
        for qr in range(NA_QROWS):
            r = g * NA_QROWS + qr
            rs = min(max(r - wr // 2, 0), rows - wr)
            for p in range(NA_KROWS // 2):
                halves = []
                for kr in (ws_row + 2 * p, ws_row + 2 * p + 1):
                    halves.append(tz_scr[kr - r + NA_ROWS - 1] if rs <= kr < rs + wr else neg)
                o_ref[0, ti, 0, qr * GRID_W:(qr + 1) * GRID_W, p * 2 * GRID_W:(p + 1) * 2 * GRID_W] = (
                    jnp.where(kk < GRID_W, halves[0], halves[1]))


def _na_tables(na_rpb, slen):
    depth = na_rpb.shape[0]
    return pl.pallas_call(
        functools.partial(_na_table_kernel, rows=slen // GRID_W),
        grid=(depth, N_HEADS),
        in_specs=[pl.BlockSpec(memory_space=pltpu.SMEM)],
        out_specs=pl.BlockSpec((1, 3, 1, NA_QB, NA_KW), lambda l, h: (l, 0, h, 0, 0)),
        out_shape=jax.ShapeDtypeStruct((depth, 3, N_HEADS, NA_QB, NA_KW), F32),
        scratch_shapes=[pltpu.VMEM((N_RPB_R, GRID_W, 2 * GRID_W), F32)],
        compiler_params=_cparams(("arbitrary", "arbitrary")),
        name="na_table",
    )(na_rpb.astype(F32).reshape(-1))


def _ret_kernel(q_ref, k_ref, v_ref, g_ref, lgl_ref, lgc_ref, *rest, slen, t, has_s0):
    if has_s0:
        s0_ref, o_ref, sb_scr, st_scr, dm_scr, dec_scr = rest
    else:
        o_ref, sfin_ref, sb_scr, st_scr, dm_scr, dec_scr = rest
    nc = slen // t
    masks = _head_masks(t)
    bmask = _block_ones()
    ones_bf16 = jnp.where(bmask, 1.0, 0.0).astype(BF16)
    lgf, lgb = lgl_ref[0], lgl_ref[1]

    @pl.when(pl.program_id(0) == 0)
    def _():
        ii = lax.broadcasted_iota(jnp.int32, (t, N_HEADS * t), 0)
        jj = lax.broadcasted_iota(jnp.int32, (t, N_HEADS * t), 1) & (t - 1)
        dist = (ii - jj).astype(F32)
        dm_scr[...] = (jnp.where(dist >= 0, jnp.exp(dist * lgc_ref[0]), 0.0)
                       + jnp.where(dist <= 0, jnp.exp(-dist * lgc_ref[1]), 0.0))
        idx = lax.broadcasted_iota(jnp.int32, (t, GROUP_W), 0).astype(F32)
        dec_scr[0] = jnp.exp((idx + 1.0) * lgf)
        dec_scr[1] = jnp.exp((t - 1.0 - idx) * lgf)
        dec_scr[2] = jnp.exp((t - idx) * lgb)
        dec_scr[3] = jnp.exp(idx * lgb)

    cdec_f = jnp.exp(float(t) * lgf)
    cdec_b = jnp.exp(float(t) * lgb)
    kscale = HEAD_DIM ** -0.5

    def state_update(st, k, v, kdec, cdec):
        upd = lax.dot_general(v, (k * kdec).astype(BF16), TN_DIMS, preferred_element_type=F32)
        return st * cdec + jnp.where(bmask, upd, 0.0)

    def init_state(direction):
        if has_s0:
            st_scr[...] = _expand_state(s0_ref[0, direction], bmask)
        else:
            st_scr[...] = jnp.zeros((GROUP_W, GROUP_W), F32)

    def emit_final(direction):
        if not has_s0:
            sfin_ref[0, direction] = _compact_state(st_scr[...], bmask)

    init_state(1)

    def sweep_bwd(ci, carry):
        c = nc - 1 - ci
        rows = pl.ds(pl.multiple_of(c * t, t), t)
        st = st_scr[...]
        sb_scr[c] = st.astype(BF16)
        k = k_ref[0, rows, :] * kscale
        st_scr[...] = state_update(st, k, v_ref[0, rows, :].astype(BF16), dec_scr[3], cdec_b)
        return carry

    lax.fori_loop(0, nc, sweep_bwd, 0)
    emit_final(1)
    init_state(0)

    def sweep_fwd(c, carry):
        rows = pl.ds(pl.multiple_of(c * t, t), t)
        q = q_ref[0, rows, :]
        k = k_ref[0, rows, :] * kscale
        v = v_ref[0, rows, :]
        ks = _stack_heads(k, masks).astype(BF16)
        vs = _stack_heads(v, masks).astype(BF16)
        sc = lax.dot_general(q.astype(BF16), ks, NT_DIMS, preferred_element_type=F32) * dm_scr[...]
        o = jnp.dot(sc.astype(BF16), vs, preferred_element_type=F32)
        st = st_scr[...]
        o = o + lax.dot_general((q * dec_scr[0]).astype(BF16), st.astype(BF16), NT_DIMS,
                                preferred_element_type=F32)
        o = o + lax.dot_general((q * dec_scr[2]).astype(BF16), sb_scr[c], NT_DIMS, preferred_element_type=F32)
        st_scr[...] = state_update(st, k, v.astype(BF16), dec_scr[1], cdec_f)
        mu = _head_sum(o, ones_bf16) * (1.0 / HEAD_DIM)
        d = o - mu
        var = _head_sum(d * d, ones_bf16) * (1.0 / HEAD_DIM)
        y = d * lax.rsqrt(var + EPS)
        o_ref[0, rows, :] = (y * _silu(g_ref[0, rows, :])).astype(o_ref.dtype)
        return carry

    lax.fori_loop(0, nc, sweep_fwd, 0, unroll=2)
    emit_final(0)


def _compact_states(s0):
    return jnp.swapaxes(s0, -1, -2).reshape(s0.shape[:-3] + (GROUP_W, HEAD_DIM))


def _uncompact_states(sc):
    bsz = sc.shape[0]
    return jnp.swapaxes(sc.reshape(bsz, 2, N_HEADS, HEAD_DIM, HEAD_DIM), -1, -2)


def _split3(x):
    h1 = x.astype(BF16)
    r1 = x - h1.astype(F32)
    h2 = r1.astype(BF16)
    return h1, h2, (r1 - h2.astype(F32)).astype(BF16)


def _expand_state(x, bmask):
    r = lax.broadcasted_iota(jnp.int32, (HEAD_DIM, GROUP_W), 0)
    c = lax.broadcasted_iota(jnp.int32, (HEAD_DIM, GROUP_W), 1) & (HEAD_DIM - 1)
    rep = jnp.where(r == c, 1.0, 0.0).astype(BF16)
    h1, h2, h3 = _split3(x)
    tiled = (jnp.dot(h1, rep, preferred_element_type=F32) + jnp.dot(h2, rep, preferred_element_type=F32)
             + jnp.dot(h3, rep, preferred_element_type=F32))
    return jnp.where(bmask, tiled, 0.0)


def _compact_state(st, bmask):
    r = lax.broadcasted_iota(jnp.int32, (GROUP_W, HEAD_DIM), 0) & (HEAD_DIM - 1)
    c = lax.broadcasted_iota(jnp.int32, (GROUP_W, HEAD_DIM), 1)
    fold = jnp.where(r == c, 1.0, 0.0).astype(BF16)
    h1, h2, h3 = _split3(jnp.where(bmask, st, 0.0))
    return (jnp.dot(h1, fold, preferred_element_type=F32) + jnp.dot(h2, fold, preferred_element_type=F32)
            + jnp.dot(h3, fold, preferred_element_type=F32))


def _state_specs(bsz, s0):
    spec = pl.BlockSpec((1, 2, GROUP_W, HEAD_DIM), lambda b: (b, 0, 0, 0))
    shape = jax.ShapeDtypeStruct((bsz, 2, GROUP_W, HEAD_DIM), F32)
    if s0 is None:
        return [], [], [spec], [shape]
    return [spec], [s0], [], []


RET_CHUNK = 256


def _retention(pc, log_gamma, s0=None, t=RET_CHUNK):
    bsz, slen, _ = pc.shape
    t = min(t, slen)
    lgl = jnp.repeat(log_gamma, HEAD_DIM, axis=1).reshape(2, 1, GROUP_W)
    lgc = jnp.repeat(log_gamma, t, axis=1).reshape(2, 1, N_HEADS * t)
    col = lambda j: pl.BlockSpec((1, slen, GROUP_W), lambda b: (b, 0, j))
    s_in_specs, s_args, s_out_specs, s_out_shape = _state_specs(bsz, s0)
    return pl.pallas_call(
        functools.partial(_ret_kernel, slen=slen, t=t, has_s0=s0 is not None),
        grid=(bsz,),
        in_specs=[col(0), col(1), col(2), col(3),
                  pl.BlockSpec((2, 1, GROUP_W), lambda b: (0, 0, 0)),
                  pl.BlockSpec((2, 1, N_HEADS * t), lambda b: (0, 0, 0))] + s_in_specs,
        out_specs=[pl.BlockSpec((1, slen, GROUP_W), lambda b: (b, 0, 0))] + s_out_specs,
        out_shape=[jax.ShapeDtypeStruct((bsz, slen, GROUP_W), BF16)] + s_out_shape,
        scratch_shapes=[pltpu.VMEM((slen // t, GROUP_W, GROUP_W), BF16), pltpu.VMEM((GROUP_W, GROUP_W), F32),
                        pltpu.VMEM((t, N_HEADS * t), F32), pltpu.VMEM((4, t, GROUP_W), F32)],
        compiler_params=_cparams(("arbitrary",)),
        name="retention",
    )(pc, pc, pc, pc, lgl, lgc, *s_args)


HGRN_DIRECT = 8


def _split2(x):
    hi = x.astype(BF16)
    return hi, (x - hi.astype(F32)).astype(BF16)


def _hgrn_kernel(q_ref, zf_ref, zb_ref, v_ref, g_ref, lb_ref, *rest, slen, t, has_s0):
    if has_s0:
        s0_ref, o_ref, of_scr, st_scr = rest
    else:
        o_ref, sfin_ref, of_scr, st_scr = rest
    nc = slen // t
    masks = _head_masks(t)
    bmask = _block_ones()
    ones_bf16 = jnp.where(bmask, 1.0, 0.0).astype(BF16)
    r_i = lax.broadcasted_iota(jnp.int32, (t, t), 0)
    c_i = lax.broadcasted_iota(jnp.int32, (t, t), 1)
    sub = lax.broadcasted_iota(jnp.int32, (t, GROUP_W), 0) & (HGRN_DIRECT - 1)
    levels = []
    s = t // 2
    while s >= HGRN_DIRECT:
        levels.append(s)
        s //= 2
    r_w = lax.broadcasted_iota(jnp.int32, (t, N_HEADS * t), 0)
    c_w = lax.broadcasted_iota(jnp.int32, (t, N_HEADS * t), 1) & (t - 1)
    lvl_masks = []
    for s in levels:
        shift = (2 * s).bit_length() - 1
        lvl_masks.append(jnp.where((r_w >> shift) == (c_w >> shift), 1.0, 0.0))

    def block_roll(x, shift):
        x3 = x.reshape(t // HGRN_DIRECT, HGRN_DIRECT, GROUP_W)
        return pltpu.roll(x3, shift, 1).reshape(t, GROUP_W)

    def run(direction, z_ref, emit):
        fwd = direction == 0
        lbd = jnp.maximum(lb_ref[direction], LB_FLOOR)
        tri = jnp.where((r_i >= c_i) if fwd else (r_i <= c_i), 1.0, 0.0).astype(BF16)
        if has_s0:
            st_scr[...] = _expand_state(s0_ref[0, direction], bmask)
        else:
            st_scr[...] = jnp.zeros((GROUP_W, GROUP_W), F32)

        def body(ci, carry):
            c = ci if fwd else nc - 1 - ci
            rows = pl.ds(pl.multiple_of(c * t, t), t)
            q = _silu(q_ref[0, rows, :])
            z = z_ref[0, rows, :]
            v = v_ref[0, rows, :]
            vb = v.astype(BF16)
            a = jnp.exp(-jnp.abs(z))
            inv = 1.0 / (1.0 + a)
            pos = z >= 0
            sig = jnp.where(pos, inv, a * inv)
            nsig = jnp.where(pos, a * inv, inv)
            lf = jnp.log(lbd + (1.0 - lbd) * sig)
            k = (1.0 - lbd) * nsig
            h1, h2 = _split2(lf)
            b = jnp.dot(tri, h1, preferred_element_type=F32) + jnp.dot(tri, h2, preferred_element_type=F32)
            b = b * LOG2E
            vs = _stack_heads(v, masks).astype(BF16)

            sc = None
            for s, lm in zip(levels, lvl_masks):
                pa, pb = [], []
                zero = jnp.zeros((s, GROUP_W), F32)
                for gi in range(t // (2 * s)):
                    b_lo = b[gi * 2 * s:gi * 2 * s + s]
                    b_hi = b[gi * 2 * s + s:(gi + 1) * 2 * s]
                    if fwd:
                        anc = b_hi[0:1]
                        pa += [zero, jnp.exp2(b_hi - anc)]
                        pb += [jnp.exp2(anc - b_lo), zero]
                    else:
                        anc = b_lo[s - 1:s]
                        pa += [jnp.exp2(b_lo - anc), zero]
                        pb += [zero, jnp.exp2(anc - b_hi)]
                qa = (q * jnp.concatenate(pa, axis=0)).astype(BF16)
                kb = _stack_heads(k * jnp.concatenate(pb, axis=0), masks).astype(BF16)
                term = lax.dot_general(qa, kb, NT_DIMS, preferred_element_type=F32) * lm
                sc = term if sc is None else sc + term
            o = None
            if sc is not None:
                o = jnp.dot(sc.astype(BF16), vs, preferred_element_type=F32)

            prods, vals = [(q * k).astype(BF16)], [v]
            for dlt in range(1, HGRN_DIRECT):
                sh = dlt if fwd else HGRN_DIRECT - dlt
                ok = (sub >= dlt) if fwd else (sub + dlt <= HGRN_DIRECT - 1)
                w = jnp.exp2(jnp.minimum(b - block_roll(b, sh), 0.0))
                prods.append(jnp.where(ok, q * block_roll(k, sh) * w, 0.0).astype(BF16))
                vals.append(block_roll(v, sh))
            hs = jnp.dot(jnp.concatenate(prods, axis=0), ones_bf16, preferred_element_type=F32)
            for dlt in range(HGRN_DIRECT):
                term = hs[dlt * t:(dlt + 1) * t] * vals[dlt]
                o = term if o is None else o + term

            st = st_scr[...]
            o = o + lax.dot_general((q * jnp.exp2(b)).astype(BF16), st.astype(BF16), NT_DIMS,
                                    preferred_element_type=F32)
            bl = b[t - 1:t] if fwd else b[0:1]
            upd = lax.dot_general(vb, (k * jnp.exp2(bl - b)).astype(BF16), TN_DIMS,
                                  preferred_element_type=F32)
            st_scr[...] = st * jnp.exp2(bl) + jnp.where(bmask, upd, 0.0)
            emit(rows, o)
            return carry

        lax.fori_loop(0, nc, body, 0, unroll=2)
        if not has_s0:
            sfin_ref[0, direction] = _compact_state(st_scr[...], bmask)

    def emit_fwd(rows, o):
        of_scr[rows, :] = o

    def emit_bwd(rows, o):
        o = o + of_scr[rows, :]
        ms = _head_sum(o * o, ones_bf16) * (1.0 / HEAD_DIM)
        y = o * lax.rsqrt(ms + EPS)
        o_ref[0, rows, :] = (y * _silu(g_ref[0, rows, :])).astype(o_ref.dtype)

    run(0, zf_ref, emit_fwd)
    run(1, zb_ref, emit_bwd)


HGRN_CHUNK = 128


def _hgrn(pd, lb, s0=None, t=HGRN_CHUNK):
    bsz, slen, _ = pd.shape
    col = lambda j: pl.BlockSpec((1, slen, GROUP_W), lambda b: (b, 0, j))
    s_in_specs, s_args, s_out_specs, s_out_shape = _state_specs(bsz, s0)
    return pl.pallas_call(
        functools.partial(_hgrn_kernel, slen=slen, t=t, has_s0=s0 is not None),
        grid=(bsz,),
        in_specs=[col(0), col(1), col(2), col(3), col(4),
                  pl.BlockSpec((2, 1, GROUP_W), lambda b: (0, 0, 0))] + s_in_specs,
        out_specs=[pl.BlockSpec((1, slen, GROUP_W), lambda b: (b, 0, 0))] + s_out_specs,
        out_shape=[jax.ShapeDtypeStruct((bsz, slen, GROUP_W), BF16)] + s_out_shape,
        scratch_shapes=[pltpu.VMEM((slen, GROUP_W), F32), pltpu.VMEM((GROUP_W, GROUP_W), F32)],
        compiler_params=_cparams(("arbitrary",)),
        name="hgrn2",
    )(pd, pd, pd, pd, pd, lb.reshape(2, 1, GROUP_W), *s_args)


def _outproj_kernel(ma_ref, mb_ref, mc_ref, md_ref, w_ref, x_ref, gate_ref, g_ref, o_ref, *, layer, mod_row):
    g = GROUP_W
    y = jnp.dot(ma_ref[0], w_ref[0, 0:g, :], preferred_element_type=F32)
    y = y + jnp.dot(mb_ref[0], w_ref[0, g:2 * g, :], preferred_element_type=F32)
    y = y + jnp.dot(mc_ref[0], w_ref[0, 2 * g:3 * g, :], preferred_element_type=F32)
    y = y + jnp.dot(md_ref[0], w_ref[0, 3 * g:4 * g, :], preferred_element_type=F32)
    ms = jnp.mean(y * y, axis=-1, keepdims=True)
    r = y * lax.rsqrt(ms + EPS) * g_ref[layer:layer + 1, :]
    o_ref[0] = x_ref[0] + _mod_row(gate_ref, mod_row) * r


OUTPROJ_TM = 1024


def _out_proj(mixed, w_out_bf16, x, mods, layer, mod_row, g_post):
    bsz, slen, d = x.shape
    depth = g_post.shape[0]
    tm = min(OUTPROJ_TM, slen)
    mspec = pl.BlockSpec((1, tm, GROUP_W), lambda b, i: (b, i, 0))
    return pl.pallas_call(
        functools.partial(_outproj_kernel, layer=layer, mod_row=mod_row),
        grid=(bsz, slen // tm),
        in_specs=[mspec, mspec, mspec, mspec,
                  pl.BlockSpec((1, d, d), lambda b, i: (layer, 0, 0)),
                  pl.BlockSpec((1, tm, d), lambda b, i: (b, i, 0)),
                  _mod_spec(2, layer),
                  pl.BlockSpec((depth, d), lambda b, i: (0, 0))],
        out_specs=pl.BlockSpec((1, tm, d), lambda b, i: (b, i, 0)),
        out_shape=jax.ShapeDtypeStruct((bsz, slen, d), F32),
        compiler_params=_cparams(("arbitrary", "arbitrary")),
        name="out_proj",
    )(*mixed, w_out_bf16, x, mods, g_post)


def _context_layer(x, mods, layer, mod_row, g_pre, g_post, w_in, w_out, sink, log_gamma, lb):
    bsz, slen, _ = x.shape
    pa, pb, pc, pd, ak, av, bk, bv = _in_proj(x, mods, layer, mod_row, g_pre, w_in, rope=False, emit_kv=True)
    o_a = _attention(pa, 0, 2, 3, 2, n_kv=A_KV, qb=slen, kw=slen, back=0, sink=sink)
    o_b = _attention(pb, 0, 1, 2, 3, n_kv=N_HEADS, qb=slen, kw=slen, back=0)
    o_c, s_c = _retention(pc, log_gamma)
    o_d, s_d = _hgrn(pd, lb)
    x = _out_proj((o_a, o_b, o_c, o_d), w_out, x, mods, layer, mod_row, g_post)
    kv4 = lambda t: t.reshape(bsz, slen, -1, HEAD_DIM)
    return x, (kv4(ak), kv4(av), kv4(bk), kv4(bv), _uncompact_states(s_c), _uncompact_states(s_d))


def _latent_layer(x, mods, layer, ca_k, ca_v, cb_k, cb_v, st_c, st_d, g_pre, g_post, w_in, w_out, sink, win_tab,
                  na_tabs, log_gamma, lb):
    pa, pb, pc, pd = _in_proj(x, mods, layer, None, g_pre, w_in, rope=True)
    o_a = _attention(pa, 0, 2, 3, 2, n_kv=A_KV, qb=WIN_QB, kw=WIN_KW, back=WIN_BACK,
                     ctx_k=ca_k, ctx_v=ca_v, table=win_tab, sink=sink, layer=layer)
    o_b = _attention(pb, 0, 1, 2, 3, n_kv=N_HEADS, qb=NA_QB, kw=NA_KW, back=NA_QB,
                     ctx_k=cb_k, ctx_v=cb_v, table=na_tabs, layer=layer)
    o_c, = _retention(pc, log_gamma, st_c)
    o_d, = _hgrn(pd, lb, st_d)
    return _out_proj((o_a, o_b, o_c, o_d), w_out, x, mods, layer, None, g_post)


def kernel(x_prompt, x_sample, c, cache_win_k, cache_win_v, cache_na_k, cache_na_v, state_ret, state_hgrn,
           c_ctx, w_ada, b_ada, g_pre, g_post, w_in, w_out, attn_sink, na_rpb, ret_decay_logit, hgrn_lb_logit):
    depth = w_ada.shape[0]
    dec_b, dec_s, d = x_sample.shape
    p_lb = jax.nn.softmax(hgrn_lb_logit.astype(F32), axis=0)
    lower_bounds = jnp.cumsum(p_lb, axis=0) - p_lb[0:1]
    log_gammas = jax.nn.log_sigmoid(ret_decay_logit.astype(F32))
    w_in_b = w_in.astype(BF16)
    w_out_b = w_out.astype(BF16)

    cvecs = jnp.zeros((ADA_ROWS, d), F32).at[:dec_b].set(c).at[dec_b].set(c_ctx)
    mods = _adaln(cvecs, w_ada, b_ada)
    win_tab = _window_table(dec_s)
    na_tabs = _na_tables(na_rpb, dec_s)

    x = x_prompt
    outs = [[] for _ in range(6)]
    for l in range(depth):
        x, extra = _context_layer(x, mods, l, dec_b, g_pre, g_post, w_in_b, w_out_b,
                                  attn_sink[l], log_gammas[l], lower_bounds[l])
        for acc, e in zip(outs, extra):
            acc.append(e)
    y_prompt = x
    stacked = [jnp.stack(o, axis=1) for o in outs]

    past = cache_win_k.shape[2]
    ca_k = cache_win_k.reshape(dec_b, depth, past, A_KV * HEAD_DIM)
    ca_v = cache_win_v.reshape(dec_b, depth, past, A_KV * HEAD_DIM)
    cb_k = cache_na_k.reshape(dec_b, depth, past, GROUP_W)
    cb_v = cache_na_v.reshape(dec_b, depth, past, GROUP_W)
    st_c = _compact_states(state_ret)
    st_d = _compact_states(state_hgrn)
    x = x_sample
    for l in range(depth):
        x = _latent_layer(x, mods, l, ca_k, ca_v, cb_k, cb_v, st_c[:, l], st_d[:, l], g_pre, g_post,
                          w_in_b, w_out_b, attn_sink[l], win_tab, na_tabs, log_gammas[l], lower_bounds[l])
    return (y_prompt, x, *stacked)
```

```python
import functools

import numpy as np
import jax
import jax.numpy as jnp
from jax import lax
from jax.experimental import pallas as pl
from jax.experimental.pallas import tpu as pltpu

F32 = jnp.float32
BF16 = jnp.bfloat16

D_MODEL = 1024
DEPTH = 4
GRID_W = 64
HEAD_DIM = 64
N_HEADS = 4
GROUP_W = N_HEADS * HEAD_DIM
A_KV = 2
WINDOW = 128
NA_ROWS = 8
NA_COLS = 16
ROPE_BASE = 10000.0
EPS = 1e-6
NEG = -1e30
LB_FLOOR = 1e-30
LOG2E = 1.4426950408889634
W_A = 3 * GROUP_W
W_B = 4 * GROUP_W
W_C = 4 * GROUP_W
W_D = 5 * GROUP_W
IN_WIDTH = W_A + W_B + W_C + W_D

V7X_VMEM_LIMIT_BYTES = 56 * 1024 * 1024
ADA_ROWS = 16

NT_DIMS = (((1,), (1,)), ((), ()))
TN_DIMS = (((0,), (0,)), ((), ()))


def _cparams(sem, flags=None):
    return pltpu.CompilerParams(dimension_semantics=sem, vmem_limit_bytes=V7X_VMEM_LIMIT_BYTES, flags=flags)


def _sigmoid(x):
    return 1.0 / (1.0 + jnp.exp(-x))


def _silu(x):
    return x * _sigmoid(x)


def _head_masks(rows):
    lane = lax.broadcasted_iota(jnp.int32, (rows, GROUP_W), 1)
    return [(lane >= h * HEAD_DIM) & (lane < (h + 1) * HEAD_DIM) for h in range(N_HEADS)]


def _stack_heads(x, masks):
    return jnp.concatenate([jnp.where(m, x, 0.0) for m in masks], axis=0)


def _unstack_heads(x, masks, t):
    out = jnp.where(masks[0], x[0:t], 0.0)
    for h in range(1, N_HEADS):
        out = out + jnp.where(masks[h], x[h * t:(h + 1) * t], 0.0)
    return out


def _block_ones():
    shift = HEAD_DIM.bit_length() - 1
    r = lax.broadcasted_iota(jnp.int32, (GROUP_W, GROUP_W), 0) >> shift
    c = lax.broadcasted_iota(jnp.int32, (GROUP_W, GROUP_W), 1) >> shift
    return r == c


def _head_sum(x, ones_bf16):
    hi = x.astype(BF16)
    lo = (x - hi.astype(F32)).astype(BF16)
    return (jnp.dot(hi, ones_bf16, preferred_element_type=F32)
            + jnp.dot(lo, ones_bf16, preferred_element_type=F32))


def _adaln_kernel(c_ref, w_ref, b_ref, o_ref):
    s = _silu(c_ref[...]).astype(BF16)
    o_ref[0, 0] = jnp.dot(s, w_ref[0].astype(BF16), preferred_element_type=F32) + b_ref[0]


def _adaln(cvecs, w_ada, b_ada):
    depth, d, d3 = w_ada.shape
    tn = 512
    per = d // tn
    return pl.pallas_call(
        _adaln_kernel,
        grid=(depth, d3 // tn),
        in_specs=[pl.BlockSpec((ADA_ROWS, d), lambda l, j: (0, 0)),
                  pl.BlockSpec((1, d, tn), lambda l, j: (l, 0, j)),
                  pl.BlockSpec((1, 1, tn), lambda l, j: (l, 0, j))],
        out_specs=pl.BlockSpec((1, 1, ADA_ROWS, tn), lambda l, j: (l, j // per, 0, j % per)),
        out_shape=jax.ShapeDtypeStruct((depth, 3, ADA_ROWS, d), F32),
        compiler_params=_cparams(("arbitrary", "arbitrary")),
        name="adaln",
    )(cvecs, w_ada, b_ada.reshape(depth, 1, d3))


def _rope(x, cos, sin):
    w = x.shape[-1]
    lane = lax.broadcasted_iota(jnp.int32, x.shape, 1)
    first = (lane & 31) < 16
    swapped = jnp.where(first, pltpu.roll(x, w - 16, 1), pltpu.roll(x, 16, 1))
    return x * cos[:, :w] + swapped * sin[:, :w]


def _mod_row(mod_ref, mod_row):
    row = pl.program_id(0) if mod_row is None else mod_row
    return mod_ref[0, 0, pl.ds(row, 1), :]


def _inproj_kernel(*refs, rope, emit_kv, layer, mod_row):
    refs = list(refs)
    x_ref, shift_ref, scale_ref, g_ref, w_ref = refs[:5]
    if rope:
        cos_ref, sin_ref = refs[5:7]
    n_out = 8 if emit_kv else 4
    pa_ref, pb_ref, pc_ref, pd_ref = refs[-n_out:][:4]
    x = x_ref[0]
    ms = jnp.mean(x * x, axis=-1, keepdims=True)
    y = x * lax.rsqrt(ms + EPS) * g_ref[layer:layer + 1, :]
    h = (y * (1.0 + _mod_row(scale_ref, mod_row)) + _mod_row(shift_ref, mod_row)).astype(BF16)

    def mm(c0, c1):
        return jnp.dot(h, w_ref[0, :, c0:c1], preferred_element_type=F32)

    if rope:
        cos = cos_ref[...]
        sin = sin_ref[...]
    g = GROUP_W
    aq = mm(0, g)
    akv = mm(g, 2 * g)
    if rope:
        aq = _rope(aq, cos, sin)
        ak = _rope(akv[:, :g // 2], cos, sin)
        akv = jnp.concatenate([ak, akv[:, g // 2:]], axis=1)
    pa_ref[0, :, 0:g] = aq.astype(pa_ref.dtype)
    pa_ref[0, :, g:2 * g] = akv.astype(pa_ref.dtype)
    pa_ref[0, :, 2 * g:3 * g] = mm(2 * g, 3 * g).astype(pa_ref.dtype)
    pbv = mm(W_A, W_A + W_B)
    pb_ref[0] = pbv.astype(pb_ref.dtype)
    if emit_kv:
        ak_ref, av_ref, bk_ref, bv_ref = refs[-4:]
        ak_ref[0] = akv[:, :g // 2]
        av_ref[0] = akv[:, g // 2:]
        bk_ref[0] = pbv[:, g:2 * g]
        bv_ref[0] = pbv[:, 2 * g:3 * g]
    c0 = W_A + W_B
    cq = mm(c0, c0 + g)
    ck = mm(c0 + g, c0 + 2 * g)
    if rope:
        cq = _rope(cq, cos, sin)
        ck = _rope(ck, cos, sin)
    pc_ref[0, :, 0:g] = cq.astype(pc_ref.dtype)
    pc_ref[0, :, g:2 * g] = ck.astype(pc_ref.dtype)
    pc_ref[0, :, 2 * g:4 * g] = mm(c0 + 2 * g, c0 + 4 * g).astype(pc_ref.dtype)
    d0 = c0 + W_C
    pd_ref[0] = mm(d0, d0 + W_D).astype(pd_ref.dtype)


def _rope_tables(slen):
    t = np.arange(slen)
    nf = HEAD_DIM // 4
    freqs = ROPE_BASE ** (-np.arange(nf, dtype=np.float64) / nf)
    d = np.arange(HEAD_DIM)
    pos = np.where(d[None, :] < HEAD_DIM // 2, (t // GRID_W)[:, None], (t % GRID_W)[:, None])
    ang = pos * freqs[d % nf][None, :]
    sign = np.where((d % (2 * nf)) < nf, -1.0, 1.0)[None, :]
    cos = np.tile(np.cos(ang), (1, N_HEADS))
    sin = np.tile(np.sin(ang) * sign, (1, N_HEADS))
    return jnp.asarray(cos, F32), jnp.asarray(sin, F32)


def _mod_spec(which, layer):
    return pl.BlockSpec((1, 1, ADA_ROWS, D_MODEL), lambda b, i: (layer, which, 0, 0))


def _in_proj(x, mods, layer, mod_row, g_pre, w_in_bf16, rope, emit_kv=False, tm=256):
    bsz, slen, d = x.shape
    depth = g_pre.shape[0]
    in_specs = [pl.BlockSpec((1, tm, d), lambda b, i: (b, i, 0)),
                _mod_spec(0, layer),
                _mod_spec(1, layer),
                pl.BlockSpec((depth, d), lambda b, i: (0, 0)),
                pl.BlockSpec((1, d, IN_WIDTH), lambda b, i: (layer, 0, 0))]
    args = [x, mods, mods, g_pre, w_in_bf16]
    if rope:
        cos, sin = _rope_tables(slen)
        in_specs += [pl.BlockSpec((tm, GROUP_W), lambda b, i: (i, 0))] * 2
        args += [cos, sin]
    widths = (W_A, W_B, W_C, W_D)
    if emit_kv:
        widths += (A_KV * HEAD_DIM, A_KV * HEAD_DIM, GROUP_W, GROUP_W)
    return pl.pallas_call(
        functools.partial(_inproj_kernel, rope=rope, emit_kv=emit_kv, layer=layer, mod_row=mod_row),
        grid=(bsz, slen // tm),
        in_specs=in_specs,
        out_specs=[pl.BlockSpec((1, tm, w), lambda b, i: (b, i, 0)) for w in widths],
        out_shape=[jax.ShapeDtypeStruct((bsz, slen, w), F32) for w in widths],
        compiler_params=_cparams(("arbitrary", "arbitrary")),
        name="in_proj",
    )(*args)


ATTN_ROWS = 256


def _attn_kernel(*refs, n_kv, qb, kw, back, slen, has_ctx, n_tab, tab_heads, has_sink):
    it = iter(refs)
    q_ref, k_ref, v_ref, g_ref = next(it), next(it), next(it), next(it)
    kc_ref = vc_ref = tab_ref = sink_ref = None
    if has_ctx:
        kc_ref, vc_ref = next(it), next(it)
    if n_tab:
        tab_ref = next(it)
    if has_sink:
        sink_ref = next(it)
    o_ref = next(it)

    n = pl.program_id(1)
    nblk = slen // qb
    ws = pl.multiple_of(jnp.clip(n * qb - back, 0, slen - kw), HEAD_DIM)
    if n_tab == 3:
        tix = jnp.where(n == 0, 0, jnp.where(n == nblk - 1, 2, 1))
    else:
        tix = 0
    grp = N_HEADS // n_kv
    qscale = HEAD_DIM ** -0.5 * LOG2E
    stack = grp > 1 and has_ctx
    groups = ([list(range(kh * grp, (kh + 1) * grp)) for kh in range(n_kv)] if stack
              else [[h] for h in range(N_HEADS)])
    rb = min(qb, ATTN_ROWS // len(groups[0]))
    for heads in groups:
        kh = heads[0] // grp
        hsl = [slice(h * HEAD_DIM, (h + 1) * HEAD_DIM) for h in heads]
        ks = slice(kh * HEAD_DIM, (kh + 1) * HEAD_DIM)
        kall = k_ref[0, pl.ds(ws, kw), ks].astype(BF16)
        vall = v_ref[0, pl.ds(ws, kw), ks].astype(BF16)
        if has_ctx:
            kall = jnp.concatenate([kall, kc_ref[0, 0, :, ks].astype(BF16)], axis=0)
            vall = jnp.concatenate([vall, vc_ref[0, 0, :, ks].astype(BF16)], axis=0)
        for r0 in range(0, qb, rb):
            rs = slice(r0, r0 + rb)
            q = (jnp.concatenate([q_ref[0, rs, sl] for sl in hsl], axis=0) * qscale).astype(BF16)
            s = lax.dot_general(q, kall, NT_DIMS, preferred_element_type=F32)
            if n_tab:
                tab = jnp.concatenate(
                    [tab_ref[0, tix, h if tab_heads == N_HEADS else 0, rs, :] for h in heads], axis=0)
                s = jnp.concatenate([s[:, :kw] + tab, s[:, kw:]], axis=1) if has_ctx else s + tab
            m = jnp.max(s, axis=-1, keepdims=True)
            if has_sink:
                sink = jnp.concatenate([jnp.full((rb, 1), sink_ref[h] * LOG2E, F32) for h in heads], axis=0)
                m = jnp.maximum(m, sink)
            p = jnp.exp2(s - m)
            den = jnp.sum(p, axis=-1, keepdims=True)
            acc = jnp.dot(p.astype(BF16), vall, preferred_element_type=F32)
            if has_sink:
                den = den + jnp.exp2(sink - m)
            o = acc / den
            for gi, sl in enumerate(hsl):
                o_ref[0, rs, sl] = (o[gi * rb:(gi + 1) * rb] * _silu(g_ref[0, rs, sl])).astype(o_ref.dtype)


def _attention(p, q_col, k_col, v_col, g_col, n_kv, qb, kw, back, ctx_k=None, ctx_v=None,
               table=None, sink=None, layer=0):
    bsz, slen, _ = p.shape
    kvw = n_kv * HEAD_DIM
    in_specs = [pl.BlockSpec((1, qb, GROUP_W), lambda b, n: (b, n, q_col)),
                pl.BlockSpec((1, slen, kvw), lambda b, n: (b, 0, k_col)),
                pl.BlockSpec((1, slen, kvw), lambda b, n: (b, 0, v_col)),
                pl.BlockSpec((1, qb, GROUP_W), lambda b, n: (b, n, g_col))]
    args = [p, p, p, p]
    has_ctx = ctx_k is not None
    if has_ctx:
        past = ctx_k.shape[2]
        in_specs += [pl.BlockSpec((1, 1, past, kvw), lambda b, n: (b, layer, 0, 0))] * 2
        args += [ctx_k, ctx_v]
    n_tab = tab_heads = 0
    if table is not None:
        n_tab, tab_heads = table.shape[1], table.shape[2]
        tab_layer = layer if table.shape[0] > 1 else 0
        in_specs.append(pl.BlockSpec((1,) + table.shape[1:], lambda b, n: (tab_layer, 0, 0, 0, 0)))
        args.append(table)
    if sink is not None:
        in_specs.append(pl.BlockSpec(memory_space=pltpu.SMEM))
        args.append(sink)
    kern = functools.partial(_attn_kernel, n_kv=n_kv, qb=qb, kw=kw, back=back, slen=slen, has_ctx=has_ctx,
                             n_tab=n_tab, tab_heads=tab_heads, has_sink=sink is not None)
    return pl.pallas_call(
        kern,
        grid=(bsz, slen // qb),
        in_specs=in_specs,
        out_specs=pl.BlockSpec((1, qb, GROUP_W), lambda b, n: (b, n, 0)),
        out_shape=jax.ShapeDtypeStruct((bsz, slen, GROUP_W), BF16),
        compiler_params=_cparams(("arbitrary", "arbitrary")),
        name="attn",
    )(*args)


WIN_QB = 256
WIN_BACK = WINDOW
WIN_KW = WIN_QB + 2 * WINDOW
NA_QROWS = 4
NA_KROWS = 12
NA_QB = NA_QROWS * GRID_W
NA_KW = NA_KROWS * GRID_W


def _window_table(slen):
    nblk = slen // WIN_QB
    tabs = []
    for n in (0, 1, nblk - 1):
        ws = int(np.clip(n * WIN_QB - WIN_BACK, 0, slen - WIN_KW))
        qpos = n * WIN_QB + np.arange(WIN_QB)[:, None]
        kpos = ws + np.arange(WIN_KW)[None, :]
        tabs.append(np.where(np.abs(qpos - kpos) <= WINDOW, 0.0, NEG))
    return jnp.asarray(np.stack(tabs)[None, :, None], F32)


N_RPB_R = 2 * NA_ROWS - 1
N_RPB_C = 2 * NA_COLS - 1


def _na_table_kernel(rpb_ref, o_ref, tz_scr, *, rows):
    base = (pl.program_id(0) * N_HEADS + pl.program_id(1)) * (N_RPB_R * N_RPB_C)
    qc = lax.broadcasted_iota(jnp.int32, (GRID_W, 2 * GRID_W), 0)
    kk = lax.broadcasted_iota(jnp.int32, (GRID_W, 2 * GRID_W), 1)
    kc = kk & (GRID_W - 1)
    diff = kc - qc
    qws = jnp.clip(qc - NA_COLS // 2, 0, GRID_W - NA_COLS)
    col_ok = (kc >= qws) & (kc < qws + NA_COLS)
    neg = jnp.full((GRID_W, 2 * GRID_W), NEG, F32)
    for dr in range(N_RPB_R):
        acc = neg
        for m in range(N_RPB_C):
            acc = jnp.where(diff == m - (NA_COLS - 1), rpb_ref[base + dr * N_RPB_C + m] * LOG2E, acc)
        tz_scr[dr] = jnp.where(col_ok, acc, NEG)
    wr = min(NA_ROWS, rows)
    nblk = rows // NA_QROWS
    for ti, g in enumerate((0, 1, nblk - 1)):
        ws_row = min(max(g * NA_QROWS - NA_QROWS, 0), rows - NA_KROWS)
        for qr in range(NA_QROWS):
            r = g * NA_QROWS + qr
            rs = min(max(r - wr // 2, 0), rows - wr)
            for p in range(NA_KROWS // 2):
                halves = []
                for kr in (ws_row + 2 * p, ws_row + 2 * p + 1):
                    halves.append(tz_scr[kr - r + NA_ROWS - 1] if rs <= kr < rs + wr else neg)
                o_ref[0, ti, 0, qr * GRID_W:(qr + 1) * GRID_W, p * 2 * GRID_W:(p + 1) * 2 * GRID_W] = (
                    jnp.where(kk < GRID_W, halves[0], halves[1]))


def _na_tables(na_rpb, slen):
    depth = na_rpb.shape[0]
    return pl.pallas_call(
        functools.partial(_na_table_kernel, rows=slen // GRID_W),
        grid=(depth, N_HEADS),
        in_specs=[pl.BlockSpec(memory_space=pltpu.SMEM)],
        out_specs=pl.BlockSpec((1, 3, 1, NA_QB, NA_KW), lambda l, h: (l, 0, h, 0, 0)),
        out_shape=jax.ShapeDtypeStruct((depth, 3, N_HEADS, NA_QB, NA_KW), F32),
        scratch_shapes=[pltpu.VMEM((N_RPB_R, GRID_W, 2 * GRID_W), F32)],
        compiler_params=_cparams(("arbitrary", "arbitrary")),
        name="na_table",
    )(na_rpb.astype(F32).reshape(-1))


def _ret_kernel(q_ref, k_ref, v_ref, g_ref, lgl_ref, lgc_ref, *rest, slen, t, has_s0):
    if has_s0:
        s0_ref, o_ref, sb_scr, st_scr, dm_scr, dec_scr = rest
    else:
        o_ref, sfin_ref, sb_scr, st_scr, dm_scr, dec_scr = rest
    nc = slen // t
    masks = _head_masks(t)
    bmask = _block_ones()
    ones_bf16 = jnp.where(bmask, 1.0, 0.0).astype(BF16)
    lgf, lgb = lgl_ref[0], lgl_ref[1]

    @pl.when(pl.program_id(0) == 0)
    def _():
        ii = lax.broadcasted_iota(jnp.int32, (t, N_HEADS * t), 0)
        jj = lax.broadcasted_iota(jnp.int32, (t, N_HEADS * t), 1) & (t - 1)
        dist = (ii - jj).astype(F32)
        dm_scr[...] = (jnp.where(dist >= 0, jnp.exp(dist * lgc_ref[0]), 0.0)
                       + jnp.where(dist <= 0, jnp.exp(-dist * lgc_ref[1]), 0.0))
        idx = lax.broadcasted_iota(jnp.int32, (t, GROUP_W), 0).astype(F32)
        dec_scr[0] = jnp.exp((idx + 1.0) * lgf)
        dec_scr[1] = jnp.exp((t - 1.0 - idx) * lgf)
        dec_scr[2] = jnp.exp((t - idx) * lgb)
        dec_scr[3] = jnp.exp(idx * lgb)

    cdec_f = jnp.exp(float(t) * lgf)
    cdec_b = jnp.exp(float(t) * lgb)
    kscale = HEAD_DIM ** -0.5

    def state_update(st, k, v, kdec, cdec):
        upd = lax.dot_general(v, (k * kdec).astype(BF16), TN_DIMS, preferred_element_type=F32)
        return st * cdec + jnp.where(bmask, upd, 0.0)

    def init_state(direction):
        if has_s0:
            st_scr[...] = _expand_state(s0_ref[0, direction], bmask)
        else:
            st_scr[...] = jnp.zeros((GROUP_W, GROUP_W), F32)

    def emit_final(direction):
        if not has_s0:
            sfin_ref[0, direction] = _compact_state(st_scr[...], bmask)

    init_state(1)

    def sweep_bwd(ci, carry):
        c = nc - 1 - ci
        rows = pl.ds(pl.multiple_of(c * t, t), t)
        st = st_scr[...]
        sb_scr[c] = st.astype(BF16)
        k = k_ref[0, rows, :] * kscale
        st_scr[...] = state_update(st, k, v_ref[0, rows, :].astype(BF16), dec_scr[3], cdec_b)
        return carry

    lax.fori_loop(0, nc, sweep_bwd, 0)
    emit_final(1)
    init_state(0)

    def sweep_fwd(c, carry):
        rows = pl.ds(pl.multiple_of(c * t, t), t)
        q = q_ref[0, rows, :]
        k = k_ref[0, rows, :] * kscale
        v = v_ref[0, rows, :]
        ks = _stack_heads(k, masks).astype(BF16)
        vs = _stack_heads(v, masks).astype(BF16)
        sc = lax.dot_general(q.astype(BF16), ks, NT_DIMS, preferred_element_type=F32) * dm_scr[...]
        o = jnp.dot(sc.astype(BF16), vs, preferred_element_type=F32)
        st = st_scr[...]
        o = o + lax.dot_general((q * dec_scr[0]).astype(BF16), st.astype(BF16), NT_DIMS,
                                preferred_element_type=F32)
        o = o + lax.dot_general((q * dec_scr[2]).astype(BF16), sb_scr[c], NT_DIMS, preferred_element_type=F32)
        st_scr[...] = state_update(st, k, v.astype(BF16), dec_scr[1], cdec_f)
        mu = _head_sum(o, ones_bf16) * (1.0 / HEAD_DIM)
        d = o - mu
        var = _head_sum(d * d, ones_bf16) * (1.0 / HEAD_DIM)
        y = d * lax.rsqrt(var + EPS)
        o_ref[0, rows, :] = (y * _silu(g_ref[0, rows, :])).astype(o_ref.dtype)
        return carry

    lax.fori_loop(0, nc, sweep_fwd, 0, unroll=2)
    emit_final(0)


def _compact_states(s0):
    return jnp.swapaxes(s0, -1, -2).reshape(s0.shape[:-3] + (GROUP_W, HEAD_DIM))


def _uncompact_states(sc):
    bsz = sc.shape[0]
    return jnp.swapaxes(sc.reshape(bsz, 2, N_HEADS, HEAD_DIM, HEAD_DIM), -1, -2)


def _split3(x):
    h1 = x.astype(BF16)
    r1 = x - h1.astype(F32)
    h2 = r1.astype(BF16)
    return h1, h2, (r1 - h2.astype(F32)).astype(BF16)


def _expand_state(x, bmask):
    r = lax.broadcasted_iota(jnp.int32, (HEAD_DIM, GROUP_W), 0)
    c = lax.broadcasted_iota(jnp.int32, (HEAD_DIM, GROUP_W), 1) & (HEAD_DIM - 1)
    rep = jnp.where(r == c, 1.0, 0.0).astype(BF16)
    h1, h2, h3 = _split3(x)
    tiled = (jnp.dot(h1, rep, preferred_element_type=F32) + jnp.dot(h2, rep, preferred_element_type=F32)
             + jnp.dot(h3, rep, preferred_element_type=F32))
    return jnp.where(bmask, tiled, 0.0)


def _compact_state(st, bmask):
    r = lax.broadcasted_iota(jnp.int32, (GROUP_W, HEAD_DIM), 0) & (HEAD_DIM - 1)
    c = lax.broadcasted_iota(jnp.int32, (GROUP_W, HEAD_DIM), 1)
    fold = jnp.where(r == c, 1.0, 0.0).astype(BF16)
    h1, h2, h3 = _split3(jnp.where(bmask, st, 0.0))
    return (jnp.dot(h1, fold, preferred_element_type=F32) + jnp.dot(h2, fold, preferred_element_type=F32)
            + jnp.dot(h3, fold, preferred_element_type=F32))


def _state_specs(bsz, s0):
    spec = pl.BlockSpec((1, 2, GROUP_W, HEAD_DIM), lambda b: (b, 0, 0, 0))
    shape = jax.ShapeDtypeStruct((bsz, 2, GROUP_W, HEAD_DIM), F32)
    if s0 is None:
        return [], [], [spec], [shape]
    return [spec], [s0], [], []


RET_CHUNK = 256


def _retention(pc, log_gamma, s0=None, t=RET_CHUNK):
    bsz, slen, _ = pc.shape
    t = min(t, slen)
    lgl = jnp.repeat(log_gamma, HEAD_DIM, axis=1).reshape(2, 1, GROUP_W)
    lgc = jnp.repeat(log_gamma, t, axis=1).reshape(2, 1, N_HEADS * t)
    col = lambda j: pl.BlockSpec((1, slen, GROUP_W), lambda b: (b, 0, j))
    s_in_specs, s_args, s_out_specs, s_out_shape = _state_specs(bsz, s0)
    return pl.pallas_call(
        functools.partial(_ret_kernel, slen=slen, t=t, has_s0=s0 is not None),
        grid=(bsz,),
        in_specs=[col(0), col(1), col(2), col(3),
                  pl.BlockSpec((2, 1, GROUP_W), lambda b: (0, 0, 0)),
                  pl.BlockSpec((2, 1, N_HEADS * t), lambda b: (0, 0, 0))] + s_in_specs,
        out_specs=[pl.BlockSpec((1, slen, GROUP_W), lambda b: (b, 0, 0))] + s_out_specs,
        out_shape=[jax.ShapeDtypeStruct((bsz, slen, GROUP_W), BF16)] + s_out_shape,
        scratch_shapes=[pltpu.VMEM((slen // t, GROUP_W, GROUP_W), BF16), pltpu.VMEM((GROUP_W, GROUP_W), F32),
                        pltpu.VMEM((t, N_HEADS * t), F32), pltpu.VMEM((4, t, GROUP_W), F32)],
        compiler_params=_cparams(("arbitrary",)),
        name="retention",
    )(pc, pc, pc, pc, lgl, lgc, *s_args)


HGRN_DIRECT = 8
HGRN_BLOCK = 16
HGRN_SAFE_EXP2 = 80.0


def _split2(x):
    hi = x.astype(BF16)
    return hi, (x - hi.astype(F32)).astype(BF16)


def _hgrn_kernel(q_ref, zf_ref, zb_ref, v_ref, g_ref, lb_ref, *rest, slen, t, has_s0):
    if has_s0:
        s0_ref, o_ref, of_scr, st_scr, b_scr, k_scr = rest
    else:
        o_ref, sfin_ref, of_scr, st_scr, b_scr, k_scr = rest
    nc = slen // t
    masks = _head_masks(t)
    bmask = _block_ones()
    ones_bf16 = jnp.where(bmask, 1.0, 0.0).astype(BF16)
    r_i = lax.broadcasted_iota(jnp.int32, (t, t), 0)
    c_i = lax.broadcasted_iota(jnp.int32, (t, t), 1)
    sub = lax.broadcasted_iota(jnp.int32, (t, GROUP_W), 0) & (HGRN_DIRECT - 1)
    levels = []
    s = t // 2
    while s >= HGRN_DIRECT:
        levels.append(s)
        s //= 2
    r_w = lax.broadcasted_iota(jnp.int32, (t, N_HEADS * t), 0)
    c_w = lax.broadcasted_iota(jnp.int32, (t, N_HEADS * t), 1) & (t - 1)
    lvl_masks = []
    for s in levels:
        shift = (2 * s).bit_length() - 1
        lvl_masks.append(jnp.where((r_w >> shift) == (c_w >> shift), 1.0, 0.0))

    def block_roll(x, shift):
        x3 = x.reshape(t // HGRN_DIRECT, HGRN_DIRECT, GROUP_W)
        return pltpu.roll(x3, shift, 1).reshape(t, GROUP_W)

    def run(direction, z_ref, emit):
        fwd = direction == 0
        lbd = jnp.maximum(lb_ref[direction], LB_FLOOR)
        tri = jnp.where((r_i >= c_i) if fwd else (r_i <= c_i), 1.0, 0.0).astype(BF16)
        blk_shift = HGRN_BLOCK.bit_length() - 1
        near_mask = jnp.where(((r_w >> blk_shift) == (c_w >> blk_shift))
                              & ((c_w <= r_w) if fwd else (c_w >= r_w)), 1.0, 0.0)
        if has_s0:
            st_scr[...] = _expand_state(s0_ref[0, direction], bmask)
        else:
            st_scr[...] = jnp.zeros((GROUP_W, GROUP_W), F32)

        def gates(c, worst):
            rows = pl.ds(pl.multiple_of(c * t, t), t)
            z = z_ref[0, rows, :]
            a = jnp.exp(-jnp.abs(z))
            inv = 1.0 / (1.0 + a)
            pos = z >= 0
            sig = jnp.where(pos, inv, a * inv)
            nsig = jnp.where(pos, a * inv, inv)
            lf = jnp.log(lbd + (1.0 - lbd) * sig)
            k_scr[rows, :] = (1.0 - lbd) * nsig
            h1, h2 = _split2(lf)
            b = (jnp.dot(tri, h1, preferred_element_type=F32) + jnp.dot(tri, h2, preferred_element_type=F32)) * LOG2E
            b_scr[rows, :] = b
            for gi in range(t // HGRN_BLOCK):
                first = b[gi * HGRN_BLOCK:gi * HGRN_BLOCK + 1]
                last = b[(gi + 1) * HGRN_BLOCK - 1:(gi + 1) * HGRN_BLOCK]
                worst = jnp.maximum(worst, jnp.abs(first - last))
            return worst

        worst = lax.fori_loop(0, nc, gates, jnp.zeros((1, GROUP_W), F32), unroll=2)
        factored_ok = jnp.max(worst) < HGRN_SAFE_EXP2

        def body(ci, carry, *, factored):
            c = ci if fwd else nc - 1 - ci
            rows = pl.ds(pl.multiple_of(c * t, t), t)
            q = _silu(q_ref[0, rows, :])
            v = v_ref[0, rows, :]
            vb = v.astype(BF16)
            k = k_scr[rows, :]
            b = b_scr[rows, :]
            vs = _stack_heads(v, masks).astype(BF16)

            def level_scores(s, lm):
                pa, pb = [], []
                zero = jnp.zeros((s, GROUP_W), F32)
                for gi in range(t // (2 * s)):
                    b_lo = b[gi * 2 * s:gi * 2 * s + s]
                    b_hi = b[gi * 2 * s + s:(gi + 1) * 2 * s]
                    if fwd:
                        anc = b_hi[0:1]
                        pa += [zero, jnp.exp2(b_hi - anc)]
                        pb += [jnp.exp2(anc - b_lo), zero]
                    else:
                        anc = b_lo[s - 1:s]
                        pa += [jnp.exp2(b_lo - anc), zero]
                        pb += [zero, jnp.exp2(anc - b_hi)]
                qa = (q * jnp.concatenate(pa, axis=0)).astype(BF16)
                kb = _stack_heads(k * jnp.concatenate(pb, axis=0), masks).astype(BF16)
                return lax.dot_general(qa, kb, NT_DIMS, preferred_element_type=F32) * lm

            def direct_pairs():
                prods, vals = [(q * k).astype(BF16)], [v]
                for dlt in range(1, HGRN_DIRECT):
                    sh = dlt if fwd else HGRN_DIRECT - dlt
                    ok = (sub >= dlt) if fwd else (sub + dlt <= HGRN_DIRECT - 1)
                    w = jnp.exp2(jnp.minimum(b - block_roll(b, sh), 0.0))
                    prods.append(jnp.where(ok, q * block_roll(k, sh) * w, 0.0).astype(BF16))
                    vals.append(block_roll(v, sh))
                hs = jnp.dot(jnp.concatenate(prods, axis=0), ones_bf16, preferred_element_type=F32)
                out = hs[0:t] * vals[0]
                for dlt in range(1, HGRN_DIRECT):
                    out = out + hs[dlt * t:(dlt + 1) * t] * vals[dlt]
                return out

            sc = None
            for s, lm in zip(levels, lvl_masks):
                if s >= HGRN_BLOCK:
                    term = level_scores(s, lm)
                    sc = term if sc is None else sc + term

            if factored:
                ea, eb = [], []
                for gi in range(t // HGRN_BLOCK):
                    blk = b[gi * HGRN_BLOCK:(gi + 1) * HGRN_BLOCK]
                    anc = blk[0:1] if fwd else blk[HGRN_BLOCK - 1:HGRN_BLOCK]
                    ea.append(jnp.exp2(blk - anc))
                    eb.append(jnp.exp2(anc - blk))
                qa = (q * jnp.concatenate(ea, axis=0)).astype(BF16)
                kb = _stack_heads(k * jnp.concatenate(eb, axis=0), masks).astype(BF16)
                sc = sc + lax.dot_general(qa, kb, NT_DIMS, preferred_element_type=F32) * near_mask
                o = jnp.dot(sc.astype(BF16), vs, preferred_element_type=F32)
            else:
                for s, lm in zip(levels, lvl_masks):
                    if s < HGRN_BLOCK:
                        sc = sc + level_scores(s, lm)
                o = jnp.dot(sc.astype(BF16), vs, preferred_element_type=F32) + direct_pairs()

            st = st_scr[...]
            o = o + lax.dot_general((q * jnp.exp2(b)).astype(BF16), st.astype(BF16), NT_DIMS,
                                    preferred_element_type=F32)
            bl = b[t - 1:t] if fwd else b[0:1]
            upd = lax.dot_general(vb, (k * jnp.exp2(bl - b)).astype(BF16), TN_DIMS,
                                  preferred_element_type=F32)
            st_scr[...] = st * jnp.exp2(bl) + jnp.where(bmask, upd, 0.0)
            emit(rows, o)
            return carry

        @pl.when(factored_ok)
        def _():
            lax.fori_loop(0, nc, functools.partial(body, factored=True), 0, unroll=4)

        @pl.when(jnp.logical_not(factored_ok))
        def _():
            lax.fori_loop(0, nc, functools.partial(body, factored=False), 0, unroll=2)

        if not has_s0:
            sfin_ref[0, direction] = _compact_state(st_scr[...], bmask)

    def emit_fwd(rows, o):
        of_scr[rows, :] = o

    def emit_bwd(rows, o):
        o = o + of_scr[rows, :]
        ms = _head_sum(o * o, ones_bf16) * (1.0 / HEAD_DIM)
        y = o * lax.rsqrt(ms + EPS)
        o_ref[0, rows, :] = (y * _silu(g_ref[0, rows, :])).astype(o_ref.dtype)

    run(0, zf_ref, emit_fwd)
    run(1, zb_ref, emit_bwd)


HGRN_CHUNK = 128


def _hgrn(pd, lb, s0=None, t=HGRN_CHUNK):
    bsz, slen, _ = pd.shape
    col = lambda j: pl.BlockSpec((1, slen, GROUP_W), lambda b: (b, 0, j))
    s_in_specs, s_args, s_out_specs, s_out_shape = _state_specs(bsz, s0)
    return pl.pallas_call(
        functools.partial(_hgrn_kernel, slen=slen, t=t, has_s0=s0 is not None),
        grid=(bsz,),
        in_specs=[col(0), col(1), col(2), col(3), col(4),
                  pl.BlockSpec((2, 1, GROUP_W), lambda b: (0, 0, 0))] + s_in_specs,
        out_specs=[pl.BlockSpec((1, slen, GROUP_W), lambda b: (b, 0, 0))] + s_out_specs,
        out_shape=[jax.ShapeDtypeStruct((bsz, slen, GROUP_W), BF16)] + s_out_shape,
        scratch_shapes=[pltpu.VMEM((slen, GROUP_W), F32), pltpu.VMEM((GROUP_W, GROUP_W), F32),
                        pltpu.VMEM((slen, GROUP_W), F32), pltpu.VMEM((slen, GROUP_W), F32)],
        compiler_params=_cparams(("arbitrary",)),
        name="hgrn2",
    )(pd, pd, pd, pd, pd, lb.reshape(2, 1, GROUP_W), *s_args)


def _outproj_kernel(ma_ref, mb_ref, mc_ref, md_ref, w_ref, x_ref, gate_ref, g_ref, o_ref, *, layer, mod_row):
    g = GROUP_W
    y = jnp.dot(ma_ref[0], w_ref[0, 0:g, :], preferred_element_type=F32)
    y = y + jnp.dot(mb_ref[0], w_ref[0, g:2 * g, :], preferred_element_type=F32)
    y = y + jnp.dot(mc_ref[0], w_ref[0, 2 * g:3 * g, :], preferred_element_type=F32)
    y = y + jnp.dot(md_ref[0], w_ref[0, 3 * g:4 * g, :], preferred_element_type=F32)
    ms = jnp.mean(y * y, axis=-1, keepdims=True)
    r = y * lax.rsqrt(ms + EPS) * g_ref[layer:layer + 1, :]
    o_ref[0] = x_ref[0] + _mod_row(gate_ref, mod_row) * r


OUTPROJ_TM = 1024


def _out_proj(mixed, w_out_bf16, x, mods, layer, mod_row, g_post):
    bsz, slen, d = x.shape
    depth = g_post.shape[0]
    tm = min(OUTPROJ_TM, slen)
    mspec = pl.BlockSpec((1, tm, GROUP_W), lambda b, i: (b, i, 0))
    return pl.pallas_call(
        functools.partial(_outproj_kernel, layer=layer, mod_row=mod_row),
        grid=(bsz, slen // tm),
        in_specs=[mspec, mspec, mspec, mspec,
                  pl.BlockSpec((1, d, d), lambda b, i: (layer, 0, 0)),
                  pl.BlockSpec((1, tm, d), lambda b, i: (b, i, 0)),
                  _mod_spec(2, layer),
                  pl.BlockSpec((depth, d), lambda b, i: (0, 0))],
        out_specs=pl.BlockSpec((1, tm, d), lambda b, i: (b, i, 0)),
        out_shape=jax.ShapeDtypeStruct((bsz, slen, d), F32),
        compiler_params=_cparams(("arbitrary", "arbitrary")),
        name="out_proj",
    )(*mixed, w_out_bf16, x, mods, g_post)


def _context_layer(x, mods, layer, mod_row, g_pre, g_post, w_in, w_out, sink, log_gamma, lb):
    bsz, slen, _ = x.shape
    pa, pb, pc, pd, ak, av, bk, bv = _in_proj(x, mods, layer, mod_row, g_pre, w_in, rope=False, emit_kv=True)
    o_a = _attention(pa, 0, 2, 3, 2, n_kv=A_KV, qb=slen, kw=slen, back=0, sink=sink)
    o_b = _attention(pb, 0, 1, 2, 3, n_kv=N_HEADS, qb=slen, kw=slen, back=0)
    o_c, s_c = _retention(pc, log_gamma)
    o_d, s_d = _hgrn(pd, lb)
    x = _out_proj((o_a, o_b, o_c, o_d), w_out, x, mods, layer, mod_row, g_post)
    kv4 = lambda t: t.reshape(bsz, slen, -1, HEAD_DIM)
    return x, (kv4(ak), kv4(av), kv4(bk), kv4(bv), _uncompact_states(s_c), _uncompact_states(s_d))


def _latent_layer(x, mods, layer, ca_k, ca_v, cb_k, cb_v, st_c, st_d, g_pre, g_post, w_in, w_out, sink, win_tab,
                  na_tabs, log_gamma, lb):
    pa, pb, pc, pd = _in_proj(x, mods, layer, None, g_pre, w_in, rope=True)
    o_a = _attention(pa, 0, 2, 3, 2, n_kv=A_KV, qb=WIN_QB, kw=WIN_KW, back=WIN_BACK,
                     ctx_k=ca_k, ctx_v=ca_v, table=win_tab, sink=sink, layer=layer)
    o_b = _attention(pb, 0, 1, 2, 3, n_kv=N_HEADS, qb=NA_QB, kw=NA_KW, back=NA_QB,
                     ctx_k=cb_k, ctx_v=cb_v, table=na_tabs, layer=layer)
    o_c, = _retention(pc, log_gamma, st_c)
    o_d, = _hgrn(pd, lb, st_d)
    return _out_proj((o_a, o_b, o_c, o_d), w_out, x, mods, layer, None, g_post)


def kernel(x_prompt, x_sample, c, cache_win_k, cache_win_v, cache_na_k, cache_na_v, state_ret, state_hgrn,
           c_ctx, w_ada, b_ada, g_pre, g_post, w_in, w_out, attn_sink, na_rpb, ret_decay_logit, hgrn_lb_logit):
    depth = w_ada.shape[0]
    dec_b, dec_s, d = x_sample.shape
    p_lb = jax.nn.softmax(hgrn_lb_logit.astype(F32), axis=0)
    lower_bounds = jnp.cumsum(p_lb, axis=0) - p_lb[0:1]
    log_gammas = jax.nn.log_sigmoid(ret_decay_logit.astype(F32))
    w_in_b = w_in.astype(BF16)
    w_out_b = w_out.astype(BF16)

    cvecs = jnp.zeros((ADA_ROWS, d), F32).at[:dec_b].set(c).at[dec_b].set(c_ctx)
    mods = _adaln(cvecs, w_ada, b_ada)
    win_tab = _window_table(dec_s)
    na_tabs = _na_tables(na_rpb, dec_s)

    x = x_prompt
    outs = [[] for _ in range(6)]
    for l in range(depth):
        x, extra = _context_layer(x, mods, l, dec_b, g_pre, g_post, w_in_b, w_out_b,
                                  attn_sink[l], log_gammas[l], lower_bounds[l])
        for acc, e in zip(outs, extra):
            acc.append(e)
    y_prompt = x
    stacked = [jnp.stack(o, axis=1) for o in outs]

    past = cache_win_k.shape[2]
    ca_k = cache_win_k.reshape(dec_b, depth, past, A_KV * HEAD_DIM)
    ca_v = cache_win_v.reshape(dec_b, depth, past, A_KV * HEAD_DIM)
    cb_k = cache_na_k.reshape(dec_b, depth, past, GROUP_W)
    cb_v = cache_na_v.reshape(dec_b, depth, past, GROUP_W)
    st_c = _compact_states(state_ret)
    st_d = _compact_states(state_hgrn)
    x = x_sample
    for l in range(depth):
        x = _latent_layer(x, mods, l, ca_k, ca_v, cb_k, cb_v, st_c[:, l], st_d[:, l], g_pre, g_post,
                          w_in_b, w_out_b, attn_sink[l], win_tab, na_tabs, log_gammas[l], lower_bounds[l])
    return (y_prompt, x, *stacked)
```

```python
import functools

import numpy as np
import jax
import jax.numpy as jnp
from jax import lax
from jax.experimental import pallas as pl
from jax.experimental.pallas import tpu as pltpu

F32 = jnp.float32
BF16 = jnp.bfloat16

D_MODEL = 1024
DEPTH = 4
GRID_W = 64
HEAD_DIM = 64
N_HEADS = 4
GROUP_W = N_HEADS * HEAD_DIM
A_KV = 2
WINDOW = 128
NA_ROWS = 8
NA_COLS = 16
ROPE_BASE = 10000.0
EPS = 1e-6
NEG = -1e30
LB_FLOOR = 1e-30
LOG2E = 1.4426950408889634
W_A = 3 * GROUP_W
W_B = 4 * GROUP_W
W_C = 4 * GROUP_W
W_D = 5 * GROUP_W
IN_WIDTH = W_A + W_B + W_C + W_D

V7X_VMEM_LIMIT_BYTES = 56 * 1024 * 1024
ADA_ROWS = 16

NT_DIMS = (((1,), (1,)), ((), ()))
TN_DIMS = (((0,), (0,)), ((), ()))


def _cparams(sem, flags=None):
    return pltpu.CompilerParams(dimension_semantics=sem, vmem_limit_bytes=V7X_VMEM_LIMIT_BYTES, flags=flags)


def _sigmoid(x):
    return 1.0 / (1.0 + jnp.exp(-x))


def _silu(x):
    return x * _sigmoid(x)


def _head_masks(rows):
    lane = lax.broadcasted_iota(jnp.int32, (rows, GROUP_W), 1)
    return [(lane >= h * HEAD_DIM) & (lane < (h + 1) * HEAD_DIM) for h in range(N_HEADS)]


def _stack_heads(x, masks):
    return jnp.concatenate([jnp.where(m, x, 0.0) for m in masks], axis=0)


def _unstack_heads(x, masks, t):
    out = jnp.where(masks[0], x[0:t], 0.0)
    for h in range(1, N_HEADS):
        out = out + jnp.where(masks[h], x[h * t:(h + 1) * t], 0.0)
    return out


def _block_ones():
    shift = HEAD_DIM.bit_length() - 1
    r = lax.broadcasted_iota(jnp.int32, (GROUP_W, GROUP_W), 0) >> shift
    c = lax.broadcasted_iota(jnp.int32, (GROUP_W, GROUP_W), 1) >> shift
    return r == c


def _head_sum(x, ones_bf16):
    hi = x.astype(BF16)
    lo = (x - hi.astype(F32)).astype(BF16)
    return (jnp.dot(hi, ones_bf16, preferred_element_type=F32)
            + jnp.dot(lo, ones_bf16, preferred_element_type=F32))


def _adaln_kernel(c_ref, w_ref, b_ref, o_ref):
    s = _silu(c_ref[...]).astype(BF16)
    o_ref[0, 0] = jnp.dot(s, w_ref[0].astype(BF16), preferred_element_type=F32) + b_ref[0]


def _adaln(cvecs, w_ada, b_ada):
    depth, d, d3 = w_ada.shape
    tn = 512
    per = d // tn
    return pl.pallas_call(
        _adaln_kernel,
        grid=(depth, d3 // tn),
        in_specs=[pl.BlockSpec((ADA_ROWS, d), lambda l, j: (0, 0)),
                  pl.BlockSpec((1, d, tn), lambda l, j: (l, 0, j)),
                  pl.BlockSpec((1, 1, tn), lambda l, j: (l, 0, j))],
        out_specs=pl.BlockSpec((1, 1, ADA_ROWS, tn), lambda l, j: (l, j // per, 0, j % per)),
        out_shape=jax.ShapeDtypeStruct((depth, 3, ADA_ROWS, d), F32),
        compiler_params=_cparams(("arbitrary", "arbitrary")),
        name="adaln",
    )(cvecs, w_ada, b_ada.reshape(depth, 1, d3))


def _rope(x, cos, sin):
    w = x.shape[-1]
    lane = lax.broadcasted_iota(jnp.int32, x.shape, 1)
    first = (lane & 31) < 16
    swapped = jnp.where(first, pltpu.roll(x, w - 16, 1), pltpu.roll(x, 16, 1))
    return x * cos[:, :w] + swapped * sin[:, :w]


def _mod_row(mod_ref, mod_row):
    row = pl.program_id(0) if mod_row is None else mod_row
    return mod_ref[0, 0, pl.ds(row, 1), :]


def _inproj_kernel(*refs, rope, emit_kv, layer, mod_row):
    refs = list(refs)
    x_ref, shift_ref, scale_ref, g_ref, w_ref = refs[:5]
    if rope:
        cos_ref, sin_ref = refs[5:7]
    n_out = 8 if emit_kv else 4
    pa_ref, pb_ref, pc_ref, pd_ref = refs[-n_out:][:4]
    x = x_ref[0]
    ms = jnp.mean(x * x, axis=-1, keepdims=True)
    y = x * lax.rsqrt(ms + EPS) * g_ref[layer:layer + 1, :]
    h = (y * (1.0 + _mod_row(scale_ref, mod_row)) + _mod_row(shift_ref, mod_row)).astype(BF16)

    def mm(c0, c1):
        return jnp.dot(h, w_ref[0, :, c0:c1], preferred_element_type=F32)

    if rope:
        cos = cos_ref[...]
        sin = sin_ref[...]
    g = GROUP_W
    aq = mm(0, g)
    akv = mm(g, 2 * g)
    if rope:
        aq = _rope(aq, cos, sin)
        ak = _rope(akv[:, :g // 2], cos, sin)
        akv = jnp.concatenate([ak, akv[:, g // 2:]], axis=1)
    pa_ref[0, :, 0:g] = aq.astype(pa_ref.dtype)
    pa_ref[0, :, g:2 * g] = akv.astype(pa_ref.dtype)
    pa_ref[0, :, 2 * g:3 * g] = mm(2 * g, 3 * g).astype(pa_ref.dtype)
    pbv = mm(W_A, W_A + W_B)
    pb_ref[0] = pbv.astype(pb_ref.dtype)
    if emit_kv:
        ak_ref, av_ref, bk_ref, bv_ref = refs[-4:]
        ak_ref[0] = akv[:, :g // 2]
        av_ref[0] = akv[:, g // 2:]
        bk_ref[0] = pbv[:, g:2 * g]
        bv_ref[0] = pbv[:, 2 * g:3 * g]
    c0 = W_A + W_B
    cq = mm(c0, c0 + g)
    ck = mm(c0 + g, c0 + 2 * g)
    if rope:
        cq = _rope(cq, cos, sin)
        ck = _rope(ck, cos, sin)
    pc_ref[0, :, 0:g] = cq.astype(pc_ref.dtype)
    pc_ref[0, :, g:2 * g] = ck.astype(pc_ref.dtype)
    pc_ref[0, :, 2 * g:4 * g] = mm(c0 + 2 * g, c0 + 4 * g).astype(pc_ref.dtype)
    d0 = c0 + W_C
    pd_ref[0] = mm(d0, d0 + W_D).astype(pd_ref.dtype)


def _rope_tables(slen):
    t = np.arange(slen)
    nf = HEAD_DIM // 4
    freqs = ROPE_BASE ** (-np.arange(nf, dtype=np.float64) / nf)
    d = np.arange(HEAD_DIM)
    pos = np.where(d[None, :] < HEAD_DIM // 2, (t // GRID_W)[:, None], (t % GRID_W)[:, None])
    ang = pos * freqs[d % nf][None, :]
    sign = np.where((d % (2 * nf)) < nf, -1.0, 1.0)[None, :]
    cos = np.tile(np.cos(ang), (1, N_HEADS))
    sin = np.tile(np.sin(ang) * sign, (1, N_HEADS))
    return jnp.asarray(cos, F32), jnp.asarray(sin, F32)


def _mod_spec(which, layer):
    return pl.BlockSpec((1, 1, ADA_ROWS, D_MODEL), lambda b, i: (layer, which, 0, 0))


def _in_proj(x, mods, layer, mod_row, g_pre, w_in_bf16, rope, emit_kv=False, tm=256):
    bsz, slen, d = x.shape
    depth = g_pre.shape[0]
    in_specs = [pl.BlockSpec((1, tm, d), lambda b, i: (b, i, 0)),
                _mod_spec(0, layer),
                _mod_spec(1, layer),
                pl.BlockSpec((depth, d), lambda b, i: (0, 0)),
                pl.BlockSpec((1, d, IN_WIDTH), lambda b, i: (layer, 0, 0))]
    args = [x, mods, mods, g_pre, w_in_bf16]
    if rope:
        cos, sin = _rope_tables(slen)
        in_specs += [pl.BlockSpec((tm, GROUP_W), lambda b, i: (i, 0))] * 2
        args += [cos, sin]
    widths = (W_A, W_B, W_C, W_D)
    if emit_kv:
        widths += (A_KV * HEAD_DIM, A_KV * HEAD_DIM, GROUP_W, GROUP_W)
    return pl.pallas_call(
        functools.partial(_inproj_kernel, rope=rope, emit_kv=emit_kv, layer=layer, mod_row=mod_row),
        grid=(bsz, slen // tm),
        in_specs=in_specs,
        out_specs=[pl.BlockSpec((1, tm, w), lambda b, i: (b, i, 0)) for w in widths],
        out_shape=[jax.ShapeDtypeStruct((bsz, slen, w), F32) for w in widths],
        compiler_params=_cparams(("arbitrary", "arbitrary")),
        name="in_proj",
    )(*args)


ATTN_ROWS = 256


def _attn_kernel(*refs, n_kv, qb, kw, back, slen, has_ctx, n_tab, tab_heads, has_sink):
    it = iter(refs)
    q_ref, k_ref, v_ref, g_ref = next(it), next(it), next(it), next(it)
    kc_ref = vc_ref = tab_ref = sink_ref = None
    if has_ctx:
        kc_ref, vc_ref = next(it), next(it)
    if n_tab:
        tab_ref = next(it)
    if has_sink:
        sink_ref = next(it)
    o_ref = next(it)

    n = pl.program_id(1)
    nblk = slen // qb
    ws = pl.multiple_of(jnp.clip(n * qb - back, 0, slen - kw), HEAD_DIM)
    if n_tab == 3:
        tix = jnp.where(n == 0, 0, jnp.where(n == nblk - 1, 2, 1))
    else:
        tix = 0
    grp = N_HEADS // n_kv
    qscale = HEAD_DIM ** -0.5 * LOG2E
    pair_w = 2 * HEAD_DIM
    lane = lax.broadcasted_iota(jnp.int32, (1, pair_w), 1)
    half_mask = (lane < HEAD_DIM, lane >= HEAD_DIM)
    rb = min(qb, ATTN_ROWS)
    for j in range(N_HEADS // 2):
        cols = slice(j * pair_w, (j + 1) * pair_w)
        if grp == 1:
            kv_cols, kv_half = cols, (0, 1)
        else:
            kv_cols, kv_half = slice(0, pair_w), (j, j)
        kall = k_ref[0, pl.ds(ws, kw), kv_cols].astype(BF16)
        vall = v_ref[0, pl.ds(ws, kw), kv_cols]
        if has_ctx:
            kall = jnp.concatenate([kall, kc_ref[0, 0, :, kv_cols].astype(BF16)], axis=0)
            vall = jnp.concatenate([vall, vc_ref[0, 0, :, kv_cols]], axis=0)
        vaug = {hf: jnp.where(half_mask[hf], vall, 1.0).astype(BF16) for hf in set(kv_half)}
        for r0 in range(0, qb, rb):
            rs = slice(r0, r0 + rb)
            q128 = q_ref[0, rs, cols] * qscale
            q_other = pltpu.roll(q128, HEAD_DIM, 1) if kv_half != (0, 1) else None
            outs = []
            for i in range(2):
                hf = kv_half[i]
                qm = jnp.where(half_mask[hf], q128 if i == hf else q_other, 0.0).astype(BF16)
                s = lax.dot_general(qm, kall, NT_DIMS, preferred_element_type=F32)
                if n_tab:
                    tab = tab_ref[0, tix, 2 * j + i if tab_heads == N_HEADS else 0, rs, :]
                    s = jnp.concatenate([s[:, :kw] + tab, s[:, kw:]], axis=1) if has_ctx else s + tab
                m = jnp.max(s, axis=-1, keepdims=True)
                if has_sink:
                    sink = sink_ref[2 * j + i] * LOG2E
                    m = jnp.maximum(m, sink)
                p = jnp.exp2(s - m).astype(BF16)
                acc = jnp.dot(p, vaug[hf], preferred_element_type=F32)
                if has_sink:
                    acc = acc + jnp.where(half_mask[1 - hf], jnp.exp2(sink - m), 0.0)
                acc = acc / pltpu.roll(acc, HEAD_DIM, 1)
                outs.append(acc if i == hf else pltpu.roll(acc, HEAD_DIM, 1))
            out = jnp.where(half_mask[0], outs[0], outs[1]) * _silu(g_ref[0, rs, cols])
            o_ref[0, rs, cols] = out.astype(o_ref.dtype)


def _attention(p, q_col, k_col, v_col, g_col, n_kv, qb, kw, back, ctx_k=None, ctx_v=None,
               table=None, sink=None, layer=0):
    bsz, slen, _ = p.shape
    kvw = n_kv * HEAD_DIM
    in_specs = [pl.BlockSpec((1, qb, GROUP_W), lambda b, n: (b, n, q_col)),
                pl.BlockSpec((1, slen, kvw), lambda b, n: (b, 0, k_col)),
                pl.BlockSpec((1, slen, kvw), lambda b, n: (b, 0, v_col)),
                pl.BlockSpec((1, qb, GROUP_W), lambda b, n: (b, n, g_col))]
    args = [p, p, p, p]
    has_ctx = ctx_k is not None
    if has_ctx:
        past = ctx_k.shape[2]
        in_specs += [pl.BlockSpec((1, 1, past, kvw), lambda b, n: (b, layer, 0, 0))] * 2
        args += [ctx_k, ctx_v]
    n_tab = tab_heads = 0
    if table is not None:
        n_tab, tab_heads = table.shape[1], table.shape[2]
        tab_layer = layer if table.shape[0] > 1 else 0
        in_specs.append(pl.BlockSpec((1,) + table.shape[1:], lambda b, n: (tab_layer, 0, 0, 0, 0)))
        args.append(table)
    if sink is not None:
        in_specs.append(pl.BlockSpec(memory_space=pltpu.SMEM))
        args.append(sink)
    kern = functools.partial(_attn_kernel, n_kv=n_kv, qb=qb, kw=kw, back=back, slen=slen, has_ctx=has_ctx,
                             n_tab=n_tab, tab_heads=tab_heads, has_sink=sink is not None)
    return pl.pallas_call(
        kern,
        grid=(bsz, slen // qb),
        in_specs=in_specs,
        out_specs=pl.BlockSpec((1, qb, GROUP_W), lambda b, n: (b, n, 0)),
        out_shape=jax.ShapeDtypeStruct((bsz, slen, GROUP_W), BF16),
        compiler_params=_cparams(("arbitrary", "arbitrary")),
        name="attn",
    )(*args)


WIN_QB = 256
WIN_BACK = WINDOW
WIN_KW = WIN_QB + 2 * WINDOW
NA_QROWS = 4
NA_KROWS = 12
NA_QB = NA_QROWS * GRID_W
NA_KW = NA_KROWS * GRID_W


def _window_table(slen):
    nblk = slen // WIN_QB
    tabs = []
    for n in (0, 1, nblk - 1):
        ws = int(np.clip(n * WIN_QB - WIN_BACK, 0, slen - WIN_KW))
        qpos = n * WIN_QB + np.arange(WIN_QB)[:, None]
        kpos = ws + np.arange(WIN_KW)[None, :]
        tabs.append(np.where(np.abs(qpos - kpos) <= WINDOW, 0.0, NEG))
    return jnp.asarray(np.stack(tabs)[None, :, None], F32)


N_RPB_R = 2 * NA_ROWS - 1
N_RPB_C = 2 * NA_COLS - 1


def _na_table_kernel(rpb_ref, o_ref, tz_scr, *, rows):
    base = (pl.program_id(0) * N_HEADS + pl.program_id(1)) * (N_RPB_R * N_RPB_C)
    qc = lax.broadcasted_iota(jnp.int32, (GRID_W, 2 * GRID_W), 0)
    kk = lax.broadcasted_iota(jnp.int32, (GRID_W, 2 * GRID_W), 1)
    kc = kk & (GRID_W - 1)
    diff = kc - qc
    qws = jnp.clip(qc - NA_COLS // 2, 0, GRID_W - NA_COLS)
    col_ok = (kc >= qws) & (kc < qws + NA_COLS)
    neg = jnp.full((GRID_W, 2 * GRID_W), NEG, F32)
    for dr in range(N_RPB_R):
        acc = neg
        for m in range(N_RPB_C):
            acc = jnp.where(diff == m - (NA_COLS - 1), rpb_ref[base + dr * N_RPB_C + m] * LOG2E, acc)
        tz_scr[dr] = jnp.where(col_ok, acc, NEG)
    wr = min(NA_ROWS, rows)
    nblk = rows // NA_QROWS
    for ti, g in enumerate((0, 1, nblk - 1)):
        ws_row = min(max(g * NA_QROWS - NA_QROWS, 0), rows - NA_KROWS)
        for qr in range(NA_QROWS):
            r = g * NA_QROWS + qr
            rs = min(max(r - wr // 2, 0), rows - wr)
            for p in range(NA_KROWS // 2):
                halves = []
                for kr in (ws_row + 2 * p, ws_row + 2 * p + 1):
                    halves.append(tz_scr[kr - r + NA_ROWS - 1] if rs <= kr < rs + wr else neg)
                o_ref[0, ti, 0, qr * GRID_W:(qr + 1) * GRID_W, p * 2 * GRID_W:(p + 1) * 2 * GRID_W] = (
                    jnp.where(kk < GRID_W, halves[0], halves[1]))


def _na_tables(na_rpb, slen):
    depth = na_rpb.shape[0]
    return pl.pallas_call(
        functools.partial(_na_table_kernel, rows=slen // GRID_W),
        grid=(depth, N_HEADS),
        in_specs=[pl.BlockSpec(memory_space=pltpu.SMEM)],
        out_specs=pl.BlockSpec((1, 3, 1, NA_QB, NA_KW), lambda l, h: (l, 0, h, 0, 0)),
        out_shape=jax.ShapeDtypeStruct((depth, 3, N_HEADS, NA_QB, NA_KW), F32),
        scratch_shapes=[pltpu.VMEM((N_RPB_R, GRID_W, 2 * GRID_W), F32)],
        compiler_params=_cparams(("arbitrary", "arbitrary")),
        name="na_table",
    )(na_rpb.astype(F32).reshape(-1))


def _ret_kernel(q_ref, k_ref, v_ref, g_ref, lgl_ref, lgc_ref, *rest, slen, t, has_s0):
    if has_s0:
        s0_ref, o_ref, sb_scr, st_scr, dm_scr, dec_scr = rest
    else:
        o_ref, sfin_ref, sb_scr, st_scr, dm_scr, dec_scr = rest
    nc = slen // t
    masks = _head_masks(t)
    bmask = _block_ones()
    ones_bf16 = jnp.where(bmask, 1.0, 0.0).astype(BF16)
    lgf, lgb = lgl_ref[0], lgl_ref[1]

    @pl.when(pl.program_id(0) == 0)
    def _():
        ii = lax.broadcasted_iota(jnp.int32, (t, N_HEADS * t), 0)
        jj = lax.broadcasted_iota(jnp.int32, (t, N_HEADS * t), 1) & (t - 1)
        dist = (ii - jj).astype(F32)
        dm_scr[...] = (jnp.where(dist >= 0, jnp.exp(dist * lgc_ref[0]), 0.0)
                       + jnp.where(dist <= 0, jnp.exp(-dist * lgc_ref[1]), 0.0))
        idx = lax.broadcasted_iota(jnp.int32, (t, GROUP_W), 0).astype(F32)
        dec_scr[0] = jnp.exp((idx + 1.0) * lgf)
        dec_scr[1] = jnp.exp((t - 1.0 - idx) * lgf)
        dec_scr[2] = jnp.exp((t - idx) * lgb)
        dec_scr[3] = jnp.exp(idx * lgb)

    cdec_f = jnp.exp(float(t) * lgf)
    cdec_b = jnp.exp(float(t) * lgb)
    kscale = HEAD_DIM ** -0.5

    def state_update(st, k, v, kdec, cdec):
        upd = lax.dot_general(v, (k * kdec).astype(BF16), TN_DIMS, preferred_element_type=F32)
        return st * cdec + jnp.where(bmask, upd, 0.0)

    def init_state(direction):
        if has_s0:
            st_scr[...] = _expand_state(s0_ref[0, direction], bmask)
        else:
            st_scr[...] = jnp.zeros((GROUP_W, GROUP_W), F32)

    def emit_final(direction):
        if not has_s0:
            sfin_ref[0, direction] = _compact_state(st_scr[...], bmask)

    init_state(1)

    def sweep_bwd(ci, carry):
        c = nc - 1 - ci
        rows = pl.ds(pl.multiple_of(c * t, t), t)
        st = st_scr[...]
        sb_scr[c] = st.astype(BF16)
        k = k_ref[0, rows, :] * kscale
        st_scr[...] = state_update(st, k, v_ref[0, rows, :].astype(BF16), dec_scr[3], cdec_b)
        return carry

    lax.fori_loop(0, nc, sweep_bwd, 0)
    emit_final(1)
    init_state(0)

    def sweep_fwd(c, carry):
        rows = pl.ds(pl.multiple_of(c * t, t), t)
        q = q_ref[0, rows, :]
        k = k_ref[0, rows, :] * kscale
        v = v_ref[0, rows, :]
        ks = _stack_heads(k, masks).astype(BF16)
        vs = _stack_heads(v, masks).astype(BF16)
        sc = lax.dot_general(q.astype(BF16), ks, NT_DIMS, preferred_element_type=F32) * dm_scr[...]
        o = jnp.dot(sc.astype(BF16), vs, preferred_element_type=F32)
        st = st_scr[...]
        o = o + lax.dot_general((q * dec_scr[0]).astype(BF16), st.astype(BF16), NT_DIMS,
                                preferred_element_type=F32)
        o = o + lax.dot_general((q * dec_scr[2]).astype(BF16), sb_scr[c], NT_DIMS, preferred_element_type=F32)
        st_scr[...] = state_update(st, k, v.astype(BF16), dec_scr[1], cdec_f)
        mu = _head_sum(o, ones_bf16) * (1.0 / HEAD_DIM)
        d = o - mu
        var = _head_sum(d * d, ones_bf16) * (1.0 / HEAD_DIM)
        y = d * lax.rsqrt(var + EPS)
        o_ref[0, rows, :] = (y * _silu(g_ref[0, rows, :])).astype(o_ref.dtype)
        return carry

    lax.fori_loop(0, nc, sweep_fwd, 0, unroll=2)
    emit_final(0)


def _compact_states(s0):
    return jnp.swapaxes(s0, -1, -2).reshape(s0.shape[:-3] + (GROUP_W, HEAD_DIM))


def _uncompact_states(sc):
    bsz = sc.shape[0]
    return jnp.swapaxes(sc.reshape(bsz, 2, N_HEADS, HEAD_DIM, HEAD_DIM), -1, -2)


def _split3(x):
    h1 = x.astype(BF16)
    r1 = x - h1.astype(F32)
    h2 = r1.astype(BF16)
    return h1, h2, (r1 - h2.astype(F32)).astype(BF16)


def _expand_state(x, bmask):
    r = lax.broadcasted_iota(jnp.int32, (HEAD_DIM, GROUP_W), 0)
    c = lax.broadcasted_iota(jnp.int32, (HEAD_DIM, GROUP_W), 1) & (HEAD_DIM - 1)
    rep = jnp.where(r == c, 1.0, 0.0).astype(BF16)
    h1, h2, h3 = _split3(x)
    tiled = (jnp.dot(h1, rep, preferred_element_type=F32) + jnp.dot(h2, rep, preferred_element_type=F32)
             + jnp.dot(h3, rep, preferred_element_type=F32))
    return jnp.where(bmask, tiled, 0.0)


def _compact_state(st, bmask):
    r = lax.broadcasted_iota(jnp.int32, (GROUP_W, HEAD_DIM), 0) & (HEAD_DIM - 1)
    c = lax.broadcasted_iota(jnp.int32, (GROUP_W, HEAD_DIM), 1)
    fold = jnp.where(r == c, 1.0, 0.0).astype(BF16)
    h1, h2, h3 = _split3(jnp.where(bmask, st, 0.0))
    return (jnp.dot(h1, fold, preferred_element_type=F32) + jnp.dot(h2, fold, preferred_element_type=F32)
            + jnp.dot(h3, fold, preferred_element_type=F32))


def _state_specs(bsz, s0):
    spec = pl.BlockSpec((1, 2, GROUP_W, HEAD_DIM), lambda b: (b, 0, 0, 0))
    shape = jax.ShapeDtypeStruct((bsz, 2, GROUP_W, HEAD_DIM), F32)
    if s0 is None:
        return [], [], [spec], [shape]
    return [spec], [s0], [], []


RET_CHUNK = 256


def _retention(pc, log_gamma, s0=None, t=RET_CHUNK):
    bsz, slen, _ = pc.shape
    t = min(t, slen)
    lgl = jnp.repeat(log_gamma, HEAD_DIM, axis=1).reshape(2, 1, GROUP_W)
    lgc = jnp.repeat(log_gamma, t, axis=1).reshape(2, 1, N_HEADS * t)
    col = lambda j: pl.BlockSpec((1, slen, GROUP_W), lambda b: (b, 0, j))
    s_in_specs, s_args, s_out_specs, s_out_shape = _state_specs(bsz, s0)
    return pl.pallas_call(
        functools.partial(_ret_kernel, slen=slen, t=t, has_s0=s0 is not None),
        grid=(bsz,),
        in_specs=[col(0), col(1), col(2), col(3),
                  pl.BlockSpec((2, 1, GROUP_W), lambda b: (0, 0, 0)),
                  pl.BlockSpec((2, 1, N_HEADS * t), lambda b: (0, 0, 0))] + s_in_specs,
        out_specs=[pl.BlockSpec((1, slen, GROUP_W), lambda b: (b, 0, 0))] + s_out_specs,
        out_shape=[jax.ShapeDtypeStruct((bsz, slen, GROUP_W), BF16)] + s_out_shape,
        scratch_shapes=[pltpu.VMEM((slen // t, GROUP_W, GROUP_W), BF16), pltpu.VMEM((GROUP_W, GROUP_W), F32),
                        pltpu.VMEM((t, N_HEADS * t), F32), pltpu.VMEM((4, t, GROUP_W), F32)],
        compiler_params=_cparams(("arbitrary",)),
        name="retention",
    )(pc, pc, pc, pc, lgl, lgc, *s_args)


HGRN_DIRECT = 8
HGRN_BLOCK = 16
HGRN_SAFE_EXP2 = 80.0


def _split2(x):
    hi = x.astype(BF16)
    return hi, (x - hi.astype(F32)).astype(BF16)


def _hgrn_kernel(q_ref, zf_ref, zb_ref, v_ref, g_ref, lb_ref, *rest, slen, t, has_s0):
    if has_s0:
        s0_ref, o_ref, of_scr, st_scr, b_scr, k_scr = rest
    else:
        o_ref, sfin_ref, of_scr, st_scr, b_scr, k_scr = rest
    nc = slen // t
    masks = _head_masks(t)
    bmask = _block_ones()
    ones_bf16 = jnp.where(bmask, 1.0, 0.0).astype(BF16)
    r_i = lax.broadcasted_iota(jnp.int32, (t, t), 0)
    c_i = lax.broadcasted_iota(jnp.int32, (t, t), 1)
    sub = lax.broadcasted_iota(jnp.int32, (t, GROUP_W), 0) & (HGRN_DIRECT - 1)
    levels = []
    s = t // 2
    while s >= HGRN_DIRECT:
        levels.append(s)
        s //= 2
    r_w = lax.broadcasted_iota(jnp.int32, (t, N_HEADS * t), 0)
    c_w = lax.broadcasted_iota(jnp.int32, (t, N_HEADS * t), 1) & (t - 1)
    lvl_masks = []
    for s in levels:
        shift = (2 * s).bit_length() - 1
        lvl_masks.append(jnp.where((r_w >> shift) == (c_w >> shift), 1.0, 0.0))

    def block_roll(x, shift):
        x3 = x.reshape(t // HGRN_DIRECT, HGRN_DIRECT, GROUP_W)
        return pltpu.roll(x3, shift, 1).reshape(t, GROUP_W)

    def run(direction, z_ref, emit):
        fwd = direction == 0
        lbd = jnp.maximum(lb_ref[direction], LB_FLOOR)
        tri = jnp.where((r_i >= c_i) if fwd else (r_i <= c_i), 1.0, 0.0).astype(BF16)
        blk_shift = HGRN_BLOCK.bit_length() - 1
        near_mask = jnp.where(((r_w >> blk_shift) == (c_w >> blk_shift))
                              & ((c_w <= r_w) if fwd else (c_w >= r_w)), 1.0, 0.0)
        if has_s0:
            st_scr[...] = _expand_state(s0_ref[0, direction], bmask)
        else:
            st_scr[...] = jnp.zeros((GROUP_W, GROUP_W), F32)

        def gates(c, worst):
            rows = pl.ds(pl.multiple_of(c * t, t), t)
            z = z_ref[0, rows, :]
            a = jnp.exp(-jnp.abs(z))
            inv = 1.0 / (1.0 + a)
            pos = z >= 0
            sig = jnp.where(pos, inv, a * inv)
            nsig = jnp.where(pos, a * inv, inv)
            lf = jnp.log(lbd + (1.0 - lbd) * sig)
            k_scr[rows, :] = (1.0 - lbd) * nsig
            h1, h2 = _split2(lf)
            b = (jnp.dot(tri, h1, preferred_element_type=F32) + jnp.dot(tri, h2, preferred_element_type=F32)) * LOG2E
            b_scr[rows, :] = b
            for gi in range(t // HGRN_BLOCK):
                first = b[gi * HGRN_BLOCK:gi * HGRN_BLOCK + 1]
                last = b[(gi + 1) * HGRN_BLOCK - 1:(gi + 1) * HGRN_BLOCK]
                worst = jnp.maximum(worst, jnp.abs(first - last))
            return worst

        worst = lax.fori_loop(0, nc, gates, jnp.zeros((1, GROUP_W), F32), unroll=2)
        factored_ok = jnp.max(worst) < HGRN_SAFE_EXP2

        def body(ci, carry, *, factored):
            c = ci if fwd else nc - 1 - ci
            rows = pl.ds(pl.multiple_of(c * t, t), t)
            q = _silu(q_ref[0, rows, :])
            v = v_ref[0, rows, :]
            vb = v.astype(BF16)
            k = k_scr[rows, :]
            b = b_scr[rows, :]
            vs = _stack_heads(v, masks).astype(BF16)

            def level_scores(s, lm):
                pa, pb = [], []
                zero = jnp.zeros((s, GROUP_W), F32)
                for gi in range(t // (2 * s)):
                    b_lo = b[gi * 2 * s:gi * 2 * s + s]
                    b_hi = b[gi * 2 * s + s:(gi + 1) * 2 * s]
                    if fwd:
                        anc = b_hi[0:1]
                        pa += [zero, jnp.exp2(b_hi - anc)]
                        pb += [jnp.exp2(anc - b_lo), zero]
                    else:
                        anc = b_lo[s - 1:s]
                        pa += [jnp.exp2(b_lo - anc), zero]
                        pb += [zero, jnp.exp2(anc - b_hi)]
                qa = (q * jnp.concatenate(pa, axis=0)).astype(BF16)
                kb = _stack_heads(k * jnp.concatenate(pb, axis=0), masks).astype(BF16)
                return lax.dot_general(qa, kb, NT_DIMS, preferred_element_type=F32) * lm

            def direct_pairs():
                prods, vals = [(q * k).astype(BF16)], [v]
                for dlt in range(1, HGRN_DIRECT):
                    sh = dlt if fwd else HGRN_DIRECT - dlt
                    ok = (sub >= dlt) if fwd else (sub + dlt <= HGRN_DIRECT - 1)
                    w = jnp.exp2(jnp.minimum(b - block_roll(b, sh), 0.0))
                    prods.append(jnp.where(ok, q * block_roll(k, sh) * w, 0.0).astype(BF16))
                    vals.append(block_roll(v, sh))
                hs = jnp.dot(jnp.concatenate(prods, axis=0), ones_bf16, preferred_element_type=F32)
                out = hs[0:t] * vals[0]
                for dlt in range(1, HGRN_DIRECT):
                    out = out + hs[dlt * t:(dlt + 1) * t] * vals[dlt]
                return out

            sc = None
            for s, lm in zip(levels, lvl_masks):
                if s >= HGRN_BLOCK:
                    term = level_scores(s, lm)
                    sc = term if sc is None else sc + term

            if factored:
                ea, eb = [], []
                for gi in range(t // HGRN_BLOCK):
                    blk = b[gi * HGRN_BLOCK:(gi + 1) * HGRN_BLOCK]
                    anc = blk[0:1] if fwd else blk[HGRN_BLOCK - 1:HGRN_BLOCK]
                    ea.append(jnp.exp2(blk - anc))
                    eb.append(jnp.exp2(anc - blk))
                qa = (q * jnp.concatenate(ea, axis=0)).astype(BF16)
                kb = _stack_heads(k * jnp.concatenate(eb, axis=0), masks).astype(BF16)
                sc = sc + lax.dot_general(qa, kb, NT_DIMS, preferred_element_type=F32) * near_mask
                o = jnp.dot(sc.astype(BF16), vs, preferred_element_type=F32)
            else:
                for s, lm in zip(levels, lvl_masks):
                    if s < HGRN_BLOCK:
                        sc = sc + level_scores(s, lm)
                o = jnp.dot(sc.astype(BF16), vs, preferred_element_type=F32) + direct_pairs()

            st = st_scr[...]
            o = o + lax.dot_general((q * jnp.exp2(b)).astype(BF16), st.astype(BF16), NT_DIMS,
                                    preferred_element_type=F32)
            bl = b[t - 1:t] if fwd else b[0:1]
            upd = lax.dot_general(vb, (k * jnp.exp2(bl - b)).astype(BF16), TN_DIMS,
                                  preferred_element_type=F32)
            st_scr[...] = st * jnp.exp2(bl) + jnp.where(bmask, upd, 0.0)
            emit(rows, o)
            return carry

        @pl.when(factored_ok)
        def _():
            lax.fori_loop(0, nc, functools.partial(body, factored=True), 0, unroll=4)

        @pl.when(jnp.logical_not(factored_ok))
        def _():
            lax.fori_loop(0, nc, functools.partial(body, factored=False), 0, unroll=2)

        if not has_s0:
            sfin_ref[0, direction] = _compact_state(st_scr[...], bmask)

    def emit_fwd(rows, o):
        of_scr[rows, :] = o

    def emit_bwd(rows, o):
        o = o + of_scr[rows, :]
        ms = _head_sum(o * o, ones_bf16) * (1.0 / HEAD_DIM)
        y = o * lax.rsqrt(ms + EPS)
        o_ref[0, rows, :] = (y * _silu(g_ref[0, rows, :])).astype(o_ref.dtype)

    run(0, zf_ref, emit_fwd)
    run(1, zb_ref, emit_bwd)


HGRN_CHUNK = 128


def _hgrn(pd, lb, s0=None, t=HGRN_CHUNK):
    bsz, slen, _ = pd.shape
    col = lambda j: pl.BlockSpec((1, slen, GROUP_W), lambda b: (b, 0, j))
    s_in_specs, s_args, s_out_specs, s_out_shape = _state_specs(bsz, s0)
    return pl.pallas_call(
        functools.partial(_hgrn_kernel, slen=slen, t=t, has_s0=s0 is not None),
        grid=(bsz,),
        in_specs=[col(0), col(1), col(2), col(3), col(4),
                  pl.BlockSpec((2, 1, GROUP_W), lambda b: (0, 0, 0))] + s_in_specs,
        out_specs=[pl.BlockSpec((1, slen, GROUP_W), lambda b: (b, 0, 0))] + s_out_specs,
        out_shape=[jax.ShapeDtypeStruct((bsz, slen, GROUP_W), BF16)] + s_out_shape,
        scratch_shapes=[pltpu.VMEM((slen, GROUP_W), F32), pltpu.VMEM((GROUP_W, GROUP_W), F32),
                        pltpu.VMEM((slen, GROUP_W), F32), pltpu.VMEM((slen, GROUP_W), F32)],
        compiler_params=_cparams(("arbitrary",)),
        name="hgrn2",
    )(pd, pd, pd, pd, pd, lb.reshape(2, 1, GROUP_W), *s_args)


def _outproj_kernel(ma_ref, mb_ref, mc_ref, md_ref, w_ref, x_ref, gate_ref, g_ref, o_ref, *, layer, mod_row):
    g = GROUP_W
    y = jnp.dot(ma_ref[0], w_ref[0, 0:g, :], preferred_element_type=F32)
    y = y + jnp.dot(mb_ref[0], w_ref[0, g:2 * g, :], preferred_element_type=F32)
    y = y + jnp.dot(mc_ref[0], w_ref[0, 2 * g:3 * g, :], preferred_element_type=F32)
    y = y + jnp.dot(md_ref[0], w_ref[0, 3 * g:4 * g, :], preferred_element_type=F32)
    ms = jnp.mean(y * y, axis=-1, keepdims=True)
    r = y * lax.rsqrt(ms + EPS) * g_ref[layer:layer + 1, :]
    o_ref[0] = x_ref[0] + _mod_row(gate_ref, mod_row) * r


OUTPROJ_TM = 1024


def _out_proj(mixed, w_out_bf16, x, mods, layer, mod_row, g_post):
    bsz, slen, d = x.shape
    depth = g_post.shape[0]
    tm = min(OUTPROJ_TM, slen)
    mspec = pl.BlockSpec((1, tm, GROUP_W), lambda b, i: (b, i, 0))
    return pl.pallas_call(
        functools.partial(_outproj_kernel, layer=layer, mod_row=mod_row),
        grid=(bsz, slen // tm),
        in_specs=[mspec, mspec, mspec, mspec,
                  pl.BlockSpec((1, d, d), lambda b, i: (layer, 0, 0)),
                  pl.BlockSpec((1, tm, d), lambda b, i: (b, i, 0)),
                  _mod_spec(2, layer),
                  pl.BlockSpec((depth, d), lambda b, i: (0, 0))],
        out_specs=pl.BlockSpec((1, tm, d), lambda b, i: (b, i, 0)),
        out_shape=jax.ShapeDtypeStruct((bsz, slen, d), F32),
        compiler_params=_cparams(("arbitrary", "arbitrary")),
        name="out_proj",
    )(*mixed, w_out_bf16, x, mods, g_post)


def _context_layer(x, mods, layer, mod_row, g_pre, g_post, w_in, w_out, sink, log_gamma, lb):
    bsz, slen, _ = x.shape
    pa, pb, pc, pd, ak, av, bk, bv = _in_proj(x, mods, layer, mod_row, g_pre, w_in, rope=False, emit_kv=True)
    o_a = _attention(pa, 0, 2, 3, 2, n_kv=A_KV, qb=slen, kw=slen, back=0, sink=sink)
    o_b = _attention(pb, 0, 1, 2, 3, n_kv=N_HEADS, qb=slen, kw=slen, back=0)
    o_c, s_c = _retention(pc, log_gamma)
    o_d, s_d = _hgrn(pd, lb)
    x = _out_proj((o_a, o_b, o_c, o_d), w_out, x, mods, layer, mod_row, g_post)
    kv4 = lambda t: t.reshape(bsz, slen, -1, HEAD_DIM)
    return x, (kv4(ak), kv4(av), kv4(bk), kv4(bv), _uncompact_states(s_c), _uncompact_states(s_d))


def _latent_layer(x, mods, layer, ca_k, ca_v, cb_k, cb_v, st_c, st_d, g_pre, g_post, w_in, w_out, sink, win_tab,
                  na_tabs, log_gamma, lb):
    pa, pb, pc, pd = _in_proj(x, mods, layer, None, g_pre, w_in, rope=True)
    o_a = _attention(pa, 0, 2, 3, 2, n_kv=A_KV, qb=WIN_QB, kw=WIN_KW, back=WIN_BACK,
                     ctx_k=ca_k, ctx_v=ca_v, table=win_tab, sink=sink, layer=layer)
    o_b = _attention(pb, 0, 1, 2, 3, n_kv=N_HEADS, qb=NA_QB, kw=NA_KW, back=NA_QB,
                     ctx_k=cb_k, ctx_v=cb_v, table=na_tabs, layer=layer)
    o_c, = _retention(pc, log_gamma, st_c)
    o_d, = _hgrn(pd, lb, st_d)
    return _out_proj((o_a, o_b, o_c, o_d), w_out, x, mods, layer, None, g_post)


def kernel(x_prompt, x_sample, c, cache_win_k, cache_win_v, cache_na_k, cache_na_v, state_ret, state_hgrn,
           c_ctx, w_ada, b_ada, g_pre, g_post, w_in, w_out, attn_sink, na_rpb, ret_decay_logit, hgrn_lb_logit):
    depth = w_ada.shape[0]
    dec_b, dec_s, d = x_sample.shape
    p_lb = jax.nn.softmax(hgrn_lb_logit.astype(F32), axis=0)
    lower_bounds = jnp.cumsum(p_lb, axis=0) - p_lb[0:1]
    log_gammas = jax.nn.log_sigmoid(ret_decay_logit.astype(F32))
    w_in_b = w_in.astype(BF16)
    w_out_b = w_out.astype(BF16)

    cvecs = jnp.zeros((ADA_ROWS, d), F32).at[:dec_b].set(c).at[dec_b].set(c_ctx)
    mods = _adaln(cvecs, w_ada, b_ada)
    win_tab = _window_table(dec_s)
    na_tabs = _na_tables(na_rpb, dec_s)

    x = x_prompt
    outs = [[] for _ in range(6)]
    for l in range(depth):
        x, extra = _context_layer(x, mods, l, dec_b, g_pre, g_post, w_in_b, w_out_b,
                                  attn_sink[l], log_gammas[l], lower_bounds[l])
        for acc, e in zip(outs, extra):
            acc.append(e)
    y_prompt = x
    stacked = [jnp.stack(o, axis=1) for o in outs]

    past = cache_win_k.shape[2]
    ca_k = cache_win_k.reshape(dec_b, depth, past, A_KV * HEAD_DIM)
    ca_v = cache_win_v.reshape(dec_b, depth, past, A_KV * HEAD_DIM)
    cb_k = cache_na_k.reshape(dec_b, depth, past, GROUP_W)
    cb_v = cache_na_v.reshape(dec_b, depth, past, GROUP_W)
    st_c = _compact_states(state_ret)
    st_d = _compact_states(state_hgrn)
    x = x_sample
    for l in range(depth):
        x = _latent_layer(x, mods, l, ca_k, ca_v, cb_k, cb_v, st_c[:, l], st_d[:, l], g_pre, g_post,
                          w_in_b, w_out_b, attn_sink[l], win_tab, na_tabs, log_gammas[l], lower_bounds[l])
    return (y_prompt, x, *stacked)
```

```python
import functools

import numpy as np
import jax
import jax.numpy as jnp
from jax import lax
from jax.experimental import pallas as pl
from jax.experimental.pallas import tpu as pltpu

F32 = jnp.float32
BF16 = jnp.bfloat16

D_MODEL = 1024
DEPTH = 4
GRID_W = 64
HEAD_DIM = 64
N_HEADS = 4
GROUP_W = N_HEADS * HEAD_DIM
A_KV = 2
WINDOW = 128
NA_ROWS = 8
NA_COLS = 16
ROPE_BASE = 10000.0
EPS = 1e-6
NEG = -1e30
LB_FLOOR = 1e-30
LOG2E = 1.4426950408889634
W_A = 3 * GROUP_W
W_B = 4 * GROUP_W
W_C = 4 * GROUP_W
W_D = 5 * GROUP_W
IN_WIDTH = W_A + W_B + W_C + W_D

V7X_VMEM_LIMIT_BYTES = 56 * 1024 * 1024
ADA_ROWS = 16

NT_DIMS = (((1,), (1,)), ((), ()))
TN_DIMS = (((0,), (0,)), ((), ()))


def _cparams(sem, flags=None):
    return pltpu.CompilerParams(dimension_semantics=sem, vmem_limit_bytes=V7X_VMEM_LIMIT_BYTES, flags=flags)


def _sigmoid(x):
    return 1.0 / (1.0 + jnp.exp(-x))


def _silu(x):
    return x * _sigmoid(x)


def _head_masks(rows):
    lane = lax.broadcasted_iota(jnp.int32, (rows, GROUP_W), 1)
    return [(lane >= h * HEAD_DIM) & (lane < (h + 1) * HEAD_DIM) for h in range(N_HEADS)]


def _stack_heads(x, masks):
    return jnp.concatenate([jnp.where(m, x, 0.0) for m in masks], axis=0)


def _unstack_heads(x, masks, t):
    out = jnp.where(masks[0], x[0:t], 0.0)
    for h in range(1, N_HEADS):
        out = out + jnp.where(masks[h], x[h * t:(h + 1) * t], 0.0)
    return out


def _block_ones():
    shift = HEAD_DIM.bit_length() - 1
    r = lax.broadcasted_iota(jnp.int32, (GROUP_W, GROUP_W), 0) >> shift
    c = lax.broadcasted_iota(jnp.int32, (GROUP_W, GROUP_W), 1) >> shift
    return r == c


def _head_sum(x, ones_bf16):
    hi = x.astype(BF16)
    lo = (x - hi.astype(F32)).astype(BF16)
    return (jnp.dot(hi, ones_bf16, preferred_element_type=F32)
            + jnp.dot(lo, ones_bf16, preferred_element_type=F32))


def _adaln_kernel(c_ref, w_ref, b_ref, o_ref):
    s = _silu(c_ref[...]).astype(BF16)
    o_ref[0, 0] = jnp.dot(s, w_ref[0].astype(BF16), preferred_element_type=F32) + b_ref[0]


def _adaln(cvecs, w_ada, b_ada):
    depth, d, d3 = w_ada.shape
    tn = 512
    per = d // tn
    return pl.pallas_call(
        _adaln_kernel,
        grid=(depth, d3 // tn),
        in_specs=[pl.BlockSpec((ADA_ROWS, d), lambda l, j: (0, 0)),
                  pl.BlockSpec((1, d, tn), lambda l, j: (l, 0, j)),
                  pl.BlockSpec((1, 1, tn), lambda l, j: (l, 0, j))],
        out_specs=pl.BlockSpec((1, 1, ADA_ROWS, tn), lambda l, j: (l, j // per, 0, j % per)),
        out_shape=jax.ShapeDtypeStruct((depth, 3, ADA_ROWS, d), F32),
        compiler_params=_cparams(("arbitrary", "arbitrary")),
        name="adaln",
    )(cvecs, w_ada, b_ada.reshape(depth, 1, d3))


def _rope(x, cos, sin):
    w = x.shape[-1]
    lane = lax.broadcasted_iota(jnp.int32, x.shape, 1)
    first = (lane & 31) < 16
    swapped = jnp.where(first, pltpu.roll(x, w - 16, 1), pltpu.roll(x, 16, 1))
    return x * cos[:, :w] + swapped * sin[:, :w]


def _mod_row(mod_ref, mod_row):
    row = pl.program_id(0) if mod_row is None else mod_row
    return mod_ref[0, 0, pl.ds(row, 1), :]


def _inproj_kernel(*refs, rope, emit_kv, layer, mod_row):
    refs = list(refs)
    x_ref, shift_ref, scale_ref, g_ref, w_ref = refs[:5]
    if rope:
        cos_ref, sin_ref = refs[5:7]
    n_out = 8 if emit_kv else 4
    pa_ref, pb_ref, pc_ref, pd_ref = refs[-n_out:][:4]
    x = x_ref[0]
    ms = jnp.mean(x * x, axis=-1, keepdims=True)
    y = x * lax.rsqrt(ms + EPS) * g_ref[layer:layer + 1, :]
    h = (y * (1.0 + _mod_row(scale_ref, mod_row)) + _mod_row(shift_ref, mod_row)).astype(BF16)

    def mm(c0, c1):
        return jnp.dot(h, w_ref[0, :, c0:c1], preferred_element_type=F32)

    if rope:
        cos = cos_ref[...]
        sin = sin_ref[...]
    g = GROUP_W
    aq = mm(0, g)
    akv = mm(g, 2 * g)
    if rope:
        aq = _rope(aq, cos, sin)
        ak = _rope(akv[:, :g // 2], cos, sin)
        akv = jnp.concatenate([ak, akv[:, g // 2:]], axis=1)
    pa_ref[0, :, 0:g] = aq.astype(pa_ref.dtype)
    pa_ref[0, :, g:2 * g] = akv.astype(pa_ref.dtype)
    pa_ref[0, :, 2 * g:3 * g] = mm(2 * g, 3 * g).astype(pa_ref.dtype)
    pbv = mm(W_A, W_A + W_B)
    pb_ref[0] = pbv.astype(pb_ref.dtype)
    if emit_kv:
        ak_ref, av_ref, bk_ref, bv_ref = refs[-4:]
        ak_ref[0] = akv[:, :g // 2]
        av_ref[0] = akv[:, g // 2:]
        bk_ref[0] = pbv[:, g:2 * g]
        bv_ref[0] = pbv[:, 2 * g:3 * g]
    c0 = W_A + W_B
    cq = mm(c0, c0 + g)
    ck = mm(c0 + g, c0 + 2 * g)
    if rope:
        cq = _rope(cq, cos, sin)
        ck = _rope(ck, cos, sin)
    pc_ref[0, :, 0:g] = cq.astype(pc_ref.dtype)
    pc_ref[0, :, g:2 * g] = ck.astype(pc_ref.dtype)
    pc_ref[0, :, 2 * g:4 * g] = mm(c0 + 2 * g, c0 + 4 * g).astype(pc_ref.dtype)
    d0 = c0 + W_C
    pd_ref[0] = mm(d0, d0 + W_D).astype(pd_ref.dtype)


def _rope_tables(slen):
    t = np.arange(slen)
    nf = HEAD_DIM // 4
    freqs = ROPE_BASE ** (-np.arange(nf, dtype=np.float64) / nf)
    d = np.arange(HEAD_DIM)
    pos = np.where(d[None, :] < HEAD_DIM // 2, (t // GRID_W)[:, None], (t % GRID_W)[:, None])
    ang = pos * freqs[d % nf][None, :]
    sign = np.where((d % (2 * nf)) < nf, -1.0, 1.0)[None, :]
    cos = np.tile(np.cos(ang), (1, N_HEADS))
    sin = np.tile(np.sin(ang) * sign, (1, N_HEADS))
    return jnp.asarray(cos, F32), jnp.asarray(sin, F32)


def _mod_spec(which, layer):
    return pl.BlockSpec((1, 1, ADA_ROWS, D_MODEL), lambda b, i: (layer, which, 0, 0))


def _in_proj(x, mods, layer, mod_row, g_pre, w_in_bf16, rope, emit_kv=False, tm=256):
    bsz, slen, d = x.shape
    depth = g_pre.shape[0]
    in_specs = [pl.BlockSpec((1, tm, d), lambda b, i: (b, i, 0)),
                _mod_spec(0, layer),
                _mod_spec(1, layer),
                pl.BlockSpec((depth, d), lambda b, i: (0, 0)),
                pl.BlockSpec((1, d, IN_WIDTH), lambda b, i: (layer, 0, 0))]
    args = [x, mods, mods, g_pre, w_in_bf16]
    if rope:
        cos, sin = _rope_tables(slen)
        in_specs += [pl.BlockSpec((tm, GROUP_W), lambda b, i: (i, 0))] * 2
        args += [cos, sin]
    widths = (W_A, W_B, W_C, W_D)
    if emit_kv:
        widths += (A_KV * HEAD_DIM, A_KV * HEAD_DIM, GROUP_W, GROUP_W)
    return pl.pallas_call(
        functools.partial(_inproj_kernel, rope=rope, emit_kv=emit_kv, layer=layer, mod_row=mod_row),
        grid=(bsz, slen // tm),
        in_specs=in_specs,
        out_specs=[pl.BlockSpec((1, tm, w), lambda b, i: (b, i, 0)) for w in widths],
        out_shape=[jax.ShapeDtypeStruct((bsz, slen, w), F32) for w in widths],
        compiler_params=_cparams(("arbitrary", "arbitrary")),
        name="in_proj",
    )(*args)


ATTN_ROWS = 256


def _attn_kernel(*refs, n_kv, qb, kw, back, slen, has_ctx, n_tab, tab_heads, has_sink):
    it = iter(refs)
    q_ref, k_ref, v_ref, g_ref = next(it), next(it), next(it), next(it)
    kc_ref = vc_ref = tab_ref = sink_ref = None
    if has_ctx:
        kc_ref, vc_ref = next(it), next(it)
    if n_tab:
        tab_ref = next(it)
    if has_sink:
        sink_ref = next(it)
    o_ref = next(it)

    n = pl.program_id(1)
    nblk = slen // qb
    ws = pl.multiple_of(jnp.clip(n * qb - back, 0, slen - kw), HEAD_DIM)
    if n_tab == 3:
        tix = jnp.where(n == 0, 0, jnp.where(n == nblk - 1, 2, 1))
    else:
        tix = 0
    grp = N_HEADS // n_kv
    qscale = HEAD_DIM ** -0.5 * LOG2E
    pair_w = 2 * HEAD_DIM
    lane = lax.broadcasted_iota(jnp.int32, (1, pair_w), 1)
    half_mask = (lane < HEAD_DIM, lane >= HEAD_DIM)
    rb = min(qb, ATTN_ROWS)
    for j in range(N_HEADS // 2):
        cols = slice(j * pair_w, (j + 1) * pair_w)
        if grp == 1:
            kv_cols, kv_half = cols, (0, 1)
        else:
            kv_cols, kv_half = slice(0, pair_w), (j, j)
        kall = k_ref[0, pl.ds(ws, kw), kv_cols].astype(BF16)
        vall = v_ref[0, pl.ds(ws, kw), kv_cols]
        if has_ctx:
            kall = jnp.concatenate([kall, kc_ref[0, 0, :, kv_cols].astype(BF16)], axis=0)
            vall = jnp.concatenate([vall, vc_ref[0, 0, :, kv_cols]], axis=0)
        vaug = {hf: jnp.where(half_mask[hf], vall, 1.0).astype(BF16) for hf in set(kv_half)}
        for r0 in range(0, qb, rb):
            rs = slice(r0, r0 + rb)
            q128 = q_ref[0, rs, cols] * qscale
            q_other = pltpu.roll(q128, HEAD_DIM, 1) if kv_half != (0, 1) else None
            outs = []
            for i in range(2):
                hf = kv_half[i]
                qm = jnp.where(half_mask[hf], q128 if i == hf else q_other, 0.0).astype(BF16)
                s = lax.dot_general(qm, kall, NT_DIMS, preferred_element_type=F32)
                if n_tab:
                    tab = tab_ref[0, tix, 2 * j + i if tab_heads == N_HEADS else 0, rs, :]
                    s = jnp.concatenate([s[:, :kw] + tab, s[:, kw:]], axis=1) if has_ctx else s + tab
                m = jnp.max(s, axis=-1, keepdims=True)
                if has_sink:
                    sink = sink_ref[2 * j + i] * LOG2E
                    m = jnp.maximum(m, sink)
                p = jnp.exp2(s - m).astype(BF16)
                acc = jnp.dot(p, vaug[hf], preferred_element_type=F32)
                if has_sink:
                    acc = acc + jnp.where(half_mask[1 - hf], jnp.exp2(sink - m), 0.0)
                acc = acc / pltpu.roll(acc, HEAD_DIM, 1)
                outs.append(acc if i == hf else pltpu.roll(acc, HEAD_DIM, 1))
            out = jnp.where(half_mask[0], outs[0], outs[1]) * _silu(g_ref[0, rs, cols])
            o_ref[0, rs, cols] = out.astype(o_ref.dtype)


def _attention(p, q_col, k_col, v_col, g_col, n_kv, qb, kw, back, ctx_k=None, ctx_v=None,
               table=None, sink=None, layer=0):
    bsz, slen, _ = p.shape
    kvw = n_kv * HEAD_DIM
    in_specs = [pl.BlockSpec((1, qb, GROUP_W), lambda b, n: (b, n, q_col)),
                pl.BlockSpec((1, slen, kvw), lambda b, n: (b, 0, k_col)),
                pl.BlockSpec((1, slen, kvw), lambda b, n: (b, 0, v_col)),
                pl.BlockSpec((1, qb, GROUP_W), lambda b, n: (b, n, g_col))]
    args = [p, p, p, p]
    has_ctx = ctx_k is not None
    if has_ctx:
        past = ctx_k.shape[2]
        in_specs += [pl.BlockSpec((1, 1, past, kvw), lambda b, n: (b, layer, 0, 0))] * 2
        args += [ctx_k, ctx_v]
    n_tab = tab_heads = 0
    if table is not None:
        n_tab, tab_heads = table.shape[1], table.shape[2]
        tab_layer = layer if table.shape[0] > 1 else 0
        in_specs.append(pl.BlockSpec((1,) + table.shape[1:], lambda b, n: (tab_layer, 0, 0, 0, 0)))
        args.append(table)
    if sink is not None:
        in_specs.append(pl.BlockSpec(memory_space=pltpu.SMEM))
        args.append(sink)
    kern = functools.partial(_attn_kernel, n_kv=n_kv, qb=qb, kw=kw, back=back, slen=slen, has_ctx=has_ctx,
                             n_tab=n_tab, tab_heads=tab_heads, has_sink=sink is not None)
    return pl.pallas_call(
        kern,
        grid=(bsz, slen // qb),
        in_specs=in_specs,
        out_specs=pl.BlockSpec((1, qb, GROUP_W), lambda b, n: (b, n, 0)),
        out_shape=jax.ShapeDtypeStruct((bsz, slen, GROUP_W), BF16),
        compiler_params=_cparams(("arbitrary", "arbitrary")),
        name="attn",
    )(*args)


WIN_QB = 256
WIN_BACK = WINDOW
WIN_KW = WIN_QB + 2 * WINDOW
NA_QROWS = 4
NA_KROWS = 12
NA_QB = NA_QROWS * GRID_W
NA_KW = NA_KROWS * GRID_W


def _window_table(slen):
    nblk = slen // WIN_QB
    tabs = []
    for n in (0, 1, nblk - 1):
        ws = int(np.clip(n * WIN_QB - WIN_BACK, 0, slen - WIN_KW))
        qpos = n * WIN_QB + np.arange(WIN_QB)[:, None]
        kpos = ws + np.arange(WIN_KW)[None, :]
        tabs.append(np.where(np.abs(qpos - kpos) <= WINDOW, 0.0, NEG))
    return jnp.asarray(np.stack(tabs)[None, :, None], F32)


N_RPB_R = 2 * NA_ROWS - 1
N_RPB_C = 2 * NA_COLS - 1


def _na_table_kernel(rpb_ref, o_ref, tz_scr, *, rows):
    base = (pl.program_id(0) * N_HEADS + pl.program_id(1)) * (N_RPB_R * N_RPB_C)
    qc = lax.broadcasted_iota(jnp.int32, (GRID_W, 2 * GRID_W), 0)
    kk = lax.broadcasted_iota(jnp.int32, (GRID_W, 2 * GRID_W), 1)
    kc = kk & (GRID_W - 1)
    diff = kc - qc
    qws = jnp.clip(qc - NA_COLS // 2, 0, GRID_W - NA_COLS)
    col_ok = (kc >= qws) & (kc < qws + NA_COLS)
    neg = jnp.full((GRID_W, 2 * GRID_W), NEG, F32)
    for dr in range(N_RPB_R):
        acc = neg
        for m in range(N_RPB_C):
            acc = jnp.where(diff == m - (NA_COLS - 1), rpb_ref[base + dr * N_RPB_C + m] * LOG2E, acc)
        tz_scr[dr] = jnp.where(col_ok, acc, NEG)
    wr = min(NA_ROWS, rows)
    nblk = rows // NA_QROWS
    for ti, g in enumerate((0, 1, nblk - 1)):
        ws_row = min(max(g * NA_QROWS - NA_QROWS, 0), rows - NA_KROWS)
        for qr in range(NA_QROWS):
            r = g * NA_QROWS + qr
            rs = min(max(r - wr // 2, 0), rows - wr)
            for p in range(NA_KROWS // 2):
                halves = []
                for kr in (ws_row + 2 * p, ws_row + 2 * p + 1):
                    halves.append(tz_scr[kr - r + NA_ROWS - 1] if rs <= kr < rs + wr else neg)
                o_ref[0, ti, 0, qr * GRID_W:(qr + 1) * GRID_W, p * 2 * GRID_W:(p + 1) * 2 * GRID_W] = (
                    jnp.where(kk < GRID_W, halves[0], halves[1]))


def _na_tables(na_rpb, slen):
    depth = na_rpb.shape[0]
    return pl.pallas_call(
        functools.partial(_na_table_kernel, rows=slen // GRID_W),
        grid=(depth, N_HEADS),
        in_specs=[pl.BlockSpec(memory_space=pltpu.SMEM)],
        out_specs=pl.BlockSpec((1, 3, 1, NA_QB, NA_KW), lambda l, h: (l, 0, h, 0, 0)),
        out_shape=jax.ShapeDtypeStruct((depth, 3, N_HEADS, NA_QB, NA_KW), F32),
        scratch_shapes=[pltpu.VMEM((N_RPB_R, GRID_W, 2 * GRID_W), F32)],
        compiler_params=_cparams(("arbitrary", "arbitrary")),
        name="na_table",
    )(na_rpb.astype(F32).reshape(-1))


def _ret_kernel(q_ref, k_ref, v_ref, g_ref, lgl_ref, lgc_ref, *rest, slen, t, has_s0):
    if has_s0:
        s0_ref, o_ref, sb_scr, st_scr, dm_scr, dec_scr = rest
    else:
        o_ref, sfin_ref, sb_scr, st_scr, dm_scr, dec_scr = rest
    nc = slen // t
    masks = _head_masks(t)
    bmask = _block_ones()
    ones_bf16 = jnp.where(bmask, 1.0, 0.0).astype(BF16)
    lgf, lgb = lgl_ref[0], lgl_ref[1]

    @pl.when(pl.program_id(0) == 0)
    def _():
        ii = lax.broadcasted_iota(jnp.int32, (t, N_HEADS * t), 0)
        jj = lax.broadcasted_iota(jnp.int32, (t, N_HEADS * t), 1) & (t - 1)
        dist = (ii - jj).astype(F32)
        dm_scr[...] = (jnp.where(dist >= 0, jnp.exp(dist * lgc_ref[0]), 0.0)
                       + jnp.where(dist <= 0, jnp.exp(-dist * lgc_ref[1]), 0.0))
        idx = lax.broadcasted_iota(jnp.int32, (t, GROUP_W), 0).astype(F32)
        dec_scr[0] = jnp.exp((idx + 1.0) * lgf)
        dec_scr[1] = jnp.exp((t - 1.0 - idx) * lgf)
        dec_scr[2] = jnp.exp((t - idx) * lgb)
        dec_scr[3] = jnp.exp(idx * lgb)

    cdec_f = jnp.exp(float(t) * lgf)
    cdec_b = jnp.exp(float(t) * lgb)
    kscale = HEAD_DIM ** -0.5

    def state_update(st, k, v, kdec, cdec):
        upd = lax.dot_general(v, (k * kdec).astype(BF16), TN_DIMS, preferred_element_type=F32)
        return st * cdec + jnp.where(bmask, upd, 0.0)

    def init_state(direction):
        if has_s0:
            st_scr[...] = _expand_state(s0_ref[0, direction], bmask)
        else:
            st_scr[...] = jnp.zeros((GROUP_W, GROUP_W), F32)

    def emit_final(direction):
        if not has_s0:
            sfin_ref[0, direction] = _compact_state(st_scr[...], bmask)

    init_state(1)

    def sweep_bwd(ci, carry):
        c = nc - 1 - ci
        rows = pl.ds(pl.multiple_of(c * t, t), t)
        st = st_scr[...]
        sb_scr[c] = st.astype(BF16)
        k = k_ref[0, rows, :] * kscale
        st_scr[...] = state_update(st, k, v_ref[0, rows, :].astype(BF16), dec_scr[3], cdec_b)
        return carry

    lax.fori_loop(0, nc, sweep_bwd, 0)
    emit_final(1)
    init_state(0)

    def sweep_fwd(c, carry):
        rows = pl.ds(pl.multiple_of(c * t, t), t)
        q = q_ref[0, rows, :]
        k = k_ref[0, rows, :] * kscale
        v = v_ref[0, rows, :]
        ks = _stack_heads(k, masks).astype(BF16)
        vs = _stack_heads(v, masks).astype(BF16)
        sc = lax.dot_general(q.astype(BF16), ks, NT_DIMS, preferred_element_type=F32) * dm_scr[...]
        o = jnp.dot(sc.astype(BF16), vs, preferred_element_type=F32)
        st = st_scr[...]
        o = o + lax.dot_general((q * dec_scr[0]).astype(BF16), st.astype(BF16), NT_DIMS,
                                preferred_element_type=F32)
        o = o + lax.dot_general((q * dec_scr[2]).astype(BF16), sb_scr[c], NT_DIMS, preferred_element_type=F32)
        st_scr[...] = state_update(st, k, v.astype(BF16), dec_scr[1], cdec_f)
        mu = _head_sum(o, ones_bf16) * (1.0 / HEAD_DIM)
        d = o - mu
        var = _head_sum(d * d, ones_bf16) * (1.0 / HEAD_DIM)
        y = d * lax.rsqrt(var + EPS)
        o_ref[0, rows, :] = (y * _silu(g_ref[0, rows, :])).astype(o_ref.dtype)
        return carry

    lax.fori_loop(0, nc, sweep_fwd, 0, unroll=2)
    emit_final(0)


def _compact_states(s0):
    return jnp.swapaxes(s0, -1, -2).reshape(s0.shape[:-3] + (GROUP_W, HEAD_DIM))


def _uncompact_states(sc):
    bsz = sc.shape[0]
    return jnp.swapaxes(sc.reshape(bsz, 2, N_HEADS, HEAD_DIM, HEAD_DIM), -1, -2)


def _split3(x):
    h1 = x.astype(BF16)
    r1 = x - h1.astype(F32)
    h2 = r1.astype(BF16)
    return h1, h2, (r1 - h2.astype(F32)).astype(BF16)


def _expand_state(x, bmask):
    r = lax.broadcasted_iota(jnp.int32, (HEAD_DIM, GROUP_W), 0)
    c = lax.broadcasted_iota(jnp.int32, (HEAD_DIM, GROUP_W), 1) & (HEAD_DIM - 1)
    rep = jnp.where(r == c, 1.0, 0.0).astype(BF16)
    h1, h2, h3 = _split3(x)
    tiled = (jnp.dot(h1, rep, preferred_element_type=F32) + jnp.dot(h2, rep, preferred_element_type=F32)
             + jnp.dot(h3, rep, preferred_element_type=F32))
    return jnp.where(bmask, tiled, 0.0)


def _compact_state(st, bmask):
    r = lax.broadcasted_iota(jnp.int32, (GROUP_W, HEAD_DIM), 0) & (HEAD_DIM - 1)
    c = lax.broadcasted_iota(jnp.int32, (GROUP_W, HEAD_DIM), 1)
    fold = jnp.where(r == c, 1.0, 0.0).astype(BF16)
    h1, h2, h3 = _split3(jnp.where(bmask, st, 0.0))
    return (jnp.dot(h1, fold, preferred_element_type=F32) + jnp.dot(h2, fold, preferred_element_type=F32)
            + jnp.dot(h3, fold, preferred_element_type=F32))


def _state_specs(bsz, s0):
    spec = pl.BlockSpec((1, 2, GROUP_W, HEAD_DIM), lambda b: (b, 0, 0, 0))
    shape = jax.ShapeDtypeStruct((bsz, 2, GROUP_W, HEAD_DIM), F32)
    if s0 is None:
        return [], [], [spec], [shape]
    return [spec], [s0], [], []


RET_CHUNK = 256


def _retention(pc, log_gamma, s0=None, t=RET_CHUNK):
    bsz, slen, _ = pc.shape
    t = min(t, slen)
    lgl = jnp.repeat(log_gamma, HEAD_DIM, axis=1).reshape(2, 1, GROUP_W)
    lgc = jnp.repeat(log_gamma, t, axis=1).reshape(2, 1, N_HEADS * t)
    col = lambda j: pl.BlockSpec((1, slen, GROUP_W), lambda b: (b, 0, j))
    s_in_specs, s_args, s_out_specs, s_out_shape = _state_specs(bsz, s0)
    return pl.pallas_call(
        functools.partial(_ret_kernel, slen=slen, t=t, has_s0=s0 is not None),
        grid=(bsz,),
        in_specs=[col(0), col(1), col(2), col(3),
                  pl.BlockSpec((2, 1, GROUP_W), lambda b: (0, 0, 0)),
                  pl.BlockSpec((2, 1, N_HEADS * t), lambda b: (0, 0, 0))] + s_in_specs,
        out_specs=[pl.BlockSpec((1, slen, GROUP_W), lambda b: (b, 0, 0))] + s_out_specs,
        out_shape=[jax.ShapeDtypeStruct((bsz, slen, GROUP_W), BF16)] + s_out_shape,
        scratch_shapes=[pltpu.VMEM((slen // t, GROUP_W, GROUP_W), BF16), pltpu.VMEM((GROUP_W, GROUP_W), F32),
                        pltpu.VMEM((t, N_HEADS * t), F32), pltpu.VMEM((4, t, GROUP_W), F32)],
        compiler_params=_cparams(("arbitrary",)),
        name="retention",
    )(pc, pc, pc, pc, lgl, lgc, *s_args)


HGRN_DIRECT = 8
HGRN_BLOCK = 16
HGRN_SAFE_EXP2 = 80.0


def _split2(x):
    hi = x.astype(BF16)
    return hi, (x - hi.astype(F32)).astype(BF16)


def _hgrn_kernel(q_ref, zf_ref, zb_ref, v_ref, g_ref, lb_ref, *rest, slen, t, has_s0):
    if has_s0:
        s0_ref, o_ref, o_scr, st_scr, b_scr, k_scr = rest
    else:
        o_ref, sfin_ref, o_scr, st_scr, b_scr, k_scr = rest
    nc = slen // t
    masks = _head_masks(t)
    bmask = _block_ones()
    ones_bf16 = jnp.where(bmask, 1.0, 0.0).astype(BF16)
    r_i = lax.broadcasted_iota(jnp.int32, (t, t), 0)
    c_i = lax.broadcasted_iota(jnp.int32, (t, t), 1)
    sub = lax.broadcasted_iota(jnp.int32, (t, GROUP_W), 0) & (HGRN_DIRECT - 1)
    levels = []
    s = t // 2
    while s >= HGRN_DIRECT:
        levels.append(s)
        s //= 2
    r_w = lax.broadcasted_iota(jnp.int32, (t, N_HEADS * t), 0)
    c_w = lax.broadcasted_iota(jnp.int32, (t, N_HEADS * t), 1) & (t - 1)
    lvl_masks = []
    for s in levels:
        shift = (2 * s).bit_length() - 1
        lvl_masks.append(jnp.where((r_w >> shift) == (c_w >> shift), 1.0, 0.0))

    def block_roll(x, shift):
        x3 = x.reshape(t // HGRN_DIRECT, HGRN_DIRECT, GROUP_W)
        return pltpu.roll(x3, shift, 1).reshape(t, GROUP_W)

    def direction_fns(direction, z_ref):
        fwd = direction == 0
        lbd = jnp.maximum(lb_ref[direction], LB_FLOOR)
        tri = jnp.where((r_i >= c_i) if fwd else (r_i <= c_i), 1.0, 0.0).astype(BF16)
        blk_shift = HGRN_BLOCK.bit_length() - 1
        near_mask = jnp.where(((r_w >> blk_shift) == (c_w >> blk_shift))
                              & ((c_w <= r_w) if fwd else (c_w >= r_w)), 1.0, 0.0)
        if has_s0:
            st_scr[direction] = _expand_state(s0_ref[0, direction], bmask)
        else:
            st_scr[direction] = jnp.zeros((GROUP_W, GROUP_W), F32)

        def gates(c, worst):
            rows = pl.ds(pl.multiple_of(c * t, t), t)
            z = z_ref[0, rows, :]
            a = jnp.exp(-jnp.abs(z))
            inv = 1.0 / (1.0 + a)
            pos = z >= 0
            sig = jnp.where(pos, inv, a * inv)
            nsig = jnp.where(pos, a * inv, inv)
            lf = jnp.log(lbd + (1.0 - lbd) * sig)
            k_scr[direction, rows, :] = (1.0 - lbd) * nsig
            h1, h2 = _split2(lf)
            b = (jnp.dot(tri, h1, preferred_element_type=F32) + jnp.dot(tri, h2, preferred_element_type=F32)) * LOG2E
            b_scr[direction, rows, :] = b
            for gi in range(t // HGRN_BLOCK):
                first = b[gi * HGRN_BLOCK:gi * HGRN_BLOCK + 1]
                last = b[(gi + 1) * HGRN_BLOCK - 1:(gi + 1) * HGRN_BLOCK]
                worst = jnp.maximum(worst, jnp.abs(first - last))
            return worst

        def chunk(c, factored):
            rows = pl.ds(pl.multiple_of(c * t, t), t)
            q = _silu(q_ref[0, rows, :])
            v = v_ref[0, rows, :]
            vb = v.astype(BF16)
            k = k_scr[direction, rows, :]
            b = b_scr[direction, rows, :]
            vs = _stack_heads(v, masks).astype(BF16)

            def level_scores(s, lm):
                pa, pb = [], []
                zero = jnp.zeros((s, GROUP_W), F32)
                for gi in range(t // (2 * s)):
                    b_lo = b[gi * 2 * s:gi * 2 * s + s]
                    b_hi = b[gi * 2 * s + s:(gi + 1) * 2 * s]
                    if fwd:
                        anc = b_hi[0:1]
                        pa += [zero, jnp.exp2(b_hi - anc)]
                        pb += [jnp.exp2(anc - b_lo), zero]
                    else:
                        anc = b_lo[s - 1:s]
                        pa += [jnp.exp2(b_lo - anc), zero]
                        pb += [zero, jnp.exp2(anc - b_hi)]
                qa = (q * jnp.concatenate(pa, axis=0)).astype(BF16)
                kb = _stack_heads(k * jnp.concatenate(pb, axis=0), masks).astype(BF16)
                return lax.dot_general(qa, kb, NT_DIMS, preferred_element_type=F32) * lm

            def direct_pairs():
                prods, vals = [(q * k).astype(BF16)], [v]
                for dlt in range(1, HGRN_DIRECT):
                    sh = dlt if fwd else HGRN_DIRECT - dlt
                    ok = (sub >= dlt) if fwd else (sub + dlt <= HGRN_DIRECT - 1)
                    w = jnp.exp2(jnp.minimum(b - block_roll(b, sh), 0.0))
                    prods.append(jnp.where(ok, q * block_roll(k, sh) * w, 0.0).astype(BF16))
                    vals.append(block_roll(v, sh))
                hs = jnp.dot(jnp.concatenate(prods, axis=0), ones_bf16, preferred_element_type=F32)
                out = hs[0:t] * vals[0]
                for dlt in range(1, HGRN_DIRECT):
                    out = out + hs[dlt * t:(dlt + 1) * t] * vals[dlt]
                return out

            sc = None
            for s, lm in zip(levels, lvl_masks):
                if s >= HGRN_BLOCK:
                    term = level_scores(s, lm)
                    sc = term if sc is None else sc + term

            if factored:
                ea, eb = [], []
                for gi in range(t // HGRN_BLOCK):
                    blk = b[gi * HGRN_BLOCK:(gi + 1) * HGRN_BLOCK]
                    anc = blk[0:1] if fwd else blk[HGRN_BLOCK - 1:HGRN_BLOCK]
                    ea.append(jnp.exp2(blk - anc))
                    eb.append(jnp.exp2(anc - blk))
                qa = (q * jnp.concatenate(ea, axis=0)).astype(BF16)
                kb = _stack_heads(k * jnp.concatenate(eb, axis=0), masks).astype(BF16)
                sc = sc + lax.dot_general(qa, kb, NT_DIMS, preferred_element_type=F32) * near_mask
                o = jnp.dot(sc.astype(BF16), vs, preferred_element_type=F32)
            else:
                for s, lm in zip(levels, lvl_masks):
                    if s < HGRN_BLOCK:
                        sc = sc + level_scores(s, lm)
                o = jnp.dot(sc.astype(BF16), vs, preferred_element_type=F32) + direct_pairs()

            st = st_scr[direction]
            o = o + lax.dot_general((q * jnp.exp2(b)).astype(BF16), st.astype(BF16), NT_DIMS,
                                    preferred_element_type=F32)
            bl = b[t - 1:t] if fwd else b[0:1]
            upd = lax.dot_general(vb, (k * jnp.exp2(bl - b)).astype(BF16), TN_DIMS,
                                  preferred_element_type=F32)
            st_scr[direction] = st * jnp.exp2(bl) + jnp.where(bmask, upd, 0.0)
            o_scr[direction, rows, :] = o

        return gates, chunk

    gates_f, chunk_f = direction_fns(0, zf_ref)
    gates_b, chunk_b = direction_fns(1, zb_ref)

    worst = lax.fori_loop(0, nc, lambda c, w: gates_b(c, gates_f(c, w)), jnp.zeros((1, GROUP_W), F32), unroll=2)
    factored_ok = jnp.max(worst) < HGRN_SAFE_EXP2

    def sweep(ci, carry, *, factored):
        chunk_f(ci, factored)
        chunk_b(nc - 1 - ci, factored)
        return carry

    @pl.when(factored_ok)
    def _():
        lax.fori_loop(0, nc, functools.partial(sweep, factored=True), 0, unroll=4)

    @pl.when(jnp.logical_not(factored_ok))
    def _():
        lax.fori_loop(0, nc, functools.partial(sweep, factored=False), 0)

    if not has_s0:
        for direction in range(2):
            sfin_ref[0, direction] = _compact_state(st_scr[direction], bmask)

    def finish(c, carry):
        rows = pl.ds(pl.multiple_of(c * t, t), t)
        o = o_scr[0, rows, :] + o_scr[1, rows, :]
        ms = _head_sum(o * o, ones_bf16) * (1.0 / HEAD_DIM)
        y = o * lax.rsqrt(ms + EPS)
        o_ref[0, rows, :] = (y * _silu(g_ref[0, rows, :])).astype(o_ref.dtype)
        return carry

    lax.fori_loop(0, nc, finish, 0, unroll=2)


HGRN_CHUNK = 128


def _hgrn(pd, lb, s0=None, t=HGRN_CHUNK):
    bsz, slen, _ = pd.shape
    col = lambda j: pl.BlockSpec((1, slen, GROUP_W), lambda b: (b, 0, j))
    s_in_specs, s_args, s_out_specs, s_out_shape = _state_specs(bsz, s0)
    return pl.pallas_call(
        functools.partial(_hgrn_kernel, slen=slen, t=t, has_s0=s0 is not None),
        grid=(bsz,),
        in_specs=[col(0), col(1), col(2), col(3), col(4),
                  pl.BlockSpec((2, 1, GROUP_W), lambda b: (0, 0, 0))] + s_in_specs,
        out_specs=[pl.BlockSpec((1, slen, GROUP_W), lambda b: (b, 0, 0))] + s_out_specs,
        out_shape=[jax.ShapeDtypeStruct((bsz, slen, GROUP_W), BF16)] + s_out_shape,
        scratch_shapes=[pltpu.VMEM((2, slen, GROUP_W), F32), pltpu.VMEM((2, GROUP_W, GROUP_W), F32),
                        pltpu.VMEM((2, slen, GROUP_W), F32), pltpu.VMEM((2, slen, GROUP_W), F32)],
        compiler_params=_cparams(("arbitrary",)),
        name="hgrn2",
    )(pd, pd, pd, pd, pd, lb.reshape(2, 1, GROUP_W), *s_args)


def _outproj_kernel(ma_ref, mb_ref, mc_ref, md_ref, w_ref, x_ref, gate_ref, g_ref, o_ref, *, layer, mod_row):
    g = GROUP_W
    y = jnp.dot(ma_ref[0], w_ref[0, 0:g, :], preferred_element_type=F32)
    y = y + jnp.dot(mb_ref[0], w_ref[0, g:2 * g, :], preferred_element_type=F32)
    y = y + jnp.dot(mc_ref[0], w_ref[0, 2 * g:3 * g, :], preferred_element_type=F32)
    y = y + jnp.dot(md_ref[0], w_ref[0, 3 * g:4 * g, :], preferred_element_type=F32)
    ms = jnp.mean(y * y, axis=-1, keepdims=True)
    r = y * lax.rsqrt(ms + EPS) * g_ref[layer:layer + 1, :]
    o_ref[0] = x_ref[0] + _mod_row(gate_ref, mod_row) * r


OUTPROJ_TM = 1024


def _out_proj(mixed, w_out_bf16, x, mods, layer, mod_row, g_post):
    bsz, slen, d = x.shape
    depth = g_post.shape[0]
    tm = min(OUTPROJ_TM, slen)
    mspec = pl.BlockSpec((1, tm, GROUP_W), lambda b, i: (b, i, 0))
    return pl.pallas_call(
        functools.partial(_outproj_kernel, layer=layer, mod_row=mod_row),
        grid=(bsz, slen // tm),
        in_specs=[mspec, mspec, mspec, mspec,
                  pl.BlockSpec((1, d, d), lambda b, i: (layer, 0, 0)),
                  pl.BlockSpec((1, tm, d), lambda b, i: (b, i, 0)),
                  _mod_spec(2, layer),
                  pl.BlockSpec((depth, d), lambda b, i: (0, 0))],
        out_specs=pl.BlockSpec((1, tm, d), lambda b, i: (b, i, 0)),
        out_shape=jax.ShapeDtypeStruct((bsz, slen, d), F32),
        compiler_params=_cparams(("arbitrary", "arbitrary")),
        name="out_proj",
    )(*mixed, w_out_bf16, x, mods, g_post)


def _context_layer(x, mods, layer, mod_row, g_pre, g_post, w_in, w_out, sink, log_gamma, lb):
    bsz, slen, _ = x.shape
    pa, pb, pc, pd, ak, av, bk, bv = _in_proj(x, mods, layer, mod_row, g_pre, w_in, rope=False, emit_kv=True)
    o_a = _attention(pa, 0, 2, 3, 2, n_kv=A_KV, qb=slen, kw=slen, back=0, sink=sink)
    o_b = _attention(pb, 0, 1, 2, 3, n_kv=N_HEADS, qb=slen, kw=slen, back=0)
    o_c, s_c = _retention(pc, log_gamma)
    o_d, s_d = _hgrn(pd, lb)
    x = _out_proj((o_a, o_b, o_c, o_d), w_out, x, mods, layer, mod_row, g_post)
    kv4 = lambda t: t.reshape(bsz, slen, -1, HEAD_DIM)
    return x, (kv4(ak), kv4(av), kv4(bk), kv4(bv), _uncompact_states(s_c), _uncompact_states(s_d))


def _latent_layer(x, mods, layer, ca_k, ca_v, cb_k, cb_v, st_c, st_d, g_pre, g_post, w_in, w_out, sink, win_tab,
                  na_tabs, log_gamma, lb):
    pa, pb, pc, pd = _in_proj(x, mods, layer, None, g_pre, w_in, rope=True)
    o_a = _attention(pa, 0, 2, 3, 2, n_kv=A_KV, qb=WIN_QB, kw=WIN_KW, back=WIN_BACK,
                     ctx_k=ca_k, ctx_v=ca_v, table=win_tab, sink=sink, layer=layer)
    o_b = _attention(pb, 0, 1, 2, 3, n_kv=N_HEADS, qb=NA_QB, kw=NA_KW, back=NA_QB,
                     ctx_k=cb_k, ctx_v=cb_v, table=na_tabs, layer=layer)
    o_c, = _retention(pc, log_gamma, st_c)
    o_d, = _hgrn(pd, lb, st_d)
    return _out_proj((o_a, o_b, o_c, o_d), w_out, x, mods, layer, None, g_post)


def kernel(x_prompt, x_sample, c, cache_win_k, cache_win_v, cache_na_k, cache_na_v, state_ret, state_hgrn,
           c_ctx, w_ada, b_ada, g_pre, g_post, w_in, w_out, attn_sink, na_rpb, ret_decay_logit, hgrn_lb_logit):
    depth = w_ada.shape[0]
    dec_b, dec_s, d = x_sample.shape
    p_lb = jax.nn.softmax(hgrn_lb_logit.astype(F32), axis=0)
    lower_bounds = jnp.cumsum(p_lb, axis=0) - p_lb[0:1]
    log_gammas = jax.nn.log_sigmoid(ret_decay_logit.astype(F32))
    w_in_b = w_in.astype(BF16)
    w_out_b = w_out.astype(BF16)

    cvecs = jnp.zeros((ADA_ROWS, d), F32).at[:dec_b].set(c).at[dec_b].set(c_ctx)
    mods = _adaln(cvecs, w_ada, b_ada)
    win_tab = _window_table(dec_s)
    na_tabs = _na_tables(na_rpb, dec_s)

    x = x_prompt
    outs = [[] for _ in range(6)]
    for l in range(depth):
        x, extra = _context_layer(x, mods, l, dec_b, g_pre, g_post, w_in_b, w_out_b,
                                  attn_sink[l], log_gammas[l], lower_bounds[l])
        for acc, e in zip(outs, extra):
            acc.append(e)
    y_prompt = x
    stacked = [jnp.stack(o, axis=1) for o in outs]

    past = cache_win_k.shape[2]
    ca_k = cache_win_k.reshape(dec_b, depth, past, A_KV * HEAD_DIM)
    ca_v = cache_win_v.reshape(dec_b, depth, past, A_KV * HEAD_DIM)
    cb_k = cache_na_k.reshape(dec_b, depth, past, GROUP_W)
    cb_v = cache_na_v.reshape(dec_b, depth, past, GROUP_W)
    st_c = _compact_states(state_ret)
    st_d = _compact_states(state_hgrn)
    x = x_sample
    for l in range(depth):
        x = _latent_layer(x, mods, l, ca_k, ca_v, cb_k, cb_v, st_c[:, l], st_d[:, l], g_pre, g_post,
                          w_in_b, w_out_b, attn_sink[l], win_tab, na_tabs, log_gammas[l], lower_bounds[l])
    return (y_prompt, x, *stacked)
```

```python
import functools

import numpy as np
import jax
import jax.numpy as jnp
from jax import lax
from jax.experimental import pallas as pl
from jax.experimental.pallas import tpu as pltpu

F32 = jnp.float32
BF16 = jnp.bfloat16

D_MODEL = 1024
DEPTH = 4
GRID_W = 64
HEAD_DIM = 64
N_HEADS = 4
GROUP_W = N_HEADS * HEAD_DIM
A_KV = 2
WINDOW = 128
NA_ROWS = 8
NA_COLS = 16
ROPE_BASE = 10000.0
EPS = 1e-6
NEG = -1e30
LB_FLOOR = 1e-30
LOG2E = 1.4426950408889634
W_A = 3 * GROUP_W
W_B = 4 * GROUP_W
W_C = 4 * GROUP_W
W_D = 5 * GROUP_W
IN_WIDTH = W_A + W_B + W_C + W_D

V7X_VMEM_LIMIT_BYTES = 56 * 1024 * 1024
ADA_ROWS = 16

NT_DIMS = (((1,), (1,)), ((), ()))
TN_DIMS = (((0,), (0,)), ((), ()))


def _cparams(sem, flags=None):
    return pltpu.CompilerParams(dimension_semantics=sem, vmem_limit_bytes=V7X_VMEM_LIMIT_BYTES, flags=flags)


def _sigmoid(x):
    return 1.0 / (1.0 + jnp.exp(-x))


def _silu(x):
    return x * _sigmoid(x)


def _head_masks(rows):
    lane = lax.broadcasted_iota(jnp.int32, (rows, GROUP_W), 1)
    return [(lane >= h * HEAD_DIM) & (lane < (h + 1) * HEAD_DIM) for h in range(N_HEADS)]


def _stack_heads(x, masks):
    return jnp.concatenate([jnp.where(m, x, 0.0) for m in masks], axis=0)


def _unstack_heads(x, masks, t):
    out = jnp.where(masks[0], x[0:t], 0.0)
    for h in range(1, N_HEADS):
        out = out + jnp.where(masks[h], x[h * t:(h + 1) * t], 0.0)
    return out


def _block_ones():
    shift = HEAD_DIM.bit_length() - 1
    r = lax.broadcasted_iota(jnp.int32, (GROUP_W, GROUP_W), 0) >> shift
    c = lax.broadcasted_iota(jnp.int32, (GROUP_W, GROUP_W), 1) >> shift
    return r == c


def _head_sum(x, ones_bf16):
    hi = x.astype(BF16)
    lo = (x - hi.astype(F32)).astype(BF16)
    return (jnp.dot(hi, ones_bf16, preferred_element_type=F32)
            + jnp.dot(lo, ones_bf16, preferred_element_type=F32))


def _adaln_kernel(c_ref, w_ref, b_ref, o_ref):
    s = _silu(c_ref[...]).astype(BF16)
    o_ref[0, 0] = jnp.dot(s, w_ref[0].astype(BF16), preferred_element_type=F32) + b_ref[0]


def _adaln(cvecs, w_ada, b_ada):
    depth, d, d3 = w_ada.shape
    tn = 512
    per = d // tn
    return pl.pallas_call(
        _adaln_kernel,
        grid=(depth, d3 // tn),
        in_specs=[pl.BlockSpec((ADA_ROWS, d), lambda l, j: (0, 0)),
                  pl.BlockSpec((1, d, tn), lambda l, j: (l, 0, j)),
                  pl.BlockSpec((1, 1, tn), lambda l, j: (l, 0, j))],
        out_specs=pl.BlockSpec((1, 1, ADA_ROWS, tn), lambda l, j: (l, j // per, 0, j % per)),
        out_shape=jax.ShapeDtypeStruct((depth, 3, ADA_ROWS, d), F32),
        compiler_params=_cparams(("arbitrary", "arbitrary")),
        name="adaln",
    )(cvecs, w_ada, b_ada.reshape(depth, 1, d3))


def _rope(x, cos, sin):
    w = x.shape[-1]
    lane = lax.broadcasted_iota(jnp.int32, x.shape, 1)
    first = (lane & 31) < 16
    swapped = jnp.where(first, pltpu.roll(x, w - 16, 1), pltpu.roll(x, 16, 1))
    return x * cos[:, :w] + swapped * sin[:, :w]


def _mod_row(mod_ref, mod_row):
    row = pl.program_id(0) if mod_row is None else mod_row
    return mod_ref[0, 0, pl.ds(row, 1), :]


def _inproj_kernel(*refs, rope, emit_kv, layer, mod_row):
    refs = list(refs)
    x_ref, shift_ref, scale_ref, g_ref, w_ref = refs[:5]
    if rope:
        cos_ref, sin_ref = refs[5:7]
    n_out = 8 if emit_kv else 4
    pa_ref, pb_ref, pc_ref, pd_ref = refs[-n_out:][:4]
    x = x_ref[0]
    ms = jnp.mean(x * x, axis=-1, keepdims=True)
    y = x * lax.rsqrt(ms + EPS) * g_ref[layer:layer + 1, :]
    h = (y * (1.0 + _mod_row(scale_ref, mod_row)) + _mod_row(shift_ref, mod_row)).astype(BF16)

    def mm(c0, c1):
        return jnp.dot(h, w_ref[0, :, c0:c1], preferred_element_type=F32)

    if rope:
        cos = cos_ref[...]
        sin = sin_ref[...]
    g = GROUP_W
    aq = mm(0, g)
    akv = mm(g, 2 * g)
    if rope:
        aq = _rope(aq, cos, sin)
        ak = _rope(akv[:, :g // 2], cos, sin)
        akv = jnp.concatenate([ak, akv[:, g // 2:]], axis=1)
    pa_ref[0, :, 0:g] = aq.astype(pa_ref.dtype)
    pa_ref[0, :, g:2 * g] = akv.astype(pa_ref.dtype)
    pa_ref[0, :, 2 * g:3 * g] = mm(2 * g, 3 * g).astype(pa_ref.dtype)
    pbv = mm(W_A, W_A + W_B)
    pb_ref[0] = pbv.astype(pb_ref.dtype)
    if emit_kv:
        ak_ref, av_ref, bk_ref, bv_ref = refs[-4:]
        ak_ref[0] = akv[:, :g // 2]
        av_ref[0] = akv[:, g // 2:]
        bk_ref[0] = pbv[:, g:2 * g]
        bv_ref[0] = pbv[:, 2 * g:3 * g]
    c0 = W_A + W_B
    cq = mm(c0, c0 + g)
    ck = mm(c0 + g, c0 + 2 * g)
    if rope:
        cq = _rope(cq, cos, sin)
        ck = _rope(ck, cos, sin)
    pc_ref[0, :, 0:g] = cq.astype(pc_ref.dtype)
    pc_ref[0, :, g:2 * g] = ck.astype(pc_ref.dtype)
    pc_ref[0, :, 2 * g:4 * g] = mm(c0 + 2 * g, c0 + 4 * g).astype(pc_ref.dtype)
    d0 = c0 + W_C
    pd_ref[0] = mm(d0, d0 + W_D).astype(pd_ref.dtype)


def _rope_tables(slen):
    t = np.arange(slen)
    nf = HEAD_DIM // 4
    freqs = ROPE_BASE ** (-np.arange(nf, dtype=np.float64) / nf)
    d = np.arange(HEAD_DIM)
    pos = np.where(d[None, :] < HEAD_DIM // 2, (t // GRID_W)[:, None], (t % GRID_W)[:, None])
    ang = pos * freqs[d % nf][None, :]
    sign = np.where((d % (2 * nf)) < nf, -1.0, 1.0)[None, :]
    cos = np.tile(np.cos(ang), (1, N_HEADS))
    sin = np.tile(np.sin(ang) * sign, (1, N_HEADS))
    return jnp.asarray(cos, F32), jnp.asarray(sin, F32)


def _mod_spec(which, layer):
    return pl.BlockSpec((1, 1, ADA_ROWS, D_MODEL), lambda b, i: (layer, which, 0, 0))


def _in_proj(x, mods, layer, mod_row, g_pre, w_in_bf16, rope, emit_kv=False, tm=256):
    bsz, slen, d = x.shape
    depth = g_pre.shape[0]
    in_specs = [pl.BlockSpec((1, tm, d), lambda b, i: (b, i, 0)),
                _mod_spec(0, layer),
                _mod_spec(1, layer),
                pl.BlockSpec((depth, d), lambda b, i: (0, 0)),
                pl.BlockSpec((1, d, IN_WIDTH), lambda b, i: (layer, 0, 0))]
    args = [x, mods, mods, g_pre, w_in_bf16]
    if rope:
        cos, sin = _rope_tables(slen)
        in_specs += [pl.BlockSpec((tm, GROUP_W), lambda b, i: (i, 0))] * 2
        args += [cos, sin]
    widths = (W_A, W_B, W_C, W_D)
    if emit_kv:
        widths += (A_KV * HEAD_DIM, A_KV * HEAD_DIM, GROUP_W, GROUP_W)
    return pl.pallas_call(
        functools.partial(_inproj_kernel, rope=rope, emit_kv=emit_kv, layer=layer, mod_row=mod_row),
        grid=(bsz, slen // tm),
        in_specs=in_specs,
        out_specs=[pl.BlockSpec((1, tm, w), lambda b, i: (b, i, 0)) for w in widths],
        out_shape=[jax.ShapeDtypeStruct((bsz, slen, w), F32) for w in widths],
        compiler_params=_cparams(("arbitrary", "arbitrary")),
        name="in_proj",
    )(*args)


ATTN_ROWS = 256


def _attn_kernel(*refs, cfgs):
    n_in = [4 + 2 * c["has_ctx"] + (c["n_tab"] > 0) + c["has_sink"] for c in cfgs]
    outs = refs[sum(n_in):]
    start = 0
    for c, n, o_ref in zip(cfgs, n_in, outs):
        _attn_body(refs[start:start + n], o_ref, **c)
        start += n


def _attn_body(in_refs, o_ref, *, n_kv, qb, kw, back, slen, has_ctx, n_tab, tab_heads, has_sink):
    it = iter(in_refs)
    q_ref, k_ref, v_ref, g_ref = next(it), next(it), next(it), next(it)
    kc_ref = vc_ref = tab_ref = sink_ref = None
    if has_ctx:
        kc_ref, vc_ref = next(it), next(it)
    if n_tab:
        tab_ref = next(it)
    if has_sink:
        sink_ref = next(it)

    n = pl.program_id(1)
    nblk = slen // qb
    ws = pl.multiple_of(jnp.clip(n * qb - back, 0, slen - kw), HEAD_DIM)
    if n_tab == 3:
        tix = jnp.where(n == 0, 0, jnp.where(n == nblk - 1, 2, 1))
    else:
        tix = 0
    grp = N_HEADS // n_kv
    qscale = HEAD_DIM ** -0.5 * LOG2E
    pair_w = 2 * HEAD_DIM
    lane = lax.broadcasted_iota(jnp.int32, (1, pair_w), 1)
    half_mask = (lane < HEAD_DIM, lane >= HEAD_DIM)
    rb = min(qb, ATTN_ROWS)
    for j in range(N_HEADS // 2):
        cols = slice(j * pair_w, (j + 1) * pair_w)
        if grp == 1:
            kv_cols, kv_half = cols, (0, 1)
        else:
            kv_cols, kv_half = slice(0, pair_w), (j, j)
        kall = k_ref[0, pl.ds(ws, kw), kv_cols].astype(BF16)
        vall = v_ref[0, pl.ds(ws, kw), kv_cols]
        if has_ctx:
            kall = jnp.concatenate([kall, kc_ref[0, 0, :, kv_cols].astype(BF16)], axis=0)
            vall = jnp.concatenate([vall, vc_ref[0, 0, :, kv_cols]], axis=0)
        vaug = {hf: jnp.where(half_mask[hf], vall, 1.0).astype(BF16) for hf in set(kv_half)}
        for r0 in range(0, qb, rb):
            rs = slice(r0, r0 + rb)
            q128 = q_ref[0, rs, cols] * qscale
            q_other = pltpu.roll(q128, HEAD_DIM, 1) if kv_half != (0, 1) else None
            outs = []
            for i in range(2):
                hf = kv_half[i]
                qm = jnp.where(half_mask[hf], q128 if i == hf else q_other, 0.0).astype(BF16)
                s = lax.dot_general(qm, kall, NT_DIMS, preferred_element_type=F32)
                if n_tab:
                    tab = tab_ref[0, tix, 2 * j + i if tab_heads == N_HEADS else 0, rs, :]
                    s = jnp.concatenate([s[:, :kw] + tab, s[:, kw:]], axis=1) if has_ctx else s + tab
                m = jnp.max(s, axis=-1, keepdims=True)
                if has_sink:
                    sink = sink_ref[2 * j + i] * LOG2E
                    m = jnp.maximum(m, sink)
                p = jnp.exp2(s - m).astype(BF16)
                acc = jnp.dot(p, vaug[hf], preferred_element_type=F32)
                if has_sink:
                    acc = acc + jnp.where(half_mask[1 - hf], jnp.exp2(sink - m), 0.0)
                acc = acc / pltpu.roll(acc, HEAD_DIM, 1)
                outs.append(acc if i == hf else pltpu.roll(acc, HEAD_DIM, 1))
            out = jnp.where(half_mask[0], outs[0], outs[1]) * _silu(g_ref[0, rs, cols])
            o_ref[0, rs, cols] = out.astype(o_ref.dtype)


def _attn_operands(p, q_col, k_col, v_col, g_col, n_kv, qb, kw, back, ctx_k=None, ctx_v=None,
                   table=None, sink=None, layer=0):
    slen = p.shape[1]
    kvw = n_kv * HEAD_DIM
    in_specs = [pl.BlockSpec((1, qb, GROUP_W), lambda b, n: (b, n, q_col)),
                pl.BlockSpec((1, slen, kvw), lambda b, n: (b, 0, k_col)),
                pl.BlockSpec((1, slen, kvw), lambda b, n: (b, 0, v_col)),
                pl.BlockSpec((1, qb, GROUP_W), lambda b, n: (b, n, g_col))]
    args = [p, p, p, p]
    has_ctx = ctx_k is not None
    if has_ctx:
        past = ctx_k.shape[2]
        in_specs += [pl.BlockSpec((1, 1, past, kvw), lambda b, n: (b, layer, 0, 0))] * 2
        args += [ctx_k, ctx_v]
    n_tab = tab_heads = 0
    if table is not None:
        n_tab, tab_heads = table.shape[1], table.shape[2]
        tab_layer = layer if table.shape[0] > 1 else 0
        in_specs.append(pl.BlockSpec((1,) + table.shape[1:], lambda b, n: (tab_layer, 0, 0, 0, 0)))
        args.append(table)
    if sink is not None:
        in_specs.append(pl.BlockSpec(memory_space=pltpu.SMEM))
        args.append(sink)
    cfg = dict(n_kv=n_kv, qb=qb, kw=kw, back=back, slen=slen, has_ctx=has_ctx, n_tab=n_tab,
               tab_heads=tab_heads, has_sink=sink is not None)
    return in_specs, args, cfg


def _attention(*operands):
    in_specs = [s for op in operands for s in op[0]]
    args = [a for op in operands for a in op[1]]
    cfgs = tuple(op[2] for op in operands)
    bsz = args[0].shape[0]
    slen, qb = cfgs[0]["slen"], cfgs[0]["qb"]
    assert all(c["slen"] == slen and c["qb"] == qb for c in cfgs)
    return pl.pallas_call(
        functools.partial(_attn_kernel, cfgs=cfgs),
        grid=(bsz, slen // qb),
        in_specs=in_specs,
        out_specs=[pl.BlockSpec((1, qb, GROUP_W), lambda b, n: (b, n, 0)) for _ in cfgs],
        out_shape=[jax.ShapeDtypeStruct((bsz, slen, GROUP_W), BF16) for _ in cfgs],
        compiler_params=_cparams(("arbitrary", "arbitrary")),
        name="attn",
    )(*args)


WIN_QB = 256
WIN_BACK = WINDOW
WIN_KW = WIN_QB + 2 * WINDOW
NA_QROWS = 4
NA_KROWS = 12
NA_QB = NA_QROWS * GRID_W
NA_KW = NA_KROWS * GRID_W


def _window_table(slen):
    nblk = slen // WIN_QB
    tabs = []
    for n in (0, 1, nblk - 1):
        ws = int(np.clip(n * WIN_QB - WIN_BACK, 0, slen - WIN_KW))
        qpos = n * WIN_QB + np.arange(WIN_QB)[:, None]
        kpos = ws + np.arange(WIN_KW)[None, :]
        tabs.append(np.where(np.abs(qpos - kpos) <= WINDOW, 0.0, NEG))
    return jnp.asarray(np.stack(tabs)[None, :, None], F32)


N_RPB_R = 2 * NA_ROWS - 1
N_RPB_C = 2 * NA_COLS - 1


def _na_table_kernel(rpb_ref, o_ref, tz_scr, *, rows):
    base = (pl.program_id(0) * N_HEADS + pl.program_id(1)) * (N_RPB_R * N_RPB_C)
    qc = lax.broadcasted_iota(jnp.int32, (GRID_W, 2 * GRID_W), 0)
    kk = lax.broadcasted_iota(jnp.int32, (GRID_W, 2 * GRID_W), 1)
    kc = kk & (GRID_W - 1)
    diff = kc - qc
    qws = jnp.clip(qc - NA_COLS // 2, 0, GRID_W - NA_COLS)
    col_ok = (kc >= qws) & (kc < qws + NA_COLS)
    neg = jnp.full((GRID_W, 2 * GRID_W), NEG, F32)
    for dr in range(N_RPB_R):
        acc = neg
        for m in range(N_RPB_C):
            acc = jnp.where(diff == m - (NA_COLS - 1), rpb_ref[base + dr * N_RPB_C + m] * LOG2E, acc)
        tz_scr[dr] = jnp.where(col_ok, acc, NEG)
    wr = min(NA_ROWS, rows)
    nblk = rows // NA_QROWS
    for ti, g in enumerate((0, 1, nblk - 1)):
        ws_row = min(max(g * NA_QROWS - NA_QROWS, 0), rows - NA_KROWS)
        for qr in range(NA_QROWS):
            r = g * NA_QROWS + qr
            rs = min(max(r - wr // 2, 0), rows - wr)
            for p in range(NA_KROWS // 2):
                halves = []
                for kr in (ws_row + 2 * p, ws_row + 2 * p + 1):
                    halves.append(tz_scr[kr - r + NA_ROWS - 1] if rs <= kr < rs + wr else neg)
                o_ref[0, ti, 0, qr * GRID_W:(qr + 1) * GRID_W, p * 2 * GRID_W:(p + 1) * 2 * GRID_W] = (
                    jnp.where(kk < GRID_W, halves[0], halves[1]))


def _na_tables(na_rpb, slen):
    depth = na_rpb.shape[0]
    return pl.pallas_call(
        functools.partial(_na_table_kernel, rows=slen // GRID_W),
        grid=(depth, N_HEADS),
        in_specs=[pl.BlockSpec(memory_space=pltpu.SMEM)],
        out_specs=pl.BlockSpec((1, 3, 1, NA_QB, NA_KW), lambda l, h: (l, 0, h, 0, 0)),
        out_shape=jax.ShapeDtypeStruct((depth, 3, N_HEADS, NA_QB, NA_KW), F32),
        scratch_shapes=[pltpu.VMEM((N_RPB_R, GRID_W, 2 * GRID_W), F32)],
        compiler_params=_cparams(("arbitrary", "arbitrary")),
        name="na_table",
    )(na_rpb.astype(F32).reshape(-1))


def _ret_kernel(q_ref, k_ref, v_ref, g_ref, lgl_ref, lgc_ref, *rest, slen, t, has_s0):
    if has_s0:
        s0_ref, o_ref, sb_scr, st_scr, dm_scr, dec_scr = rest
    else:
        o_ref, sfin_ref, sb_scr, st_scr, dm_scr, dec_scr = rest
    nc = slen // t
    masks = _head_masks(t)
    bmask = _block_ones()
    ones_bf16 = jnp.where(bmask, 1.0, 0.0).astype(BF16)
    lgf, lgb = lgl_ref[0], lgl_ref[1]

    @pl.when(pl.program_id(0) == 0)
    def _():
        ii = lax.broadcasted_iota(jnp.int32, (t, N_HEADS * t), 0)
        jj = lax.broadcasted_iota(jnp.int32, (t, N_HEADS * t), 1) & (t - 1)
        dist = (ii - jj).astype(F32)
        dm_scr[...] = (jnp.where(dist >= 0, jnp.exp(dist * lgc_ref[0]), 0.0)
                       + jnp.where(dist <= 0, jnp.exp(-dist * lgc_ref[1]), 0.0))
        idx = lax.broadcasted_iota(jnp.int32, (t, GROUP_W), 0).astype(F32)
        dec_scr[0] = jnp.exp((idx + 1.0) * lgf)
        dec_scr[1] = jnp.exp((t - 1.0 - idx) * lgf)
        dec_scr[2] = jnp.exp((t - idx) * lgb)
        dec_scr[3] = jnp.exp(idx * lgb)

    cdec_f = jnp.exp(float(t) * lgf)
    cdec_b = jnp.exp(float(t) * lgb)
    kscale = HEAD_DIM ** -0.5

    def state_update(st, k, v, kdec, cdec):
        upd = lax.dot_general(v, (k * kdec).astype(BF16), TN_DIMS, preferred_element_type=F32)
        return st * cdec + jnp.where(bmask, upd, 0.0)

    def init_state(direction):
        if has_s0:
            st_scr[...] = _expand_state(s0_ref[0, direction], bmask)
        else:
            st_scr[...] = jnp.zeros((GROUP_W, GROUP_W), F32)

    def emit_final(direction):
        if not has_s0:
            sfin_ref[0, direction] = _compact_state(st_scr[...], bmask)

    init_state(1)

    def sweep_bwd(ci, carry):
        c = nc - 1 - ci
        rows = pl.ds(pl.multiple_of(c * t, t), t)
        st = st_scr[...]
        sb_scr[c] = st.astype(BF16)
        k = k_ref[0, rows, :] * kscale
        st_scr[...] = state_update(st, k, v_ref[0, rows, :].astype(BF16), dec_scr[3], cdec_b)
        return carry

    lax.fori_loop(0, nc, sweep_bwd, 0)
    emit_final(1)
    init_state(0)

    def sweep_fwd(c, carry):
        rows = pl.ds(pl.multiple_of(c * t, t), t)
        q = q_ref[0, rows, :]
        k = k_ref[0, rows, :] * kscale
        v = v_ref[0, rows, :]
        ks = _stack_heads(k, masks).astype(BF16)
        vs = _stack_heads(v, masks).astype(BF16)
        sc = lax.dot_general(q.astype(BF16), ks, NT_DIMS, preferred_element_type=F32) * dm_scr[...]
        o = jnp.dot(sc.astype(BF16), vs, preferred_element_type=F32)
        st = st_scr[...]
        o = o + lax.dot_general((q * dec_scr[0]).astype(BF16), st.astype(BF16), NT_DIMS,
                                preferred_element_type=F32)
        o = o + lax.dot_general((q * dec_scr[2]).astype(BF16), sb_scr[c], NT_DIMS, preferred_element_type=F32)
        st_scr[...] = state_update(st, k, v.astype(BF16), dec_scr[1], cdec_f)
        mu = _head_sum(o, ones_bf16) * (1.0 / HEAD_DIM)
        d = o - mu
        var = _head_sum(d * d, ones_bf16) * (1.0 / HEAD_DIM)
        y = d * lax.rsqrt(var + EPS)
        o_ref[0, rows, :] = (y * _silu(g_ref[0, rows, :])).astype(o_ref.dtype)
        return carry

    lax.fori_loop(0, nc, sweep_fwd, 0, unroll=2)
    emit_final(0)


def _compact_states(s0):
    return jnp.swapaxes(s0, -1, -2).reshape(s0.shape[:-3] + (GROUP_W, HEAD_DIM))


def _uncompact_states(sc):
    bsz = sc.shape[0]
    return jnp.swapaxes(sc.reshape(bsz, 2, N_HEADS, HEAD_DIM, HEAD_DIM), -1, -2)


def _split3(x):
    h1 = x.astype(BF16)
    r1 = x - h1.astype(F32)
    h2 = r1.astype(BF16)
    return h1, h2, (r1 - h2.astype(F32)).astype(BF16)


def _expand_state(x, bmask):
    r = lax.broadcasted_iota(jnp.int32, (HEAD_DIM, GROUP_W), 0)
    c = lax.broadcasted_iota(jnp.int32, (HEAD_DIM, GROUP_W), 1) & (HEAD_DIM - 1)
    rep = jnp.where(r == c, 1.0, 0.0).astype(BF16)
    h1, h2, h3 = _split3(x)
    tiled = (jnp.dot(h1, rep, preferred_element_type=F32) + jnp.dot(h2, rep, preferred_element_type=F32)
             + jnp.dot(h3, rep, preferred_element_type=F32))
    return jnp.where(bmask, tiled, 0.0)


def _compact_state(st, bmask):
    r = lax.broadcasted_iota(jnp.int32, (GROUP_W, HEAD_DIM), 0) & (HEAD_DIM - 1)
    c = lax.broadcasted_iota(jnp.int32, (GROUP_W, HEAD_DIM), 1)
    fold = jnp.where(r == c, 1.0, 0.0).astype(BF16)
    h1, h2, h3 = _split3(jnp.where(bmask, st, 0.0))
    return (jnp.dot(h1, fold, preferred_element_type=F32) + jnp.dot(h2, fold, preferred_element_type=F32)
            + jnp.dot(h3, fold, preferred_element_type=F32))


def _state_specs(bsz, s0):
    spec = pl.BlockSpec((1, 2, GROUP_W, HEAD_DIM), lambda b: (b, 0, 0, 0))
    shape = jax.ShapeDtypeStruct((bsz, 2, GROUP_W, HEAD_DIM), F32)
    if s0 is None:
        return [], [], [spec], [shape]
    return [spec], [s0], [], []


RET_CHUNK = 256


def _retention(pc, log_gamma, s0=None, t=RET_CHUNK):
    bsz, slen, _ = pc.shape
    t = min(t, slen)
    lgl = jnp.repeat(log_gamma, HEAD_DIM, axis=1).reshape(2, 1, GROUP_W)
    lgc = jnp.repeat(log_gamma, t, axis=1).reshape(2, 1, N_HEADS * t)
    col = lambda j: pl.BlockSpec((1, slen, GROUP_W), lambda b: (b, 0, j))
    s_in_specs, s_args, s_out_specs, s_out_shape = _state_specs(bsz, s0)
    return pl.pallas_call(
        functools.partial(_ret_kernel, slen=slen, t=t, has_s0=s0 is not None),
        grid=(bsz,),
        in_specs=[col(0), col(1), col(2), col(3),
                  pl.BlockSpec((2, 1, GROUP_W), lambda b: (0, 0, 0)),
                  pl.BlockSpec((2, 1, N_HEADS * t), lambda b: (0, 0, 0))] + s_in_specs,
        out_specs=[pl.BlockSpec((1, slen, GROUP_W), lambda b: (b, 0, 0))] + s_out_specs,
        out_shape=[jax.ShapeDtypeStruct((bsz, slen, GROUP_W), BF16)] + s_out_shape,
        scratch_shapes=[pltpu.VMEM((slen // t, GROUP_W, GROUP_W), BF16), pltpu.VMEM((GROUP_W, GROUP_W), F32),
                        pltpu.VMEM((t, N_HEADS * t), F32), pltpu.VMEM((4, t, GROUP_W), F32)],
        compiler_params=_cparams(("arbitrary",)),
        name="retention",
    )(pc, pc, pc, pc, lgl, lgc, *s_args)


HGRN_DIRECT = 8
HGRN_BLOCK = 16
HGRN_SAFE_EXP2 = 80.0


def _split2(x):
    hi = x.astype(BF16)
    return hi, (x - hi.astype(F32)).astype(BF16)


def _hgrn_kernel(q_ref, zf_ref, zb_ref, v_ref, g_ref, lb_ref, *rest, slen, t, has_s0):
    if has_s0:
        s0_ref, o_ref, o_scr, st_scr, b_scr, k_scr = rest
    else:
        o_ref, sfin_ref, o_scr, st_scr, b_scr, k_scr = rest
    nc = slen // t
    masks = _head_masks(t)
    bmask = _block_ones()
    ones_bf16 = jnp.where(bmask, 1.0, 0.0).astype(BF16)
    r_i = lax.broadcasted_iota(jnp.int32, (t, t), 0)
    c_i = lax.broadcasted_iota(jnp.int32, (t, t), 1)
    sub = lax.broadcasted_iota(jnp.int32, (t, GROUP_W), 0) & (HGRN_DIRECT - 1)
    levels = []
    s = t // 2
    while s >= HGRN_DIRECT:
        levels.append(s)
        s //= 2
    r_w = lax.broadcasted_iota(jnp.int32, (t, N_HEADS * t), 0)
    c_w = lax.broadcasted_iota(jnp.int32, (t, N_HEADS * t), 1) & (t - 1)
    lvl_masks = []
    for s in levels:
        shift = (2 * s).bit_length() - 1
        lvl_masks.append(jnp.where((r_w >> shift) == (c_w >> shift), 1.0, 0.0))

    def block_roll(x, shift):
        x3 = x.reshape(t // HGRN_DIRECT, HGRN_DIRECT, GROUP_W)
        return pltpu.roll(x3, shift, 1).reshape(t, GROUP_W)

    def direction_fns(direction, z_ref):
        fwd = direction == 0
        lbd = jnp.maximum(lb_ref[direction], LB_FLOOR)
        tri = jnp.where((r_i >= c_i) if fwd else (r_i <= c_i), 1.0, 0.0).astype(BF16)
        blk_shift = HGRN_BLOCK.bit_length() - 1
        near_mask = jnp.where(((r_w >> blk_shift) == (c_w >> blk_shift))
                              & ((c_w <= r_w) if fwd else (c_w >= r_w)), 1.0, 0.0)
        if has_s0:
            st_scr[direction] = _expand_state(s0_ref[0, direction], bmask)
        else:
            st_scr[direction] = jnp.zeros((GROUP_W, GROUP_W), F32)

        def gates(c, worst):
            rows = pl.ds(pl.multiple_of(c * t, t), t)
            z = z_ref[0, rows, :]
            a = jnp.exp(-jnp.abs(z))
            inv = 1.0 / (1.0 + a)
            pos = z >= 0
            sig = jnp.where(pos, inv, a * inv)
            nsig = jnp.where(pos, a * inv, inv)
            lf = jnp.log(lbd + (1.0 - lbd) * sig)
            k_scr[direction, rows, :] = (1.0 - lbd) * nsig
            h1, h2 = _split2(lf)
            b = (jnp.dot(tri, h1, preferred_element_type=F32) + jnp.dot(tri, h2, preferred_element_type=F32)) * LOG2E
            b_scr[direction, rows, :] = b
            for gi in range(t // HGRN_BLOCK):
                first = b[gi * HGRN_BLOCK:gi * HGRN_BLOCK + 1]
                last = b[(gi + 1) * HGRN_BLOCK - 1:(gi + 1) * HGRN_BLOCK]
                worst = jnp.maximum(worst, jnp.abs(first - last))
            return worst

        def chunk(c, factored):
            rows = pl.ds(pl.multiple_of(c * t, t), t)
            q = _silu(q_ref[0, rows, :])
            v = v_ref[0, rows, :]
            vb = v.astype(BF16)
            k = k_scr[direction, rows, :]
            b = b_scr[direction, rows, :]
            vs = _stack_heads(v, masks).astype(BF16)

            def level_scores(s, lm):
                pa, pb = [], []
                zero = jnp.zeros((s, GROUP_W), F32)
                for gi in range(t // (2 * s)):
                    b_lo = b[gi * 2 * s:gi * 2 * s + s]
                    b_hi = b[gi * 2 * s + s:(gi + 1) * 2 * s]
                    if fwd:
                        anc = b_hi[0:1]
                        pa += [zero, jnp.exp2(b_hi - anc)]
                        pb += [jnp.exp2(anc - b_lo), zero]
                    else:
                        anc = b_lo[s - 1:s]
                        pa += [jnp.exp2(b_lo - anc), zero]
                        pb += [zero, jnp.exp2(anc - b_hi)]
                qa = (q * jnp.concatenate(pa, axis=0)).astype(BF16)
                kb = _stack_heads(k * jnp.concatenate(pb, axis=0), masks).astype(BF16)
                return lax.dot_general(qa, kb, NT_DIMS, preferred_element_type=F32) * lm

            def direct_pairs():
                prods, vals = [(q * k).astype(BF16)], [v]
                for dlt in range(1, HGRN_DIRECT):
                    sh = dlt if fwd else HGRN_DIRECT - dlt
                    ok = (sub >= dlt) if fwd else (sub + dlt <= HGRN_DIRECT - 1)
                    w = jnp.exp2(jnp.minimum(b - block_roll(b, sh), 0.0))
                    prods.append(jnp.where(ok, q * block_roll(k, sh) * w, 0.0).astype(BF16))
                    vals.append(block_roll(v, sh))
                hs = jnp.dot(jnp.concatenate(prods, axis=0), ones_bf16, preferred_element_type=F32)
                out = hs[0:t] * vals[0]
                for dlt in range(1, HGRN_DIRECT):
                    out = out + hs[dlt * t:(dlt + 1) * t] * vals[dlt]
                return out

            sc = None
            for s, lm in zip(levels, lvl_masks):
                if s >= HGRN_BLOCK:
                    term = level_scores(s, lm)
                    sc = term if sc is None else sc + term

            if factored:
                ea, eb = [], []
                for gi in range(t // HGRN_BLOCK):
                    blk = b[gi * HGRN_BLOCK:(gi + 1) * HGRN_BLOCK]
                    anc = blk[0:1] if fwd else blk[HGRN_BLOCK - 1:HGRN_BLOCK]
                    ea.append(jnp.exp2(blk - anc))
                    eb.append(jnp.exp2(anc - blk))
                qa = (q * jnp.concatenate(ea, axis=0)).astype(BF16)
                kb = _stack_heads(k * jnp.concatenate(eb, axis=0), masks).astype(BF16)
                sc = sc + lax.dot_general(qa, kb, NT_DIMS, preferred_element_type=F32) * near_mask
                o = jnp.dot(sc.astype(BF16), vs, preferred_element_type=F32)
            else:
                for s, lm in zip(levels, lvl_masks):
                    if s < HGRN_BLOCK:
                        sc = sc + level_scores(s, lm)
                o = jnp.dot(sc.astype(BF16), vs, preferred_element_type=F32) + direct_pairs()

            st = st_scr[direction]
            o = o + lax.dot_general((q * jnp.exp2(b)).astype(BF16), st.astype(BF16), NT_DIMS,
                                    preferred_element_type=F32)
            bl = b[t - 1:t] if fwd else b[0:1]
            upd = lax.dot_general(vb, (k * jnp.exp2(bl - b)).astype(BF16), TN_DIMS,
                                  preferred_element_type=F32)
            st_scr[direction] = st * jnp.exp2(bl) + jnp.where(bmask, upd, 0.0)
            o_scr[direction, rows, :] = o

        return gates, chunk

    gates_f, chunk_f = direction_fns(0, zf_ref)
    gates_b, chunk_b = direction_fns(1, zb_ref)

    worst = lax.fori_loop(0, nc, lambda c, w: gates_b(c, gates_f(c, w)), jnp.zeros((1, GROUP_W), F32), unroll=2)
    factored_ok = jnp.max(worst) < HGRN_SAFE_EXP2

    def sweep(ci, carry, *, factored):
        chunk_f(ci, factored)
        chunk_b(nc - 1 - ci, factored)
        return carry

    @pl.when(factored_ok)
    def _():
        lax.fori_loop(0, nc, functools.partial(sweep, factored=True), 0, unroll=4)

    @pl.when(jnp.logical_not(factored_ok))
    def _():
        lax.fori_loop(0, nc, functools.partial(sweep, factored=False), 0)

    if not has_s0:
        for direction in range(2):
            sfin_ref[0, direction] = _compact_state(st_scr[direction], bmask)

    def finish(c, carry):
        rows = pl.ds(pl.multiple_of(c * t, t), t)
        o = o_scr[0, rows, :] + o_scr[1, rows, :]
        ms = _head_sum(o * o, ones_bf16) * (1.0 / HEAD_DIM)
        y = o * lax.rsqrt(ms + EPS)
        o_ref[0, rows, :] = (y * _silu(g_ref[0, rows, :])).astype(o_ref.dtype)
        return carry

    lax.fori_loop(0, nc, finish, 0, unroll=2)


HGRN_CHUNK = 128


def _hgrn(pd, lb, s0=None, t=HGRN_CHUNK):
    bsz, slen, _ = pd.shape
    col = lambda j: pl.BlockSpec((1, slen, GROUP_W), lambda b: (b, 0, j))
    s_in_specs, s_args, s_out_specs, s_out_shape = _state_specs(bsz, s0)
    return pl.pallas_call(
        functools.partial(_hgrn_kernel, slen=slen, t=t, has_s0=s0 is not None),
        grid=(bsz,),
        in_specs=[col(0), col(1), col(2), col(3), col(4),
                  pl.BlockSpec((2, 1, GROUP_W), lambda b: (0, 0, 0))] + s_in_specs,
        out_specs=[pl.BlockSpec((1, slen, GROUP_W), lambda b: (b, 0, 0))] + s_out_specs,
        out_shape=[jax.ShapeDtypeStruct((bsz, slen, GROUP_W), BF16)] + s_out_shape,
        scratch_shapes=[pltpu.VMEM((2, slen, GROUP_W), F32), pltpu.VMEM((2, GROUP_W, GROUP_W), F32),
                        pltpu.VMEM((2, slen, GROUP_W), F32), pltpu.VMEM((2, slen, GROUP_W), F32)],
        compiler_params=_cparams(("arbitrary",)),
        name="hgrn2",
    )(pd, pd, pd, pd, pd, lb.reshape(2, 1, GROUP_W), *s_args)


def _outproj_kernel(ma_ref, mb_ref, mc_ref, md_ref, w_ref, x_ref, gate_ref, g_ref, o_ref, *, layer, mod_row):
    g = GROUP_W
    y = jnp.dot(ma_ref[0], w_ref[0, 0:g, :], preferred_element_type=F32)
    y = y + jnp.dot(mb_ref[0], w_ref[0, g:2 * g, :], preferred_element_type=F32)
    y = y + jnp.dot(mc_ref[0], w_ref[0, 2 * g:3 * g, :], preferred_element_type=F32)
    y = y + jnp.dot(md_ref[0], w_ref[0, 3 * g:4 * g, :], preferred_element_type=F32)
    ms = jnp.mean(y * y, axis=-1, keepdims=True)
    r = y * lax.rsqrt(ms + EPS) * g_ref[layer:layer + 1, :]
    o_ref[0] = x_ref[0] + _mod_row(gate_ref, mod_row) * r


OUTPROJ_TM = 1024


def _out_proj(mixed, w_out_bf16, x, mods, layer, mod_row, g_post):
    bsz, slen, d = x.shape
    depth = g_post.shape[0]
    tm = min(OUTPROJ_TM, slen)
    mspec = pl.BlockSpec((1, tm, GROUP_W), lambda b, i: (b, i, 0))
    return pl.pallas_call(
        functools.partial(_outproj_kernel, layer=layer, mod_row=mod_row),
        grid=(bsz, slen // tm),
        in_specs=[mspec, mspec, mspec, mspec,
                  pl.BlockSpec((1, d, d), lambda b, i: (layer, 0, 0)),
                  pl.BlockSpec((1, tm, d), lambda b, i: (b, i, 0)),
                  _mod_spec(2, layer),
                  pl.BlockSpec((depth, d), lambda b, i: (0, 0))],
        out_specs=pl.BlockSpec((1, tm, d), lambda b, i: (b, i, 0)),
        out_shape=jax.ShapeDtypeStruct((bsz, slen, d), F32),
        compiler_params=_cparams(("arbitrary", "arbitrary")),
        name="out_proj",
    )(*mixed, w_out_bf16, x, mods, g_post)


def _context_layer(x, mods, layer, mod_row, g_pre, g_post, w_in, w_out, sink, log_gamma, lb):
    bsz, slen, _ = x.shape
    pa, pb, pc, pd, ak, av, bk, bv = _in_proj(x, mods, layer, mod_row, g_pre, w_in, rope=False, emit_kv=True)
    o_a, = _attention(_attn_operands(pa, 0, 2, 3, 2, n_kv=A_KV, qb=slen, kw=slen, back=0, sink=sink))
    o_b, = _attention(_attn_operands(pb, 0, 1, 2, 3, n_kv=N_HEADS, qb=slen, kw=slen, back=0))
    o_c, s_c = _retention(pc, log_gamma)
    o_d, s_d = _hgrn(pd, lb)
    x = _out_proj((o_a, o_b, o_c, o_d), w_out, x, mods, layer, mod_row, g_post)
    kv4 = lambda t: t.reshape(bsz, slen, -1, HEAD_DIM)
    return x, (kv4(ak), kv4(av), kv4(bk), kv4(bv), _uncompact_states(s_c), _uncompact_states(s_d))


def _latent_layer(x, mods, layer, ca_k, ca_v, cb_k, cb_v, st_c, st_d, g_pre, g_post, w_in, w_out, sink, win_tab,
                  na_tabs, log_gamma, lb):
    pa, pb, pc, pd = _in_proj(x, mods, layer, None, g_pre, w_in, rope=True)
    o_a, o_b = _attention(
        _attn_operands(pa, 0, 2, 3, 2, n_kv=A_KV, qb=WIN_QB, kw=WIN_KW, back=WIN_BACK,
                       ctx_k=ca_k, ctx_v=ca_v, table=win_tab, sink=sink, layer=layer),
        _attn_operands(pb, 0, 1, 2, 3, n_kv=N_HEADS, qb=NA_QB, kw=NA_KW, back=NA_QB,
                       ctx_k=cb_k, ctx_v=cb_v, table=na_tabs, layer=layer))
    o_c, = _retention(pc, log_gamma, st_c)
    o_d, = _hgrn(pd, lb, st_d)
    return _out_proj((o_a, o_b, o_c, o_d), w_out, x, mods, layer, None, g_post)


def kernel(x_prompt, x_sample, c, cache_win_k, cache_win_v, cache_na_k, cache_na_v, state_ret, state_hgrn,
           c_ctx, w_ada, b_ada, g_pre, g_post, w_in, w_out, attn_sink, na_rpb, ret_decay_logit, hgrn_lb_logit):
    depth = w_ada.shape[0]
    dec_b, dec_s, d = x_sample.shape
    p_lb = jax.nn.softmax(hgrn_lb_logit.astype(F32), axis=0)
    lower_bounds = jnp.cumsum(p_lb, axis=0) - p_lb[0:1]
    log_gammas = jax.nn.log_sigmoid(ret_decay_logit.astype(F32))
    w_in_b = w_in.astype(BF16)
    w_out_b = w_out.astype(BF16)

    cvecs = jnp.zeros((ADA_ROWS, d), F32).at[:dec_b].set(c).at[dec_b].set(c_ctx)
    mods = _adaln(cvecs, w_ada, b_ada)
    win_tab = _window_table(dec_s)
    na_tabs = _na_tables(na_rpb, dec_s)

    x = x_prompt
    outs = [[] for _ in range(6)]
    for l in range(depth):
        x, extra = _context_layer(x, mods, l, dec_b, g_pre, g_post, w_in_b, w_out_b,
                                  attn_sink[l], log_gammas[l], lower_bounds[l])
        for acc, e in zip(outs, extra):
            acc.append(e)
    y_prompt = x
    stacked = [jnp.stack(o, axis=1) for o in outs]

    past = cache_win_k.shape[2]
    ca_k = cache_win_k.reshape(dec_b, depth, past, A_KV * HEAD_DIM)
    ca_v = cache_win_v.reshape(dec_b, depth, past, A_KV * HEAD_DIM)
    cb_k = cache_na_k.reshape(dec_b, depth, past, GROUP_W)
    cb_v = cache_na_v.reshape(dec_b, depth, past, GROUP_W)
    st_c = _compact_states(state_ret)
    st_d = _compact_states(state_hgrn)
    x = x_sample
    for l in range(depth):
        x = _latent_layer(x, mods, l, ca_k, ca_v, cb_k, cb_v, st_c[:, l], st_d[:, l], g_pre, g_post,
                          w_in_b, w_out_b, attn_sink[l], win_tab, na_tabs, log_gammas[l], lower_bounds[l])
    return (y_prompt, x, *stacked)
```

```python
import functools

import numpy as np
import jax
import jax.numpy as jnp
from jax import lax
from jax.experimental import pallas as pl
from jax.experimental.pallas import tpu as pltpu

F32 = jnp.float32
BF16 = jnp.bfloat16

D_MODEL = 1024
DEPTH = 4
GRID_W = 64
HEAD_DIM = 64
N_HEADS = 4
GROUP_W = N_HEADS * HEAD_DIM
A_KV = 2
WINDOW = 128
NA_ROWS = 8
NA_COLS = 16
ROPE_BASE = 10000.0
EPS = 1e-6
NEG = -1e30
LB_FLOOR = 1e-30
LOG2E = 1.4426950408889634
W_A = 3 * GROUP_W
W_B = 4 * GROUP_W
W_C = 4 * GROUP_W
W_D = 5 * GROUP_W
IN_WIDTH = W_A + W_B + W_C + W_D

V7X_VMEM_LIMIT_BYTES = 56 * 1024 * 1024
ADA_ROWS = 16

NT_DIMS = (((1,), (1,)), ((), ()))
TN_DIMS = (((0,), (0,)), ((), ()))


def _cparams(sem, flags=None):
    return pltpu.CompilerParams(dimension_semantics=sem, vmem_limit_bytes=V7X_VMEM_LIMIT_BYTES, flags=flags)


def _sigmoid(x):
    return 1.0 / (1.0 + jnp.exp(-x))


def _silu(x):
    return x * _sigmoid(x)


def _head_masks(rows):
    lane = lax.broadcasted_iota(jnp.int32, (rows, GROUP_W), 1)
    return [(lane >= h * HEAD_DIM) & (lane < (h + 1) * HEAD_DIM) for h in range(N_HEADS)]


def _stack_heads(x, masks):
    return jnp.concatenate([jnp.where(m, x, 0.0) for m in masks], axis=0)


def _unstack_heads(x, masks, t):
    out = jnp.where(masks[0], x[0:t], 0.0)
    for h in range(1, N_HEADS):
        out = out + jnp.where(masks[h], x[h * t:(h + 1) * t], 0.0)
    return out


def _block_ones():
    shift = HEAD_DIM.bit_length() - 1
    r = lax.broadcasted_iota(jnp.int32, (GROUP_W, GROUP_W), 0) >> shift
    c = lax.broadcasted_iota(jnp.int32, (GROUP_W, GROUP_W), 1) >> shift
    return r == c


def _head_sum(x, ones_bf16):
    hi = x.astype(BF16)
    lo = (x - hi.astype(F32)).astype(BF16)
    return (jnp.dot(hi, ones_bf16, preferred_element_type=F32)
            + jnp.dot(lo, ones_bf16, preferred_element_type=F32))


def _adaln_kernel(c_ref, w_ref, b_ref, o_ref):
    s = _silu(c_ref[...]).astype(BF16)
    o_ref[0, 0] = jnp.dot(s, w_ref[0].astype(BF16), preferred_element_type=F32) + b_ref[0]


def _adaln(cvecs, w_ada, b_ada):
    depth, d, d3 = w_ada.shape
    tn = 512
    per = d // tn
    return pl.pallas_call(
        _adaln_kernel,
        grid=(depth, d3 // tn),
        in_specs=[pl.BlockSpec((ADA_ROWS, d), lambda l, j: (0, 0)),
                  pl.BlockSpec((1, d, tn), lambda l, j: (l, 0, j)),
                  pl.BlockSpec((1, 1, tn), lambda l, j: (l, 0, j))],
        out_specs=pl.BlockSpec((1, 1, ADA_ROWS, tn), lambda l, j: (l, j // per, 0, j % per)),
        out_shape=jax.ShapeDtypeStruct((depth, 3, ADA_ROWS, d), F32),
        compiler_params=_cparams(("arbitrary", "arbitrary")),
        name="adaln",
    )(cvecs, w_ada, b_ada.reshape(depth, 1, d3))


def _rope(x, cos, sin):
    w = x.shape[-1]
    lane = lax.broadcasted_iota(jnp.int32, x.shape, 1)
    first = (lane & 31) < 16
    swapped = jnp.where(first, pltpu.roll(x, w - 16, 1), pltpu.roll(x, 16, 1))
    return x * cos[:, :w] + swapped * sin[:, :w]


def _mod_row(mod_ref, mod_row):
    row = pl.program_id(0) if mod_row is None else mod_row
    return mod_ref[0, 0, pl.ds(row, 1), :]


def _inproj_kernel(*refs, rope, emit_kv, layer, mod_row):
    refs = list(refs)
    x_ref, shift_ref, scale_ref, g_ref, w_ref = refs[:5]
    if rope:
        cos_ref, sin_ref = refs[5:7]
    n_out = 8 if emit_kv else 4
    pa_ref, pb_ref, pc_ref, pd_ref = refs[-n_out:][:4]
    x = x_ref[0]
    ms = jnp.mean(x * x, axis=-1, keepdims=True)
    y = x * lax.rsqrt(ms + EPS) * g_ref[layer:layer + 1, :]
    h = (y * (1.0 + _mod_row(scale_ref, mod_row)) + _mod_row(shift_ref, mod_row)).astype(BF16)

    def mm(c0, c1):
        return jnp.dot(h, w_ref[0, :, c0:c1], preferred_element_type=F32)

    if rope:
        cos = cos_ref[...]
        sin = sin_ref[...]
    g = GROUP_W
    aq = mm(0, g)
    akv = mm(g, 2 * g)
    if rope:
        aq = _rope(aq, cos, sin)
        ak = _rope(akv[:, :g // 2], cos, sin)
        akv = jnp.concatenate([ak, akv[:, g // 2:]], axis=1)
    pa_ref[0, :, 0:g] = aq.astype(pa_ref.dtype)
    pa_ref[0, :, g:2 * g] = akv.astype(pa_ref.dtype)
    pa_ref[0, :, 2 * g:3 * g] = mm(2 * g, 3 * g).astype(pa_ref.dtype)
    pbv = mm(W_A, W_A + W_B)
    pb_ref[0] = pbv.astype(pb_ref.dtype)
    if emit_kv:
        ak_ref, av_ref, bk_ref, bv_ref = refs[-4:]
        ak_ref[0] = akv[:, :g // 2]
        av_ref[0] = akv[:, g // 2:]
        bk_ref[0] = pbv[:, g:2 * g]
        bv_ref[0] = pbv[:, 2 * g:3 * g]
    c0 = W_A + W_B
    cq = mm(c0, c0 + g)
    ck = mm(c0 + g, c0 + 2 * g)
    if rope:
        cq = _rope(cq, cos, sin)
        ck = _rope(ck, cos, sin)
    pc_ref[0, :, 0:g] = cq.astype(pc_ref.dtype)
    pc_ref[0, :, g:2 * g] = ck.astype(pc_ref.dtype)
    pc_ref[0, :, 2 * g:4 * g] = mm(c0 + 2 * g, c0 + 4 * g).astype(pc_ref.dtype)
    d0 = c0 + W_C
    pd_ref[0] = mm(d0, d0 + W_D).astype(pd_ref.dtype)


def _rope_tables(slen):
    t = np.arange(slen)
    nf = HEAD_DIM // 4
    freqs = ROPE_BASE ** (-np.arange(nf, dtype=np.float64) / nf)
    d = np.arange(HEAD_DIM)
    pos = np.where(d[None, :] < HEAD_DIM // 2, (t // GRID_W)[:, None], (t % GRID_W)[:, None])
    ang = pos * freqs[d % nf][None, :]
    sign = np.where((d % (2 * nf)) < nf, -1.0, 1.0)[None, :]
    cos = np.tile(np.cos(ang), (1, N_HEADS))
    sin = np.tile(np.sin(ang) * sign, (1, N_HEADS))
    return jnp.asarray(cos, F32), jnp.asarray(sin, F32)


def _mod_spec(which, layer):
    return pl.BlockSpec((1, 1, ADA_ROWS, D_MODEL), lambda b, i: (layer, which, 0, 0))


def _in_proj(x, mods, layer, mod_row, g_pre, w_in_bf16, rope, emit_kv=False, tm=256):
    bsz, slen, d = x.shape
    depth = g_pre.shape[0]
    in_specs = [pl.BlockSpec((1, tm, d), lambda b, i: (b, i, 0)),
                _mod_spec(0, layer),
                _mod_spec(1, layer),
                pl.BlockSpec((depth, d), lambda b, i: (0, 0)),
                pl.BlockSpec((1, d, IN_WIDTH), lambda b, i: (layer, 0, 0))]
    args = [x, mods, mods, g_pre, w_in_bf16]
    if rope:
        cos, sin = _rope_tables(slen)
        in_specs += [pl.BlockSpec((tm, GROUP_W), lambda b, i: (i, 0))] * 2
        args += [cos, sin]
    widths = (W_A, W_B, W_C, W_D)
    if emit_kv:
        widths += (A_KV * HEAD_DIM, A_KV * HEAD_DIM, GROUP_W, GROUP_W)
    return pl.pallas_call(
        functools.partial(_inproj_kernel, rope=rope, emit_kv=emit_kv, layer=layer, mod_row=mod_row),
        grid=(bsz, slen // tm),
        in_specs=in_specs,
        out_specs=[pl.BlockSpec((1, tm, w), lambda b, i: (b, i, 0)) for w in widths],
        out_shape=[jax.ShapeDtypeStruct((bsz, slen, w), F32) for w in widths],
        compiler_params=_cparams(("arbitrary", "arbitrary")),
        name="in_proj",
    )(*args)


ATTN_ROWS = 256


def _attn_kernel(*refs, cfgs):
    n_in = [4 + 2 * c["has_ctx"] + (c["n_tab"] > 0) + c["has_sink"] for c in cfgs]
    outs = refs[sum(n_in):]
    start = 0
    for c, n, o_ref in zip(cfgs, n_in, outs):
        _attn_body(refs[start:start + n], o_ref, **c)
        start += n


def _attn_body(in_refs, o_ref, *, n_kv, qb, kw, back, slen, has_ctx, n_tab, tab_heads, has_sink):
    it = iter(in_refs)
    q_ref, k_ref, v_ref, g_ref = next(it), next(it), next(it), next(it)
    kc_ref = vc_ref = tab_ref = sink_ref = None
    if has_ctx:
        kc_ref, vc_ref = next(it), next(it)
    if n_tab:
        tab_ref = next(it)
    if has_sink:
        sink_ref = next(it)

    n = pl.program_id(1)
    nblk = slen // qb
    ws = pl.multiple_of(jnp.clip(n * qb - back, 0, slen - kw), HEAD_DIM)
    if n_tab == 3:
        tix = jnp.where(n == 0, 0, jnp.where(n == nblk - 1, 2, 1))
    else:
        tix = 0
    grp = N_HEADS // n_kv
    qscale = HEAD_DIM ** -0.5 * LOG2E
    pair_w = 2 * HEAD_DIM
    lane = lax.broadcasted_iota(jnp.int32, (1, pair_w), 1)
    half_mask = (lane < HEAD_DIM, lane >= HEAD_DIM)
    rb = min(qb, ATTN_ROWS)
    for j in range(N_HEADS // 2):
        cols = slice(j * pair_w, (j + 1) * pair_w)
        if grp == 1:
            kv_cols, kv_half = cols, (0, 1)
        else:
            kv_cols, kv_half = slice(0, pair_w), (j, j)
        kall = k_ref[0, pl.ds(ws, kw), kv_cols].astype(BF16)
        vall = v_ref[0, pl.ds(ws, kw), kv_cols]
        if has_ctx:
            kall = jnp.concatenate([kall, kc_ref[0, 0, :, kv_cols].astype(BF16)], axis=0)
            vall = jnp.concatenate([vall, vc_ref[0, 0, :, kv_cols]], axis=0)
        vaug = {hf: jnp.where(half_mask[hf], vall, 1.0).astype(BF16) for hf in set(kv_half)}
        for r0 in range(0, qb, rb):
            rs = slice(r0, r0 + rb)
            q128 = q_ref[0, rs, cols] * qscale
            q_other = pltpu.roll(q128, HEAD_DIM, 1) if kv_half != (0, 1) else None
            outs = []
            for i in range(2):
                hf = kv_half[i]
                qm = jnp.where(half_mask[hf], q128 if i == hf else q_other, 0.0).astype(BF16)
                s = lax.dot_general(qm, kall, NT_DIMS, preferred_element_type=F32)
                if n_tab:
                    tab = tab_ref[0, tix, 2 * j + i if tab_heads == N_HEADS else 0, rs, :]
                    s = jnp.concatenate([s[:, :kw] + tab, s[:, kw:]], axis=1) if has_ctx else s + tab
                m = jnp.max(s, axis=-1, keepdims=True)
                if has_sink:
                    sink = sink_ref[2 * j + i] * LOG2E
                    m = jnp.maximum(m, sink)
                p = jnp.exp2(s - m).astype(BF16)
                acc = jnp.dot(p, vaug[hf], preferred_element_type=F32)
                if has_sink:
                    acc = acc + jnp.where(half_mask[1 - hf], jnp.exp2(sink - m), 0.0)
                acc = acc / pltpu.roll(acc, HEAD_DIM, 1)
                outs.append(acc if i == hf else pltpu.roll(acc, HEAD_DIM, 1))
            out = jnp.where(half_mask[0], outs[0], outs[1]) * _silu(g_ref[0, rs, cols])
            o_ref[0, rs, cols] = out.astype(o_ref.dtype)


def _attn_operands(p, q_col, k_col, v_col, g_col, n_kv, qb, kw, back, ctx_k=None, ctx_v=None,
                   table=None, sink=None, layer=0):
    slen = p.shape[1]
    kvw = n_kv * HEAD_DIM
    in_specs = [pl.BlockSpec((1, qb, GROUP_W), lambda b, n: (b, n, q_col)),
                pl.BlockSpec((1, slen, kvw), lambda b, n: (b, 0, k_col)),
                pl.BlockSpec((1, slen, kvw), lambda b, n: (b, 0, v_col)),
                pl.BlockSpec((1, qb, GROUP_W), lambda b, n: (b, n, g_col))]
    args = [p, p, p, p]
    has_ctx = ctx_k is not None
    if has_ctx:
        past = ctx_k.shape[2]
        in_specs += [pl.BlockSpec((1, 1, past, kvw), lambda b, n: (b, layer, 0, 0))] * 2
        args += [ctx_k, ctx_v]
    n_tab = tab_heads = 0
    if table is not None:
        n_tab, tab_heads = table.shape[1], table.shape[2]
        tab_layer = layer if table.shape[0] > 1 else 0
        in_specs.append(pl.BlockSpec((1,) + table.shape[1:], lambda b, n: (tab_layer, 0, 0, 0, 0)))
        args.append(table)
    if sink is not None:
        in_specs.append(pl.BlockSpec(memory_space=pltpu.SMEM))
        args.append(sink)
    cfg = dict(n_kv=n_kv, qb=qb, kw=kw, back=back, slen=slen, has_ctx=has_ctx, n_tab=n_tab,
               tab_heads=tab_heads, has_sink=sink is not None)
    return in_specs, args, cfg


def _attention(*operands):
    in_specs = [s for op in operands for s in op[0]]
    args = [a for op in operands for a in op[1]]
    cfgs = tuple(op[2] for op in operands)
    bsz = args[0].shape[0]
    slen, qb = cfgs[0]["slen"], cfgs[0]["qb"]
    assert all(c["slen"] == slen and c["qb"] == qb for c in cfgs)
    return pl.pallas_call(
        functools.partial(_attn_kernel, cfgs=cfgs),
        grid=(bsz, slen // qb),
        in_specs=in_specs,
        out_specs=[pl.BlockSpec((1, qb, GROUP_W), lambda b, n: (b, n, 0)) for _ in cfgs],
        out_shape=[jax.ShapeDtypeStruct((bsz, slen, GROUP_W), BF16) for _ in cfgs],
        compiler_params=_cparams(("arbitrary", "arbitrary")),
        name="attn",
    )(*args)


WIN_QB = 256
WIN_BACK = WINDOW
WIN_KW = WIN_QB + 2 * WINDOW
NA_QROWS = 4
NA_KROWS = 12
NA_QB = NA_QROWS * GRID_W
NA_KW = NA_KROWS * GRID_W


def _window_table(slen):
    nblk = slen // WIN_QB
    tabs = []
    for n in (0, 1, nblk - 1):
        ws = int(np.clip(n * WIN_QB - WIN_BACK, 0, slen - WIN_KW))
        qpos = n * WIN_QB + np.arange(WIN_QB)[:, None]
        kpos = ws + np.arange(WIN_KW)[None, :]
        tabs.append(np.where(np.abs(qpos - kpos) <= WINDOW, 0.0, NEG))
    return jnp.asarray(np.stack(tabs)[None, :, None], F32)


N_RPB_R = 2 * NA_ROWS - 1
N_RPB_C = 2 * NA_COLS - 1


def _na_table_kernel(rpb_ref, o_ref, tz_scr, *, rows):
    base = (pl.program_id(0) * N_HEADS + pl.program_id(1)) * (N_RPB_R * N_RPB_C)
    qc = lax.broadcasted_iota(jnp.int32, (GRID_W, 2 * GRID_W), 0)
    kk = lax.broadcasted_iota(jnp.int32, (GRID_W, 2 * GRID_W), 1)
    kc = kk & (GRID_W - 1)
    diff = kc - qc
    qws = jnp.clip(qc - NA_COLS // 2, 0, GRID_W - NA_COLS)
    col_ok = (kc >= qws) & (kc < qws + NA_COLS)
    neg = jnp.full((GRID_W, 2 * GRID_W), NEG, F32)
    for dr in range(N_RPB_R):
        acc = neg
        for m in range(N_RPB_C):
            acc = jnp.where(diff == m - (NA_COLS - 1), rpb_ref[base + dr * N_RPB_C + m] * LOG2E, acc)
        tz_scr[dr] = jnp.where(col_ok, acc, NEG)
    wr = min(NA_ROWS, rows)
    nblk = rows // NA_QROWS
    for ti, g in enumerate((0, 1, nblk - 1)):
        ws_row = min(max(g * NA_QROWS - NA_QROWS, 0), rows - NA_KROWS)
        for qr in range(NA_QROWS):
            r = g * NA_QROWS + qr
            rs = min(max(r - wr // 2, 0), rows - wr)
            for p in range(NA_KROWS // 2):
                halves = []
                for kr in (ws_row + 2 * p, ws_row + 2 * p + 1):
                    halves.append(tz_scr[kr - r + NA_ROWS - 1] if rs <= kr < rs + wr else neg)
                o_ref[0, ti, 0, qr * GRID_W:(qr + 1) * GRID_W, p * 2 * GRID_W:(p + 1) * 2 * GRID_W] = (
                    jnp.where(kk < GRID_W, halves[0], halves[1]))


def _na_tables(na_rpb, slen):
    depth = na_rpb.shape[0]
    return pl.pallas_call(
        functools.partial(_na_table_kernel, rows=slen // GRID_W),
        grid=(depth, N_HEADS),
        in_specs=[pl.BlockSpec(memory_space=pltpu.SMEM)],
        out_specs=pl.BlockSpec((1, 3, 1, NA_QB, NA_KW), lambda l, h: (l, 0, h, 0, 0)),
        out_shape=jax.ShapeDtypeStruct((depth, 3, N_HEADS, NA_QB, NA_KW), F32),
        scratch_shapes=[pltpu.VMEM((N_RPB_R, GRID_W, 2 * GRID_W), F32)],
        compiler_params=_cparams(("arbitrary", "arbitrary")),
        name="na_table",
    )(na_rpb.astype(F32).reshape(-1))


def _ret_kernel(q_ref, k_ref, v_ref, g_ref, lgl_ref, lgc_ref, *rest, slen, t, has_s0):
    if has_s0:
        s0_ref, o_ref, sb_scr, st_scr, dm_scr, dec_scr = rest
    else:
        o_ref, sfin_ref, sb_scr, st_scr, dm_scr, dec_scr = rest
    nc = slen // t
    masks = _head_masks(t)
    bmask = _block_ones()
    ones_bf16 = jnp.where(bmask, 1.0, 0.0).astype(BF16)
    lgf, lgb = lgl_ref[0], lgl_ref[1]

    @pl.when(pl.program_id(0) == 0)
    def _():
        ii = lax.broadcasted_iota(jnp.int32, (t, N_HEADS * t), 0)
        jj = lax.broadcasted_iota(jnp.int32, (t, N_HEADS * t), 1) & (t - 1)
        dist = (ii - jj).astype(F32)
        dm_scr[...] = (jnp.where(dist >= 0, jnp.exp(dist * lgc_ref[0]), 0.0)
                       + jnp.where(dist <= 0, jnp.exp(-dist * lgc_ref[1]), 0.0))
        idx = lax.broadcasted_iota(jnp.int32, (t, GROUP_W), 0).astype(F32)
        dec_scr[0] = jnp.exp((idx + 1.0) * lgf)
        dec_scr[1] = jnp.exp((t - 1.0 - idx) * lgf)
        dec_scr[2] = jnp.exp((t - idx) * lgb)
        dec_scr[3] = jnp.exp(idx * lgb)

    cdec_f = jnp.exp(float(t) * lgf)
    cdec_b = jnp.exp(float(t) * lgb)
    kscale = HEAD_DIM ** -0.5

    def state_update(st, k, v, kdec, cdec):
        upd = lax.dot_general(v, (k * kdec).astype(BF16), TN_DIMS, preferred_element_type=F32)
        return st * cdec + jnp.where(bmask, upd, 0.0)

    def init_state(direction):
        if has_s0:
            st_scr[...] = _expand_state(s0_ref[0, direction], bmask)
        else:
            st_scr[...] = jnp.zeros((GROUP_W, GROUP_W), F32)

    def emit_final(direction):
        if not has_s0:
            sfin_ref[0, direction] = _compact_state(st_scr[...], bmask)

    init_state(1)

    def sweep_bwd(ci, carry):
        c = nc - 1 - ci
        rows = pl.ds(pl.multiple_of(c * t, t), t)
        st = st_scr[...]
        sb_scr[c] = st.astype(BF16)
        k = k_ref[0, rows, :] * kscale
        st_scr[...] = state_update(st, k, v_ref[0, rows, :].astype(BF16), dec_scr[3], cdec_b)
        return carry

    lax.fori_loop(0, nc, sweep_bwd, 0, unroll=4)
    emit_final(1)
    init_state(0)

    def sweep_fwd(c, carry):
        rows = pl.ds(pl.multiple_of(c * t, t), t)
        q = q_ref[0, rows, :]
        k = k_ref[0, rows, :] * kscale
        v = v_ref[0, rows, :]
        ks = _stack_heads(k, masks).astype(BF16)
        vs = _stack_heads(v, masks).astype(BF16)
        sc = lax.dot_general(q.astype(BF16), ks, NT_DIMS, preferred_element_type=F32) * dm_scr[...]
        o = jnp.dot(sc.astype(BF16), vs, preferred_element_type=F32)
        st = st_scr[...]
        o = o + lax.dot_general((q * dec_scr[0]).astype(BF16), st.astype(BF16), NT_DIMS,
                                preferred_element_type=F32)
        o = o + lax.dot_general((q * dec_scr[2]).astype(BF16), sb_scr[c], NT_DIMS, preferred_element_type=F32)
        st_scr[...] = state_update(st, k, v.astype(BF16), dec_scr[1], cdec_f)
        mu = _head_sum(o, ones_bf16) * (1.0 / HEAD_DIM)
        d = o - mu
        var = _head_sum(d * d, ones_bf16) * (1.0 / HEAD_DIM)
        y = d * lax.rsqrt(var + EPS)
        o_ref[0, rows, :] = (y * _silu(g_ref[0, rows, :])).astype(o_ref.dtype)
        return carry

    lax.fori_loop(0, nc, sweep_fwd, 0, unroll=4)
    emit_final(0)


def _compact_states(s0):
    return jnp.swapaxes(s0, -1, -2).reshape(s0.shape[:-3] + (GROUP_W, HEAD_DIM))


def _uncompact_states(sc):
    bsz = sc.shape[0]
    return jnp.swapaxes(sc.reshape(bsz, 2, N_HEADS, HEAD_DIM, HEAD_DIM), -1, -2)


def _split3(x):
    h1 = x.astype(BF16)
    r1 = x - h1.astype(F32)
    h2 = r1.astype(BF16)
    return h1, h2, (r1 - h2.astype(F32)).astype(BF16)


def _expand_state(x, bmask):
    r = lax.broadcasted_iota(jnp.int32, (HEAD_DIM, GROUP_W), 0)
    c = lax.broadcasted_iota(jnp.int32, (HEAD_DIM, GROUP_W), 1) & (HEAD_DIM - 1)
    rep = jnp.where(r == c, 1.0, 0.0).astype(BF16)
    h1, h2, h3 = _split3(x)
    tiled = (jnp.dot(h1, rep, preferred_element_type=F32) + jnp.dot(h2, rep, preferred_element_type=F32)
             + jnp.dot(h3, rep, preferred_element_type=F32))
    return jnp.where(bmask, tiled, 0.0)


def _compact_state(st, bmask):
    r = lax.broadcasted_iota(jnp.int32, (GROUP_W, HEAD_DIM), 0) & (HEAD_DIM - 1)
    c = lax.broadcasted_iota(jnp.int32, (GROUP_W, HEAD_DIM), 1)
    fold = jnp.where(r == c, 1.0, 0.0).astype(BF16)
    h1, h2, h3 = _split3(jnp.where(bmask, st, 0.0))
    return (jnp.dot(h1, fold, preferred_element_type=F32) + jnp.dot(h2, fold, preferred_element_type=F32)
            + jnp.dot(h3, fold, preferred_element_type=F32))


def _state_specs(bsz, s0):
    spec = pl.BlockSpec((1, 2, GROUP_W, HEAD_DIM), lambda b: (b, 0, 0, 0))
    shape = jax.ShapeDtypeStruct((bsz, 2, GROUP_W, HEAD_DIM), F32)
    if s0 is None:
        return [], [], [spec], [shape]
    return [spec], [s0], [], []


RET_CHUNK = 256


def _retention(pc, log_gamma, s0=None, t=RET_CHUNK):
    bsz, slen, _ = pc.shape
    t = min(t, slen)
    lgl = jnp.repeat(log_gamma, HEAD_DIM, axis=1).reshape(2, 1, GROUP_W)
    lgc = jnp.repeat(log_gamma, t, axis=1).reshape(2, 1, N_HEADS * t)
    col = lambda j: pl.BlockSpec((1, slen, GROUP_W), lambda b: (b, 0, j))
    s_in_specs, s_args, s_out_specs, s_out_shape = _state_specs(bsz, s0)
    return pl.pallas_call(
        functools.partial(_ret_kernel, slen=slen, t=t, has_s0=s0 is not None),
        grid=(bsz,),
        in_specs=[col(0), col(1), col(2), col(3),
                  pl.BlockSpec((2, 1, GROUP_W), lambda b: (0, 0, 0)),
                  pl.BlockSpec((2, 1, N_HEADS * t), lambda b: (0, 0, 0))] + s_in_specs,
        out_specs=[pl.BlockSpec((1, slen, GROUP_W), lambda b: (b, 0, 0))] + s_out_specs,
        out_shape=[jax.ShapeDtypeStruct((bsz, slen, GROUP_W), BF16)] + s_out_shape,
        scratch_shapes=[pltpu.VMEM((slen // t, GROUP_W, GROUP_W), BF16), pltpu.VMEM((GROUP_W, GROUP_W), F32),
                        pltpu.VMEM((t, N_HEADS * t), F32), pltpu.VMEM((4, t, GROUP_W), F32)],
        compiler_params=_cparams(("arbitrary",)),
        name="retention",
    )(pc, pc, pc, pc, lgl, lgc, *s_args)


HGRN_DIRECT = 8
HGRN_BLOCKS = (32, 16)
HGRN_SAFE_EXP2 = 80.0


def _split2(x):
    hi = x.astype(BF16)
    return hi, (x - hi.astype(F32)).astype(BF16)


def _hgrn_kernel(q_ref, zf_ref, zb_ref, v_ref, g_ref, lb_ref, *rest, slen, t, has_s0):
    if has_s0:
        s0_ref, o_ref, o_scr, st_scr, b_scr, k_scr = rest
    else:
        o_ref, sfin_ref, o_scr, st_scr, b_scr, k_scr = rest
    nc = slen // t
    masks = _head_masks(t)
    bmask = _block_ones()
    ones_bf16 = jnp.where(bmask, 1.0, 0.0).astype(BF16)
    r_i = lax.broadcasted_iota(jnp.int32, (t, t), 0)
    c_i = lax.broadcasted_iota(jnp.int32, (t, t), 1)
    sub = lax.broadcasted_iota(jnp.int32, (t, GROUP_W), 0) & (HGRN_DIRECT - 1)
    levels = []
    s = t // 2
    while s >= HGRN_DIRECT:
        levels.append(s)
        s //= 2
    r_w = lax.broadcasted_iota(jnp.int32, (t, N_HEADS * t), 0)
    c_w = lax.broadcasted_iota(jnp.int32, (t, N_HEADS * t), 1) & (t - 1)
    lvl_masks = []
    for s in levels:
        shift = (2 * s).bit_length() - 1
        lvl_masks.append(jnp.where((r_w >> shift) == (c_w >> shift), 1.0, 0.0))

    def block_roll(x, shift):
        x3 = x.reshape(t // HGRN_DIRECT, HGRN_DIRECT, GROUP_W)
        return pltpu.roll(x3, shift, 1).reshape(t, GROUP_W)

    def direction_fns(direction, z_ref):
        fwd = direction == 0
        lbd = jnp.maximum(lb_ref[direction], LB_FLOOR)
        tri = jnp.where((r_i >= c_i) if fwd else (r_i <= c_i), 1.0, 0.0).astype(BF16)
        near_masks = {}
        for w in HGRN_BLOCKS:
            blk_shift = w.bit_length() - 1
            near_masks[w] = jnp.where(((r_w >> blk_shift) == (c_w >> blk_shift))
                                      & ((c_w <= r_w) if fwd else (c_w >= r_w)), 1.0, 0.0)
        if has_s0:
            st_scr[direction] = _expand_state(s0_ref[0, direction], bmask)
        else:
            st_scr[direction] = jnp.zeros((GROUP_W, GROUP_W), F32)

        def gates(c, worst):
            rows = pl.ds(pl.multiple_of(c * t, t), t)
            z = z_ref[0, rows, :]
            a = jnp.exp(-jnp.abs(z))
            inv = 1.0 / (1.0 + a)
            pos = z >= 0
            sig = jnp.where(pos, inv, a * inv)
            nsig = jnp.where(pos, a * inv, inv)
            lf = jnp.log(lbd + (1.0 - lbd) * sig)
            k_scr[direction, rows, :] = (1.0 - lbd) * nsig
            h1, h2 = _split2(lf)
            b = (jnp.dot(tri, h1, preferred_element_type=F32) + jnp.dot(tri, h2, preferred_element_type=F32)) * LOG2E
            b_scr[direction, rows, :] = b
            worst = list(worst)
            for wi, w in enumerate(HGRN_BLOCKS):
                for gi in range(t // w):
                    first = b[gi * w:gi * w + 1]
                    last = b[(gi + 1) * w - 1:(gi + 1) * w]
                    worst[wi] = jnp.maximum(worst[wi], jnp.abs(first - last))
            return tuple(worst)

        def chunk(c, near_w):
            rows = pl.ds(pl.multiple_of(c * t, t), t)
            q = _silu(q_ref[0, rows, :])
            v = v_ref[0, rows, :]
            vb = v.astype(BF16)
            k = k_scr[direction, rows, :]
            b = b_scr[direction, rows, :]
            vs = _stack_heads(v, masks).astype(BF16)

            def level_scores(s, lm):
                pa, pb = [], []
                zero = jnp.zeros((s, GROUP_W), F32)
                for gi in range(t // (2 * s)):
                    b_lo = b[gi * 2 * s:gi * 2 * s + s]
                    b_hi = b[gi * 2 * s + s:(gi + 1) * 2 * s]
                    if fwd:
                        anc = b_hi[0:1]
                        pa += [zero, jnp.exp2(b_hi - anc)]
                        pb += [jnp.exp2(anc - b_lo), zero]
                    else:
                        anc = b_lo[s - 1:s]
                        pa += [jnp.exp2(b_lo - anc), zero]
                        pb += [zero, jnp.exp2(anc - b_hi)]
                qa = (q * jnp.concatenate(pa, axis=0)).astype(BF16)
                kb = _stack_heads(k * jnp.concatenate(pb, axis=0), masks).astype(BF16)
                return lax.dot_general(qa, kb, NT_DIMS, preferred_element_type=F32) * lm

            def direct_pairs():
                prods, vals = [(q * k).astype(BF16)], [v]
                for dlt in range(1, HGRN_DIRECT):
                    sh = dlt if fwd else HGRN_DIRECT - dlt
                    ok = (sub >= dlt) if fwd else (sub + dlt <= HGRN_DIRECT - 1)
                    w = jnp.exp2(jnp.minimum(b - block_roll(b, sh), 0.0))
                    prods.append(jnp.where(ok, q * block_roll(k, sh) * w, 0.0).astype(BF16))
                    vals.append(block_roll(v, sh))
                hs = jnp.dot(jnp.concatenate(prods, axis=0), ones_bf16, preferred_element_type=F32)
                out = hs[0:t] * vals[0]
                for dlt in range(1, HGRN_DIRECT):
                    out = out + hs[dlt * t:(dlt + 1) * t] * vals[dlt]
                return out

            sc = None
            for s, lm in zip(levels, lvl_masks):
                if s >= max(near_w, HGRN_DIRECT):
                    term = level_scores(s, lm)
                    sc = term if sc is None else sc + term

            if near_w:
                ea, eb = [], []
                for gi in range(t // near_w):
                    blk = b[gi * near_w:(gi + 1) * near_w]
                    anc = blk[0:1] if fwd else blk[near_w - 1:near_w]
                    ea.append(jnp.exp2(blk - anc))
                    eb.append(jnp.exp2(anc - blk))
                qa = (q * jnp.concatenate(ea, axis=0)).astype(BF16)
                kb = _stack_heads(k * jnp.concatenate(eb, axis=0), masks).astype(BF16)
                sc = sc + lax.dot_general(qa, kb, NT_DIMS, preferred_element_type=F32) * near_masks[near_w]
                o = jnp.dot(sc.astype(BF16), vs, preferred_element_type=F32)
            else:
                o = jnp.dot(sc.astype(BF16), vs, preferred_element_type=F32) + direct_pairs()

            st = st_scr[direction]
            o = o + lax.dot_general((q * jnp.exp2(b)).astype(BF16), st.astype(BF16), NT_DIMS,
                                    preferred_element_type=F32)
            bl = b[t - 1:t] if fwd else b[0:1]
            upd = lax.dot_general(vb, (k * jnp.exp2(bl - b)).astype(BF16), TN_DIMS,
                                  preferred_element_type=F32)
            st_scr[direction] = st * jnp.exp2(bl) + jnp.where(bmask, upd, 0.0)
            o_scr[direction, rows, :] = o

        return gates, chunk

    gates_f, chunk_f = direction_fns(0, zf_ref)
    gates_b, chunk_b = direction_fns(1, zb_ref)

    worst = lax.fori_loop(0, nc, lambda c, w: gates_b(c, gates_f(c, w)),
                          tuple(jnp.zeros((1, GROUP_W), F32) for _ in HGRN_BLOCKS), unroll=2)
    safe = [jnp.max(w) < HGRN_SAFE_EXP2 for w in worst]

    def sweep(ci, carry, *, near_w):
        chunk_f(ci, near_w)
        chunk_b(nc - 1 - ci, near_w)
        return carry

    taken = False
    for w, ok in zip(HGRN_BLOCKS, safe):
        cond = ok if taken is False else jnp.logical_and(jnp.logical_not(taken), ok)

        @pl.when(cond)
        def _(w=w):
            lax.fori_loop(0, nc, functools.partial(sweep, near_w=w), 0, unroll=4)

        taken = ok if taken is False else jnp.logical_or(taken, ok)

    @pl.when(jnp.logical_not(taken))
    def _():
        lax.fori_loop(0, nc, functools.partial(sweep, near_w=0), 0)

    if not has_s0:
        for direction in range(2):
            sfin_ref[0, direction] = _compact_state(st_scr[direction], bmask)

    def finish(c, carry):
        rows = pl.ds(pl.multiple_of(c * t, t), t)
        o = o_scr[0, rows, :] + o_scr[1, rows, :]
        ms = _head_sum(o * o, ones_bf16) * (1.0 / HEAD_DIM)
        y = o * lax.rsqrt(ms + EPS)
        o_ref[0, rows, :] = (y * _silu(g_ref[0, rows, :])).astype(o_ref.dtype)
        return carry

    lax.fori_loop(0, nc, finish, 0, unroll=2)


HGRN_CHUNK = 128


def _hgrn(pd, lb, s0=None, t=HGRN_CHUNK):
    bsz, slen, _ = pd.shape
    col = lambda j: pl.BlockSpec((1, slen, GROUP_W), lambda b: (b, 0, j))
    s_in_specs, s_args, s_out_specs, s_out_shape = _state_specs(bsz, s0)
    return pl.pallas_call(
        functools.partial(_hgrn_kernel, slen=slen, t=t, has_s0=s0 is not None),
        grid=(bsz,),
        in_specs=[col(0), col(1), col(2), col(3), col(4),
                  pl.BlockSpec((2, 1, GROUP_W), lambda b: (0, 0, 0))] + s_in_specs,
        out_specs=[pl.BlockSpec((1, slen, GROUP_W), lambda b: (b, 0, 0))] + s_out_specs,
        out_shape=[jax.ShapeDtypeStruct((bsz, slen, GROUP_W), BF16)] + s_out_shape,
        scratch_shapes=[pltpu.VMEM((2, slen, GROUP_W), F32), pltpu.VMEM((2, GROUP_W, GROUP_W), F32),
                        pltpu.VMEM((2, slen, GROUP_W), F32), pltpu.VMEM((2, slen, GROUP_W), F32)],
        compiler_params=_cparams(("arbitrary",)),
        name="hgrn2",
    )(pd, pd, pd, pd, pd, lb.reshape(2, 1, GROUP_W), *s_args)


def _outproj_kernel(ma_ref, mb_ref, mc_ref, md_ref, w_ref, x_ref, gate_ref, g_ref, o_ref, *, layer, mod_row):
    g = GROUP_W
    y = jnp.dot(ma_ref[0], w_ref[0, 0:g, :], preferred_element_type=F32)
    y = y + jnp.dot(mb_ref[0], w_ref[0, g:2 * g, :], preferred_element_type=F32)
    y = y + jnp.dot(mc_ref[0], w_ref[0, 2 * g:3 * g, :], preferred_element_type=F32)
    y = y + jnp.dot(md_ref[0], w_ref[0, 3 * g:4 * g, :], preferred_element_type=F32)
    ms = jnp.mean(y * y, axis=-1, keepdims=True)
    r = y * lax.rsqrt(ms + EPS) * g_ref[layer:layer + 1, :]
    o_ref[0] = x_ref[0] + _mod_row(gate_ref, mod_row) * r


OUTPROJ_TM = 1024


def _out_proj(mixed, w_out_bf16, x, mods, layer, mod_row, g_post):
    bsz, slen, d = x.shape
    depth = g_post.shape[0]
    tm = min(OUTPROJ_TM, slen)
    mspec = pl.BlockSpec((1, tm, GROUP_W), lambda b, i: (b, i, 0))
    return pl.pallas_call(
        functools.partial(_outproj_kernel, layer=layer, mod_row=mod_row),
        grid=(bsz, slen // tm),
        in_specs=[mspec, mspec, mspec, mspec,
                  pl.BlockSpec((1, d, d), lambda b, i: (layer, 0, 0)),
                  pl.BlockSpec((1, tm, d), lambda b, i: (b, i, 0)),
                  _mod_spec(2, layer),
                  pl.BlockSpec((depth, d), lambda b, i: (0, 0))],
        out_specs=pl.BlockSpec((1, tm, d), lambda b, i: (b, i, 0)),
        out_shape=jax.ShapeDtypeStruct((bsz, slen, d), F32),
        compiler_params=_cparams(("arbitrary", "arbitrary")),
        name="out_proj",
    )(*mixed, w_out_bf16, x, mods, g_post)


def _context_layer(x, mods, layer, mod_row, g_pre, g_post, w_in, w_out, sink, log_gamma, lb):
    bsz, slen, _ = x.shape
    pa, pb, pc, pd, ak, av, bk, bv = _in_proj(x, mods, layer, mod_row, g_pre, w_in, rope=False, emit_kv=True)
    o_a, = _attention(_attn_operands(pa, 0, 2, 3, 2, n_kv=A_KV, qb=slen, kw=slen, back=0, sink=sink))
    o_b, = _attention(_attn_operands(pb, 0, 1, 2, 3, n_kv=N_HEADS, qb=slen, kw=slen, back=0))
    o_c, s_c = _retention(pc, log_gamma)
    o_d, s_d = _hgrn(pd, lb)
    x = _out_proj((o_a, o_b, o_c, o_d), w_out, x, mods, layer, mod_row, g_post)
    kv4 = lambda t: t.reshape(bsz, slen, -1, HEAD_DIM)
    return x, (kv4(ak), kv4(av), kv4(bk), kv4(bv), _uncompact_states(s_c), _uncompact_states(s_d))


def _latent_layer(x, mods, layer, ca_k, ca_v, cb_k, cb_v, st_c, st_d, g_pre, g_post, w_in, w_out, sink, win_tab,
                  na_tabs, log_gamma, lb):
    pa, pb, pc, pd = _in_proj(x, mods, layer, None, g_pre, w_in, rope=True)
    o_a, o_b = _attention(
        _attn_operands(pa, 0, 2, 3, 2, n_kv=A_KV, qb=WIN_QB, kw=WIN_KW, back=WIN_BACK,
                       ctx_k=ca_k, ctx_v=ca_v, table=win_tab, sink=sink, layer=layer),
        _attn_operands(pb, 0, 1, 2, 3, n_kv=N_HEADS, qb=NA_QB, kw=NA_KW, back=NA_QB,
                       ctx_k=cb_k, ctx_v=cb_v, table=na_tabs, layer=layer))
    o_c, = _retention(pc, log_gamma, st_c)
    o_d, = _hgrn(pd, lb, st_d)
    return _out_proj((o_a, o_b, o_c, o_d), w_out, x, mods, layer, None, g_post)


def kernel(x_prompt, x_sample, c, cache_win_k, cache_win_v, cache_na_k, cache_na_v, state_ret, state_hgrn,
           c_ctx, w_ada, b_ada, g_pre, g_post, w_in, w_out, attn_sink, na_rpb, ret_decay_logit, hgrn_lb_logit):
    depth = w_ada.shape[0]
    dec_b, dec_s, d = x_sample.shape
    p_lb = jax.nn.softmax(hgrn_lb_logit.astype(F32), axis=0)
    lower_bounds = jnp.cumsum(p_lb, axis=0) - p_lb[0:1]
    log_gammas = jax.nn.log_sigmoid(ret_decay_logit.astype(F32))
    w_in_b = w_in.astype(BF16)
    w_out_b = w_out.astype(BF16)

    cvecs = jnp.zeros((ADA_ROWS, d), F32).at[:dec_b].set(c).at[dec_b].set(c_ctx)
    mods = _adaln(cvecs, w_ada, b_ada)
    win_tab = _window_table(dec_s)
    na_tabs = _na_tables(na_rpb, dec_s)

    x = x_prompt
    outs = [[] for _ in range(6)]
    for l in range(depth):
        x, extra = _context_layer(x, mods, l, dec_b, g_pre, g_post, w_in_b, w_out_b,
                                  attn_sink[l], log_gammas[l], lower_bounds[l])
        for acc, e in zip(outs, extra):
            acc.append(e)
    y_prompt = x
    stacked = [jnp.stack(o, axis=1) for o in outs]

    past = cache_win_k.shape[2]
    ca_k = cache_win_k.reshape(dec_b, depth, past, A_KV * HEAD_DIM)
    ca_v = cache_win_v.reshape(dec_b, depth, past, A_KV * HEAD_DIM)
    cb_k = cache_na_k.reshape(dec_b, depth, past, GROUP_W)
    cb_v = cache_na_v.reshape(dec_b, depth, past, GROUP_W)
    st_c = _compact_states(state_ret)
    st_d = _compact_states(state_hgrn)
    x = x_sample
    for l in range(depth):
        x = _latent_layer(x, mods, l, ca_k, ca_v, cb_k, cb_v, st_c[:, l], st_d[:, l], g_pre, g_post,
                          w_in_b, w_out_b, attn_sink[l], win_tab, na_tabs, log_gammas[l], lower_bounds[l])
    return (y_prompt, x, *stacked)
```

```python
import functools

import numpy as np
import jax
import jax.numpy as jnp
from jax import lax
from jax.experimental import pallas as pl
from jax.experimental.pallas import tpu as pltpu

F32 = jnp.float32
BF16 = jnp.bfloat16

D_MODEL = 1024
DEPTH = 4
GRID_W = 64
HEAD_DIM = 64
N_HEADS = 4
GROUP_W = N_HEADS * HEAD_DIM
A_KV = 2
WINDOW = 128
NA_ROWS = 8
NA_COLS = 16
ROPE_BASE = 10000.0
EPS = 1e-6
NEG = -1e30
LB_FLOOR = 1e-30
LOG2E = 1.4426950408889634
W_A = 3 * GROUP_W
W_B = 4 * GROUP_W
W_C = 4 * GROUP_W
W_D = 5 * GROUP_W
IN_WIDTH = W_A + W_B + W_C + W_D

V7X_VMEM_LIMIT_BYTES = 56 * 1024 * 1024
ADA_ROWS = 16

NT_DIMS = (((1,), (1,)), ((), ()))
TN_DIMS = (((0,), (0,)), ((), ()))


def _cparams(sem, flags=None):
    return pltpu.CompilerParams(dimension_semantics=sem, vmem_limit_bytes=V7X_VMEM_LIMIT_BYTES, flags=flags)


def _sigmoid(x):
    return 1.0 / (1.0 + jnp.exp(-x))


def _silu(x):
    return x * _sigmoid(x)


def _head_masks(rows):
    lane = lax.broadcasted_iota(jnp.int32, (rows, GROUP_W), 1)
    return [(lane >= h * HEAD_DIM) & (lane < (h + 1) * HEAD_DIM) for h in range(N_HEADS)]


def _stack_heads(x, masks):
    return jnp.concatenate([jnp.where(m, x, 0.0) for m in masks], axis=0)


def _unstack_heads(x, masks, t):
    out = jnp.where(masks[0], x[0:t], 0.0)
    for h in range(1, N_HEADS):
        out = out + jnp.where(masks[h], x[h * t:(h + 1) * t], 0.0)
    return out


def _block_ones():
    shift = HEAD_DIM.bit_length() - 1
    r = lax.broadcasted_iota(jnp.int32, (GROUP_W, GROUP_W), 0) >> shift
    c = lax.broadcasted_iota(jnp.int32, (GROUP_W, GROUP_W), 1) >> shift
    return r == c


def _head_sum(x, ones_bf16):
    hi = x.astype(BF16)
    lo = (x - hi.astype(F32)).astype(BF16)
    return (jnp.dot(hi, ones_bf16, preferred_element_type=F32)
            + jnp.dot(lo, ones_bf16, preferred_element_type=F32))


def _adaln_kernel(c_ref, w_ref, b_ref, o_ref):
    s = _silu(c_ref[...]).astype(BF16)
    o_ref[0, 0] = jnp.dot(s, w_ref[0].astype(BF16), preferred_element_type=F32) + b_ref[0]


def _adaln(cvecs, w_ada, b_ada):
    depth, d, d3 = w_ada.shape
    tn = 512
    per = d // tn
    return pl.pallas_call(
        _adaln_kernel,
        grid=(depth, d3 // tn),
        in_specs=[pl.BlockSpec((ADA_ROWS, d), lambda l, j: (0, 0)),
                  pl.BlockSpec((1, d, tn), lambda l, j: (l, 0, j)),
                  pl.BlockSpec((1, 1, tn), lambda l, j: (l, 0, j))],
        out_specs=pl.BlockSpec((1, 1, ADA_ROWS, tn), lambda l, j: (l, j // per, 0, j % per)),
        out_shape=jax.ShapeDtypeStruct((depth, 3, ADA_ROWS, d), F32),
        compiler_params=_cparams(("arbitrary", "arbitrary")),
        name="adaln",
    )(cvecs, w_ada, b_ada.reshape(depth, 1, d3))


def _rope(x, cos, sin):
    w = x.shape[-1]
    lane = lax.broadcasted_iota(jnp.int32, x.shape, 1)
    first = (lane & 31) < 16
    swapped = jnp.where(first, pltpu.roll(x, w - 16, 1), pltpu.roll(x, 16, 1))
    return x * cos[:, :w] + swapped * sin[:, :w]


def _mod_row(mod_ref, mod_row):
    row = pl.program_id(0) if mod_row is None else mod_row
    return mod_ref[0, 0, pl.ds(row, 1), :]


def _inproj_kernel(*refs, rope, emit_kv, layer, mod_row):
    refs = list(refs)
    x_ref, shift_ref, scale_ref, g_ref, w_ref = refs[:5]
    if rope:
        cos_ref, sin_ref = refs[5:7]
    n_out = 8 if emit_kv else 4
    pa_ref, pb_ref, pc_ref, pd_ref = refs[-n_out:][:4]
    x = x_ref[0]
    ms = jnp.mean(x * x, axis=-1, keepdims=True)
    y = x * lax.rsqrt(ms + EPS) * g_ref[layer:layer + 1, :]
    h = (y * (1.0 + _mod_row(scale_ref, mod_row)) + _mod_row(shift_ref, mod_row)).astype(BF16)

    def mm(c0, c1):
        return jnp.dot(h, w_ref[0, :, c0:c1], preferred_element_type=F32)

    if rope:
        cos = cos_ref[...]
        sin = sin_ref[...]
    g = GROUP_W
    aq = mm(0, g)
    akv = mm(g, 2 * g)
    if rope:
        aq = _rope(aq, cos, sin)
        ak = _rope(akv[:, :g // 2], cos, sin)
        akv = jnp.concatenate([ak, akv[:, g // 2:]], axis=1)
    pa_ref[0, :, 0:g] = aq.astype(pa_ref.dtype)
    pa_ref[0, :, g:2 * g] = akv.astype(pa_ref.dtype)
    pa_ref[0, :, 2 * g:3 * g] = mm(2 * g, 3 * g).astype(pa_ref.dtype)
    pbv = mm(W_A, W_A + W_B)
    pb_ref[0] = pbv.astype(pb_ref.dtype)
    if emit_kv:
        ak_ref, av_ref, bk_ref, bv_ref = refs[-4:]
        ak_ref[0] = akv[:, :g // 2]
        av_ref[0] = akv[:, g // 2:]
        bk_ref[0] = pbv[:, g:2 * g]
        bv_ref[0] = pbv[:, 2 * g:3 * g]
    c0 = W_A + W_B
    cq = mm(c0, c0 + g)
    ck = mm(c0 + g, c0 + 2 * g)
    if rope:
        cq = _rope(cq, cos, sin)
        ck = _rope(ck, cos, sin)
    pc_ref[0, :, 0:g] = cq.astype(pc_ref.dtype)
    pc_ref[0, :, g:2 * g] = ck.astype(pc_ref.dtype)
    pc_ref[0, :, 2 * g:4 * g] = mm(c0 + 2 * g, c0 + 4 * g).astype(pc_ref.dtype)
    d0 = c0 + W_C
    pd_ref[0] = mm(d0, d0 + W_D).astype(pd_ref.dtype)


def _rope_tables(slen):
    t = np.arange(slen)
    nf = HEAD_DIM // 4
    freqs = ROPE_BASE ** (-np.arange(nf, dtype=np.float64) / nf)
    d = np.arange(HEAD_DIM)
    pos = np.where(d[None, :] < HEAD_DIM // 2, (t // GRID_W)[:, None], (t % GRID_W)[:, None])
    ang = pos * freqs[d % nf][None, :]
    sign = np.where((d % (2 * nf)) < nf, -1.0, 1.0)[None, :]
    cos = np.tile(np.cos(ang), (1, N_HEADS))
    sin = np.tile(np.sin(ang) * sign, (1, N_HEADS))
    return jnp.asarray(cos, F32), jnp.asarray(sin, F32)


def _mod_spec(which, layer):
    return pl.BlockSpec((1, 1, ADA_ROWS, D_MODEL), lambda b, i: (layer, which, 0, 0))


INPROJ_TM = 512


def _in_proj(x, mods, layer, mod_row, g_pre, w_in_bf16, rope, emit_kv=False):
    bsz, slen, d = x.shape
    tm = min(INPROJ_TM, slen)
    depth = g_pre.shape[0]
    in_specs = [pl.BlockSpec((1, tm, d), lambda b, i: (b, i, 0)),
                _mod_spec(0, layer),
                _mod_spec(1, layer),
                pl.BlockSpec((depth, d), lambda b, i: (0, 0)),
                pl.BlockSpec((1, d, IN_WIDTH), lambda b, i: (layer, 0, 0))]
    args = [x, mods, mods, g_pre, w_in_bf16]
    if rope:
        cos, sin = _rope_tables(slen)
        in_specs += [pl.BlockSpec((tm, GROUP_W), lambda b, i: (i, 0))] * 2
        args += [cos, sin]
    widths = (W_A, W_B, W_C, W_D)
    if emit_kv:
        widths += (A_KV * HEAD_DIM, A_KV * HEAD_DIM, GROUP_W, GROUP_W)
    return pl.pallas_call(
        functools.partial(_inproj_kernel, rope=rope, emit_kv=emit_kv, layer=layer, mod_row=mod_row),
        grid=(bsz, slen // tm),
        in_specs=in_specs,
        out_specs=[pl.BlockSpec((1, tm, w), lambda b, i: (b, i, 0)) for w in widths],
        out_shape=[jax.ShapeDtypeStruct((bsz, slen, w), F32) for w in widths],
        compiler_params=_cparams(("arbitrary", "arbitrary")),
        name="in_proj",
    )(*args)


ATTN_ROWS = 256


def _attn_kernel(*refs, cfgs):
    n_in = [4 + 2 * c["has_ctx"] + (c["n_tab"] > 0) + c["has_sink"] for c in cfgs]
    outs = refs[sum(n_in):]
    start = 0
    for c, n, o_ref in zip(cfgs, n_in, outs):
        _attn_body(refs[start:start + n], o_ref, **c)
        start += n


def _attn_body(in_refs, o_ref, *, n_kv, qb, kw, back, slen, has_ctx, n_tab, tab_heads, has_sink):
    it = iter(in_refs)
    q_ref, k_ref, v_ref, g_ref = next(it), next(it), next(it), next(it)
    kc_ref = vc_ref = tab_ref = sink_ref = None
    if has_ctx:
        kc_ref, vc_ref = next(it), next(it)
    if n_tab:
        tab_ref = next(it)
    if has_sink:
        sink_ref = next(it)

    n = pl.program_id(1)
    nblk = slen // qb
    ws = pl.multiple_of(jnp.clip(n * qb - back, 0, slen - kw), HEAD_DIM)
    if n_tab == 3:
        tix = jnp.where(n == 0, 0, jnp.where(n == nblk - 1, 2, 1))
    else:
        tix = 0
    grp = N_HEADS // n_kv
    qscale = HEAD_DIM ** -0.5 * LOG2E
    pair_w = 2 * HEAD_DIM
    lane = lax.broadcasted_iota(jnp.int32, (1, pair_w), 1)
    half_mask = (lane < HEAD_DIM, lane >= HEAD_DIM)
    rb = min(qb, ATTN_ROWS)
    for j in range(N_HEADS // 2):
        cols = slice(j * pair_w, (j + 1) * pair_w)
        if grp == 1:
            kv_cols, kv_half = cols, (0, 1)
        else:
            kv_cols, kv_half = slice(0, pair_w), (j, j)
        kall = k_ref[0, pl.ds(ws, kw), kv_cols].astype(BF16)
        vall = v_ref[0, pl.ds(ws, kw), kv_cols]
        if has_ctx:
            kall = jnp.concatenate([kall, kc_ref[0, 0, :, kv_cols].astype(BF16)], axis=0)
            vall = jnp.concatenate([vall, vc_ref[0, 0, :, kv_cols]], axis=0)
        vaug = {hf: jnp.where(half_mask[hf], vall, 1.0).astype(BF16) for hf in set(kv_half)}
        for r0 in range(0, qb, rb):
            rs = slice(r0, r0 + rb)
            q128 = q_ref[0, rs, cols] * qscale
            q_other = pltpu.roll(q128, HEAD_DIM, 1) if kv_half != (0, 1) else None
            outs = []
            for i in range(2):
                hf = kv_half[i]
                qm = jnp.where(half_mask[hf], q128 if i == hf else q_other, 0.0).astype(BF16)
                s = lax.dot_general(qm, kall, NT_DIMS, preferred_element_type=F32)
                if n_tab:
                    tab = tab_ref[0, tix, 2 * j + i if tab_heads == N_HEADS else 0, rs, :]
                    s = jnp.concatenate([s[:, :kw] + tab, s[:, kw:]], axis=1) if has_ctx else s + tab
                m = jnp.max(s, axis=-1, keepdims=True)
                if has_sink:
                    sink = sink_ref[2 * j + i] * LOG2E
                    m = jnp.maximum(m, sink)
                p = jnp.exp2(s - m).astype(BF16)
                acc = jnp.dot(p, vaug[hf], preferred_element_type=F32)
                if has_sink:
                    acc = acc + jnp.where(half_mask[1 - hf], jnp.exp2(sink - m), 0.0)
                acc = acc / pltpu.roll(acc, HEAD_DIM, 1)
                outs.append(acc if i == hf else pltpu.roll(acc, HEAD_DIM, 1))
            out = jnp.where(half_mask[0], outs[0], outs[1]) * _silu(g_ref[0, rs, cols])
            o_ref[0, rs, cols] = out.astype(o_ref.dtype)


def _attn_operands(p, q_col, k_col, v_col, g_col, n_kv, qb, kw, back, ctx_k=None, ctx_v=None,
                   table=None, sink=None, layer=0):
    slen = p.shape[1]
    kvw = n_kv * HEAD_DIM
    in_specs = [pl.BlockSpec((1, qb, GROUP_W), lambda b, n: (b, n, q_col)),
                pl.BlockSpec((1, slen, kvw), lambda b, n: (b, 0, k_col)),
                pl.BlockSpec((1, slen, kvw), lambda b, n: (b, 0, v_col)),
                pl.BlockSpec((1, qb, GROUP_W), lambda b, n: (b, n, g_col))]
    args = [p, p, p, p]
    has_ctx = ctx_k is not None
    if has_ctx:
        past = ctx_k.shape[2]
        in_specs += [pl.BlockSpec((1, 1, past, kvw), lambda b, n: (b, layer, 0, 0))] * 2
        args += [ctx_k, ctx_v]
    n_tab = tab_heads = 0
    if table is not None:
        n_tab, tab_heads = table.shape[1], table.shape[2]
        tab_layer = layer if table.shape[0] > 1 else 0
        in_specs.append(pl.BlockSpec((1,) + table.shape[1:], lambda b, n: (tab_layer, 0, 0, 0, 0)))
        args.append(table)
    if sink is not None:
        in_specs.append(pl.BlockSpec(memory_space=pltpu.SMEM))
        args.append(sink)
    cfg = dict(n_kv=n_kv, qb=qb, kw=kw, back=back, slen=slen, has_ctx=has_ctx, n_tab=n_tab,
               tab_heads=tab_heads, has_sink=sink is not None)
    return in_specs, args, cfg


def _attention(*operands):
    in_specs = [s for op in operands for s in op[0]]
    args = [a for op in operands for a in op[1]]
    cfgs = tuple(op[2] for op in operands)
    bsz = args[0].shape[0]
    slen, qb = cfgs[0]["slen"], cfgs[0]["qb"]
    assert all(c["slen"] == slen and c["qb"] == qb for c in cfgs)
    return pl.pallas_call(
        functools.partial(_attn_kernel, cfgs=cfgs),
        grid=(bsz, slen // qb),
        in_specs=in_specs,
        out_specs=[pl.BlockSpec((1, qb, GROUP_W), lambda b, n: (b, n, 0)) for _ in cfgs],
        out_shape=[jax.ShapeDtypeStruct((bsz, slen, GROUP_W), BF16) for _ in cfgs],
        compiler_params=_cparams(("arbitrary", "arbitrary")),
        name="attn",
    )(*args)


WIN_QB = 256
WIN_BACK = WINDOW
WIN_KW = WIN_QB + 2 * WINDOW
NA_QROWS = 4
NA_KROWS = 12
NA_QB = NA_QROWS * GRID_W
NA_KW = NA_KROWS * GRID_W


def _window_table(slen):
    nblk = slen // WIN_QB
    tabs = []
    for n in (0, 1, nblk - 1):
        ws = int(np.clip(n * WIN_QB - WIN_BACK, 0, slen - WIN_KW))
        qpos = n * WIN_QB + np.arange(WIN_QB)[:, None]
        kpos = ws + np.arange(WIN_KW)[None, :]
        tabs.append(np.where(np.abs(qpos - kpos) <= WINDOW, 0.0, NEG))
    return jnp.asarray(np.stack(tabs)[None, :, None], F32)


N_RPB_R = 2 * NA_ROWS - 1
N_RPB_C = 2 * NA_COLS - 1


def _na_table_kernel(rpb_ref, o_ref, tz_scr, *, rows):
    base = (pl.program_id(0) * N_HEADS + pl.program_id(1)) * (N_RPB_R * N_RPB_C)
    qc = lax.broadcasted_iota(jnp.int32, (GRID_W, 2 * GRID_W), 0)
    kk = lax.broadcasted_iota(jnp.int32, (GRID_W, 2 * GRID_W), 1)
    kc = kk & (GRID_W - 1)
    diff = kc - qc
    qws = jnp.clip(qc - NA_COLS // 2, 0, GRID_W - NA_COLS)
    col_ok = (kc >= qws) & (kc < qws + NA_COLS)
    neg = jnp.full((GRID_W, 2 * GRID_W), NEG, F32)
    for dr in range(N_RPB_R):
        acc = neg
        for m in range(N_RPB_C):
            acc = jnp.where(diff == m - (NA_COLS - 1), rpb_ref[base + dr * N_RPB_C + m] * LOG2E, acc)
        tz_scr[dr] = jnp.where(col_ok, acc, NEG)
    wr = min(NA_ROWS, rows)
    nblk = rows // NA_QROWS
    for ti, g in enumerate((0, 1, nblk - 1)):
        ws_row = min(max(g * NA_QROWS - NA_QROWS, 0), rows - NA_KROWS)
        for qr in range(NA_QROWS):
            r = g * NA_QROWS + qr
            rs = min(max(r - wr // 2, 0), rows - wr)
            for p in range(NA_KROWS // 2):
                halves = []
                for kr in (ws_row + 2 * p, ws_row + 2 * p + 1):
                    halves.append(tz_scr[kr - r + NA_ROWS - 1] if rs <= kr < rs + wr else neg)
                o_ref[0, ti, 0, qr * GRID_W:(qr + 1) * GRID_W, p * 2 * GRID_W:(p + 1) * 2 * GRID_W] = (
                    jnp.where(kk < GRID_W, halves[0], halves[1]))


def _na_tables(na_rpb, slen):
    depth = na_rpb.shape[0]
    return pl.pallas_call(
        functools.partial(_na_table_kernel, rows=slen // GRID_W),
        grid=(depth, N_HEADS),
        in_specs=[pl.BlockSpec(memory_space=pltpu.SMEM)],
        out_specs=pl.BlockSpec((1, 3, 1, NA_QB, NA_KW), lambda l, h: (l, 0, h, 0, 0)),
        out_shape=jax.ShapeDtypeStruct((depth, 3, N_HEADS, NA_QB, NA_KW), F32),
        scratch_shapes=[pltpu.VMEM((N_RPB_R, GRID_W, 2 * GRID_W), F32)],
        compiler_params=_cparams(("arbitrary", "arbitrary")),
        name="na_table",
    )(na_rpb.astype(F32).reshape(-1))


def _ret_kernel(q_ref, k_ref, v_ref, g_ref, lgl_ref, lgc_ref, *rest, slen, t, has_s0):
    if has_s0:
        s0_ref, o_ref, sb_scr, st_scr, dm_scr, dec_scr = rest
    else:
        o_ref, sfin_ref, sb_scr, st_scr, dm_scr, dec_scr = rest
    nc = slen // t
    masks = _head_masks(t)
    bmask = _block_ones()
    ones_bf16 = jnp.where(bmask, 1.0, 0.0).astype(BF16)
    lgf, lgb = lgl_ref[0], lgl_ref[1]

    @pl.when(pl.program_id(0) == 0)
    def _():
        ii = lax.broadcasted_iota(jnp.int32, (t, N_HEADS * t), 0)
        jj = lax.broadcasted_iota(jnp.int32, (t, N_HEADS * t), 1) & (t - 1)
        dist = (ii - jj).astype(F32)
        dm_scr[...] = (jnp.where(dist >= 0, jnp.exp(dist * lgc_ref[0]), 0.0)
                       + jnp.where(dist <= 0, jnp.exp(-dist * lgc_ref[1]), 0.0))
        idx = lax.broadcasted_iota(jnp.int32, (t, GROUP_W), 0).astype(F32)
        dec_scr[0] = jnp.exp((idx + 1.0) * lgf)
        dec_scr[1] = jnp.exp((t - 1.0 - idx) * lgf)
        dec_scr[2] = jnp.exp((t - idx) * lgb)
        dec_scr[3] = jnp.exp(idx * lgb)

    cdec_f = jnp.exp(float(t) * lgf)
    cdec_b = jnp.exp(float(t) * lgb)
    kscale = HEAD_DIM ** -0.5

    def state_update(st, k, v, kdec, cdec):
        upd = lax.dot_general(v, (k * kdec).astype(BF16), TN_DIMS, preferred_element_type=F32)
        return st * cdec + jnp.where(bmask, upd, 0.0)

    def init_state(direction):
        if has_s0:
            st_scr[...] = _expand_state(s0_ref[0, direction], bmask)
        else:
            st_scr[...] = jnp.zeros((GROUP_W, GROUP_W), F32)

    def emit_final(direction):
        if not has_s0:
            sfin_ref[0, direction] = _compact_state(st_scr[...], bmask)

    init_state(1)

    def sweep_bwd(ci, carry):
        c = nc - 1 - ci
        rows = pl.ds(pl.multiple_of(c * t, t), t)
        st = st_scr[...]
        sb_scr[c] = st.astype(BF16)
        k = k_ref[0, rows, :] * kscale
        st_scr[...] = state_update(st, k, v_ref[0, rows, :].astype(BF16), dec_scr[3], cdec_b)
        return carry

    lax.fori_loop(0, nc, sweep_bwd, 0, unroll=4)
    emit_final(1)
    init_state(0)

    def sweep_fwd(c, carry):
        rows = pl.ds(pl.multiple_of(c * t, t), t)
        q = q_ref[0, rows, :]
        k = k_ref[0, rows, :] * kscale
        v = v_ref[0, rows, :]
        ks = _stack_heads(k, masks).astype(BF16)
        vs = _stack_heads(v, masks).astype(BF16)
        sc = lax.dot_general(q.astype(BF16), ks, NT_DIMS, preferred_element_type=F32) * dm_scr[...]
        o = jnp.dot(sc.astype(BF16), vs, preferred_element_type=F32)
        st = st_scr[...]
        o = o + lax.dot_general((q * dec_scr[0]).astype(BF16), st.astype(BF16), NT_DIMS,
                                preferred_element_type=F32)
        o = o + lax.dot_general((q * dec_scr[2]).astype(BF16), sb_scr[c], NT_DIMS, preferred_element_type=F32)
        st_scr[...] = state_update(st, k, v.astype(BF16), dec_scr[1], cdec_f)
        mu = _head_sum(o, ones_bf16) * (1.0 / HEAD_DIM)
        d = o - mu
        var = _head_sum(d * d, ones_bf16) * (1.0 / HEAD_DIM)
        y = d * lax.rsqrt(var + EPS)
        o_ref[0, rows, :] = (y * _silu(g_ref[0, rows, :])).astype(o_ref.dtype)
        return carry

    lax.fori_loop(0, nc, sweep_fwd, 0, unroll=4)
    emit_final(0)


def _compact_states(s0):
    return jnp.swapaxes(s0, -1, -2).reshape(s0.shape[:-3] + (GROUP_W, HEAD_DIM))


def _uncompact_states(sc):
    bsz = sc.shape[0]
    return jnp.swapaxes(sc.reshape(bsz, 2, N_HEADS, HEAD_DIM, HEAD_DIM), -1, -2)


def _split3(x):
    h1 = x.astype(BF16)
    r1 = x - h1.astype(F32)
    h2 = r1.astype(BF16)
    return h1, h2, (r1 - h2.astype(F32)).astype(BF16)


def _expand_state(x, bmask):
    r = lax.broadcasted_iota(jnp.int32, (HEAD_DIM, GROUP_W), 0)
    c = lax.broadcasted_iota(jnp.int32, (HEAD_DIM, GROUP_W), 1) & (HEAD_DIM - 1)
    rep = jnp.where(r == c, 1.0, 0.0).astype(BF16)
    h1, h2, h3 = _split3(x)
    tiled = (jnp.dot(h1, rep, preferred_element_type=F32) + jnp.dot(h2, rep, preferred_element_type=F32)
             + jnp.dot(h3, rep, preferred_element_type=F32))
    return jnp.where(bmask, tiled, 0.0)


def _compact_state(st, bmask):
    r = lax.broadcasted_iota(jnp.int32, (GROUP_W, HEAD_DIM), 0) & (HEAD_DIM - 1)
    c = lax.broadcasted_iota(jnp.int32, (GROUP_W, HEAD_DIM), 1)
    fold = jnp.where(r == c, 1.0, 0.0).astype(BF16)
    h1, h2, h3 = _split3(jnp.where(bmask, st, 0.0))
    return (jnp.dot(h1, fold, preferred_element_type=F32) + jnp.dot(h2, fold, preferred_element_type=F32)
            + jnp.dot(h3, fold, preferred_element_type=F32))


def _state_specs(bsz, s0):
    spec = pl.BlockSpec((1, 2, GROUP_W, HEAD_DIM), lambda b: (b, 0, 0, 0))
    shape = jax.ShapeDtypeStruct((bsz, 2, GROUP_W, HEAD_DIM), F32)
    if s0 is None:
        return [], [], [spec], [shape]
    return [spec], [s0], [], []


RET_CHUNK = 256


def _retention(pc, log_gamma, s0=None, t=RET_CHUNK):
    bsz, slen, _ = pc.shape
    t = min(t, slen)
    lgl = jnp.repeat(log_gamma, HEAD_DIM, axis=1).reshape(2, 1, GROUP_W)
    lgc = jnp.repeat(log_gamma, t, axis=1).reshape(2, 1, N_HEADS * t)
    col = lambda j: pl.BlockSpec((1, slen, GROUP_W), lambda b: (b, 0, j))
    s_in_specs, s_args, s_out_specs, s_out_shape = _state_specs(bsz, s0)
    return pl.pallas_call(
        functools.partial(_ret_kernel, slen=slen, t=t, has_s0=s0 is not None),
        grid=(bsz,),
        in_specs=[col(0), col(1), col(2), col(3),
                  pl.BlockSpec((2, 1, GROUP_W), lambda b: (0, 0, 0)),
                  pl.BlockSpec((2, 1, N_HEADS * t), lambda b: (0, 0, 0))] + s_in_specs,
        out_specs=[pl.BlockSpec((1, slen, GROUP_W), lambda b: (b, 0, 0))] + s_out_specs,
        out_shape=[jax.ShapeDtypeStruct((bsz, slen, GROUP_W), BF16)] + s_out_shape,
        scratch_shapes=[pltpu.VMEM((slen // t, GROUP_W, GROUP_W), BF16), pltpu.VMEM((GROUP_W, GROUP_W), F32),
                        pltpu.VMEM((t, N_HEADS * t), F32), pltpu.VMEM((4, t, GROUP_W), F32)],
        compiler_params=_cparams(("arbitrary",)),
        name="retention",
    )(pc, pc, pc, pc, lgl, lgc, *s_args)


HGRN_DIRECT = 8
HGRN_BLOCKS = (128, 64, 32, 16)
HGRN_SAFE_EXP2 = 80.0


def _split2(x):
    hi = x.astype(BF16)
    return hi, (x - hi.astype(F32)).astype(BF16)


def _hgrn_kernel(q_ref, zf_ref, zb_ref, v_ref, g_ref, lb_ref, *rest, slen, t, has_s0):
    if has_s0:
        s0_ref, o_ref, o_scr, st_scr, b_scr, k_scr = rest
    else:
        o_ref, sfin_ref, o_scr, st_scr, b_scr, k_scr = rest
    nc = slen // t
    blocks = tuple(w for w in HGRN_BLOCKS if w <= t)
    masks = _head_masks(t)
    bmask = _block_ones()
    ones_bf16 = jnp.where(bmask, 1.0, 0.0).astype(BF16)
    r_i = lax.broadcasted_iota(jnp.int32, (t, t), 0)
    c_i = lax.broadcasted_iota(jnp.int32, (t, t), 1)
    sub = lax.broadcasted_iota(jnp.int32, (t, GROUP_W), 0) & (HGRN_DIRECT - 1)
    levels = []
    s = t // 2
    while s >= HGRN_DIRECT:
        levels.append(s)
        s //= 2
    r_w = lax.broadcasted_iota(jnp.int32, (t, N_HEADS * t), 0)
    c_w = lax.broadcasted_iota(jnp.int32, (t, N_HEADS * t), 1) & (t - 1)
    lvl_masks = []
    for s in levels:
        shift = (2 * s).bit_length() - 1
        lvl_masks.append(jnp.where((r_w >> shift) == (c_w >> shift), 1.0, 0.0))

    def block_roll(x, shift):
        x3 = x.reshape(t // HGRN_DIRECT, HGRN_DIRECT, GROUP_W)
        return pltpu.roll(x3, shift, 1).reshape(t, GROUP_W)

    def direction_fns(direction, z_ref):
        fwd = direction == 0
        lbd = jnp.maximum(lb_ref[direction], LB_FLOOR)
        tri = jnp.where((r_i >= c_i) if fwd else (r_i <= c_i), 1.0, 0.0).astype(BF16)
        near_masks = {}
        for w in blocks:
            blk_shift = w.bit_length() - 1
            near_masks[w] = jnp.where(((r_w >> blk_shift) == (c_w >> blk_shift))
                                      & ((c_w <= r_w) if fwd else (c_w >= r_w)), 1.0, 0.0)
        if has_s0:
            st_scr[direction] = _expand_state(s0_ref[0, direction], bmask)
        else:
            st_scr[direction] = jnp.zeros((GROUP_W, GROUP_W), F32)

        def gates(c, worst):
            rows = pl.ds(pl.multiple_of(c * t, t), t)
            z = z_ref[0, rows, :]
            a = jnp.exp(-jnp.abs(z))
            inv = 1.0 / (1.0 + a)
            pos = z >= 0
            sig = jnp.where(pos, inv, a * inv)
            nsig = jnp.where(pos, a * inv, inv)
            lf = jnp.log(lbd + (1.0 - lbd) * sig)
            k_scr[direction, rows, :] = (1.0 - lbd) * nsig
            h1, h2 = _split2(lf)
            b = (jnp.dot(tri, h1, preferred_element_type=F32) + jnp.dot(tri, h2, preferred_element_type=F32)) * LOG2E
            b_scr[direction, rows, :] = b
            worst = list(worst)
            for wi, w in enumerate(blocks):
                for gi in range(t // w):
                    first = b[gi * w:gi * w + 1]
                    last = b[(gi + 1) * w - 1:(gi + 1) * w]
                    worst[wi] = jnp.maximum(worst[wi], jnp.abs(first - last))
            return tuple(worst)

        def chunk(c, near_w):
            rows = pl.ds(pl.multiple_of(c * t, t), t)
            q = _silu(q_ref[0, rows, :])
            v = v_ref[0, rows, :]
            vb = v.astype(BF16)
            k = k_scr[direction, rows, :]
            b = b_scr[direction, rows, :]
            vs = _stack_heads(v, masks).astype(BF16)

            def level_scores(s, lm):
                pa, pb = [], []
                zero = jnp.zeros((s, GROUP_W), F32)
                for gi in range(t // (2 * s)):
                    b_lo = b[gi * 2 * s:gi * 2 * s + s]
                    b_hi = b[gi * 2 * s + s:(gi + 1) * 2 * s]
                    if fwd:
                        anc = b_hi[0:1]
                        pa += [zero, jnp.exp2(b_hi - anc)]
                        pb += [jnp.exp2(anc - b_lo), zero]
                    else:
                        anc = b_lo[s - 1:s]
                        pa += [jnp.exp2(b_lo - anc), zero]
                        pb += [zero, jnp.exp2(anc - b_hi)]
                qa = (q * jnp.concatenate(pa, axis=0)).astype(BF16)
                kb = _stack_heads(k * jnp.concatenate(pb, axis=0), masks).astype(BF16)
                return lax.dot_general(qa, kb, NT_DIMS, preferred_element_type=F32) * lm

            def direct_pairs():
                prods, vals = [(q * k).astype(BF16)], [v]
                for dlt in range(1, HGRN_DIRECT):
                    sh = dlt if fwd else HGRN_DIRECT - dlt
                    ok = (sub >= dlt) if fwd else (sub + dlt <= HGRN_DIRECT - 1)
                    w = jnp.exp2(jnp.minimum(b - block_roll(b, sh), 0.0))
                    prods.append(jnp.where(ok, q * block_roll(k, sh) * w, 0.0).astype(BF16))
                    vals.append(block_roll(v, sh))
                hs = jnp.dot(jnp.concatenate(prods, axis=0), ones_bf16, preferred_element_type=F32)
                out = hs[0:t] * vals[0]
                for dlt in range(1, HGRN_DIRECT):
                    out = out + hs[dlt * t:(dlt + 1) * t] * vals[dlt]
                return out

            sc = None
            for s, lm in zip(levels, lvl_masks):
                if s >= max(near_w, HGRN_DIRECT):
                    term = level_scores(s, lm)
                    sc = term if sc is None else sc + term

            if near_w:
                ea, eb = [], []
                for gi in range(t // near_w):
                    blk = b[gi * near_w:(gi + 1) * near_w]
                    anc = blk[0:1] if fwd else blk[near_w - 1:near_w]
                    ea.append(jnp.exp2(blk - anc))
                    eb.append(jnp.exp2(anc - blk))
                qa = (q * jnp.concatenate(ea, axis=0)).astype(BF16)
                kb = _stack_heads(k * jnp.concatenate(eb, axis=0), masks).astype(BF16)
                term = lax.dot_general(qa, kb, NT_DIMS, preferred_element_type=F32) * near_masks[near_w]
                sc = term if sc is None else sc + term
                o = jnp.dot(sc.astype(BF16), vs, preferred_element_type=F32)
            else:
                o = jnp.dot(sc.astype(BF16), vs, preferred_element_type=F32) + direct_pairs()

            st = st_scr[direction]
            o = o + lax.dot_general((q * jnp.exp2(b)).astype(BF16), st.astype(BF16), NT_DIMS,
                                    preferred_element_type=F32)
            bl = b[t - 1:t] if fwd else b[0:1]
            upd = lax.dot_general(vb, (k * jnp.exp2(bl - b)).astype(BF16), TN_DIMS,
                                  preferred_element_type=F32)
            st_scr[direction] = st * jnp.exp2(bl) + jnp.where(bmask, upd, 0.0)
            o_scr[direction, rows, :] = o

        return gates, chunk

    gates_f, chunk_f = direction_fns(0, zf_ref)
    gates_b, chunk_b = direction_fns(1, zb_ref)

    worst = lax.fori_loop(0, nc, lambda c, w: gates_b(c, gates_f(c, w)),
                          tuple(jnp.zeros((1, GROUP_W), F32) for _ in blocks), unroll=2)
    safe = [jnp.max(w) < HGRN_SAFE_EXP2 for w in worst]

    def sweep(ci, carry, *, near_w):
        chunk_f(ci, near_w)
        chunk_b(nc - 1 - ci, near_w)
        return carry

    taken = False
    for w, ok in zip(blocks, safe):
        cond = ok if taken is False else jnp.logical_and(jnp.logical_not(taken), ok)

        @pl.when(cond)
        def _(w=w):
            lax.fori_loop(0, nc, functools.partial(sweep, near_w=w), 0, unroll=4)

        taken = ok if taken is False else jnp.logical_or(taken, ok)

    @pl.when(jnp.logical_not(taken))
    def _():
        lax.fori_loop(0, nc, functools.partial(sweep, near_w=0), 0)

    if not has_s0:
        for direction in range(2):
            sfin_ref[0, direction] = _compact_state(st_scr[direction], bmask)

    def finish(c, carry):
        rows = pl.ds(pl.multiple_of(c * t, t), t)
        o = o_scr[0, rows, :] + o_scr[1, rows, :]
        ms = _head_sum(o * o, ones_bf16) * (1.0 / HEAD_DIM)
        y = o * lax.rsqrt(ms + EPS)
        o_ref[0, rows, :] = (y * _silu(g_ref[0, rows, :])).astype(o_ref.dtype)
        return carry

    lax.fori_loop(0, nc, finish, 0, unroll=2)


HGRN_CHUNK = 128


def _hgrn(pd, lb, s0=None, t=HGRN_CHUNK):
    bsz, slen, _ = pd.shape
    col = lambda j: pl.BlockSpec((1, slen, GROUP_W), lambda b: (b, 0, j))
    s_in_specs, s_args, s_out_specs, s_out_shape = _state_specs(bsz, s0)
    return pl.pallas_call(
        functools.partial(_hgrn_kernel, slen=slen, t=t, has_s0=s0 is not None),
        grid=(bsz,),
        in_specs=[col(0), col(1), col(2), col(3), col(4),
                  pl.BlockSpec((2, 1, GROUP_W), lambda b: (0, 0, 0))] + s_in_specs,
        out_specs=[pl.BlockSpec((1, slen, GROUP_W), lambda b: (b, 0, 0))] + s_out_specs,
        out_shape=[jax.ShapeDtypeStruct((bsz, slen, GROUP_W), BF16)] + s_out_shape,
        scratch_shapes=[pltpu.VMEM((2, slen, GROUP_W), F32), pltpu.VMEM((2, GROUP_W, GROUP_W), F32),
                        pltpu.VMEM((2, slen, GROUP_W), F32), pltpu.VMEM((2, slen, GROUP_W), F32)],
        compiler_params=_cparams(("arbitrary",)),
        name="hgrn2",
    )(pd, pd, pd, pd, pd, lb.reshape(2, 1, GROUP_W), *s_args)


def _outproj_kernel(ma_ref, mb_ref, mc_ref, md_ref, w_ref, x_ref, gate_ref, g_ref, o_ref, *, layer, mod_row):
    g = GROUP_W
    y = jnp.dot(ma_ref[0], w_ref[0, 0:g, :], preferred_element_type=F32)
    y = y + jnp.dot(mb_ref[0], w_ref[0, g:2 * g, :], preferred_element_type=F32)
    y = y + jnp.dot(mc_ref[0], w_ref[0, 2 * g:3 * g, :], preferred_element_type=F32)
    y = y + jnp.dot(md_ref[0], w_ref[0, 3 * g:4 * g, :], preferred_element_type=F32)
    ms = jnp.mean(y * y, axis=-1, keepdims=True)
    r = y * lax.rsqrt(ms + EPS) * g_ref[layer:layer + 1, :]
    o_ref[0] = x_ref[0] + _mod_row(gate_ref, mod_row) * r


OUTPROJ_TM = 1024


def _out_proj(mixed, w_out_bf16, x, mods, layer, mod_row, g_post):
    bsz, slen, d = x.shape
    depth = g_post.shape[0]
    tm = min(OUTPROJ_TM, slen)
    mspec = pl.BlockSpec((1, tm, GROUP_W), lambda b, i: (b, i, 0))
    return pl.pallas_call(
        functools.partial(_outproj_kernel, layer=layer, mod_row=mod_row),
        grid=(bsz, slen // tm),
        in_specs=[mspec, mspec, mspec, mspec,
                  pl.BlockSpec((1, d, d), lambda b, i: (layer, 0, 0)),
                  pl.BlockSpec((1, tm, d), lambda b, i: (b, i, 0)),
                  _mod_spec(2, layer),
                  pl.BlockSpec((depth, d), lambda b, i: (0, 0))],
        out_specs=pl.BlockSpec((1, tm, d), lambda b, i: (b, i, 0)),
        out_shape=jax.ShapeDtypeStruct((bsz, slen, d), F32),
        compiler_params=_cparams(("arbitrary", "arbitrary")),
        name="out_proj",
    )(*mixed, w_out_bf16, x, mods, g_post)


def _context_layer(x, mods, layer, mod_row, g_pre, g_post, w_in, w_out, sink, log_gamma, lb):
    bsz, slen, _ = x.shape
    pa, pb, pc, pd, ak, av, bk, bv = _in_proj(x, mods, layer, mod_row, g_pre, w_in, rope=False, emit_kv=True)
    o_a, = _attention(_attn_operands(pa, 0, 2, 3, 2, n_kv=A_KV, qb=slen, kw=slen, back=0, sink=sink))
    o_b, = _attention(_attn_operands(pb, 0, 1, 2, 3, n_kv=N_HEADS, qb=slen, kw=slen, back=0))
    o_c, s_c = _retention(pc, log_gamma)
    o_d, s_d = _hgrn(pd, lb)
    x = _out_proj((o_a, o_b, o_c, o_d), w_out, x, mods, layer, mod_row, g_post)
    kv4 = lambda t: t.reshape(bsz, slen, -1, HEAD_DIM)
    return x, (kv4(ak), kv4(av), kv4(bk), kv4(bv), _uncompact_states(s_c), _uncompact_states(s_d))


def _latent_layer(x, mods, layer, ca_k, ca_v, cb_k, cb_v, st_c, st_d, g_pre, g_post, w_in, w_out, sink, win_tab,
                  na_tabs, log_gamma, lb):
    pa, pb, pc, pd = _in_proj(x, mods, layer, None, g_pre, w_in, rope=True)
    o_a, o_b = _attention(
        _attn_operands(pa, 0, 2, 3, 2, n_kv=A_KV, qb=WIN_QB, kw=WIN_KW, back=WIN_BACK,
                       ctx_k=ca_k, ctx_v=ca_v, table=win_tab, sink=sink, layer=layer),
        _attn_operands(pb, 0, 1, 2, 3, n_kv=N_HEADS, qb=NA_QB, kw=NA_KW, back=NA_QB,
                       ctx_k=cb_k, ctx_v=cb_v, table=na_tabs, layer=layer))
    o_c, = _retention(pc, log_gamma, st_c)
    o_d, = _hgrn(pd, lb, st_d)
    return _out_proj((o_a, o_b, o_c, o_d), w_out, x, mods, layer, None, g_post)


def kernel(x_prompt, x_sample, c, cache_win_k, cache_win_v, cache_na_k, cache_na_v, state_ret, state_hgrn,
           c_ctx, w_ada, b_ada, g_pre, g_post, w_in, w_out, attn_sink, na_rpb, ret_decay_logit, hgrn_lb_logit):
    depth = w_ada.shape[0]
    dec_b, dec_s, d = x_sample.shape
    p_lb = jax.nn.softmax(hgrn_lb_logit.astype(F32), axis=0)
    lower_bounds = jnp.cumsum(p_lb, axis=0) - p_lb[0:1]
    log_gammas = jax.nn.log_sigmoid(ret_decay_logit.astype(F32))
    w_in_b = w_in.astype(BF16)
    w_out_b = w_out.astype(BF16)

    cvecs = jnp.zeros((ADA_ROWS, d), F32).at[:dec_b].set(c).at[dec_b].set(c_ctx)
    mods = _adaln(cvecs, w_ada, b_ada)
    win_tab = _window_table(dec_s)
    na_tabs = _na_tables(na_rpb, dec_s)

    x = x_prompt
    outs = [[] for _ in range(6)]
    for l in range(depth):
        x, extra = _context_layer(x, mods, l, dec_b, g_pre, g_post, w_in_b, w_out_b,
                                  attn_sink[l], log_gammas[l], lower_bounds[l])
        for acc, e in zip(outs, extra):
            acc.append(e)
    y_prompt = x
    stacked = [jnp.stack(o, axis=1) for o in outs]

    past = cache_win_k.shape[2]
    ca_k = cache_win_k.reshape(dec_b, depth, past, A_KV * HEAD_DIM)
    ca_v = cache_win_v.reshape(dec_b, depth, past, A_KV * HEAD_DIM)
    cb_k = cache_na_k.reshape(dec_b, depth, past, GROUP_W)
    cb_v = cache_na_v.reshape(dec_b, depth, past, GROUP_W)
    st_c = _compact_states(state_ret)
    st_d = _compact_states(state_hgrn)
    x = x_sample
    for l in range(depth):
        x = _latent_layer(x, mods, l, ca_k, ca_v, cb_k, cb_v, st_c[:, l], st_d[:, l], g_pre, g_post,
                          w_in_b, w_out_b, attn_sink[l], win_tab, na_tabs, log_gammas[l], lower_bounds[l])
    return (y_prompt, x, *stacked)
```

```python
import functools

import numpy as np
import jax
import jax.numpy as jnp
from jax import lax
from jax.experimental import pallas as pl
from jax.experimental.pallas import tpu as pltpu

F32 = jnp.float32
BF16 = jnp.bfloat16

D_MODEL = 1024
DEPTH = 4
GRID_W = 64
HEAD_DIM = 64
N_HEADS = 4
GROUP_W = N_HEADS * HEAD_DIM
A_KV = 2
WINDOW = 128
NA_ROWS = 8
NA_COLS = 16
ROPE_BASE = 10000.0
EPS = 1e-6
NEG = -1e30
LB_FLOOR = 1e-30
LOG2E = 1.4426950408889634
W_A = 3 * GROUP_W
W_B = 4 * GROUP_W
W_C = 4 * GROUP_W
W_D = 5 * GROUP_W
IN_WIDTH = W_A + W_B + W_C + W_D

V7X_VMEM_LIMIT_BYTES = 56 * 1024 * 1024
ADA_ROWS = 16

NT_DIMS = (((1,), (1,)), ((), ()))
TN_DIMS = (((0,), (0,)), ((), ()))


def _cparams(sem, flags=None):
    return pltpu.CompilerParams(dimension_semantics=sem, vmem_limit_bytes=V7X_VMEM_LIMIT_BYTES, flags=flags)


def _sigmoid(x):
    return 1.0 / (1.0 + jnp.exp(-x))


def _silu(x):
    return x * _sigmoid(x)


def _head_masks(rows):
    lane = lax.broadcasted_iota(jnp.int32, (rows, GROUP_W), 1)
    return [(lane >= h * HEAD_DIM) & (lane < (h + 1) * HEAD_DIM) for h in range(N_HEADS)]


def _stack_heads(x, masks):
    return jnp.concatenate([jnp.where(m, x, 0.0) for m in masks], axis=0)


def _block_ones():
    shift = HEAD_DIM.bit_length() - 1
    r = lax.broadcasted_iota(jnp.int32, (GROUP_W, GROUP_W), 0) >> shift
    c = lax.broadcasted_iota(jnp.int32, (GROUP_W, GROUP_W), 1) >> shift
    return r == c


def _head_sum(x, ones_bf16):
    hi = x.astype(BF16)
    lo = (x - hi.astype(F32)).astype(BF16)
    return (jnp.dot(hi, ones_bf16, preferred_element_type=F32)
            + jnp.dot(lo, ones_bf16, preferred_element_type=F32))


def _adaln_kernel(c_ref, w_ref, b_ref, o_ref):
    s = _silu(c_ref[...]).astype(BF16)
    o_ref[0, 0] = jnp.dot(s, w_ref[0].astype(BF16), preferred_element_type=F32) + b_ref[0]


def _adaln(cvecs, w_ada, b_ada):
    depth, d, d3 = w_ada.shape
    tn = 512
    per = d // tn
    return pl.pallas_call(
        _adaln_kernel,
        grid=(depth, d3 // tn),
        in_specs=[pl.BlockSpec((ADA_ROWS, d), lambda l, j: (0, 0)),
                  pl.BlockSpec((1, d, tn), lambda l, j: (l, 0, j)),
                  pl.BlockSpec((1, 1, tn), lambda l, j: (l, 0, j))],
        out_specs=pl.BlockSpec((1, 1, ADA_ROWS, tn), lambda l, j: (l, j // per, 0, j % per)),
        out_shape=jax.ShapeDtypeStruct((depth, 3, ADA_ROWS, d), F32),
        compiler_params=_cparams(("arbitrary", "arbitrary")),
        name="adaln",
    )(cvecs, w_ada, b_ada.reshape(depth, 1, d3))


def _rope(x, cos, sin):
    w = x.shape[-1]
    lane = lax.broadcasted_iota(jnp.int32, x.shape, 1)
    first = (lane & 31) < 16
    swapped = jnp.where(first, pltpu.roll(x, w - 16, 1), pltpu.roll(x, 16, 1))
    return x * cos[:, :w] + swapped * sin[:, :w]


def _mod_row(mod_ref, mod_row):
    row = pl.program_id(0) if mod_row is None else mod_row
    return mod_ref[0, 0, pl.ds(row, 1), :]


def _inproj_kernel(*refs, rope, emit_kv, layer, mod_row):
    refs = list(refs)
    x_ref, shift_ref, scale_ref, g_ref, w_ref = refs[:5]
    if rope:
        cos_ref, sin_ref = refs[5:7]
    n_out = 8 if emit_kv else 4
    pa_ref, pb_ref, pc_ref, pd_ref = refs[-n_out:][:4]
    x = x_ref[0]
    ms = jnp.mean(x * x, axis=-1, keepdims=True)
    y = x * lax.rsqrt(ms + EPS) * g_ref[layer:layer + 1, :]
    h = (y * (1.0 + _mod_row(scale_ref, mod_row)) + _mod_row(shift_ref, mod_row)).astype(BF16)

    def mm(c0, c1):
        return jnp.dot(h, w_ref[0, :, c0:c1], preferred_element_type=F32)

    if rope:
        cos = cos_ref[...]
        sin = sin_ref[...]
    g = GROUP_W
    aq = mm(0, g)
    akv = mm(g, 2 * g)
    if rope:
        aq = _rope(aq, cos, sin)
        ak = _rope(akv[:, :g // 2], cos, sin)
        akv = jnp.concatenate([ak, akv[:, g // 2:]], axis=1)
    pa_ref[0, :, 0:g] = aq.astype(pa_ref.dtype)
    pa_ref[0, :, g:2 * g] = akv.astype(pa_ref.dtype)
    pa_ref[0, :, 2 * g:3 * g] = mm(2 * g, 3 * g).astype(pa_ref.dtype)
    pbv = mm(W_A, W_A + W_B)
    pb_ref[0] = pbv.astype(pb_ref.dtype)
    if emit_kv:
        ak_ref, av_ref, bk_ref, bv_ref = refs[-4:]
        ak_ref[0] = akv[:, :g // 2]
        av_ref[0] = akv[:, g // 2:]
        bk_ref[0] = pbv[:, g:2 * g]
        bv_ref[0] = pbv[:, 2 * g:3 * g]
    c0 = W_A + W_B
    cq = mm(c0, c0 + g)
    ck = mm(c0 + g, c0 + 2 * g)
    if rope:
        cq = _rope(cq, cos, sin)
        ck = _rope(ck, cos, sin)
    pc_ref[0, :, 0:g] = cq.astype(pc_ref.dtype)
    pc_ref[0, :, g:2 * g] = ck.astype(pc_ref.dtype)
    pc_ref[0, :, 2 * g:4 * g] = mm(c0 + 2 * g, c0 + 4 * g).astype(pc_ref.dtype)
    d0 = c0 + W_C
    pd_ref[0] = mm(d0, d0 + W_D).astype(pd_ref.dtype)


def _rope_tables(slen):
    t = np.arange(slen)
    nf = HEAD_DIM // 4
    freqs = ROPE_BASE ** (-np.arange(nf, dtype=np.float64) / nf)
    d = np.arange(HEAD_DIM)
    pos = np.where(d[None, :] < HEAD_DIM // 2, (t // GRID_W)[:, None], (t % GRID_W)[:, None])
    ang = pos * freqs[d % nf][None, :]
    sign = np.where((d % (2 * nf)) < nf, -1.0, 1.0)[None, :]
    cos = np.tile(np.cos(ang), (1, N_HEADS))
    sin = np.tile(np.sin(ang) * sign, (1, N_HEADS))
    return jnp.asarray(cos, F32), jnp.asarray(sin, F32)


def _mod_spec(which, layer):
    return pl.BlockSpec((1, 1, ADA_ROWS, D_MODEL), lambda b, i: (layer, which, 0, 0))


INPROJ_TM = 512


def _in_proj(x, mods, layer, mod_row, g_pre, w_in_bf16, rope, emit_kv=False):
    bsz, slen, d = x.shape
    tm = min(INPROJ_TM, slen)
    depth = g_pre.shape[0]
    in_specs = [pl.BlockSpec((1, tm, d), lambda b, i: (b, i, 0)),
                _mod_spec(0, layer),
                _mod_spec(1, layer),
                pl.BlockSpec((depth, d), lambda b, i: (0, 0)),
                pl.BlockSpec((1, d, IN_WIDTH), lambda b, i: (layer, 0, 0))]
    args = [x, mods, mods, g_pre, w_in_bf16]
    if rope:
        cos, sin = _rope_tables(slen)
        in_specs += [pl.BlockSpec((tm, GROUP_W), lambda b, i: (i, 0))] * 2
        args += [cos, sin]
    widths = (W_A, W_B, W_C, W_D)
    if emit_kv:
        widths += (A_KV * HEAD_DIM, A_KV * HEAD_DIM, GROUP_W, GROUP_W)
    return pl.pallas_call(
        functools.partial(_inproj_kernel, rope=rope, emit_kv=emit_kv, layer=layer, mod_row=mod_row),
        grid=(bsz, slen // tm),
        in_specs=in_specs,
        out_specs=[pl.BlockSpec((1, tm, w), lambda b, i: (b, i, 0)) for w in widths],
        out_shape=[jax.ShapeDtypeStruct((bsz, slen, w), F32) for w in widths],
        compiler_params=_cparams(("arbitrary", "arbitrary")),
        name="in_proj",
    )(*args)


ATTN_ROWS = 256


def _attn_kernel(*refs, cfgs):
    n_in = [4 + 2 * c["has_ctx"] + (c["n_tab"] > 0) + c["has_sink"] for c in cfgs]
    outs = refs[sum(n_in):]
    start = 0
    for c, n, o_ref in zip(cfgs, n_in, outs):
        _attn_body(refs[start:start + n], o_ref, **c)
        start += n


def _attn_body(in_refs, o_ref, *, n_kv, qb, kw, back, slen, has_ctx, n_tab, tab_heads, has_sink):
    it = iter(in_refs)
    q_ref, k_ref, v_ref, g_ref = next(it), next(it), next(it), next(it)
    kc_ref = vc_ref = tab_ref = sink_ref = None
    if has_ctx:
        kc_ref, vc_ref = next(it), next(it)
    if n_tab:
        tab_ref = next(it)
    if has_sink:
        sink_ref = next(it)

    n = pl.program_id(1)
    nblk = slen // qb
    ws = pl.multiple_of(jnp.clip(n * qb - back, 0, slen - kw), HEAD_DIM)
    if n_tab == 3:
        tix = jnp.where(n == 0, 0, jnp.where(n == nblk - 1, 2, 1))
    else:
        tix = 0
    grp = N_HEADS // n_kv
    qscale = HEAD_DIM ** -0.5 * LOG2E
    pair_w = 2 * HEAD_DIM
    lane = lax.broadcasted_iota(jnp.int32, (1, pair_w), 1)
    half_mask = (lane < HEAD_DIM, lane >= HEAD_DIM)
    rb = min(qb, ATTN_ROWS)
    for j in range(N_HEADS // 2):
        cols = slice(j * pair_w, (j + 1) * pair_w)
        if grp == 1:
            kv_cols, kv_half = cols, (0, 1)
        else:
            kv_cols, kv_half = slice(0, pair_w), (j, j)
        kall = k_ref[0, pl.ds(ws, kw), kv_cols].astype(BF16)
        vall = v_ref[0, pl.ds(ws, kw), kv_cols]
        if has_ctx:
            kall = jnp.concatenate([kall, kc_ref[0, 0, :, kv_cols].astype(BF16)], axis=0)
            vall = jnp.concatenate([vall, vc_ref[0, 0, :, kv_cols]], axis=0)
        vaug = {hf: jnp.where(half_mask[hf], vall, 1.0).astype(BF16) for hf in set(kv_half)}
        for r0 in range(0, qb, rb):
            rs = slice(r0, r0 + rb)
            q128 = q_ref[0, rs, cols] * qscale
            q_other = pltpu.roll(q128, HEAD_DIM, 1) if kv_half != (0, 1) else None
            outs = []
            for i in range(2):
                hf = kv_half[i]
                qm = jnp.where(half_mask[hf], q128 if i == hf else q_other, 0.0).astype(BF16)
                s = lax.dot_general(qm, kall, NT_DIMS, preferred_element_type=F32)
                if n_tab:
                    tab = tab_ref[0, tix, 2 * j + i if tab_heads == N_HEADS else 0, rs, :]
                    s = jnp.concatenate([s[:, :kw] + tab, s[:, kw:]], axis=1) if has_ctx else s + tab
                m = jnp.max(s, axis=-1, keepdims=True)
                if has_sink:
                    sink = sink_ref[2 * j + i] * LOG2E
                    m = jnp.maximum(m, sink)
                p = jnp.exp2(s - m).astype(BF16)
                acc = jnp.dot(p, vaug[hf], preferred_element_type=F32)
                if has_sink:
                    acc = acc + jnp.where(half_mask[1 - hf], jnp.exp2(sink - m), 0.0)
                acc = acc / pltpu.roll(acc, HEAD_DIM, 1)
                outs.append(acc if i == hf else pltpu.roll(acc, HEAD_DIM, 1))
            out = jnp.where(half_mask[0], outs[0], outs[1]) * _silu(g_ref[0, rs, cols])
            o_ref[0, rs, cols] = out.astype(o_ref.dtype)


def _attn_operands(p, q_col, k_col, v_col, g_col, n_kv, qb, kw, back, ctx_k=None, ctx_v=None,
                   table=None, sink=None, layer=0):
    slen = p.shape[1]
    kvw = n_kv * HEAD_DIM
    in_specs = [pl.BlockSpec((1, qb, GROUP_W), lambda b, n: (b, n, q_col)),
                pl.BlockSpec((1, slen, kvw), lambda b, n: (b, 0, k_col)),
                pl.BlockSpec((1, slen, kvw), lambda b, n: (b, 0, v_col)),
                pl.BlockSpec((1, qb, GROUP_W), lambda b, n: (b, n, g_col))]
    args = [p, p, p, p]
    has_ctx = ctx_k is not None
    if has_ctx:
        past = ctx_k.shape[2]
        in_specs += [pl.BlockSpec((1, 1, past, kvw), lambda b, n: (b, layer, 0, 0))] * 2
        args += [ctx_k, ctx_v]
    n_tab = tab_heads = 0
    if table is not None:
        n_tab, tab_heads = table.shape[1], table.shape[2]
        tab_layer = layer if table.shape[0] > 1 else 0
        in_specs.append(pl.BlockSpec((1,) + table.shape[1:], lambda b, n: (tab_layer, 0, 0, 0, 0)))
        args.append(table)
    if sink is not None:
        in_specs.append(pl.BlockSpec(memory_space=pltpu.SMEM))
        args.append(sink)
    cfg = dict(n_kv=n_kv, qb=qb, kw=kw, back=back, slen=slen, has_ctx=has_ctx, n_tab=n_tab,
               tab_heads=tab_heads, has_sink=sink is not None)
    return in_specs, args, cfg


def _attention(*operands):
    in_specs = [s for op in operands for s in op[0]]
    args = [a for op in operands for a in op[1]]
    cfgs = tuple(op[2] for op in operands)
    bsz = args[0].shape[0]
    slen, qb = cfgs[0]["slen"], cfgs[0]["qb"]
    assert all(c["slen"] == slen and c["qb"] == qb for c in cfgs)
    return pl.pallas_call(
        functools.partial(_attn_kernel, cfgs=cfgs),
        grid=(bsz, slen // qb),
        in_specs=in_specs,
        out_specs=[pl.BlockSpec((1, qb, GROUP_W), lambda b, n: (b, n, 0)) for _ in cfgs],
        out_shape=[jax.ShapeDtypeStruct((bsz, slen, GROUP_W), BF16) for _ in cfgs],
        compiler_params=_cparams(("arbitrary", "arbitrary")),
        name="attn",
    )(*args)


WIN_QB = 256
WIN_BACK = WINDOW
WIN_KW = WIN_QB + 2 * WINDOW
NA_QROWS = 4
NA_KROWS = 12
NA_QB = NA_QROWS * GRID_W
NA_KW = NA_KROWS * GRID_W


def _window_table(slen):
    nblk = slen // WIN_QB
    tabs = []
    for n in (0, 1, nblk - 1):
        ws = int(np.clip(n * WIN_QB - WIN_BACK, 0, slen - WIN_KW))
        qpos = n * WIN_QB + np.arange(WIN_QB)[:, None]
        kpos = ws + np.arange(WIN_KW)[None, :]
        tabs.append(np.where(np.abs(qpos - kpos) <= WINDOW, 0.0, NEG))
    return jnp.asarray(np.stack(tabs)[None, :, None], F32)


N_RPB_R = 2 * NA_ROWS - 1
N_RPB_C = 2 * NA_COLS - 1


def _na_table_kernel(rpb_ref, o_ref, tz_scr, *, rows):
    base = (pl.program_id(0) * N_HEADS + pl.program_id(1)) * (N_RPB_R * N_RPB_C)
    qc = lax.broadcasted_iota(jnp.int32, (GRID_W, 2 * GRID_W), 0)
    kk = lax.broadcasted_iota(jnp.int32, (GRID_W, 2 * GRID_W), 1)
    kc = kk & (GRID_W - 1)
    diff = kc - qc
    qws = jnp.clip(qc - NA_COLS // 2, 0, GRID_W - NA_COLS)
    col_ok = (kc >= qws) & (kc < qws + NA_COLS)
    neg = jnp.full((GRID_W, 2 * GRID_W), NEG, F32)
    for dr in range(N_RPB_R):
        acc = neg
        for m in range(N_RPB_C):
            acc = jnp.where(diff == m - (NA_COLS - 1), rpb_ref[base + dr * N_RPB_C + m] * LOG2E, acc)
        tz_scr[dr] = jnp.where(col_ok, acc, NEG)
    wr = min(NA_ROWS, rows)
    nblk = rows // NA_QROWS
    for ti, g in enumerate((0, 1, nblk - 1)):
        ws_row = min(max(g * NA_QROWS - NA_QROWS, 0), rows - NA_KROWS)
        for qr in range(NA_QROWS):
            r = g * NA_QROWS + qr
            rs = min(max(r - wr // 2, 0), rows - wr)
            for p in range(NA_KROWS // 2):
                halves = []
                for kr in (ws_row + 2 * p, ws_row + 2 * p + 1):
                    halves.append(tz_scr[kr - r + NA_ROWS - 1] if rs <= kr < rs + wr else neg)
                o_ref[0, ti, 0, qr * GRID_W:(qr + 1) * GRID_W, p * 2 * GRID_W:(p + 1) * 2 * GRID_W] = (
                    jnp.where(kk < GRID_W, halves[0], halves[1]))


def _na_tables(na_rpb, slen):
    depth = na_rpb.shape[0]
    return pl.pallas_call(
        functools.partial(_na_table_kernel, rows=slen // GRID_W),
        grid=(depth, N_HEADS),
        in_specs=[pl.BlockSpec(memory_space=pltpu.SMEM)],
        out_specs=pl.BlockSpec((1, 3, 1, NA_QB, NA_KW), lambda l, h: (l, 0, h, 0, 0)),
        out_shape=jax.ShapeDtypeStruct((depth, 3, N_HEADS, NA_QB, NA_KW), F32),
        scratch_shapes=[pltpu.VMEM((N_RPB_R, GRID_W, 2 * GRID_W), F32)],
        compiler_params=_cparams(("arbitrary", "arbitrary")),
        name="na_table",
    )(na_rpb.astype(F32).reshape(-1))


def _ret_kernel(q_ref, k_ref, v_ref, g_ref, lgl_ref, lgc_ref, *rest, slen, t, has_s0):
    if has_s0:
        s0_ref, o_ref, sb_scr, st_scr, dm_scr, dec_scr = rest
    else:
        o_ref, sfin_ref, sb_scr, st_scr, dm_scr, dec_scr = rest
    nc = slen // t
    masks = _head_masks(t)
    bmask = _block_ones()
    ones_bf16 = jnp.where(bmask, 1.0, 0.0).astype(BF16)
    lgf, lgb = lgl_ref[0], lgl_ref[1]

    @pl.when(pl.program_id(0) == 0)
    def _():
        ii = lax.broadcasted_iota(jnp.int32, (t, N_HEADS * t), 0)
        jj = lax.broadcasted_iota(jnp.int32, (t, N_HEADS * t), 1) & (t - 1)
        dist = (ii - jj).astype(F32)
        dm_scr[...] = (jnp.where(dist >= 0, jnp.exp(dist * lgc_ref[0]), 0.0)
                       + jnp.where(dist <= 0, jnp.exp(-dist * lgc_ref[1]), 0.0))
        idx = lax.broadcasted_iota(jnp.int32, (t, GROUP_W), 0).astype(F32)
        dec_scr[0] = jnp.exp((idx + 1.0) * lgf)
        dec_scr[1] = jnp.exp((t - 1.0 - idx) * lgf)
        dec_scr[2] = jnp.exp((t - idx) * lgb)
        dec_scr[3] = jnp.exp(idx * lgb)

    cdec_f = jnp.exp(float(t) * lgf)
    cdec_b = jnp.exp(float(t) * lgb)
    kscale = HEAD_DIM ** -0.5

    def state_update(st, k, v, kdec, cdec):
        upd = lax.dot_general(v, (k * kdec).astype(BF16), TN_DIMS, preferred_element_type=F32)
        return st * cdec + jnp.where(bmask, upd, 0.0)

    def init_state(direction):
        if has_s0:
            st_scr[...] = _expand_state(s0_ref[0, direction], bmask)
        else:
            st_scr[...] = jnp.zeros((GROUP_W, GROUP_W), F32)

    def emit_final(direction):
        if not has_s0:
            sfin_ref[0, direction] = _compact_state(st_scr[...], bmask)

    init_state(1)

    def sweep_bwd(ci, carry):
        c = nc - 1 - ci
        rows = pl.ds(pl.multiple_of(c * t, t), t)
        st = st_scr[...]
        sb_scr[c] = st.astype(BF16)
        k = k_ref[0, rows, :] * kscale
        st_scr[...] = state_update(st, k, v_ref[0, rows, :].astype(BF16), dec_scr[3], cdec_b)
        return carry

    lax.fori_loop(0, nc, sweep_bwd, 0, unroll=4)
    emit_final(1)
    init_state(0)

    def sweep_fwd(c, carry):
        rows = pl.ds(pl.multiple_of(c * t, t), t)
        q = q_ref[0, rows, :]
        k = k_ref[0, rows, :] * kscale
        v = v_ref[0, rows, :]
        ks = _stack_heads(k, masks).astype(BF16)
        vs = _stack_heads(v, masks).astype(BF16)
        sc = lax.dot_general(q.astype(BF16), ks, NT_DIMS, preferred_element_type=F32) * dm_scr[...]
        o = jnp.dot(sc.astype(BF16), vs, preferred_element_type=F32)
        st = st_scr[...]
        o = o + lax.dot_general((q * dec_scr[0]).astype(BF16), st.astype(BF16), NT_DIMS,
                                preferred_element_type=F32)
        o = o + lax.dot_general((q * dec_scr[2]).astype(BF16), sb_scr[c], NT_DIMS, preferred_element_type=F32)
        st_scr[...] = state_update(st, k, v.astype(BF16), dec_scr[1], cdec_f)
        mu = _head_sum(o, ones_bf16) * (1.0 / HEAD_DIM)
        d = o - mu
        var = _head_sum(d * d, ones_bf16) * (1.0 / HEAD_DIM)
        y = d * lax.rsqrt(var + EPS)
        o_ref[0, rows, :] = (y * _silu(g_ref[0, rows, :])).astype(o_ref.dtype)
        return carry

    lax.fori_loop(0, nc, sweep_fwd, 0, unroll=4)
    emit_final(0)


def _compact_states(s0):
    return jnp.swapaxes(s0, -1, -2).reshape(s0.shape[:-3] + (GROUP_W, HEAD_DIM))


def _uncompact_states(sc):
    bsz = sc.shape[0]
    return jnp.swapaxes(sc.reshape(bsz, 2, N_HEADS, HEAD_DIM, HEAD_DIM), -1, -2)


def _split3(x):
    h1 = x.astype(BF16)
    r1 = x - h1.astype(F32)
    h2 = r1.astype(BF16)
    return h1, h2, (r1 - h2.astype(F32)).astype(BF16)


def _expand_state(x, bmask):
    r = lax.broadcasted_iota(jnp.int32, (HEAD_DIM, GROUP_W), 0)
    c = lax.broadcasted_iota(jnp.int32, (HEAD_DIM, GROUP_W), 1) & (HEAD_DIM - 1)
    rep = jnp.where(r == c, 1.0, 0.0).astype(BF16)
    h1, h2, h3 = _split3(x)
    tiled = (jnp.dot(h1, rep, preferred_element_type=F32) + jnp.dot(h2, rep, preferred_element_type=F32)
             + jnp.dot(h3, rep, preferred_element_type=F32))
    return jnp.where(bmask, tiled, 0.0)


def _compact_state(st, bmask):
    r = lax.broadcasted_iota(jnp.int32, (GROUP_W, HEAD_DIM), 0) & (HEAD_DIM - 1)
    c = lax.broadcasted_iota(jnp.int32, (GROUP_W, HEAD_DIM), 1)
    fold = jnp.where(r == c, 1.0, 0.0).astype(BF16)
    h1, h2, h3 = _split3(jnp.where(bmask, st, 0.0))
    return (jnp.dot(h1, fold, preferred_element_type=F32) + jnp.dot(h2, fold, preferred_element_type=F32)
            + jnp.dot(h3, fold, preferred_element_type=F32))


def _state_specs(bsz, s0):
    spec = pl.BlockSpec((1, 2, GROUP_W, HEAD_DIM), lambda b: (b, 0, 0, 0))
    shape = jax.ShapeDtypeStruct((bsz, 2, GROUP_W, HEAD_DIM), F32)
    if s0 is None:
        return [], [], [spec], [shape]
    return [spec], [s0], [], []


RET_CHUNK = 256


def _retention(pc, log_gamma, s0=None, t=RET_CHUNK):
    bsz, slen, _ = pc.shape
    t = min(t, slen)
    lgl = jnp.repeat(log_gamma, HEAD_DIM, axis=1).reshape(2, 1, GROUP_W)
    lgc = jnp.repeat(log_gamma, t, axis=1).reshape(2, 1, N_HEADS * t)
    col = lambda j: pl.BlockSpec((1, slen, GROUP_W), lambda b: (b, 0, j))
    s_in_specs, s_args, s_out_specs, s_out_shape = _state_specs(bsz, s0)
    return pl.pallas_call(
        functools.partial(_ret_kernel, slen=slen, t=t, has_s0=s0 is not None),
        grid=(bsz,),
        in_specs=[col(0), col(1), col(2), col(3),
                  pl.BlockSpec((2, 1, GROUP_W), lambda b: (0, 0, 0)),
                  pl.BlockSpec((2, 1, N_HEADS * t), lambda b: (0, 0, 0))] + s_in_specs,
        out_specs=[pl.BlockSpec((1, slen, GROUP_W), lambda b: (b, 0, 0))] + s_out_specs,
        out_shape=[jax.ShapeDtypeStruct((bsz, slen, GROUP_W), BF16)] + s_out_shape,
        scratch_shapes=[pltpu.VMEM((slen // t, GROUP_W, GROUP_W), BF16), pltpu.VMEM((GROUP_W, GROUP_W), F32),
                        pltpu.VMEM((t, N_HEADS * t), F32), pltpu.VMEM((4, t, GROUP_W), F32)],
        compiler_params=_cparams(("arbitrary",)),
        name="retention",
    )(pc, pc, pc, pc, lgl, lgc, *s_args)


HGRN_DIRECT = 8
HGRN_BLOCKS = (128, 64, 32, 16)
HGRN_SAFE_EXP2 = 80.0


def _split2(x):
    hi = x.astype(BF16)
    return hi, (x - hi.astype(F32)).astype(BF16)


def _hgrn_blocks_levels(t):
    blocks = tuple(w for w in HGRN_BLOCKS if w <= t)
    levels = []
    s = t // 2
    while s >= HGRN_DIRECT:
        levels.append(s)
        s //= 2
    return blocks, levels


def _hgrn_kernel(q_ref, zf_ref, zb_ref, v_ref, g_ref, lb_ref, *rest, slen, t, has_s0):
    if has_s0:
        s0_ref, o_ref, o_scr, st_scr, b_scr, k_scr, mask_scr = rest
    else:
        o_ref, sfin_ref, o_scr, st_scr, b_scr, k_scr, mask_scr = rest
    nc = slen // t
    blocks, levels = _hgrn_blocks_levels(t)
    masks = _head_masks(t)
    bmask = _block_ones()
    ones_bf16 = jnp.where(bmask, 1.0, 0.0).astype(BF16)
    r_i = lax.broadcasted_iota(jnp.int32, (t, t), 0)
    c_i = lax.broadcasted_iota(jnp.int32, (t, t), 1)
    sub = lax.broadcasted_iota(jnp.int32, (t, GROUP_W), 0) & (HGRN_DIRECT - 1)

    @pl.when(pl.program_id(0) == 0)
    def _():
        r_w = lax.broadcasted_iota(jnp.int32, (t, N_HEADS * t), 0)
        c_w = lax.broadcasted_iota(jnp.int32, (t, N_HEADS * t), 1) & (t - 1)
        for li, s in enumerate(levels):
            shift = (2 * s).bit_length() - 1
            mask_scr[li] = jnp.where((r_w >> shift) == (c_w >> shift), 1.0, 0.0)
        for direction in range(2):
            for wi, w in enumerate(blocks):
                shift = w.bit_length() - 1
                causal = (c_w <= r_w) if direction == 0 else (c_w >= r_w)
                mask_scr[len(levels) + direction * len(blocks) + wi] = jnp.where(
                    ((r_w >> shift) == (c_w >> shift)) & causal, 1.0, 0.0)

    def block_roll(x, shift):
        x3 = x.reshape(t // HGRN_DIRECT, HGRN_DIRECT, GROUP_W)
        return pltpu.roll(x3, shift, 1).reshape(t, GROUP_W)

    def direction_fns(direction, z_ref):
        fwd = direction == 0
        lbd = jnp.maximum(lb_ref[direction], LB_FLOOR)
        tri = jnp.where((r_i >= c_i) if fwd else (r_i <= c_i), 1.0, 0.0).astype(BF16)
        near_mask_idx = {w: len(levels) + direction * len(blocks) + wi for wi, w in enumerate(blocks)}
        if has_s0:
            st_scr[direction] = _expand_state(s0_ref[0, direction], bmask)
        else:
            st_scr[direction] = jnp.zeros((GROUP_W, GROUP_W), F32)

        def gates(c, worst):
            rows = pl.ds(pl.multiple_of(c * t, t), t)
            z = z_ref[0, rows, :]
            a = jnp.exp(-jnp.abs(z))
            inv = 1.0 / (1.0 + a)
            pos = z >= 0
            sig = jnp.where(pos, inv, a * inv)
            nsig = jnp.where(pos, a * inv, inv)
            lf = jnp.log(lbd + (1.0 - lbd) * sig)
            k_scr[direction, rows, :] = (1.0 - lbd) * nsig
            h1, h2 = _split2(lf)
            b = (jnp.dot(tri, h1, preferred_element_type=F32) + jnp.dot(tri, h2, preferred_element_type=F32)) * LOG2E
            b_scr[direction, rows, :] = b
            worst = list(worst)
            for wi, w in enumerate(blocks):
                for gi in range(t // w):
                    first = b[gi * w:gi * w + 1]
                    last = b[(gi + 1) * w - 1:(gi + 1) * w]
                    worst[wi] = jnp.maximum(worst[wi], jnp.abs(first - last))
            return tuple(worst)

        def chunk(c, near_w):
            rows = pl.ds(pl.multiple_of(c * t, t), t)
            q = _silu(q_ref[0, rows, :])
            v = v_ref[0, rows, :]
            vb = v.astype(BF16)
            k = k_scr[direction, rows, :]
            b = b_scr[direction, rows, :]
            vs = _stack_heads(v, masks).astype(BF16)

            def level_scores(li, s):
                pa, pb = [], []
                zero = jnp.zeros((s, GROUP_W), F32)
                for gi in range(t // (2 * s)):
                    b_lo = b[gi * 2 * s:gi * 2 * s + s]
                    b_hi = b[gi * 2 * s + s:(gi + 1) * 2 * s]
                    if fwd:
                        anc = b_hi[0:1]
                        pa += [zero, jnp.exp2(b_hi - anc)]
                        pb += [jnp.exp2(anc - b_lo), zero]
                    else:
                        anc = b_lo[s - 1:s]
                        pa += [jnp.exp2(b_lo - anc), zero]
                        pb += [zero, jnp.exp2(anc - b_hi)]
                qa = (q * jnp.concatenate(pa, axis=0)).astype(BF16)
                kb = _stack_heads(k * jnp.concatenate(pb, axis=0), masks).astype(BF16)
                return lax.dot_general(qa, kb, NT_DIMS, preferred_element_type=F32) * mask_scr[li]

            def direct_pairs():
                prods, vals = [(q * k).astype(BF16)], [v]
                for dlt in range(1, HGRN_DIRECT):
                    sh = dlt if fwd else HGRN_DIRECT - dlt
                    ok = (sub >= dlt) if fwd else (sub + dlt <= HGRN_DIRECT - 1)
                    w = jnp.exp2(jnp.minimum(b - block_roll(b, sh), 0.0))
                    prods.append(jnp.where(ok, q * block_roll(k, sh) * w, 0.0).astype(BF16))
                    vals.append(block_roll(v, sh))
                hs = jnp.dot(jnp.concatenate(prods, axis=0), ones_bf16, preferred_element_type=F32)
                out = hs[0:t] * vals[0]
                for dlt in range(1, HGRN_DIRECT):
                    out = out + hs[dlt * t:(dlt + 1) * t] * vals[dlt]
                return out

            sc = None
            for li, s in enumerate(levels):
                if s >= max(near_w, HGRN_DIRECT):
                    term = level_scores(li, s)
                    sc = term if sc is None else sc + term

            if near_w:
                ea, eb = [], []
                for gi in range(t // near_w):
                    blk = b[gi * near_w:(gi + 1) * near_w]
                    anc = blk[0:1] if fwd else blk[near_w - 1:near_w]
                    ea.append(jnp.exp2(blk - anc))
                    eb.append(jnp.exp2(anc - blk))
                qa = (q * jnp.concatenate(ea, axis=0)).astype(BF16)
                kb = _stack_heads(k * jnp.concatenate(eb, axis=0), masks).astype(BF16)
                term = lax.dot_general(qa, kb, NT_DIMS, preferred_element_type=F32) * mask_scr[near_mask_idx[near_w]]
                sc = term if sc is None else sc + term
                o = jnp.dot(sc.astype(BF16), vs, preferred_element_type=F32)
            else:
                o = jnp.dot(sc.astype(BF16), vs, preferred_element_type=F32) + direct_pairs()

            st = st_scr[direction]
            o = o + lax.dot_general((q * jnp.exp2(b)).astype(BF16), st.astype(BF16), NT_DIMS,
                                    preferred_element_type=F32)
            bl = b[t - 1:t] if fwd else b[0:1]
            upd = lax.dot_general(vb, (k * jnp.exp2(bl - b)).astype(BF16), TN_DIMS,
                                  preferred_element_type=F32)
            st_scr[direction] = st * jnp.exp2(bl) + jnp.where(bmask, upd, 0.0)
            o_scr[direction, rows, :] = o

        return gates, chunk

    gates_f, chunk_f = direction_fns(0, zf_ref)
    gates_b, chunk_b = direction_fns(1, zb_ref)

    worst = lax.fori_loop(0, nc, lambda c, w: gates_b(c, gates_f(c, w)),
                          tuple(jnp.zeros((1, GROUP_W), F32) for _ in blocks), unroll=2)
    safe = [jnp.max(w) < HGRN_SAFE_EXP2 for w in worst]

    def sweep(ci, carry, *, near_w):
        chunk_f(ci, near_w)
        chunk_b(nc - 1 - ci, near_w)
        return carry

    taken = False
    for w, ok in zip(blocks, safe):
        cond = ok if taken is False else jnp.logical_and(jnp.logical_not(taken), ok)

        @pl.when(cond)
        def _(w=w):
            lax.fori_loop(0, nc, functools.partial(sweep, near_w=w), 0, unroll=4)

        taken = ok if taken is False else jnp.logical_or(taken, ok)

    @pl.when(jnp.logical_not(taken))
    def _():
        lax.fori_loop(0, nc, functools.partial(sweep, near_w=0), 0)

    if not has_s0:
        for direction in range(2):
            sfin_ref[0, direction] = _compact_state(st_scr[direction], bmask)

    def finish(c, carry):
        rows = pl.ds(pl.multiple_of(c * t, t), t)
        o = o_scr[0, rows, :] + o_scr[1, rows, :]
        ms = _head_sum(o * o, ones_bf16) * (1.0 / HEAD_DIM)
        y = o * lax.rsqrt(ms + EPS)
        o_ref[0, rows, :] = (y * _silu(g_ref[0, rows, :])).astype(o_ref.dtype)
        return carry

    lax.fori_loop(0, nc, finish, 0, unroll=2)


HGRN_CHUNK = 128


def _hgrn(pd, lb, s0=None, t=HGRN_CHUNK):
    bsz, slen, _ = pd.shape
    col = lambda j: pl.BlockSpec((1, slen, GROUP_W), lambda b: (b, 0, j))
    s_in_specs, s_args, s_out_specs, s_out_shape = _state_specs(bsz, s0)
    blocks, levels = _hgrn_blocks_levels(t)
    n_masks = len(levels) + 2 * len(blocks)
    return pl.pallas_call(
        functools.partial(_hgrn_kernel, slen=slen, t=t, has_s0=s0 is not None),
        grid=(bsz,),
        in_specs=[col(0), col(1), col(2), col(3), col(4),
                  pl.BlockSpec((2, 1, GROUP_W), lambda b: (0, 0, 0))] + s_in_specs,
        out_specs=[pl.BlockSpec((1, slen, GROUP_W), lambda b: (b, 0, 0))] + s_out_specs,
        out_shape=[jax.ShapeDtypeStruct((bsz, slen, GROUP_W), BF16)] + s_out_shape,
        scratch_shapes=[pltpu.VMEM((2, slen, GROUP_W), F32), pltpu.VMEM((2, GROUP_W, GROUP_W), F32),
                        pltpu.VMEM((2, slen, GROUP_W), F32), pltpu.VMEM((2, slen, GROUP_W), F32),
                        pltpu.VMEM((n_masks, t, N_HEADS * t), F32)],
        compiler_params=_cparams(("arbitrary",)),
        name="hgrn2",
    )(pd, pd, pd, pd, pd, lb.reshape(2, 1, GROUP_W), *s_args)


def _outproj_kernel(ma_ref, mb_ref, mc_ref, md_ref, w_ref, x_ref, gate_ref, g_ref, o_ref, *, layer, mod_row):
    g = GROUP_W
    y = jnp.dot(ma_ref[0], w_ref[0, 0:g, :], preferred_element_type=F32)
    y = y + jnp.dot(mb_ref[0], w_ref[0, g:2 * g, :], preferred_element_type=F32)
    y = y + jnp.dot(mc_ref[0], w_ref[0, 2 * g:3 * g, :], preferred_element_type=F32)
    y = y + jnp.dot(md_ref[0], w_ref[0, 3 * g:4 * g, :], preferred_element_type=F32)
    ms = jnp.mean(y * y, axis=-1, keepdims=True)
    r = y * lax.rsqrt(ms + EPS) * g_ref[layer:layer + 1, :]
    o_ref[0] = x_ref[0] + _mod_row(gate_ref, mod_row) * r


OUTPROJ_TM = 1024


def _out_proj(mixed, w_out_bf16, x, mods, layer, mod_row, g_post):
    bsz, slen, d = x.shape
    depth = g_post.shape[0]
    tm = min(OUTPROJ_TM, slen)
    mspec = pl.BlockSpec((1, tm, GROUP_W), lambda b, i: (b, i, 0))
    return pl.pallas_call(
        functools.partial(_outproj_kernel, layer=layer, mod_row=mod_row),
        grid=(bsz, slen // tm),
        in_specs=[mspec, mspec, mspec, mspec,
                  pl.BlockSpec((1, d, d), lambda b, i: (layer, 0, 0)),
                  pl.BlockSpec((1, tm, d), lambda b, i: (b, i, 0)),
                  _mod_spec(2, layer),
                  pl.BlockSpec((depth, d), lambda b, i: (0, 0))],
        out_specs=pl.BlockSpec((1, tm, d), lambda b, i: (b, i, 0)),
        out_shape=jax.ShapeDtypeStruct((bsz, slen, d), F32),
        compiler_params=_cparams(("arbitrary", "arbitrary")),
        name="out_proj",
    )(*mixed, w_out_bf16, x, mods, g_post)


def _context_layer(x, mods, layer, mod_row, g_pre, g_post, w_in, w_out, sink, log_gamma, lb):
    bsz, slen, _ = x.shape
    pa, pb, pc, pd, ak, av, bk, bv = _in_proj(x, mods, layer, mod_row, g_pre, w_in, rope=False, emit_kv=True)
    o_a, = _attention(_attn_operands(pa, 0, 2, 3, 2, n_kv=A_KV, qb=slen, kw=slen, back=0, sink=sink))
    o_b, = _attention(_attn_operands(pb, 0, 1, 2, 3, n_kv=N_HEADS, qb=slen, kw=slen, back=0))
    o_c, s_c = _retention(pc, log_gamma)
    o_d, s_d = _hgrn(pd, lb)
    x = _out_proj((o_a, o_b, o_c, o_d), w_out, x, mods, layer, mod_row, g_post)
    kv4 = lambda t: t.reshape(bsz, slen, -1, HEAD_DIM)
    return x, (kv4(ak), kv4(av), kv4(bk), kv4(bv), _uncompact_states(s_c), _uncompact_states(s_d))


def _latent_layer(x, mods, layer, ca_k, ca_v, cb_k, cb_v, st_c, st_d, g_pre, g_post, w_in, w_out, sink, win_tab,
                  na_tabs, log_gamma, lb):
    pa, pb, pc, pd = _in_proj(x, mods, layer, None, g_pre, w_in, rope=True)
    o_a, o_b = _attention(
        _attn_operands(pa, 0, 2, 3, 2, n_kv=A_KV, qb=WIN_QB, kw=WIN_KW, back=WIN_BACK,
                       ctx_k=ca_k, ctx_v=ca_v, table=win_tab, sink=sink, layer=layer),
        _attn_operands(pb, 0, 1, 2, 3, n_kv=N_HEADS, qb=NA_QB, kw=NA_KW, back=NA_QB,
                       ctx_k=cb_k, ctx_v=cb_v, table=na_tabs, layer=layer))
    o_c, = _retention(pc, log_gamma, st_c)
    o_d, = _hgrn(pd, lb, st_d)
    return _out_proj((o_a, o_b, o_c, o_d), w_out, x, mods, layer, None, g_post)


def kernel(x_prompt, x_sample, c, cache_win_k, cache_win_v, cache_na_k, cache_na_v, state_ret, state_hgrn,
           c_ctx, w_ada, b_ada, g_pre, g_post, w_in, w_out, attn_sink, na_rpb, ret_decay_logit, hgrn_lb_logit):
    depth = w_ada.shape[0]
    dec_b, dec_s, d = x_sample.shape
    p_lb = jax.nn.softmax(hgrn_lb_logit.astype(F32), axis=0)
    lower_bounds = jnp.cumsum(p_lb, axis=0) - p_lb[0:1]
    log_gammas = jax.nn.log_sigmoid(ret_decay_logit.astype(F32))
    w_in_b = w_in.astype(BF16)
    w_out_b = w_out.astype(BF16)

    cvecs = jnp.zeros((ADA_ROWS, d), F32).at[:dec_b].set(c).at[dec_b].set(c_ctx)
    mods = _adaln(cvecs, w_ada, b_ada)
    win_tab = _window_table(dec_s)
    na_tabs = _na_tables(na_rpb, dec_s)

    x = x_prompt
    outs = [[] for _ in range(6)]
    for l in range(depth):
        x, extra = _context_layer(x, mods, l, dec_b, g_pre, g_post, w_in_b, w_out_b,
                                  attn_sink[l], log_gammas[l], lower_bounds[l])
        for acc, e in zip(outs, extra):
            acc.append(e)
    y_prompt = x
    stacked = [jnp.stack(o, axis=1) for o in outs]

    past = cache_win_k.shape[2]
    ca_k = cache_win_k.reshape(dec_b, depth, past, A_KV * HEAD_DIM)
    ca_v = cache_win_v.reshape(dec_b, depth, past, A_KV * HEAD_DIM)
    cb_k = cache_na_k.reshape(dec_b, depth, past, GROUP_W)
    cb_v = cache_na_v.reshape(dec_b, depth, past, GROUP_W)
    st_c = _compact_states(state_ret)
    st_d = _compact_states(state_hgrn)
    x = x_sample
    for l in range(depth):
        x = _latent_layer(x, mods, l, ca_k, ca_v, cb_k, cb_v, st_c[:, l], st_d[:, l], g_pre, g_post,
                          w_in_b, w_out_b, attn_sink[l], win_tab, na_tabs, log_gammas[l], lower_bounds[l])
    return (y_prompt, x, *stacked)
```

```python
import functools

import numpy as np
import jax
import jax.numpy as jnp
from jax import lax
from jax.experimental import pallas as pl
from jax.experimental.pallas import tpu as pltpu

F32 = jnp.float32
BF16 = jnp.bfloat16

D_MODEL = 1024
DEPTH = 4
GRID_W = 64
HEAD_DIM = 64
N_HEADS = 4
GROUP_W = N_HEADS * HEAD_DIM
A_KV = 2
WINDOW = 128
NA_ROWS = 8
NA_COLS = 16
ROPE_BASE = 10000.0
EPS = 1e-6
NEG = -1e30
LB_FLOOR = 1e-30
LOG2E = 1.4426950408889634
W_A = 3 * GROUP_W
W_B = 4 * GROUP_W
W_C = 4 * GROUP_W
W_D = 5 * GROUP_W
IN_WIDTH = W_A + W_B + W_C + W_D

V7X_VMEM_LIMIT_BYTES = 56 * 1024 * 1024
ADA_ROWS = 16

NT_DIMS = (((1,), (1,)), ((), ()))
TN_DIMS = (((0,), (0,)), ((), ()))


def _cparams(sem, flags=None):
    return pltpu.CompilerParams(dimension_semantics=sem, vmem_limit_bytes=V7X_VMEM_LIMIT_BYTES, flags=flags)


def _sigmoid(x):
    return 1.0 / (1.0 + jnp.exp(-x))


def _silu(x):
    return x * _sigmoid(x)


def _head_masks(rows):
    lane = lax.broadcasted_iota(jnp.int32, (rows, GROUP_W), 1)
    return [(lane >= h * HEAD_DIM) & (lane < (h + 1) * HEAD_DIM) for h in range(N_HEADS)]


def _stack_heads(x, masks):
    return jnp.concatenate([jnp.where(m, x, 0.0) for m in masks], axis=0)


def _block_ones():
    shift = HEAD_DIM.bit_length() - 1
    r = lax.broadcasted_iota(jnp.int32, (GROUP_W, GROUP_W), 0) >> shift
    c = lax.broadcasted_iota(jnp.int32, (GROUP_W, GROUP_W), 1) >> shift
    return r == c


def _head_sum(x, ones_bf16):
    hi = x.astype(BF16)
    lo = (x - hi.astype(F32)).astype(BF16)
    return (jnp.dot(hi, ones_bf16, preferred_element_type=F32)
            + jnp.dot(lo, ones_bf16, preferred_element_type=F32))


def _adaln_kernel(c_ref, w_ref, b_ref, o_ref):
    s = _silu(c_ref[...]).astype(BF16)
    o_ref[0, 0] = jnp.dot(s, w_ref[0].astype(BF16), preferred_element_type=F32) + b_ref[0]


def _adaln(cvecs, w_ada, b_ada):
    depth, d, d3 = w_ada.shape
    tn = 512
    per = d // tn
    return pl.pallas_call(
        _adaln_kernel,
        grid=(depth, d3 // tn),
        in_specs=[pl.BlockSpec((ADA_ROWS, d), lambda l, j: (0, 0)),
                  pl.BlockSpec((1, d, tn), lambda l, j: (l, 0, j)),
                  pl.BlockSpec((1, 1, tn), lambda l, j: (l, 0, j))],
        out_specs=pl.BlockSpec((1, 1, ADA_ROWS, tn), lambda l, j: (l, j // per, 0, j % per)),
        out_shape=jax.ShapeDtypeStruct((depth, 3, ADA_ROWS, d), F32),
        compiler_params=_cparams(("arbitrary", "arbitrary")),
        name="adaln",
    )(cvecs, w_ada, b_ada.reshape(depth, 1, d3))


def _rope(x, cos, sin):
    w = x.shape[-1]
    lane = lax.broadcasted_iota(jnp.int32, x.shape, 1)
    first = (lane & 31) < 16
    swapped = jnp.where(first, pltpu.roll(x, w - 16, 1), pltpu.roll(x, 16, 1))
    return x * cos[:, :w] + swapped * sin[:, :w]


def _mod_row(mod_ref, mod_row):
    row = pl.program_id(0) if mod_row is None else mod_row
    return mod_ref[0, 0, pl.ds(row, 1), :]


def _inproj_kernel(*refs, rope, emit_kv, layer, mod_row):
    refs = list(refs)
    x_ref, shift_ref, scale_ref, g_ref, w_ref = refs[:5]
    if rope:
        cos_ref, sin_ref = refs[5:7]
    n_out = 10 if emit_kv else 6
    pa_ref, pb_ref, pc_ref, pd_ref, ga_ref, gb_ref = refs[-n_out:][:6]
    x = x_ref[0]
    ms = jnp.mean(x * x, axis=-1, keepdims=True)
    y = x * lax.rsqrt(ms + EPS) * g_ref[layer:layer + 1, :]
    h = (y * (1.0 + _mod_row(scale_ref, mod_row)) + _mod_row(shift_ref, mod_row)).astype(BF16)

    def mm(c0, c1):
        return jnp.dot(h, w_ref[0, :, c0:c1], preferred_element_type=F32)

    if rope:
        cos = cos_ref[...]
        sin = sin_ref[...]
    g = GROUP_W
    aq = mm(0, g)
    akv = mm(g, 2 * g)
    if rope:
        aq = _rope(aq, cos, sin)
        ak = _rope(akv[:, :g // 2], cos, sin)
        akv = jnp.concatenate([ak, akv[:, g // 2:]], axis=1)
    pa_ref[0, :, 0:g] = aq.astype(pa_ref.dtype)
    pa_ref[0, :, g:2 * g] = akv.astype(pa_ref.dtype)
    ga_ref[0] = mm(2 * g, 3 * g)
    pbv = mm(W_A, W_A + W_B)
    pb_ref[0] = pbv[:, :3 * g].astype(pb_ref.dtype)
    gb_ref[0] = pbv[:, 3 * g:]
    if emit_kv:
        ak_ref, av_ref, bk_ref, bv_ref = refs[-4:]
        ak_ref[0] = akv[:, :g // 2]
        av_ref[0] = akv[:, g // 2:]
        bk_ref[0] = pbv[:, g:2 * g]
        bv_ref[0] = pbv[:, 2 * g:3 * g]
    c0 = W_A + W_B
    cq = mm(c0, c0 + g)
    ck = mm(c0 + g, c0 + 2 * g)
    if rope:
        cq = _rope(cq, cos, sin)
        ck = _rope(ck, cos, sin)
    pc_ref[0, :, 0:g] = cq.astype(pc_ref.dtype)
    pc_ref[0, :, g:2 * g] = ck.astype(pc_ref.dtype)
    pc_ref[0, :, 2 * g:4 * g] = mm(c0 + 2 * g, c0 + 4 * g).astype(pc_ref.dtype)
    d0 = c0 + W_C
    pd_ref[0] = mm(d0, d0 + W_D).astype(pd_ref.dtype)


def _rope_tables(slen):
    t = np.arange(slen)
    nf = HEAD_DIM // 4
    freqs = ROPE_BASE ** (-np.arange(nf, dtype=np.float64) / nf)
    d = np.arange(HEAD_DIM)
    pos = np.where(d[None, :] < HEAD_DIM // 2, (t // GRID_W)[:, None], (t % GRID_W)[:, None])
    ang = pos * freqs[d % nf][None, :]
    sign = np.where((d % (2 * nf)) < nf, -1.0, 1.0)[None, :]
    cos = np.tile(np.cos(ang), (1, N_HEADS))
    sin = np.tile(np.sin(ang) * sign, (1, N_HEADS))
    return jnp.asarray(cos, F32), jnp.asarray(sin, F32)


def _mod_spec(which, layer):
    return pl.BlockSpec((1, 1, ADA_ROWS, D_MODEL), lambda b, i: (layer, which, 0, 0))


INPROJ_TM = 512


def _in_proj(x, mods, layer, mod_row, g_pre, w_in_bf16, rope, emit_kv=False):
    bsz, slen, d = x.shape
    tm = min(INPROJ_TM, slen)
    depth = g_pre.shape[0]
    in_specs = [pl.BlockSpec((1, tm, d), lambda b, i: (b, i, 0)),
                _mod_spec(0, layer),
                _mod_spec(1, layer),
                pl.BlockSpec((depth, d), lambda b, i: (0, 0)),
                pl.BlockSpec((1, d, IN_WIDTH), lambda b, i: (layer, 0, 0))]
    args = [x, mods, mods, g_pre, w_in_bf16]
    if rope:
        cos, sin = _rope_tables(slen)
        in_specs += [pl.BlockSpec((tm, GROUP_W), lambda b, i: (i, 0))] * 2
        args += [cos, sin]
    outs = [(W_A - GROUP_W, BF16), (W_B - GROUP_W, BF16), (W_C, F32), (W_D, F32), (GROUP_W, F32), (GROUP_W, F32)]
    if emit_kv:
        outs += [(A_KV * HEAD_DIM, F32), (A_KV * HEAD_DIM, F32), (GROUP_W, F32), (GROUP_W, F32)]
    return pl.pallas_call(
        functools.partial(_inproj_kernel, rope=rope, emit_kv=emit_kv, layer=layer, mod_row=mod_row),
        grid=(bsz, slen // tm),
        in_specs=in_specs,
        out_specs=[pl.BlockSpec((1, tm, w), lambda b, i: (b, i, 0)) for w, _ in outs],
        out_shape=[jax.ShapeDtypeStruct((bsz, slen, w), dt) for w, dt in outs],
        compiler_params=_cparams(("arbitrary", "arbitrary")),
        name="in_proj",
    )(*args)


ATTN_ROWS = 256


def _attn_kernel(*refs, cfgs):
    n_in = [4 + 2 * c["has_ctx"] + (c["n_tab"] > 0) + c["has_sink"] for c in cfgs]
    outs = refs[sum(n_in):]
    start = 0
    for c, n, o_ref in zip(cfgs, n_in, outs):
        _attn_body(refs[start:start + n], o_ref, **c)
        start += n


def _attn_body(in_refs, o_ref, *, n_kv, qb, kw, back, slen, has_ctx, n_tab, tab_heads, has_sink):
    it = iter(in_refs)
    q_ref, k_ref, v_ref, g_ref = next(it), next(it), next(it), next(it)
    kc_ref = vc_ref = tab_ref = sink_ref = None
    if has_ctx:
        kc_ref, vc_ref = next(it), next(it)
    if n_tab:
        tab_ref = next(it)
    if has_sink:
        sink_ref = next(it)

    n = pl.program_id(1)
    nblk = slen // qb
    ws = pl.multiple_of(jnp.clip(n * qb - back, 0, slen - kw), HEAD_DIM)
    if n_tab == 3:
        tix = jnp.where(n == 0, 0, jnp.where(n == nblk - 1, 2, 1))
    else:
        tix = 0
    grp = N_HEADS // n_kv
    qscale = HEAD_DIM ** -0.5 * LOG2E
    pair_w = 2 * HEAD_DIM
    lane = lax.broadcasted_iota(jnp.int32, (1, pair_w), 1)
    half_mask = (lane < HEAD_DIM, lane >= HEAD_DIM)
    rb = min(qb, ATTN_ROWS)
    for j in range(N_HEADS // 2):
        cols = slice(j * pair_w, (j + 1) * pair_w)
        if grp == 1:
            kv_cols, kv_half = cols, (0, 1)
        else:
            kv_cols, kv_half = slice(0, pair_w), (j, j)
        kall = k_ref[0, pl.ds(ws, kw), kv_cols].astype(BF16)
        vall = v_ref[0, pl.ds(ws, kw), kv_cols].astype(BF16)
        if has_ctx:
            kall = jnp.concatenate([kall, kc_ref[0, 0, :, kv_cols].astype(BF16)], axis=0)
            vall = jnp.concatenate([vall, vc_ref[0, 0, :, kv_cols].astype(BF16)], axis=0)
        vaug = {hf: jnp.where(half_mask[hf], vall, 1.0).astype(BF16) for hf in set(kv_half)}
        for r0 in range(0, qb, rb):
            rs = slice(r0, r0 + rb)
            q128 = q_ref[0, rs, cols].astype(F32) * qscale
            q_other = pltpu.roll(q128, HEAD_DIM, 1) if kv_half != (0, 1) else None
            outs = []
            for i in range(2):
                hf = kv_half[i]
                qm = jnp.where(half_mask[hf], q128 if i == hf else q_other, 0.0).astype(BF16)
                s = lax.dot_general(qm, kall, NT_DIMS, preferred_element_type=F32)
                if n_tab:
                    tab = tab_ref[0, tix, 2 * j + i if tab_heads == N_HEADS else 0, rs, :]
                    s = jnp.concatenate([s[:, :kw] + tab, s[:, kw:]], axis=1) if has_ctx else s + tab
                m = jnp.max(s, axis=-1, keepdims=True)
                if has_sink:
                    sink = sink_ref[2 * j + i] * LOG2E
                    m = jnp.maximum(m, sink)
                p = jnp.exp2(s - m).astype(BF16)
                acc = jnp.dot(p, vaug[hf], preferred_element_type=F32)
                if has_sink:
                    acc = acc + jnp.where(half_mask[1 - hf], jnp.exp2(sink - m), 0.0)
                acc = acc / pltpu.roll(acc, HEAD_DIM, 1)
                outs.append(acc if i == hf else pltpu.roll(acc, HEAD_DIM, 1))
            out = jnp.where(half_mask[0], outs[0], outs[1]) * _silu(g_ref[0, rs, cols])
            o_ref[0, rs, cols] = out.astype(o_ref.dtype)


def _attn_operands(p, q_col, k_col, v_col, gate, n_kv, qb, kw, back, ctx_k=None, ctx_v=None,
                   table=None, sink=None, layer=0):
    slen = p.shape[1]
    kvw = n_kv * HEAD_DIM
    in_specs = [pl.BlockSpec((1, qb, GROUP_W), lambda b, n: (b, n, q_col)),
                pl.BlockSpec((1, slen, kvw), lambda b, n: (b, 0, k_col)),
                pl.BlockSpec((1, slen, kvw), lambda b, n: (b, 0, v_col)),
                pl.BlockSpec((1, qb, GROUP_W), lambda b, n: (b, n, 0))]
    args = [p, p, p, gate]
    has_ctx = ctx_k is not None
    if has_ctx:
        past = ctx_k.shape[2]
        in_specs += [pl.BlockSpec((1, 1, past, kvw), lambda b, n: (b, layer, 0, 0))] * 2
        args += [ctx_k, ctx_v]
    n_tab = tab_heads = 0
    if table is not None:
        n_tab, tab_heads = table.shape[1], table.shape[2]
        tab_layer = layer if table.shape[0] > 1 else 0
        in_specs.append(pl.BlockSpec((1,) + table.shape[1:], lambda b, n: (tab_layer, 0, 0, 0, 0)))
        args.append(table)
    if sink is not None:
        in_specs.append(pl.BlockSpec(memory_space=pltpu.SMEM))
        args.append(sink)
    cfg = dict(n_kv=n_kv, qb=qb, kw=kw, back=back, slen=slen, has_ctx=has_ctx, n_tab=n_tab,
               tab_heads=tab_heads, has_sink=sink is not None)
    return in_specs, args, cfg


def _attention(*operands):
    in_specs = [s for op in operands for s in op[0]]
    args = [a for op in operands for a in op[1]]
    cfgs = tuple(op[2] for op in operands)
    bsz = args[0].shape[0]
    slen, qb = cfgs[0]["slen"], cfgs[0]["qb"]
    assert all(c["slen"] == slen and c["qb"] == qb for c in cfgs)
    return pl.pallas_call(
        functools.partial(_attn_kernel, cfgs=cfgs),
        grid=(bsz, slen // qb),
        in_specs=in_specs,
        out_specs=[pl.BlockSpec((1, qb, GROUP_W), lambda b, n: (b, n, 0)) for _ in cfgs],
        out_shape=[jax.ShapeDtypeStruct((bsz, slen, GROUP_W), BF16) for _ in cfgs],
        compiler_params=_cparams(("arbitrary", "arbitrary")),
        name="attn",
    )(*args)


WIN_QB = 256
WIN_BACK = WINDOW
WIN_KW = WIN_QB + 2 * WINDOW
NA_QROWS = 4
NA_KROWS = 12
NA_QB = NA_QROWS * GRID_W
NA_KW = NA_KROWS * GRID_W


def _window_table(slen):
    nblk = slen // WIN_QB
    tabs = []
    for n in (0, 1, nblk - 1):
        ws = int(np.clip(n * WIN_QB - WIN_BACK, 0, slen - WIN_KW))
        qpos = n * WIN_QB + np.arange(WIN_QB)[:, None]
        kpos = ws + np.arange(WIN_KW)[None, :]
        tabs.append(np.where(np.abs(qpos - kpos) <= WINDOW, 0.0, NEG))
    return jnp.asarray(np.stack(tabs)[None, :, None], F32)


N_RPB_R = 2 * NA_ROWS - 1
N_RPB_C = 2 * NA_COLS - 1


def _na_table_kernel(rpb_ref, o_ref, tz_scr, *, rows):
    base = (pl.program_id(0) * N_HEADS + pl.program_id(1)) * (N_RPB_R * N_RPB_C)
    qc = lax.broadcasted_iota(jnp.int32, (GRID_W, 2 * GRID_W), 0)
    kk = lax.broadcasted_iota(jnp.int32, (GRID_W, 2 * GRID_W), 1)
    kc = kk & (GRID_W - 1)
    diff = kc - qc
    qws = jnp.clip(qc - NA_COLS // 2, 0, GRID_W - NA_COLS)
    col_ok = (kc >= qws) & (kc < qws + NA_COLS)
    neg = jnp.full((GRID_W, 2 * GRID_W), NEG, F32)
    for dr in range(N_RPB_R):
        acc = neg
        for m in range(N_RPB_C):
            acc = jnp.where(diff == m - (NA_COLS - 1), rpb_ref[base + dr * N_RPB_C + m] * LOG2E, acc)
        tz_scr[dr] = jnp.where(col_ok, acc, NEG)
    wr = min(NA_ROWS, rows)
    nblk = rows // NA_QROWS
    for ti, g in enumerate((0, 1, nblk - 1)):
        ws_row = min(max(g * NA_QROWS - NA_QROWS, 0), rows - NA_KROWS)
        for qr in range(NA_QROWS):
            r = g * NA_QROWS + qr
            rs = min(max(r - wr // 2, 0), rows - wr)
            for p in range(NA_KROWS // 2):
                halves = []
                for kr in (ws_row + 2 * p, ws_row + 2 * p + 1):
                    halves.append(tz_scr[kr - r + NA_ROWS - 1] if rs <= kr < rs + wr else neg)
                o_ref[0, ti, 0, qr * GRID_W:(qr + 1) * GRID_W, p * 2 * GRID_W:(p + 1) * 2 * GRID_W] = (
                    jnp.where(kk < GRID_W, halves[0], halves[1]))


def _na_tables(na_rpb, slen):
    depth = na_rpb.shape[0]
    return pl.pallas_call(
        functools.partial(_na_table_kernel, rows=slen // GRID_W),
        grid=(depth, N_HEADS),
        in_specs=[pl.BlockSpec(memory_space=pltpu.SMEM)],
        out_specs=pl.BlockSpec((1, 3, 1, NA_QB, NA_KW), lambda l, h: (l, 0, h, 0, 0)),
        out_shape=jax.ShapeDtypeStruct((depth, 3, N_HEADS, NA_QB, NA_KW), F32),
        scratch_shapes=[pltpu.VMEM((N_RPB_R, GRID_W, 2 * GRID_W), F32)],
        compiler_params=_cparams(("arbitrary", "arbitrary")),
        name="na_table",
    )(na_rpb.astype(F32).reshape(-1))


def _ret_kernel(q_ref, k_ref, v_ref, g_ref, lgl_ref, lgc_ref, *rest, slen, t, has_s0):
    if has_s0:
        s0_ref, o_ref, sb_scr, st_scr, dm_scr, dec_scr = rest
    else:
        o_ref, sfin_ref, sb_scr, st_scr, dm_scr, dec_scr = rest
    nc = slen // t
    masks = _head_masks(t)
    bmask = _block_ones()
    ones_bf16 = jnp.where(bmask, 1.0, 0.0).astype(BF16)
    lgf, lgb = lgl_ref[0], lgl_ref[1]

    @pl.when(pl.program_id(0) == 0)
    def _():
        ii = lax.broadcasted_iota(jnp.int32, (t, N_HEADS * t), 0)
        jj = lax.broadcasted_iota(jnp.int32, (t, N_HEADS * t), 1) & (t - 1)
        dist = (ii - jj).astype(F32)
        dm_scr[...] = (jnp.where(dist >= 0, jnp.exp(dist * lgc_ref[0]), 0.0)
                       + jnp.where(dist <= 0, jnp.exp(-dist * lgc_ref[1]), 0.0))
        idx = lax.broadcasted_iota(jnp.int32, (t, GROUP_W), 0).astype(F32)
        dec_scr[0] = jnp.exp((idx + 1.0) * lgf)
        dec_scr[1] = jnp.exp((t - 1.0 - idx) * lgf)
        dec_scr[2] = jnp.exp((t - idx) * lgb)
        dec_scr[3] = jnp.exp(idx * lgb)

    cdec_f = jnp.exp(float(t) * lgf)
    cdec_b = jnp.exp(float(t) * lgb)
    kscale = HEAD_DIM ** -0.5

    def state_update(st, k, v, kdec, cdec):
        upd = lax.dot_general(v, (k * kdec).astype(BF16), TN_DIMS, preferred_element_type=F32)
        return st * cdec + jnp.where(bmask, upd, 0.0)

    def init_state(direction):
        if has_s0:
            st_scr[...] = _expand_state(s0_ref[0, direction], bmask)
        else:
            st_scr[...] = jnp.zeros((GROUP_W, GROUP_W), F32)

    def emit_final(direction):
        if not has_s0:
            sfin_ref[0, direction] = _compact_state(st_scr[...], bmask)

    init_state(1)

    def sweep_bwd(ci, carry):
        c = nc - 1 - ci
        rows = pl.ds(pl.multiple_of(c * t, t), t)
        st = st_scr[...]
        sb_scr[c] = st.astype(BF16)
        k = k_ref[0, rows, :] * kscale
        st_scr[...] = state_update(st, k, v_ref[0, rows, :].astype(BF16), dec_scr[3], cdec_b)
        return carry

    lax.fori_loop(0, nc, sweep_bwd, 0, unroll=4)
    emit_final(1)
    init_state(0)

    def sweep_fwd(c, carry):
        rows = pl.ds(pl.multiple_of(c * t, t), t)
        q = q_ref[0, rows, :]
        k = k_ref[0, rows, :] * kscale
        v = v_ref[0, rows, :]
        ks = _stack_heads(k, masks).astype(BF16)
        vs = _stack_heads(v, masks).astype(BF16)
        sc = lax.dot_general(q.astype(BF16), ks, NT_DIMS, preferred_element_type=F32) * dm_scr[...]
        o = jnp.dot(sc.astype(BF16), vs, preferred_element_type=F32)
        st = st_scr[...]
        o = o + lax.dot_general((q * dec_scr[0]).astype(BF16), st.astype(BF16), NT_DIMS,
                                preferred_element_type=F32)
        o = o + lax.dot_general((q * dec_scr[2]).astype(BF16), sb_scr[c], NT_DIMS, preferred_element_type=F32)
        st_scr[...] = state_update(st, k, v.astype(BF16), dec_scr[1], cdec_f)
        mu = _head_sum(o, ones_bf16) * (1.0 / HEAD_DIM)
        d = o - mu
        var = _head_sum(d * d, ones_bf16) * (1.0 / HEAD_DIM)
        y = d * lax.rsqrt(var + EPS)
        o_ref[0, rows, :] = (y * _silu(g_ref[0, rows, :])).astype(o_ref.dtype)
        return carry

    lax.fori_loop(0, nc, sweep_fwd, 0, unroll=4)
    emit_final(0)


def _compact_states(s0):
    return jnp.swapaxes(s0, -1, -2).reshape(s0.shape[:-3] + (GROUP_W, HEAD_DIM))


def _uncompact_states(sc):
    bsz = sc.shape[0]
    return jnp.swapaxes(sc.reshape(bsz, 2, N_HEADS, HEAD_DIM, HEAD_DIM), -1, -2)


def _split3(x):
    h1 = x.astype(BF16)
    r1 = x - h1.astype(F32)
    h2 = r1.astype(BF16)
    return h1, h2, (r1 - h2.astype(F32)).astype(BF16)


def _expand_state(x, bmask):
    r = lax.broadcasted_iota(jnp.int32, (HEAD_DIM, GROUP_W), 0)
    c = lax.broadcasted_iota(jnp.int32, (HEAD_DIM, GROUP_W), 1) & (HEAD_DIM - 1)
    rep = jnp.where(r == c, 1.0, 0.0).astype(BF16)
    h1, h2, h3 = _split3(x)
    tiled = (jnp.dot(h1, rep, preferred_element_type=F32) + jnp.dot(h2, rep, preferred_element_type=F32)
             + jnp.dot(h3, rep, preferred_element_type=F32))
    return jnp.where(bmask, tiled, 0.0)


def _compact_state(st, bmask):
    r = lax.broadcasted_iota(jnp.int32, (GROUP_W, HEAD_DIM), 0) & (HEAD_DIM - 1)
    c = lax.broadcasted_iota(jnp.int32, (GROUP_W, HEAD_DIM), 1)
    fold = jnp.where(r == c, 1.0, 0.0).astype(BF16)
    h1, h2, h3 = _split3(jnp.where(bmask, st, 0.0))
    return (jnp.dot(h1, fold, preferred_element_type=F32) + jnp.dot(h2, fold, preferred_element_type=F32)
            + jnp.dot(h3, fold, preferred_element_type=F32))


def _state_specs(bsz, s0):
    spec = pl.BlockSpec((1, 2, GROUP_W, HEAD_DIM), lambda b: (b, 0, 0, 0))
    shape = jax.ShapeDtypeStruct((bsz, 2, GROUP_W, HEAD_DIM), F32)
    if s0 is None:
        return [], [], [spec], [shape]
    return [spec], [s0], [], []


RET_CHUNK = 256


def _retention(pc, log_gamma, s0=None, t=RET_CHUNK):
    bsz, slen, _ = pc.shape
    t = min(t, slen)
    lgl = jnp.repeat(log_gamma, HEAD_DIM, axis=1).reshape(2, 1, GROUP_W)
    lgc = jnp.repeat(log_gamma, t, axis=1).reshape(2, 1, N_HEADS * t)
    col = lambda j: pl.BlockSpec((1, slen, GROUP_W), lambda b: (b, 0, j))
    s_in_specs, s_args, s_out_specs, s_out_shape = _state_specs(bsz, s0)
    return pl.pallas_call(
        functools.partial(_ret_kernel, slen=slen, t=t, has_s0=s0 is not None),
        grid=(bsz,),
        in_specs=[col(0), col(1), col(2), col(3),
                  pl.BlockSpec((2, 1, GROUP_W), lambda b: (0, 0, 0)),
                  pl.BlockSpec((2, 1, N_HEADS * t), lambda b: (0, 0, 0))] + s_in_specs,
        out_specs=[pl.BlockSpec((1, slen, GROUP_W), lambda b: (b, 0, 0))] + s_out_specs,
        out_shape=[jax.ShapeDtypeStruct((bsz, slen, GROUP_W), BF16)] + s_out_shape,
        scratch_shapes=[pltpu.VMEM((slen // t, GROUP_W, GROUP_W), BF16), pltpu.VMEM((GROUP_W, GROUP_W), F32),
                        pltpu.VMEM((t, N_HEADS * t), F32), pltpu.VMEM((4, t, GROUP_W), F32)],
        compiler_params=_cparams(("arbitrary",)),
        name="retention",
    )(pc, pc, pc, pc, lgl, lgc, *s_args)


HGRN_DIRECT = 8
HGRN_BLOCKS = (128, 64, 32, 16)
HGRN_SAFE_EXP2 = 80.0


def _split2(x):
    hi = x.astype(BF16)
    return hi, (x - hi.astype(F32)).astype(BF16)


def _hgrn_blocks_levels(t):
    blocks = tuple(w for w in HGRN_BLOCKS if w <= t)
    levels = []
    s = t // 2
    while s >= HGRN_DIRECT:
        levels.append(s)
        s //= 2
    return blocks, levels


def _hgrn_kernel(q_ref, zf_ref, zb_ref, v_ref, g_ref, lb_ref, *rest, slen, t, has_s0):
    if has_s0:
        s0_ref, o_ref, o_scr, st_scr, b_scr, k_scr, mask_scr = rest
    else:
        o_ref, sfin_ref, o_scr, st_scr, b_scr, k_scr, mask_scr = rest
    nc = slen // t
    blocks, levels = _hgrn_blocks_levels(t)
    masks = _head_masks(t)
    bmask = _block_ones()
    ones_bf16 = jnp.where(bmask, 1.0, 0.0).astype(BF16)
    r_i = lax.broadcasted_iota(jnp.int32, (t, t), 0)
    c_i = lax.broadcasted_iota(jnp.int32, (t, t), 1)
    sub = lax.broadcasted_iota(jnp.int32, (t, GROUP_W), 0) & (HGRN_DIRECT - 1)

    @pl.when(pl.program_id(0) == 0)
    def _():
        r_w = lax.broadcasted_iota(jnp.int32, (t, N_HEADS * t), 0)
        c_w = lax.broadcasted_iota(jnp.int32, (t, N_HEADS * t), 1) & (t - 1)
        for li, s in enumerate(levels):
            shift = (2 * s).bit_length() - 1
            mask_scr[li] = jnp.where((r_w >> shift) == (c_w >> shift), 1.0, 0.0)
        for direction in range(2):
            for wi, w in enumerate(blocks):
                shift = w.bit_length() - 1
                causal = (c_w <= r_w) if direction == 0 else (c_w >= r_w)
                mask_scr[len(levels) + direction * len(blocks) + wi] = jnp.where(
                    ((r_w >> shift) == (c_w >> shift)) & causal, 1.0, 0.0)

    def block_roll(x, shift):
        x3 = x.reshape(t // HGRN_DIRECT, HGRN_DIRECT, GROUP_W)
        return pltpu.roll(x3, shift, 1).reshape(t, GROUP_W)

    def direction_fns(direction, z_ref):
        fwd = direction == 0
        lbd = jnp.maximum(lb_ref[direction], LB_FLOOR)
        tri = jnp.where((r_i >= c_i) if fwd else (r_i <= c_i), 1.0, 0.0).astype(BF16)
        near_mask_idx = {w: len(levels) + direction * len(blocks) + wi for wi, w in enumerate(blocks)}
        if has_s0:
            st_scr[direction] = _expand_state(s0_ref[0, direction], bmask)
        else:
            st_scr[direction] = jnp.zeros((GROUP_W, GROUP_W), F32)

        def gates(c, worst):
            rows = pl.ds(pl.multiple_of(c * t, t), t)
            z = z_ref[0, rows, :]
            a = jnp.exp(-jnp.abs(z))
            inv = 1.0 / (1.0 + a)
            pos = z >= 0
            sig = jnp.where(pos, inv, a * inv)
            nsig = jnp.where(pos, a * inv, inv)
            lf = jnp.log(lbd + (1.0 - lbd) * sig)
            k_scr[direction, rows, :] = (1.0 - lbd) * nsig
            h1, h2 = _split2(lf)
            b = (jnp.dot(tri, h1, preferred_element_type=F32) + jnp.dot(tri, h2, preferred_element_type=F32)) * LOG2E
            b_scr[direction, rows, :] = b
            worst = list(worst)
            for wi, w in enumerate(blocks):
                for gi in range(t // w):
                    first = b[gi * w:gi * w + 1]
                    last = b[(gi + 1) * w - 1:(gi + 1) * w]
                    worst[wi] = jnp.maximum(worst[wi], jnp.abs(first - last))
            return tuple(worst)

        def chunk(c, near_w):
            rows = pl.ds(pl.multiple_of(c * t, t), t)
            q = _silu(q_ref[0, rows, :])
            v = v_ref[0, rows, :]
            vb = v.astype(BF16)
            k = k_scr[direction, rows, :]
            b = b_scr[direction, rows, :]
            vs = _stack_heads(v, masks).astype(BF16)

            def level_scores(li, s):
                pa, pb = [], []
                zero = jnp.zeros((s, GROUP_W), F32)
                for gi in range(t // (2 * s)):
                    b_lo = b[gi * 2 * s:gi * 2 * s + s]
                    b_hi = b[gi * 2 * s + s:(gi + 1) * 2 * s]
                    if fwd:
                        anc = b_hi[0:1]
                        pa += [zero, jnp.exp2(b_hi - anc)]
                        pb += [jnp.exp2(anc - b_lo), zero]
                    else:
                        anc = b_lo[s - 1:s]
                        pa += [jnp.exp2(b_lo - anc), zero]
                        pb += [zero, jnp.exp2(anc - b_hi)]
                qa = (q * jnp.concatenate(pa, axis=0)).astype(BF16)
                kb = _stack_heads(k * jnp.concatenate(pb, axis=0), masks).astype(BF16)
                return lax.dot_general(qa, kb, NT_DIMS, preferred_element_type=F32) * mask_scr[li]

            def direct_pairs():
                prods, vals = [(q * k).astype(BF16)], [v]
                for dlt in range(1, HGRN_DIRECT):
                    sh = dlt if fwd else HGRN_DIRECT - dlt
                    ok = (sub >= dlt) if fwd else (sub + dlt <= HGRN_DIRECT - 1)
                    w = jnp.exp2(jnp.minimum(b - block_roll(b, sh), 0.0))
                    prods.append(jnp.where(ok, q * block_roll(k, sh) * w, 0.0).astype(BF16))
                    vals.append(block_roll(v, sh))
                hs = jnp.dot(jnp.concatenate(prods, axis=0), ones_bf16, preferred_element_type=F32)
                out = hs[0:t] * vals[0]
                for dlt in range(1, HGRN_DIRECT):
                    out = out + hs[dlt * t:(dlt + 1) * t] * vals[dlt]
                return out

            sc = None
            for li, s in enumerate(levels):
                if s >= max(near_w, HGRN_DIRECT):
                    term = level_scores(li, s)
                    sc = term if sc is None else sc + term

            if near_w:
                ea, eb = [], []
                for gi in range(t // near_w):
                    blk = b[gi * near_w:(gi + 1) * near_w]
                    anc = blk[0:1] if fwd else blk[near_w - 1:near_w]
                    ea.append(jnp.exp2(blk - anc))
                    eb.append(jnp.exp2(anc - blk))
                qa = (q * jnp.concatenate(ea, axis=0)).astype(BF16)
                kb = _stack_heads(k * jnp.concatenate(eb, axis=0), masks).astype(BF16)
                term = lax.dot_general(qa, kb, NT_DIMS, preferred_element_type=F32) * mask_scr[near_mask_idx[near_w]]
                sc = term if sc is None else sc + term
                o = jnp.dot(sc.astype(BF16), vs, preferred_element_type=F32)
            else:
                o = jnp.dot(sc.astype(BF16), vs, preferred_element_type=F32) + direct_pairs()

            st = st_scr[direction]
            o = o + lax.dot_general((q * jnp.exp2(b)).astype(BF16), st.astype(BF16), NT_DIMS,
                                    preferred_element_type=F32)
            bl = b[t - 1:t] if fwd else b[0:1]
            upd = lax.dot_general(vb, (k * jnp.exp2(bl - b)).astype(BF16), TN_DIMS,
                                  preferred_element_type=F32)
            st_scr[direction] = st * jnp.exp2(bl) + jnp.where(bmask, upd, 0.0)
            o_scr[direction, rows, :] = o

        return gates, chunk

    gates_f, chunk_f = direction_fns(0, zf_ref)
    gates_b, chunk_b = direction_fns(1, zb_ref)

    worst = lax.fori_loop(0, nc, lambda c, w: gates_b(c, gates_f(c, w)),
                          tuple(jnp.zeros((1, GROUP_W), F32) for _ in blocks), unroll=2)
    safe = [jnp.max(w) < HGRN_SAFE_EXP2 for w in worst]

    def sweep(ci, carry, *, near_w):
        chunk_f(ci, near_w)
        chunk_b(nc - 1 - ci, near_w)
        return carry

    taken = False
    for w, ok in zip(blocks, safe):
        cond = ok if taken is False else jnp.logical_and(jnp.logical_not(taken), ok)

        @pl.when(cond)
        def _(w=w):
            lax.fori_loop(0, nc, functools.partial(sweep, near_w=w), 0, unroll=4)

        taken = ok if taken is False else jnp.logical_or(taken, ok)

    @pl.when(jnp.logical_not(taken))
    def _():
        lax.fori_loop(0, nc, functools.partial(sweep, near_w=0), 0)

    if not has_s0:
        for direction in range(2):
            sfin_ref[0, direction] = _compact_state(st_scr[direction], bmask)

    def finish(c, carry):
        rows = pl.ds(pl.multiple_of(c * t, t), t)
        o = o_scr[0, rows, :] + o_scr[1, rows, :]
        ms = _head_sum(o * o, ones_bf16) * (1.0 / HEAD_DIM)
        y = o * lax.rsqrt(ms + EPS)
        o_ref[0, rows, :] = (y * _silu(g_ref[0, rows, :])).astype(o_ref.dtype)
        return carry

    lax.fori_loop(0, nc, finish, 0, unroll=2)


HGRN_CHUNK = 128


def _hgrn(pd, lb, s0=None, t=HGRN_CHUNK):
    bsz, slen, _ = pd.shape
    col = lambda j: pl.BlockSpec((1, slen, GROUP_W), lambda b: (b, 0, j))
    s_in_specs, s_args, s_out_specs, s_out_shape = _state_specs(bsz, s0)
    blocks, levels = _hgrn_blocks_levels(t)
    n_masks = len(levels) + 2 * len(blocks)
    return pl.pallas_call(
        functools.partial(_hgrn_kernel, slen=slen, t=t, has_s0=s0 is not None),
        grid=(bsz,),
        in_specs=[col(0), col(1), col(2), col(3), col(4),
                  pl.BlockSpec((2, 1, GROUP_W), lambda b: (0, 0, 0))] + s_in_specs,
        out_specs=[pl.BlockSpec((1, slen, GROUP_W), lambda b: (b, 0, 0))] + s_out_specs,
        out_shape=[jax.ShapeDtypeStruct((bsz, slen, GROUP_W), BF16)] + s_out_shape,
        scratch_shapes=[pltpu.VMEM((2, slen, GROUP_W), F32), pltpu.VMEM((2, GROUP_W, GROUP_W), F32),
                        pltpu.VMEM((2, slen, GROUP_W), F32), pltpu.VMEM((2, slen, GROUP_W), F32),
                        pltpu.VMEM((n_masks, t, N_HEADS * t), F32)],
        compiler_params=_cparams(("arbitrary",)),
        name="hgrn2",
    )(pd, pd, pd, pd, pd, lb.reshape(2, 1, GROUP_W), *s_args)


def _outproj_kernel(ma_ref, mb_ref, mc_ref, md_ref, w_ref, x_ref, gate_ref, g_ref, o_ref, *, layer, mod_row):
    g = GROUP_W
    y = jnp.dot(ma_ref[0], w_ref[0, 0:g, :], preferred_element_type=F32)
    y = y + jnp.dot(mb_ref[0], w_ref[0, g:2 * g, :], preferred_element_type=F32)
    y = y + jnp.dot(mc_ref[0], w_ref[0, 2 * g:3 * g, :], preferred_element_type=F32)
    y = y + jnp.dot(md_ref[0], w_ref[0, 3 * g:4 * g, :], preferred_element_type=F32)
    ms = jnp.mean(y * y, axis=-1, keepdims=True)
    r = y * lax.rsqrt(ms + EPS) * g_ref[layer:layer + 1, :]
    o_ref[0] = x_ref[0] + _mod_row(gate_ref, mod_row) * r


OUTPROJ_TM = 1024


def _out_proj(mixed, w_out_bf16, x, mods, layer, mod_row, g_post):
    bsz, slen, d = x.shape
    depth = g_post.shape[0]
    tm = min(OUTPROJ_TM, slen)
    mspec = pl.BlockSpec((1, tm, GROUP_W), lambda b, i: (b, i, 0))
    return pl.pallas_call(
        functools.partial(_outproj_kernel, layer=layer, mod_row=mod_row),
        grid=(bsz, slen // tm),
        in_specs=[mspec, mspec, mspec, mspec,
                  pl.BlockSpec((1, d, d), lambda b, i: (layer, 0, 0)),
                  pl.BlockSpec((1, tm, d), lambda b, i: (b, i, 0)),
                  _mod_spec(2, layer),
                  pl.BlockSpec((depth, d), lambda b, i: (0, 0))],
        out_specs=pl.BlockSpec((1, tm, d), lambda b, i: (b, i, 0)),
        out_shape=jax.ShapeDtypeStruct((bsz, slen, d), F32),
        compiler_params=_cparams(("arbitrary", "arbitrary")),
        name="out_proj",
    )(*mixed, w_out_bf16, x, mods, g_post)


def _context_layer(x, mods, layer, mod_row, g_pre, g_post, w_in, w_out, sink, log_gamma, lb):
    bsz, slen, _ = x.shape
    pa, pb, pc, pd, ga, gb, ak, av, bk, bv = _in_proj(x, mods, layer, mod_row, g_pre, w_in, rope=False,
                                                      emit_kv=True)
    o_a, = _attention(_attn_operands(pa, 0, 2, 3, ga, n_kv=A_KV, qb=slen, kw=slen, back=0, sink=sink))
    o_b, = _attention(_attn_operands(pb, 0, 1, 2, gb, n_kv=N_HEADS, qb=slen, kw=slen, back=0))
    o_c, s_c = _retention(pc, log_gamma)
    o_d, s_d = _hgrn(pd, lb)
    x = _out_proj((o_a, o_b, o_c, o_d), w_out, x, mods, layer, mod_row, g_post)
    kv4 = lambda t: t.reshape(bsz, slen, -1, HEAD_DIM)
    return x, (kv4(ak), kv4(av), kv4(bk), kv4(bv), _uncompact_states(s_c), _uncompact_states(s_d))


def _latent_layer(x, mods, layer, ca_k, ca_v, cb_k, cb_v, st_c, st_d, g_pre, g_post, w_in, w_out, sink, win_tab,
                  na_tabs, log_gamma, lb):
    pa, pb, pc, pd, ga, gb = _in_proj(x, mods, layer, None, g_pre, w_in, rope=True)
    o_a, o_b = _attention(
        _attn_operands(pa, 0, 2, 3, ga, n_kv=A_KV, qb=WIN_QB, kw=WIN_KW, back=WIN_BACK,
                       ctx_k=ca_k, ctx_v=ca_v, table=win_tab, sink=sink, layer=layer),
        _attn_operands(pb, 0, 1, 2, gb, n_kv=N_HEADS, qb=NA_QB, kw=NA_KW, back=NA_QB,
                       ctx_k=cb_k, ctx_v=cb_v, table=na_tabs, layer=layer))
    o_c, = _retention(pc, log_gamma, st_c)
    o_d, = _hgrn(pd, lb, st_d)
    return _out_proj((o_a, o_b, o_c, o_d), w_out, x, mods, layer, None, g_post)


def kernel(x_prompt, x_sample, c, cache_win_k, cache_win_v, cache_na_k, cache_na_v, state_ret, state_hgrn,
           c_ctx, w_ada, b_ada, g_pre, g_post, w_in, w_out, attn_sink, na_rpb, ret_decay_logit, hgrn_lb_logit):
    depth = w_ada.shape[0]
    dec_b, dec_s, d = x_sample.shape
    p_lb = jax.nn.softmax(hgrn_lb_logit.astype(F32), axis=0)
    lower_bounds = jnp.cumsum(p_lb, axis=0) - p_lb[0:1]
    log_gammas = jax.nn.log_sigmoid(ret_decay_logit.astype(F32))
    w_in_b = w_in.astype(BF16)
    w_out_b = w_out.astype(BF16)

    cvecs = jnp.zeros((ADA_ROWS, d), F32).at[:dec_b].set(c).at[dec_b].set(c_ctx)
    mods = _adaln(cvecs, w_ada, b_ada)
    win_tab = _window_table(dec_s)
    na_tabs = _na_tables(na_rpb, dec_s)

    x = x_prompt
    outs = [[] for _ in range(6)]
    for l in range(depth):
        x, extra = _context_layer(x, mods, l, dec_b, g_pre, g_post, w_in_b, w_out_b,
                                  attn_sink[l], log_gammas[l], lower_bounds[l])
        for acc, e in zip(outs, extra):
            acc.append(e)
    y_prompt = x
    stacked = [jnp.stack(o, axis=1) for o in outs]

    past = cache_win_k.shape[2]
    ca_k = cache_win_k.reshape(dec_b, depth, past, A_KV * HEAD_DIM)
    ca_v = cache_win_v.reshape(dec_b, depth, past, A_KV * HEAD_DIM)
    cb_k = cache_na_k.reshape(dec_b, depth, past, GROUP_W)
    cb_v = cache_na_v.reshape(dec_b, depth, past, GROUP_W)
    st_c = _compact_states(state_ret)
    st_d = _compact_states(state_hgrn)
    x = x_sample
    for l in range(depth):
        x = _latent_layer(x, mods, l, ca_k, ca_v, cb_k, cb_v, st_c[:, l], st_d[:, l], g_pre, g_post,
                          w_in_b, w_out_b, attn_sink[l], win_tab, na_tabs, log_gammas[l], lower_bounds[l])
    return (y_prompt, x, *stacked)
```

```python
import functools

import numpy as np
import jax
import jax.numpy as jnp
from jax import lax
from jax.experimental import pallas as pl
from jax.experimental.pallas import tpu as pltpu

F32 = jnp.float32
BF16 = jnp.bfloat16

D_MODEL = 1024
DEPTH = 4
GRID_W = 64
HEAD_DIM = 64
N_HEADS = 4
GROUP_W = N_HEADS * HEAD_DIM
A_KV = 2
WINDOW = 128
NA_ROWS = 8
NA_COLS = 16
ROPE_BASE = 10000.0
EPS = 1e-6
NEG = -1e30
LB_FLOOR = 1e-30
LOG2E = 1.4426950408889634
W_A = 3 * GROUP_W
W_B = 4 * GROUP_W
W_C = 4 * GROUP_W
W_D = 5 * GROUP_W
IN_WIDTH = W_A + W_B + W_C + W_D

V7X_VMEM_LIMIT_BYTES = 56 * 1024 * 1024
ADA_ROWS = 16

NT_DIMS = (((1,), (1,)), ((), ()))
TN_DIMS = (((0,), (0,)), ((), ()))


def _cparams(sem, flags=None):
    return pltpu.CompilerParams(dimension_semantics=sem, vmem_limit_bytes=V7X_VMEM_LIMIT_BYTES, flags=flags)


def _sigmoid(x):
    return 1.0 / (1.0 + jnp.exp(-x))


def _silu(x):
    return x * _sigmoid(x)


def _head_masks(rows):
    lane = lax.broadcasted_iota(jnp.int32, (rows, GROUP_W), 1)
    return [(lane >= h * HEAD_DIM) & (lane < (h + 1) * HEAD_DIM) for h in range(N_HEADS)]


def _stack_heads(x, masks):
    return jnp.concatenate([jnp.where(m, x, 0.0) for m in masks], axis=0)


def _block_ones():
    shift = HEAD_DIM.bit_length() - 1
    r = lax.broadcasted_iota(jnp.int32, (GROUP_W, GROUP_W), 0) >> shift
    c = lax.broadcasted_iota(jnp.int32, (GROUP_W, GROUP_W), 1) >> shift
    return r == c


def _head_sum(x, ones_bf16):
    hi = x.astype(BF16)
    lo = (x - hi.astype(F32)).astype(BF16)
    return (jnp.dot(hi, ones_bf16, preferred_element_type=F32)
            + jnp.dot(lo, ones_bf16, preferred_element_type=F32))


def _adaln_kernel(c_ref, w_ref, b_ref, o_ref):
    s = _silu(c_ref[...]).astype(BF16)
    o_ref[0, 0] = jnp.dot(s, w_ref[0].astype(BF16), preferred_element_type=F32) + b_ref[0]


def _adaln(cvecs, w_ada, b_ada):
    depth, d, d3 = w_ada.shape
    tn = 512
    per = d // tn
    return pl.pallas_call(
        _adaln_kernel,
        grid=(depth, d3 // tn),
        in_specs=[pl.BlockSpec((ADA_ROWS, d), lambda l, j: (0, 0)),
                  pl.BlockSpec((1, d, tn), lambda l, j: (l, 0, j)),
                  pl.BlockSpec((1, 1, tn), lambda l, j: (l, 0, j))],
        out_specs=pl.BlockSpec((1, 1, ADA_ROWS, tn), lambda l, j: (l, j // per, 0, j % per)),
        out_shape=jax.ShapeDtypeStruct((depth, 3, ADA_ROWS, d), F32),
        compiler_params=_cparams(("arbitrary", "arbitrary")),
        name="adaln",
    )(cvecs, w_ada, b_ada.reshape(depth, 1, d3))


def _rope(x, cos, sin):
    w = x.shape[-1]
    lane = lax.broadcasted_iota(jnp.int32, x.shape, 1)
    first = (lane & 31) < 16
    swapped = jnp.where(first, pltpu.roll(x, w - 16, 1), pltpu.roll(x, 16, 1))
    return x * cos[:, :w] + swapped * sin[:, :w]


def _mod_row(mod_ref, mod_row):
    row = pl.program_id(0) if mod_row is None else mod_row
    return mod_ref[0, 0, pl.ds(row, 1), :]


def _inproj_kernel(*refs, rope, emit_kv, layer, mod_row):
    refs = list(refs)
    x_ref, shift_ref, scale_ref, g_ref, w_ref = refs[:5]
    if rope:
        cos_ref, sin_ref = refs[5:7]
    n_out = 8 if emit_kv else 4
    pa_ref, pb_ref, pc_ref, pd_ref = refs[-n_out:][:4]
    x = x_ref[0]
    ms = jnp.mean(x * x, axis=-1, keepdims=True)
    y = x * lax.rsqrt(ms + EPS) * g_ref[layer:layer + 1, :]
    h = (y * (1.0 + _mod_row(scale_ref, mod_row)) + _mod_row(shift_ref, mod_row)).astype(BF16)

    def mm(c0, c1):
        return jnp.dot(h, w_ref[0, :, c0:c1], preferred_element_type=F32)

    if rope:
        cos = cos_ref[...]
        sin = sin_ref[...]
    g = GROUP_W
    aq = mm(0, g)
    akv = mm(g, 2 * g)
    if rope:
        aq = _rope(aq, cos, sin)
        ak = _rope(akv[:, :g // 2], cos, sin)
        akv = jnp.concatenate([ak, akv[:, g // 2:]], axis=1)
    pa_ref[0, :, 0:g] = aq.astype(pa_ref.dtype)
    pa_ref[0, :, g:2 * g] = akv.astype(pa_ref.dtype)
    pa_ref[0, :, 2 * g:3 * g] = mm(2 * g, 3 * g).astype(pa_ref.dtype)
    pbv = mm(W_A, W_A + W_B)
    pb_ref[0] = pbv.astype(pb_ref.dtype)
    if emit_kv:
        ak_ref, av_ref, bk_ref, bv_ref = refs[-4:]
        ak_ref[0] = akv[:, :g // 2]
        av_ref[0] = akv[:, g // 2:]
        bk_ref[0] = pbv[:, g:2 * g]
        bv_ref[0] = pbv[:, 2 * g:3 * g]
    c0 = W_A + W_B
    cq = mm(c0, c0 + g)
    ck = mm(c0 + g, c0 + 2 * g)
    if rope:
        cq = _rope(cq, cos, sin)
        ck = _rope(ck, cos, sin)
    pc_ref[0, :, 0:g] = cq.astype(pc_ref.dtype)
    pc_ref[0, :, g:2 * g] = ck.astype(pc_ref.dtype)
    pc_ref[0, :, 2 * g:4 * g] = mm(c0 + 2 * g, c0 + 4 * g).astype(pc_ref.dtype)
    d0 = c0 + W_C
    pd_ref[0] = mm(d0, d0 + W_D).astype(pd_ref.dtype)


def _rope_tables(slen):
    t = np.arange(slen)
    nf = HEAD_DIM // 4
    freqs = ROPE_BASE ** (-np.arange(nf, dtype=np.float64) / nf)
    d = np.arange(HEAD_DIM)
    pos = np.where(d[None, :] < HEAD_DIM // 2, (t // GRID_W)[:, None], (t % GRID_W)[:, None])
    ang = pos * freqs[d % nf][None, :]
    sign = np.where((d % (2 * nf)) < nf, -1.0, 1.0)[None, :]
    cos = np.tile(np.cos(ang), (1, N_HEADS))
    sin = np.tile(np.sin(ang) * sign, (1, N_HEADS))
    return jnp.asarray(cos, F32), jnp.asarray(sin, F32)


def _mod_spec(which, layer):
    return pl.BlockSpec((1, 1, ADA_ROWS, D_MODEL), lambda b, i: (layer, which, 0, 0))


INPROJ_TM = 512


def _in_proj(x, mods, layer, mod_row, g_pre, w_in_bf16, rope, emit_kv=False):
    bsz, slen, d = x.shape
    tm = min(INPROJ_TM, slen)
    depth = g_pre.shape[0]
    in_specs = [pl.BlockSpec((1, tm, d), lambda b, i: (b, i, 0)),
                _mod_spec(0, layer),
                _mod_spec(1, layer),
                pl.BlockSpec((depth, d), lambda b, i: (0, 0)),
                pl.BlockSpec((1, d, IN_WIDTH), lambda b, i: (layer, 0, 0))]
    args = [x, mods, mods, g_pre, w_in_bf16]
    if rope:
        cos, sin = _rope_tables(slen)
        in_specs += [pl.BlockSpec((tm, GROUP_W), lambda b, i: (i, 0))] * 2
        args += [cos, sin]
    widths = (W_A, W_B, W_C, W_D)
    if emit_kv:
        widths += (A_KV * HEAD_DIM, A_KV * HEAD_DIM, GROUP_W, GROUP_W)
    return pl.pallas_call(
        functools.partial(_inproj_kernel, rope=rope, emit_kv=emit_kv, layer=layer, mod_row=mod_row),
        grid=(bsz, slen // tm),
        in_specs=in_specs,
        out_specs=[pl.BlockSpec((1, tm, w), lambda b, i: (b, i, 0)) for w in widths],
        out_shape=[jax.ShapeDtypeStruct((bsz, slen, w), F32) for w in widths],
        compiler_params=_cparams(("arbitrary", "arbitrary")),
        name="in_proj",
    )(*args)


ATTN_ROWS = 256


def _attn_kernel(*refs, cfgs):
    n_in = [4 + 2 * c["has_ctx"] + (c["n_tab"] > 0) + c["has_sink"] for c in cfgs]
    outs = refs[sum(n_in):]
    start = 0
    for c, n, o_ref in zip(cfgs, n_in, outs):
        _attn_body(refs[start:start + n], o_ref, **c)
        start += n


def _attn_body(in_refs, o_ref, *, n_kv, qb, kw, back, slen, has_ctx, n_tab, tab_heads, has_sink):
    it = iter(in_refs)
    q_ref, k_ref, v_ref, g_ref = next(it), next(it), next(it), next(it)
    kc_ref = vc_ref = tab_ref = sink_ref = None
    if has_ctx:
        kc_ref, vc_ref = next(it), next(it)
    if n_tab:
        tab_ref = next(it)
    if has_sink:
        sink_ref = next(it)

    n = pl.program_id(1)
    nblk = slen // qb
    ws = pl.multiple_of(jnp.clip(n * qb - back, 0, slen - kw), HEAD_DIM)
    if n_tab == 3:
        tix = jnp.where(n == 0, 0, jnp.where(n == nblk - 1, 2, 1))
    else:
        tix = 0
    grp = N_HEADS // n_kv
    qscale = HEAD_DIM ** -0.5 * LOG2E
    pair_w = 2 * HEAD_DIM
    lane = lax.broadcasted_iota(jnp.int32, (1, pair_w), 1)
    half_mask = (lane < HEAD_DIM, lane >= HEAD_DIM)
    rb = min(qb, ATTN_ROWS)
    for j in range(N_HEADS // 2):
        cols = slice(j * pair_w, (j + 1) * pair_w)
        if grp == 1:
            kv_cols, kv_half = cols, (0, 1)
        else:
            kv_cols, kv_half = slice(0, pair_w), (j, j)
        kall = k_ref[0, pl.ds(ws, kw), kv_cols].astype(BF16)
        vall = v_ref[0, pl.ds(ws, kw), kv_cols]
        if has_ctx:
            kall = jnp.concatenate([kall, kc_ref[0, 0, :, kv_cols].astype(BF16)], axis=0)
            vall = jnp.concatenate([vall, vc_ref[0, 0, :, kv_cols]], axis=0)
        vaug = {hf: jnp.where(half_mask[hf], vall, 1.0).astype(BF16) for hf in set(kv_half)}
        for r0 in range(0, qb, rb):
            rs = slice(r0, r0 + rb)
            q128 = q_ref[0, rs, cols] * qscale
            q_other = pltpu.roll(q128, HEAD_DIM, 1) if kv_half != (0, 1) else None
            outs = []
            for i in range(2):
                hf = kv_half[i]
                qm = jnp.where(half_mask[hf], q128 if i == hf else q_other, 0.0).astype(BF16)
                s = lax.dot_general(qm, kall, NT_DIMS, preferred_element_type=F32)
                if n_tab:
                    tab = tab_ref[0, tix, 2 * j + i if tab_heads == N_HEADS else 0, rs, :]
                    s = jnp.concatenate([s[:, :kw] + tab, s[:, kw:]], axis=1) if has_ctx else s + tab
                m = jnp.max(s, axis=-1, keepdims=True)
                if has_sink:
                    sink = sink_ref[2 * j + i] * LOG2E
                    m = jnp.maximum(m, sink)
                p = jnp.exp2(s - m).astype(BF16)
                acc = jnp.dot(p, vaug[hf], preferred_element_type=F32)
                if has_sink:
                    acc = acc + jnp.where(half_mask[1 - hf], jnp.exp2(sink - m), 0.0)
                acc = acc / pltpu.roll(acc, HEAD_DIM, 1)
                outs.append(acc if i == hf else pltpu.roll(acc, HEAD_DIM, 1))
            out = jnp.where(half_mask[0], outs[0], outs[1]) * _silu(g_ref[0, rs, cols])
            o_ref[0, rs, cols] = out.astype(o_ref.dtype)


def _attn_operands(p, q_col, k_col, v_col, g_col, n_kv, qb, kw, back, ctx_k=None, ctx_v=None,
                   table=None, sink=None, layer=0):
    slen = p.shape[1]
    kvw = n_kv * HEAD_DIM
    in_specs = [pl.BlockSpec((1, qb, GROUP_W), lambda b, n: (b, n, q_col)),
                pl.BlockSpec((1, slen, kvw), lambda b, n: (b, 0, k_col)),
                pl.BlockSpec((1, slen, kvw), lambda b, n: (b, 0, v_col)),
                pl.BlockSpec((1, qb, GROUP_W), lambda b, n: (b, n, g_col))]
    args = [p, p, p, p]
    has_ctx = ctx_k is not None
    if has_ctx:
        past = ctx_k.shape[2]
        in_specs += [pl.BlockSpec((1, 1, past, kvw), lambda b, n: (b, layer, 0, 0))] * 2
        args += [ctx_k, ctx_v]
    n_tab = tab_heads = 0
    if table is not None:
        n_tab, tab_heads = table.shape[1], table.shape[2]
        tab_layer = layer if table.shape[0] > 1 else 0
        in_specs.append(pl.BlockSpec((1,) + table.shape[1:], lambda b, n: (tab_layer, 0, 0, 0, 0)))
        args.append(table)
    if sink is not None:
        in_specs.append(pl.BlockSpec(memory_space=pltpu.SMEM))
        args.append(sink)
    cfg = dict(n_kv=n_kv, qb=qb, kw=kw, back=back, slen=slen, has_ctx=has_ctx, n_tab=n_tab,
               tab_heads=tab_heads, has_sink=sink is not None)
    return in_specs, args, cfg


def _attention(*operands):
    in_specs = [s for op in operands for s in op[0]]
    args = [a for op in operands for a in op[1]]
    cfgs = tuple(op[2] for op in operands)
    bsz = args[0].shape[0]
    slen, qb = cfgs[0]["slen"], cfgs[0]["qb"]
    assert all(c["slen"] == slen and c["qb"] == qb for c in cfgs)
    return pl.pallas_call(
        functools.partial(_attn_kernel, cfgs=cfgs),
        grid=(bsz, slen // qb),
        in_specs=in_specs,
        out_specs=[pl.BlockSpec((1, qb, GROUP_W), lambda b, n: (b, n, 0)) for _ in cfgs],
        out_shape=[jax.ShapeDtypeStruct((bsz, slen, GROUP_W), BF16) for _ in cfgs],
        compiler_params=_cparams(("arbitrary", "arbitrary")),
        name="attn",
    )(*args)


WIN_QB = 256
WIN_BACK = WINDOW
WIN_KW = WIN_QB + 2 * WINDOW
NA_QROWS = 4
NA_KROWS = 12
NA_QB = NA_QROWS * GRID_W
NA_KW = NA_KROWS * GRID_W


def _window_table(slen):
    nblk = slen // WIN_QB
    tabs = []
    for n in (0, 1, nblk - 1):
        ws = int(np.clip(n * WIN_QB - WIN_BACK, 0, slen - WIN_KW))
        qpos = n * WIN_QB + np.arange(WIN_QB)[:, None]
        kpos = ws + np.arange(WIN_KW)[None, :]
        tabs.append(np.where(np.abs(qpos - kpos) <= WINDOW, 0.0, NEG))
    return jnp.asarray(np.stack(tabs)[None, :, None], F32)


N_RPB_R = 2 * NA_ROWS - 1
N_RPB_C = 2 * NA_COLS - 1


def _na_table_kernel(rpb_ref, o_ref, tz_scr, *, rows):
    base = (pl.program_id(0) * N_HEADS + pl.program_id(1)) * (N_RPB_R * N_RPB_C)
    qc = lax.broadcasted_iota(jnp.int32, (GRID_W, 2 * GRID_W), 0)
    kk = lax.broadcasted_iota(jnp.int32, (GRID_W, 2 * GRID_W), 1)
    kc = kk & (GRID_W - 1)
    diff = kc - qc
    qws = jnp.clip(qc - NA_COLS // 2, 0, GRID_W - NA_COLS)
    col_ok = (kc >= qws) & (kc < qws + NA_COLS)
    neg = jnp.full((GRID_W, 2 * GRID_W), NEG, F32)
    for dr in range(N_RPB_R):
        acc = neg
        for m in range(N_RPB_C):
            acc = jnp.where(diff == m - (NA_COLS - 1), rpb_ref[base + dr * N_RPB_C + m] * LOG2E, acc)
        tz_scr[dr] = jnp.where(col_ok, acc, NEG)
    wr = min(NA_ROWS, rows)
    nblk = rows // NA_QROWS
    for ti, g in enumerate((0, 1, nblk - 1)):
        ws_row = min(max(g * NA_QROWS - NA_QROWS, 0), rows - NA_KROWS)
        for qr in range(NA_QROWS):
            r = g * NA_QROWS + qr
            rs = min(max(r - wr // 2, 0), rows - wr)
            for p in range(NA_KROWS // 2):
                halves = []
                for kr in (ws_row + 2 * p, ws_row + 2 * p + 1):
                    halves.append(tz_scr[kr - r + NA_ROWS - 1] if rs <= kr < rs + wr else neg)
                o_ref[0, ti, 0, qr * GRID_W:(qr + 1) * GRID_W, p * 2 * GRID_W:(p + 1) * 2 * GRID_W] = (
                    jnp.where(kk < GRID_W, halves[0], halves[1]))


def _na_tables(na_rpb, slen):
    depth = na_rpb.shape[0]
    return pl.pallas_call(
        functools.partial(_na_table_kernel, rows=slen // GRID_W),
        grid=(depth, N_HEADS),
        in_specs=[pl.BlockSpec(memory_space=pltpu.SMEM)],
        out_specs=pl.BlockSpec((1, 3, 1, NA_QB, NA_KW), lambda l, h: (l, 0, h, 0, 0)),
        out_shape=jax.ShapeDtypeStruct((depth, 3, N_HEADS, NA_QB, NA_KW), F32),
        scratch_shapes=[pltpu.VMEM((N_RPB_R, GRID_W, 2 * GRID_W), F32)],
        compiler_params=_cparams(("arbitrary", "arbitrary")),
        name="na_table",
    )(na_rpb.astype(F32).reshape(-1))


def _ret_kernel(q_ref, k_ref, v_ref, g_ref, lgl_ref, lgc_ref, *rest, slen, t, has_s0):
    if has_s0:
        s0_ref, o_ref, sb_scr, st_scr, dm_scr, dec_scr = rest
    else:
        o_ref, sfin_ref, sb_scr, st_scr, dm_scr, dec_scr = rest
    nc = slen // t
    masks = _head_masks(t)
    bmask = _block_ones()
    ones_bf16 = jnp.where(bmask, 1.0, 0.0).astype(BF16)
    lgf, lgb = lgl_ref[0], lgl_ref[1]

    @pl.when(pl.program_id(0) == 0)
    def _():
        ii = lax.broadcasted_iota(jnp.int32, (t, N_HEADS * t), 0)
        jj = lax.broadcasted_iota(jnp.int32, (t, N_HEADS * t), 1) & (t - 1)
        dist = (ii - jj).astype(F32)
        dm_scr[...] = (jnp.where(dist >= 0, jnp.exp(dist * lgc_ref[0]), 0.0)
                       + jnp.where(dist <= 0, jnp.exp(-dist * lgc_ref[1]), 0.0))
        idx = lax.broadcasted_iota(jnp.int32, (t, GROUP_W), 0).astype(F32)
        dec_scr[0] = jnp.exp((idx + 1.0) * lgf)
        dec_scr[1] = jnp.exp((t - 1.0 - idx) * lgf)
        dec_scr[2] = jnp.exp((t - idx) * lgb)
        dec_scr[3] = jnp.exp(idx * lgb)

    cdec_f = jnp.exp(float(t) * lgf)
    cdec_b = jnp.exp(float(t) * lgb)
    kscale = HEAD_DIM ** -0.5

    def state_update(st, k, v, kdec, cdec):
        upd = lax.dot_general(v, (k * kdec).astype(BF16), TN_DIMS, preferred_element_type=F32)
        return st * cdec + jnp.where(bmask, upd, 0.0)

    def init_state(direction):
        if has_s0:
            st_scr[...] = _expand_state(s0_ref[0, direction], bmask)
        else:
            st_scr[...] = jnp.zeros((GROUP_W, GROUP_W), F32)

    def emit_final(direction):
        if not has_s0:
            sfin_ref[0, direction] = _compact_state(st_scr[...], bmask)

    init_state(1)

    def sweep_bwd(ci, carry):
        c = nc - 1 - ci
        rows = pl.ds(pl.multiple_of(c * t, t), t)
        st = st_scr[...]
        sb_scr[c] = st.astype(BF16)
        k = k_ref[0, rows, :] * kscale
        st_scr[...] = state_update(st, k, v_ref[0, rows, :].astype(BF16), dec_scr[3], cdec_b)
        return carry

    lax.fori_loop(0, nc, sweep_bwd, 0, unroll=4)
    emit_final(1)
    init_state(0)

    def sweep_fwd(c, carry):
        rows = pl.ds(pl.multiple_of(c * t, t), t)
        q = q_ref[0, rows, :]
        k = k_ref[0, rows, :] * kscale
        v = v_ref[0, rows, :]
        ks = _stack_heads(k, masks).astype(BF16)
        vs = _stack_heads(v, masks).astype(BF16)
        sc = lax.dot_general(q.astype(BF16), ks, NT_DIMS, preferred_element_type=F32) * dm_scr[...]
        o = jnp.dot(sc.astype(BF16), vs, preferred_element_type=F32)
        st = st_scr[...]
        o = o + lax.dot_general((q * dec_scr[0]).astype(BF16), st.astype(BF16), NT_DIMS,
                                preferred_element_type=F32)
        o = o + lax.dot_general((q * dec_scr[2]).astype(BF16), sb_scr[c], NT_DIMS, preferred_element_type=F32)
        st_scr[...] = state_update(st, k, v.astype(BF16), dec_scr[1], cdec_f)
        mu = _head_sum(o, ones_bf16) * (1.0 / HEAD_DIM)
        d = o - mu
        var = _head_sum(d * d, ones_bf16) * (1.0 / HEAD_DIM)
        y = d * lax.rsqrt(var + EPS)
        o_ref[0, rows, :] = (y * _silu(g_ref[0, rows, :])).astype(o_ref.dtype)
        return carry

    lax.fori_loop(0, nc, sweep_fwd, 0, unroll=4)
    emit_final(0)


def _compact_states(s0):
    return jnp.swapaxes(s0, -1, -2).reshape(s0.shape[:-3] + (GROUP_W, HEAD_DIM))


def _uncompact_states(sc):
    bsz = sc.shape[0]
    return jnp.swapaxes(sc.reshape(bsz, 2, N_HEADS, HEAD_DIM, HEAD_DIM), -1, -2)


def _split3(x):
    h1 = x.astype(BF16)
    r1 = x - h1.astype(F32)
    h2 = r1.astype(BF16)
    return h1, h2, (r1 - h2.astype(F32)).astype(BF16)


def _expand_state(x, bmask):
    r = lax.broadcasted_iota(jnp.int32, (HEAD_DIM, GROUP_W), 0)
    c = lax.broadcasted_iota(jnp.int32, (HEAD_DIM, GROUP_W), 1) & (HEAD_DIM - 1)
    rep = jnp.where(r == c, 1.0, 0.0).astype(BF16)
    h1, h2, h3 = _split3(x)
    tiled = (jnp.dot(h1, rep, preferred_element_type=F32) + jnp.dot(h2, rep, preferred_element_type=F32)
             + jnp.dot(h3, rep, preferred_element_type=F32))
    return jnp.where(bmask, tiled, 0.0)


def _compact_state(st, bmask):
    r = lax.broadcasted_iota(jnp.int32, (GROUP_W, HEAD_DIM), 0) & (HEAD_DIM - 1)
    c = lax.broadcasted_iota(jnp.int32, (GROUP_W, HEAD_DIM), 1)
    fold = jnp.where(r == c, 1.0, 0.0).astype(BF16)
    h1, h2, h3 = _split3(jnp.where(bmask, st, 0.0))
    return (jnp.dot(h1, fold, preferred_element_type=F32) + jnp.dot(h2, fold, preferred_element_type=F32)
            + jnp.dot(h3, fold, preferred_element_type=F32))


def _state_specs(bsz, s0):
    spec = pl.BlockSpec((1, 2, GROUP_W, HEAD_DIM), lambda b: (b, 0, 0, 0))
    shape = jax.ShapeDtypeStruct((bsz, 2, GROUP_W, HEAD_DIM), F32)
    if s0 is None:
        return [], [], [spec], [shape]
    return [spec], [s0], [], []


RET_CHUNK = 256


def _retention(pc, log_gamma, s0=None, t=RET_CHUNK):
    bsz, slen, _ = pc.shape
    t = min(t, slen)
    lgl = jnp.repeat(log_gamma, HEAD_DIM, axis=1).reshape(2, 1, GROUP_W)
    lgc = jnp.repeat(log_gamma, t, axis=1).reshape(2, 1, N_HEADS * t)
    col = lambda j: pl.BlockSpec((1, slen, GROUP_W), lambda b: (b, 0, j))
    s_in_specs, s_args, s_out_specs, s_out_shape = _state_specs(bsz, s0)
    return pl.pallas_call(
        functools.partial(_ret_kernel, slen=slen, t=t, has_s0=s0 is not None),
        grid=(bsz,),
        in_specs=[col(0), col(1), col(2), col(3),
                  pl.BlockSpec((2, 1, GROUP_W), lambda b: (0, 0, 0)),
                  pl.BlockSpec((2, 1, N_HEADS * t), lambda b: (0, 0, 0))] + s_in_specs,
        out_specs=[pl.BlockSpec((1, slen, GROUP_W), lambda b: (b, 0, 0))] + s_out_specs,
        out_shape=[jax.ShapeDtypeStruct((bsz, slen, GROUP_W), BF16)] + s_out_shape,
        scratch_shapes=[pltpu.VMEM((slen // t, GROUP_W, GROUP_W), BF16), pltpu.VMEM((GROUP_W, GROUP_W), F32),
                        pltpu.VMEM((t, N_HEADS * t), F32), pltpu.VMEM((4, t, GROUP_W), F32)],
        compiler_params=_cparams(("arbitrary",)),
        name="retention",
    )(pc, pc, pc, pc, lgl, lgc, *s_args)


HGRN_DIRECT = 8
HGRN_BLOCKS = (128, 64, 32, 16)
HGRN_SAFE_EXP2 = 80.0


def _split2(x):
    hi = x.astype(BF16)
    return hi, (x - hi.astype(F32)).astype(BF16)


def _hgrn_blocks_levels(t):
    blocks = tuple(w for w in HGRN_BLOCKS if w <= t)
    levels = []
    s = t // 2
    while s >= HGRN_DIRECT:
        levels.append(s)
        s //= 2
    return blocks, levels


def _hgrn_kernel(q_ref, zf_ref, zb_ref, v_ref, g_ref, lb_ref, *rest, slen, t, has_s0):
    if has_s0:
        s0_ref, o_ref, o_scr, st_scr, b_scr, k_scr, mask_scr = rest
    else:
        o_ref, sfin_ref, o_scr, st_scr, b_scr, k_scr, mask_scr = rest
    nc = slen // t
    blocks, levels = _hgrn_blocks_levels(t)
    masks = _head_masks(t)
    bmask = _block_ones()
    ones_bf16 = jnp.where(bmask, 1.0, 0.0).astype(BF16)
    r_i = lax.broadcasted_iota(jnp.int32, (t, t), 0)
    c_i = lax.broadcasted_iota(jnp.int32, (t, t), 1)
    sub = lax.broadcasted_iota(jnp.int32, (t, GROUP_W), 0) & (HGRN_DIRECT - 1)

    @pl.when(pl.program_id(0) == 0)
    def _():
        r_w = lax.broadcasted_iota(jnp.int32, (t, N_HEADS * t), 0)
        c_w = lax.broadcasted_iota(jnp.int32, (t, N_HEADS * t), 1) & (t - 1)
        for li, s in enumerate(levels):
            shift = (2 * s).bit_length() - 1
            mask_scr[li] = jnp.where((r_w >> shift) == (c_w >> shift), 1.0, 0.0)
        for direction in range(2):
            for wi, w in enumerate(blocks):
                shift = w.bit_length() - 1
                causal = (c_w <= r_w) if direction == 0 else (c_w >= r_w)
                mask_scr[len(levels) + direction * len(blocks) + wi] = jnp.where(
                    ((r_w >> shift) == (c_w >> shift)) & causal, 1.0, 0.0)

    def block_roll(x, shift):
        x3 = x.reshape(t // HGRN_DIRECT, HGRN_DIRECT, GROUP_W)
        return pltpu.roll(x3, shift, 1).reshape(t, GROUP_W)

    def direction_fns(direction, z_ref):
        fwd = direction == 0
        lbd = jnp.maximum(lb_ref[direction], LB_FLOOR)
        tri = jnp.where((r_i >= c_i) if fwd else (r_i <= c_i), 1.0, 0.0).astype(BF16)
        near_mask_idx = {w: len(levels) + direction * len(blocks) + wi for wi, w in enumerate(blocks)}
        if has_s0:
            st_scr[direction] = _expand_state(s0_ref[0, direction], bmask)
        else:
            st_scr[direction] = jnp.zeros((GROUP_W, GROUP_W), F32)

        def gates(c, worst):
            rows = pl.ds(pl.multiple_of(c * t, t), t)
            z = z_ref[0, rows, :]
            a = jnp.exp(-jnp.abs(z))
            inv = 1.0 / (1.0 + a)
            pos = z >= 0
            sig = jnp.where(pos, inv, a * inv)
            nsig = jnp.where(pos, a * inv, inv)
            lf = jnp.log(lbd + (1.0 - lbd) * sig)
            k_scr[direction, rows, :] = (1.0 - lbd) * nsig
            h1, h2 = _split2(lf)
            b = (jnp.dot(tri, h1, preferred_element_type=F32) + jnp.dot(tri, h2, preferred_element_type=F32)) * LOG2E
            b_scr[direction, rows, :] = b
            worst = list(worst)
            for wi, w in enumerate(blocks):
                for gi in range(t // w):
                    first = b[gi * w:gi * w + 1]
                    last = b[(gi + 1) * w - 1:(gi + 1) * w]
                    worst[wi] = jnp.maximum(worst[wi], jnp.abs(first - last))
            return tuple(worst)

        def chunk(c, near_w):
            rows = pl.ds(pl.multiple_of(c * t, t), t)
            q = _silu(q_ref[0, rows, :])
            v = v_ref[0, rows, :]
            vb = v.astype(BF16)
            k = k_scr[direction, rows, :]
            b = b_scr[direction, rows, :]
            vs = _stack_heads(v, masks).astype(BF16)

            def level_scores(li, s):
                pa, pb = [], []
                zero = jnp.zeros((s, GROUP_W), F32)
                for gi in range(t // (2 * s)):
                    b_lo = b[gi * 2 * s:gi * 2 * s + s]
                    b_hi = b[gi * 2 * s + s:(gi + 1) * 2 * s]
                    if fwd:
                        anc = b_hi[0:1]
                        pa += [zero, jnp.exp2(b_hi - anc)]
                        pb += [jnp.exp2(anc - b_lo), zero]
                    else:
                        anc = b_lo[s - 1:s]
                        pa += [jnp.exp2(b_lo - anc), zero]
                        pb += [zero, jnp.exp2(anc - b_hi)]
                qa = (q * jnp.concatenate(pa, axis=0)).astype(BF16)
                kb = _stack_heads(k * jnp.concatenate(pb, axis=0), masks).astype(BF16)
                return lax.dot_general(qa, kb, NT_DIMS, preferred_element_type=F32) * mask_scr[li]

            def direct_pairs():
                prods, vals = [(q * k).astype(BF16)], [v]
                for dlt in range(1, HGRN_DIRECT):
                    sh = dlt if fwd else HGRN_DIRECT - dlt
                    ok = (sub >= dlt) if fwd else (sub + dlt <= HGRN_DIRECT - 1)
                    w = jnp.exp2(jnp.minimum(b - block_roll(b, sh), 0.0))
                    prods.append(jnp.where(ok, q * block_roll(k, sh) * w, 0.0).astype(BF16))
                    vals.append(block_roll(v, sh))
                hs = jnp.dot(jnp.concatenate(prods, axis=0), ones_bf16, preferred_element_type=F32)
                out = hs[0:t] * vals[0]
                for dlt in range(1, HGRN_DIRECT):
                    out = out + hs[dlt * t:(dlt + 1) * t] * vals[dlt]
                return out

            sc = None
            for li, s in enumerate(levels):
                if s >= max(near_w, HGRN_DIRECT):
                    term = level_scores(li, s)
                    sc = term if sc is None else sc + term

            if near_w:
                ea, eb = [], []
                for gi in range(t // near_w):
                    blk = b[gi * near_w:(gi + 1) * near_w]
                    anc = blk[0:1] if fwd else blk[near_w - 1:near_w]
                    ea.append(jnp.exp2(blk - anc))
                    eb.append(jnp.exp2(anc - blk))
                qa = (q * jnp.concatenate(ea, axis=0)).astype(BF16)
                kb = _stack_heads(k * jnp.concatenate(eb, axis=0), masks).astype(BF16)
                term = lax.dot_general(qa, kb, NT_DIMS, preferred_element_type=F32) * mask_scr[near_mask_idx[near_w]]
                sc = term if sc is None else sc + term
                o = jnp.dot(sc.astype(BF16), vs, preferred_element_type=F32)
            else:
                o = jnp.dot(sc.astype(BF16), vs, preferred_element_type=F32) + direct_pairs()

            st = st_scr[direction]
            o = o + lax.dot_general((q * jnp.exp2(b)).astype(BF16), st.astype(BF16), NT_DIMS,
                                    preferred_element_type=F32)
            bl = b[t - 1:t] if fwd else b[0:1]
            upd = lax.dot_general(vb, (k * jnp.exp2(bl - b)).astype(BF16), TN_DIMS,
                                  preferred_element_type=F32)
            st_scr[direction] = st * jnp.exp2(bl) + jnp.where(bmask, upd, 0.0)
            o_scr[direction, rows, :] = o

        return gates, chunk

    gates_f, chunk_f = direction_fns(0, zf_ref)
    gates_b, chunk_b = direction_fns(1, zb_ref)

    worst = lax.fori_loop(0, nc, lambda c, w: gates_b(c, gates_f(c, w)),
                          tuple(jnp.zeros((1, GROUP_W), F32) for _ in blocks), unroll=4)
    safe = [jnp.max(w) < HGRN_SAFE_EXP2 for w in worst]

    def sweep(ci, carry, *, near_w):
        chunk_f(ci, near_w)
        chunk_b(nc - 1 - ci, near_w)
        return carry

    taken = False
    for w, ok in zip(blocks, safe):
        cond = ok if taken is False else jnp.logical_and(jnp.logical_not(taken), ok)

        @pl.when(cond)
        def _(w=w):
            lax.fori_loop(0, nc, functools.partial(sweep, near_w=w), 0, unroll=4)

        taken = ok if taken is False else jnp.logical_or(taken, ok)

    @pl.when(jnp.logical_not(taken))
    def _():
        lax.fori_loop(0, nc, functools.partial(sweep, near_w=0), 0)

    if not has_s0:
        for direction in range(2):
            sfin_ref[0, direction] = _compact_state(st_scr[direction], bmask)

    def finish(c, carry):
        rows = pl.ds(pl.multiple_of(c * t, t), t)
        o = o_scr[0, rows, :] + o_scr[1, rows, :]
        ms = _head_sum(o * o, ones_bf16) * (1.0 / HEAD_DIM)
        y = o * lax.rsqrt(ms + EPS)
        o_ref[0, rows, :] = (y * _silu(g_ref[0, rows, :])).astype(o_ref.dtype)
        return carry

    lax.fori_loop(0, nc, finish, 0, unroll=4)


HGRN_CHUNK = 128


def _hgrn(pd, lb, s0=None, t=HGRN_CHUNK):
    bsz, slen, _ = pd.shape
    col = lambda j: pl.BlockSpec((1, slen, GROUP_W), lambda b: (b, 0, j))
    s_in_specs, s_args, s_out_specs, s_out_shape = _state_specs(bsz, s0)
    blocks, levels = _hgrn_blocks_levels(t)
    n_masks = len(levels) + 2 * len(blocks)
    return pl.pallas_call(
        functools.partial(_hgrn_kernel, slen=slen, t=t, has_s0=s0 is not None),
        grid=(bsz,),
        in_specs=[col(0), col(1), col(2), col(3), col(4),
                  pl.BlockSpec((2, 1, GROUP_W), lambda b: (0, 0, 0))] + s_in_specs,
        out_specs=[pl.BlockSpec((1, slen, GROUP_W), lambda b: (b, 0, 0))] + s_out_specs,
        out_shape=[jax.ShapeDtypeStruct((bsz, slen, GROUP_W), BF16)] + s_out_shape,
        scratch_shapes=[pltpu.VMEM((2, slen, GROUP_W), F32), pltpu.VMEM((2, GROUP_W, GROUP_W), F32),
                        pltpu.VMEM((2, slen, GROUP_W), F32), pltpu.VMEM((2, slen, GROUP_W), F32),
                        pltpu.VMEM((n_masks, t, N_HEADS * t), F32)],
        compiler_params=_cparams(("arbitrary",)),
        name="hgrn2",
    )(pd, pd, pd, pd, pd, lb.reshape(2, 1, GROUP_W), *s_args)


def _outproj_kernel(ma_ref, mb_ref, mc_ref, md_ref, w_ref, x_ref, gate_ref, g_ref, o_ref, *, layer, mod_row):
    g = GROUP_W
    y = jnp.dot(ma_ref[0], w_ref[0, 0:g, :], preferred_element_type=F32)
    y = y + jnp.dot(mb_ref[0], w_ref[0, g:2 * g, :], preferred_element_type=F32)
    y = y + jnp.dot(mc_ref[0], w_ref[0, 2 * g:3 * g, :], preferred_element_type=F32)
    y = y + jnp.dot(md_ref[0], w_ref[0, 3 * g:4 * g, :], preferred_element_type=F32)
    ms = jnp.mean(y * y, axis=-1, keepdims=True)
    r = y * lax.rsqrt(ms + EPS) * g_ref[layer:layer + 1, :]
    o_ref[0] = x_ref[0] + _mod_row(gate_ref, mod_row) * r


OUTPROJ_TM = 1024


def _out_proj(mixed, w_out_bf16, x, mods, layer, mod_row, g_post):
    bsz, slen, d = x.shape
    depth = g_post.shape[0]
    tm = min(OUTPROJ_TM, slen)
    mspec = pl.BlockSpec((1, tm, GROUP_W), lambda b, i: (b, i, 0))
    return pl.pallas_call(
        functools.partial(_outproj_kernel, layer=layer, mod_row=mod_row),
        grid=(bsz, slen // tm),
        in_specs=[mspec, mspec, mspec, mspec,
                  pl.BlockSpec((1, d, d), lambda b, i: (layer, 0, 0)),
                  pl.BlockSpec((1, tm, d), lambda b, i: (b, i, 0)),
                  _mod_spec(2, layer),
                  pl.BlockSpec((depth, d), lambda b, i: (0, 0))],
        out_specs=pl.BlockSpec((1, tm, d), lambda b, i: (b, i, 0)),
        out_shape=jax.ShapeDtypeStruct((bsz, slen, d), F32),
        compiler_params=_cparams(("arbitrary", "arbitrary")),
        name="out_proj",
    )(*mixed, w_out_bf16, x, mods, g_post)


def _context_layer(x, mods, layer, mod_row, g_pre, g_post, w_in, w_out, sink, log_gamma, lb):
    bsz, slen, _ = x.shape
    pa, pb, pc, pd, ak, av, bk, bv = _in_proj(x, mods, layer, mod_row, g_pre, w_in, rope=False, emit_kv=True)
    o_a, = _attention(_attn_operands(pa, 0, 2, 3, 2, n_kv=A_KV, qb=slen, kw=slen, back=0, sink=sink))
    o_b, = _attention(_attn_operands(pb, 0, 1, 2, 3, n_kv=N_HEADS, qb=slen, kw=slen, back=0))
    o_c, s_c = _retention(pc, log_gamma)
    o_d, s_d = _hgrn(pd, lb)
    x = _out_proj((o_a, o_b, o_c, o_d), w_out, x, mods, layer, mod_row, g_post)
    kv4 = lambda t: t.reshape(bsz, slen, -1, HEAD_DIM)
    return x, (kv4(ak), kv4(av), kv4(bk), kv4(bv), _uncompact_states(s_c), _uncompact_states(s_d))


def _latent_layer(x, mods, layer, ca_k, ca_v, cb_k, cb_v, st_c, st_d, g_pre, g_post, w_in, w_out, sink, win_tab,
                  na_tabs, log_gamma, lb):
    pa, pb, pc, pd = _in_proj(x, mods, layer, None, g_pre, w_in, rope=True)
    o_a, o_b = _attention(
        _attn_operands(pa, 0, 2, 3, 2, n_kv=A_KV, qb=WIN_QB, kw=WIN_KW, back=WIN_BACK,
                       ctx_k=ca_k, ctx_v=ca_v, table=win_tab, sink=sink, layer=layer),
        _attn_operands(pb, 0, 1, 2, 3, n_kv=N_HEADS, qb=NA_QB, kw=NA_KW, back=NA_QB,
                       ctx_k=cb_k, ctx_v=cb_v, table=na_tabs, layer=layer))
    o_c, = _retention(pc, log_gamma, st_c)
    o_d, = _hgrn(pd, lb, st_d)
    return _out_proj((o_a, o_b, o_c, o_d), w_out, x, mods, layer, None, g_post)


def kernel(x_prompt, x_sample, c, cache_win_k, cache_win_v, cache_na_k, cache_na_v, state_ret, state_hgrn,
           c_ctx, w_ada, b_ada, g_pre, g_post, w_in, w_out, attn_sink, na_rpb, ret_decay_logit, hgrn_lb_logit):
    depth = w_ada.shape[0]
    dec_b, dec_s, d = x_sample.shape
    p_lb = jax.nn.softmax(hgrn_lb_logit.astype(F32), axis=0)
    lower_bounds = jnp.cumsum(p_lb, axis=0) - p_lb[0:1]
    log_gammas = jax.nn.log_sigmoid(ret_decay_logit.astype(F32))
    w_in_b = w_in.astype(BF16)
    w_out_b = w_out.astype(BF16)

    cvecs = jnp.zeros((ADA_ROWS, d), F32).at[:dec_b].set(c).at[dec_b].set(c_ctx)
    mods = _adaln(cvecs, w_ada, b_ada)
    win_tab = _window_table(dec_s)
    na_tabs = _na_tables(na_rpb, dec_s)

    x = x_prompt
    outs = [[] for _ in range(6)]
    for l in range(depth):
        x, extra = _context_layer(x, mods, l, dec_b, g_pre, g_post, w_in_b, w_out_b,
                                  attn_sink[l], log_gammas[l], lower_bounds[l])
        for acc, e in zip(outs, extra):
            acc.append(e)
    y_prompt = x
    stacked = [jnp.stack(o, axis=1) for o in outs]

    past = cache_win_k.shape[2]
    ca_k = cache_win_k.reshape(dec_b, depth, past, A_KV * HEAD_DIM)
    ca_v = cache_win_v.reshape(dec_b, depth, past, A_KV * HEAD_DIM)
    cb_k = cache_na_k.reshape(dec_b, depth, past, GROUP_W)
    cb_v = cache_na_v.reshape(dec_b, depth, past, GROUP_W)
    st_c = _compact_states(state_ret)
    st_d = _compact_states(state_hgrn)
    x = x_sample
    for l in range(depth):
        x = _latent_layer(x, mods, l, ca_k, ca_v, cb_k, cb_v, st_c[:, l], st_d[:, l], g_pre, g_post,
                          w_in_b, w_out_b, attn_sink[l], win_tab, na_tabs, log_gammas[l], lower_bounds[l])
    return (y_prompt, x, *stacked)
```

```python
import functools

import numpy as np
import jax
import jax.numpy as jnp
from jax import lax
from jax.experimental import pallas as pl
from jax.experimental.pallas import tpu as pltpu

F32 = jnp.float32
BF16 = jnp.bfloat16

D_MODEL = 1024
DEPTH = 4
GRID_W = 64
HEAD_DIM = 64
N_HEADS = 4
GROUP_W = N_HEADS * HEAD_DIM
A_KV = 2
WINDOW = 128
NA_ROWS = 8
NA_COLS = 16
ROPE_BASE = 10000.0
EPS = 1e-6
NEG = -1e30
LB_FLOOR = 1e-30
LOG2E = 1.4426950408889634
W_A = 3 * GROUP_W
W_B = 4 * GROUP_W
W_C = 4 * GROUP_W
W_D = 5 * GROUP_W
IN_WIDTH = W_A + W_B + W_C + W_D

V7X_VMEM_LIMIT_BYTES = 56 * 1024 * 1024
ADA_ROWS = 16

NT_DIMS = (((1,), (1,)), ((), ()))
TN_DIMS = (((0,), (0,)), ((), ()))


def _cparams(sem):
    return pltpu.CompilerParams(dimension_semantics=sem, vmem_limit_bytes=V7X_VMEM_LIMIT_BYTES)


def _sigmoid(x):
    return 1.0 / (1.0 + jnp.exp(-x))


def _silu(x):
    return x * _sigmoid(x)


def _head_masks(rows):
    lane = lax.broadcasted_iota(jnp.int32, (rows, GROUP_W), 1)
    return [(lane >= h * HEAD_DIM) & (lane < (h + 1) * HEAD_DIM) for h in range(N_HEADS)]


def _stack_heads(x, masks):
    return jnp.concatenate([jnp.where(m, x, 0.0) for m in masks], axis=0)


def _block_ones():
    shift = HEAD_DIM.bit_length() - 1
    r = lax.broadcasted_iota(jnp.int32, (GROUP_W, GROUP_W), 0) >> shift
    c = lax.broadcasted_iota(jnp.int32, (GROUP_W, GROUP_W), 1) >> shift
    return r == c


def _head_sum(x, ones_bf16):
    hi = x.astype(BF16)
    lo = (x - hi.astype(F32)).astype(BF16)
    return (jnp.dot(hi, ones_bf16, preferred_element_type=F32)
            + jnp.dot(lo, ones_bf16, preferred_element_type=F32))


def _adaln_kernel(c_ref, w_ref, b_ref, o_ref):
    s = _silu(c_ref[...]).astype(BF16)
    o_ref[0, 0] = jnp.dot(s, w_ref[0].astype(BF16), preferred_element_type=F32) + b_ref[0]


def _adaln(cvecs, w_ada, b_ada):
    depth, d, d3 = w_ada.shape
    tn = 512
    per = d // tn
    return pl.pallas_call(
        _adaln_kernel,
        grid=(depth, d3 // tn),
        in_specs=[pl.BlockSpec((ADA_ROWS, d), lambda l, j: (0, 0)),
                  pl.BlockSpec((1, d, tn), lambda l, j: (l, 0, j)),
                  pl.BlockSpec((1, 1, tn), lambda l, j: (l, 0, j))],
        out_specs=pl.BlockSpec((1, 1, ADA_ROWS, tn), lambda l, j: (l, j // per, 0, j % per)),
        out_shape=jax.ShapeDtypeStruct((depth, 3, ADA_ROWS, d), F32),
        compiler_params=_cparams(("arbitrary", "arbitrary")),
        name="adaln",
    )(cvecs, w_ada, b_ada.reshape(depth, 1, d3))


def _rope(x, cos, sin):
    w = x.shape[-1]
    lane = lax.broadcasted_iota(jnp.int32, x.shape, 1)
    first = (lane & 31) < 16
    swapped = jnp.where(first, pltpu.roll(x, w - 16, 1), pltpu.roll(x, 16, 1))
    return x * cos[:, :w] + swapped * sin[:, :w]


def _mod_row(mod_ref, mod_row):
    row = pl.program_id(0) if mod_row is None else mod_row
    return mod_ref[0, 0, pl.ds(row, 1), :]


def _inproj_kernel(*refs, rope, emit_kv, layer, mod_row):
    refs = list(refs)
    x_ref, shift_ref, scale_ref, g_ref, w_ref = refs[:5]
    if rope:
        cos_ref, sin_ref = refs[5:7]
    n_out = 8 if emit_kv else 4
    pa_ref, pb_ref, pc_ref, pd_ref = refs[-n_out:][:4]
    x = x_ref[0]
    ms = jnp.mean(x * x, axis=-1, keepdims=True)
    y = x * lax.rsqrt(ms + EPS) * g_ref[layer:layer + 1, :]
    h = (y * (1.0 + _mod_row(scale_ref, mod_row)) + _mod_row(shift_ref, mod_row)).astype(BF16)

    def mm(c0, c1):
        return jnp.dot(h, w_ref[0, :, c0:c1], preferred_element_type=F32)

    if rope:
        cos = cos_ref[...]
        sin = sin_ref[...]
    g = GROUP_W
    aq = mm(0, g)
    akv = mm(g, 2 * g)
    if rope:
        aq = _rope(aq, cos, sin)
        ak = _rope(akv[:, :g // 2], cos, sin)
        akv = jnp.concatenate([ak, akv[:, g // 2:]], axis=1)
    pa_ref[0, :, 0:g] = aq.astype(pa_ref.dtype)
    pa_ref[0, :, g:2 * g] = akv.astype(pa_ref.dtype)
    pa_ref[0, :, 2 * g:3 * g] = mm(2 * g, 3 * g).astype(pa_ref.dtype)
    pbv = mm(W_A, W_A + W_B)
    pb_ref[0] = pbv.astype(pb_ref.dtype)
    if emit_kv:
        ak_ref, av_ref, bk_ref, bv_ref = refs[-4:]
        ak_ref[0] = akv[:, :g // 2]
        av_ref[0] = akv[:, g // 2:]
        bk_ref[0] = pbv[:, g:2 * g]
        bv_ref[0] = pbv[:, 2 * g:3 * g]
    c0 = W_A + W_B
    cq = mm(c0, c0 + g)
    ck = mm(c0 + g, c0 + 2 * g)
    if rope:
        cq = _rope(cq, cos, sin)
        ck = _rope(ck, cos, sin)
    pc_ref[0, :, 0:g] = cq.astype(pc_ref.dtype)
    pc_ref[0, :, g:2 * g] = ck.astype(pc_ref.dtype)
    pc_ref[0, :, 2 * g:4 * g] = mm(c0 + 2 * g, c0 + 4 * g).astype(pc_ref.dtype)
    d0 = c0 + W_C
    pd_ref[0] = mm(d0, d0 + W_D).astype(pd_ref.dtype)


def _rope_tables(slen):
    t = np.arange(slen)
    nf = HEAD_DIM // 4
    freqs = ROPE_BASE ** (-np.arange(nf, dtype=np.float64) / nf)
    d = np.arange(HEAD_DIM)
    pos = np.where(d[None, :] < HEAD_DIM // 2, (t // GRID_W)[:, None], (t % GRID_W)[:, None])
    ang = pos * freqs[d % nf][None, :]
    sign = np.where((d % (2 * nf)) < nf, -1.0, 1.0)[None, :]
    cos = np.tile(np.cos(ang), (1, N_HEADS))
    sin = np.tile(np.sin(ang) * sign, (1, N_HEADS))
    return jnp.asarray(cos, F32), jnp.asarray(sin, F32)


def _mod_spec(which, layer):
    return pl.BlockSpec((1, 1, ADA_ROWS, D_MODEL), lambda b, i: (layer, which, 0, 0))


INPROJ_TM = 512


def _in_proj(x, mods, layer, mod_row, g_pre, w_in_bf16, rope, emit_kv=False):
    bsz, slen, d = x.shape
    tm = min(INPROJ_TM, slen)
    depth = g_pre.shape[0]
    in_specs = [pl.BlockSpec((1, tm, d), lambda b, i: (b, i, 0)),
                _mod_spec(0, layer),
                _mod_spec(1, layer),
                pl.BlockSpec((depth, d), lambda b, i: (0, 0)),
                pl.BlockSpec((1, d, IN_WIDTH), lambda b, i: (layer, 0, 0))]
    args = [x, mods, mods, g_pre, w_in_bf16]
    if rope:
        cos, sin = _rope_tables(slen)
        in_specs += [pl.BlockSpec((tm, GROUP_W), lambda b, i: (i, 0))] * 2
        args += [cos, sin]
    widths = (W_A, W_B, W_C, W_D)
    if emit_kv:
        widths += (A_KV * HEAD_DIM, A_KV * HEAD_DIM, GROUP_W, GROUP_W)
    return pl.pallas_call(
        functools.partial(_inproj_kernel, rope=rope, emit_kv=emit_kv, layer=layer, mod_row=mod_row),
        grid=(bsz, slen // tm),
        in_specs=in_specs,
        out_specs=[pl.BlockSpec((1, tm, w), lambda b, i: (b, i, 0)) for w in widths],
        out_shape=[jax.ShapeDtypeStruct((bsz, slen, w), F32) for w in widths],
        compiler_params=_cparams(("arbitrary", "arbitrary")),
        name="in_proj",
    )(*args)


ATTN_ROWS = 256


def _attn_kernel(*refs, cfgs):
    n_in = [4 + 2 * c["has_ctx"] + (c["n_tab"] > 0) + c["has_sink"] for c in cfgs]
    outs = refs[sum(n_in):]
    start = 0
    for c, n, o_ref in zip(cfgs, n_in, outs):
        _attn_body(refs[start:start + n], o_ref, **c)
        start += n


def _attn_body(in_refs, o_ref, *, n_kv, qb, kw, back, slen, has_ctx, n_tab, tab_heads, has_sink):
    it = iter(in_refs)
    q_ref, k_ref, v_ref, g_ref = next(it), next(it), next(it), next(it)
    kc_ref = vc_ref = tab_ref = sink_ref = None
    if has_ctx:
        kc_ref, vc_ref = next(it), next(it)
    if n_tab:
        tab_ref = next(it)
    if has_sink:
        sink_ref = next(it)

    n = pl.program_id(1)
    nblk = slen // qb
    ws = pl.multiple_of(jnp.clip(n * qb - back, 0, slen - kw), HEAD_DIM)
    if n_tab == 3:
        tix = jnp.where(n == 0, 0, jnp.where(n == nblk - 1, 2, 1))
    else:
        tix = 0
    grp = N_HEADS // n_kv
    qscale = HEAD_DIM ** -0.5 * LOG2E
    pair_w = 2 * HEAD_DIM
    lane = lax.broadcasted_iota(jnp.int32, (1, pair_w), 1)
    half_mask = (lane < HEAD_DIM, lane >= HEAD_DIM)
    rb = min(qb, ATTN_ROWS)
    for j in range(N_HEADS // 2):
        cols = slice(j * pair_w, (j + 1) * pair_w)
        if grp == 1:
            kv_cols, kv_half = cols, (0, 1)
        else:
            kv_cols, kv_half = slice(0, pair_w), (j, j)
        kall = k_ref[0, pl.ds(ws, kw), kv_cols].astype(BF16)
        vall = v_ref[0, pl.ds(ws, kw), kv_cols]
        if has_ctx:
            kall = jnp.concatenate([kall, kc_ref[0, 0, :, kv_cols].astype(BF16)], axis=0)
            vall = jnp.concatenate([vall, vc_ref[0, 0, :, kv_cols]], axis=0)
        vaug = {hf: jnp.where(half_mask[hf], vall, 1.0).astype(BF16) for hf in set(kv_half)}
        for r0 in range(0, qb, rb):
            rs = slice(r0, r0 + rb)
            q128 = q_ref[0, rs, cols] * qscale
            q_other = pltpu.roll(q128, HEAD_DIM, 1) if kv_half != (0, 1) else None
            outs = []
            for i in range(2):
                hf = kv_half[i]
                qm = jnp.where(half_mask[hf], q128 if i == hf else q_other, 0.0).astype(BF16)
                s = lax.dot_general(qm, kall, NT_DIMS, preferred_element_type=F32)
                if n_tab:
                    tab = tab_ref[0, tix, 2 * j + i if tab_heads == N_HEADS else 0, rs, :]
                    s = jnp.concatenate([s[:, :kw] + tab, s[:, kw:]], axis=1) if has_ctx else s + tab
                m = jnp.max(s, axis=-1, keepdims=True)
                if has_sink:
                    sink = sink_ref[2 * j + i] * LOG2E
                    m = jnp.maximum(m, sink)
                p = jnp.exp2(s - m).astype(BF16)
                acc = jnp.dot(p, vaug[hf], preferred_element_type=F32)
                if has_sink:
                    acc = acc + jnp.where(half_mask[1 - hf], jnp.exp2(sink - m), 0.0)
                acc = acc / pltpu.roll(acc, HEAD_DIM, 1)
                outs.append(acc if i == hf else pltpu.roll(acc, HEAD_DIM, 1))
            out = jnp.where(half_mask[0], outs[0], outs[1]) * _silu(g_ref[0, rs, cols])
            o_ref[0, rs, cols] = out.astype(o_ref.dtype)


def _attn_operands(p, q_col, k_col, v_col, g_col, n_kv, qb, kw, back, ctx_k=None, ctx_v=None,
                   table=None, sink=None, layer=0):
    slen = p.shape[1]
    kvw = n_kv * HEAD_DIM
    in_specs = [pl.BlockSpec((1, qb, GROUP_W), lambda b, n: (b, n, q_col)),
                pl.BlockSpec((1, slen, kvw), lambda b, n: (b, 0, k_col)),
                pl.BlockSpec((1, slen, kvw), lambda b, n: (b, 0, v_col)),
                pl.BlockSpec((1, qb, GROUP_W), lambda b, n: (b, n, g_col))]
    args = [p, p, p, p]
    has_ctx = ctx_k is not None
    if has_ctx:
        past = ctx_k.shape[2]
        in_specs += [pl.BlockSpec((1, 1, past, kvw), lambda b, n: (b, layer, 0, 0))] * 2
        args += [ctx_k, ctx_v]
    n_tab = tab_heads = 0
    if table is not None:
        n_tab, tab_heads = table.shape[1], table.shape[2]
        tab_layer = layer if table.shape[0] > 1 else 0
        in_specs.append(pl.BlockSpec((1,) + table.shape[1:], lambda b, n: (tab_layer, 0, 0, 0, 0)))
        args.append(table)
    if sink is not None:
        in_specs.append(pl.BlockSpec(memory_space=pltpu.SMEM))
        args.append(sink)
    cfg = dict(n_kv=n_kv, qb=qb, kw=kw, back=back, slen=slen, has_ctx=has_ctx, n_tab=n_tab,
               tab_heads=tab_heads, has_sink=sink is not None)
    return in_specs, args, cfg


def _attention(*operands):
    in_specs = [s for op in operands for s in op[0]]
    args = [a for op in operands for a in op[1]]
    cfgs = tuple(op[2] for op in operands)
    bsz = args[0].shape[0]
    slen, qb = cfgs[0]["slen"], cfgs[0]["qb"]
    assert all(c["slen"] == slen and c["qb"] == qb for c in cfgs)
    return pl.pallas_call(
        functools.partial(_attn_kernel, cfgs=cfgs),
        grid=(bsz, slen // qb),
        in_specs=in_specs,
        out_specs=[pl.BlockSpec((1, qb, GROUP_W), lambda b, n: (b, n, 0)) for _ in cfgs],
        out_shape=[jax.ShapeDtypeStruct((bsz, slen, GROUP_W), BF16) for _ in cfgs],
        compiler_params=_cparams(("arbitrary", "arbitrary")),
        name="attn",
    )(*args)


WIN_QB = 256
WIN_BACK = WINDOW
WIN_KW = WIN_QB + 2 * WINDOW
NA_QROWS = 4
NA_KROWS = 12
NA_QB = NA_QROWS * GRID_W
NA_KW = NA_KROWS * GRID_W


def _window_table(slen):
    nblk = slen // WIN_QB
    tabs = []
    for n in (0, 1, nblk - 1):
        ws = int(np.clip(n * WIN_QB - WIN_BACK, 0, slen - WIN_KW))
        qpos = n * WIN_QB + np.arange(WIN_QB)[:, None]
        kpos = ws + np.arange(WIN_KW)[None, :]
        tabs.append(np.where(np.abs(qpos - kpos) <= WINDOW, 0.0, NEG))
    return jnp.asarray(np.stack(tabs)[None, :, None], F32)


N_RPB_R = 2 * NA_ROWS - 1
N_RPB_C = 2 * NA_COLS - 1


def _na_table_kernel(rpb_ref, o_ref, tz_scr, *, rows):
    base = (pl.program_id(0) * N_HEADS + pl.program_id(1)) * (N_RPB_R * N_RPB_C)
    qc = lax.broadcasted_iota(jnp.int32, (GRID_W, 2 * GRID_W), 0)
    kk = lax.broadcasted_iota(jnp.int32, (GRID_W, 2 * GRID_W), 1)
    kc = kk & (GRID_W - 1)
    diff = kc - qc
    qws = jnp.clip(qc - NA_COLS // 2, 0, GRID_W - NA_COLS)
    col_ok = (kc >= qws) & (kc < qws + NA_COLS)
    neg = jnp.full((GRID_W, 2 * GRID_W), NEG, F32)
    for dr in range(N_RPB_R):
        acc = neg
        for m in range(N_RPB_C):
            acc = jnp.where(diff == m - (NA_COLS - 1), rpb_ref[base + dr * N_RPB_C + m] * LOG2E, acc)
        tz_scr[dr] = jnp.where(col_ok, acc, NEG)
    wr = min(NA_ROWS, rows)
    nblk = rows // NA_QROWS
    for ti, g in enumerate((0, 1, nblk - 1)):
        ws_row = min(max(g * NA_QROWS - NA_QROWS, 0), rows - NA_KROWS)
        for qr in range(NA_QROWS):
            r = g * NA_QROWS + qr
            rs = min(max(r - wr // 2, 0), rows - wr)
            for p in range(NA_KROWS // 2):
                halves = []
                for kr in (ws_row + 2 * p, ws_row + 2 * p + 1):
                    halves.append(tz_scr[kr - r + NA_ROWS - 1] if rs <= kr < rs + wr else neg)
                o_ref[0, ti, 0, qr * GRID_W:(qr + 1) * GRID_W, p * 2 * GRID_W:(p + 1) * 2 * GRID_W] = (
                    jnp.where(kk < GRID_W, halves[0], halves[1]))


def _na_tables(na_rpb, slen):
    depth = na_rpb.shape[0]
    return pl.pallas_call(
        functools.partial(_na_table_kernel, rows=slen // GRID_W),
        grid=(depth, N_HEADS),
        in_specs=[pl.BlockSpec(memory_space=pltpu.SMEM)],
        out_specs=pl.BlockSpec((1, 3, 1, NA_QB, NA_KW), lambda l, h: (l, 0, h, 0, 0)),
        out_shape=jax.ShapeDtypeStruct((depth, 3, N_HEADS, NA_QB, NA_KW), F32),
        scratch_shapes=[pltpu.VMEM((N_RPB_R, GRID_W, 2 * GRID_W), F32)],
        compiler_params=_cparams(("arbitrary", "arbitrary")),
        name="na_table",
    )(na_rpb.astype(F32).reshape(-1))


def _ret_kernel(q_ref, k_ref, v_ref, g_ref, lgl_ref, lgc_ref, *rest, slen, t, has_s0):
    if has_s0:
        s0_ref, o_ref, sb_scr, st_scr, dm_scr, dec_scr = rest
    else:
        o_ref, sfin_ref, sb_scr, st_scr, dm_scr, dec_scr = rest
    nc = slen // t
    masks = _head_masks(t)
    bmask = _block_ones()
    ones_bf16 = jnp.where(bmask, 1.0, 0.0).astype(BF16)
    lgf, lgb = lgl_ref[0], lgl_ref[1]

    @pl.when(pl.program_id(0) == 0)
    def _():
        ii = lax.broadcasted_iota(jnp.int32, (t, N_HEADS * t), 0)
        jj = lax.broadcasted_iota(jnp.int32, (t, N_HEADS * t), 1) & (t - 1)
        dist = (ii - jj).astype(F32)
        dm_scr[...] = (jnp.where(dist >= 0, jnp.exp(dist * lgc_ref[0]), 0.0)
                       + jnp.where(dist <= 0, jnp.exp(-dist * lgc_ref[1]), 0.0))
        idx = lax.broadcasted_iota(jnp.int32, (t, GROUP_W), 0).astype(F32)
        dec_scr[0] = jnp.exp((idx + 1.0) * lgf)
        dec_scr[1] = jnp.exp((t - 1.0 - idx) * lgf)
        dec_scr[2] = jnp.exp((t - idx) * lgb)
        dec_scr[3] = jnp.exp(idx * lgb)

    cdec_f = jnp.exp(float(t) * lgf)
    cdec_b = jnp.exp(float(t) * lgb)
    kscale = HEAD_DIM ** -0.5

    def state_update(st, k, v, kdec, cdec):
        upd = lax.dot_general(v, (k * kdec).astype(BF16), TN_DIMS, preferred_element_type=F32)
        return st * cdec + jnp.where(bmask, upd, 0.0)

    def init_state(direction):
        if has_s0:
            st_scr[...] = _expand_state(s0_ref[0, direction], bmask)
        else:
            st_scr[...] = jnp.zeros((GROUP_W, GROUP_W), F32)

    def emit_final(direction):
        if not has_s0:
            sfin_ref[0, direction] = _compact_state(st_scr[...], bmask)

    init_state(1)

    def sweep_bwd(ci, carry):
        c = nc - 1 - ci
        rows = pl.ds(pl.multiple_of(c * t, t), t)
        st = st_scr[...]
        sb_scr[c] = st.astype(BF16)
        k = k_ref[0, rows, :] * kscale
        st_scr[...] = state_update(st, k, v_ref[0, rows, :].astype(BF16), dec_scr[3], cdec_b)
        return carry

    lax.fori_loop(0, nc, sweep_bwd, 0, unroll=4)
    emit_final(1)
    init_state(0)

    def sweep_fwd(c, carry):
        rows = pl.ds(pl.multiple_of(c * t, t), t)
        q = q_ref[0, rows, :]
        k = k_ref[0, rows, :] * kscale
        v = v_ref[0, rows, :]
        ks = _stack_heads(k, masks).astype(BF16)
        vs = _stack_heads(v, masks).astype(BF16)
        sc = lax.dot_general(q.astype(BF16), ks, NT_DIMS, preferred_element_type=F32) * dm_scr[...]
        o = jnp.dot(sc.astype(BF16), vs, preferred_element_type=F32)
        st = st_scr[...]
        o = o + lax.dot_general((q * dec_scr[0]).astype(BF16), st.astype(BF16), NT_DIMS,
                                preferred_element_type=F32)
        o = o + lax.dot_general((q * dec_scr[2]).astype(BF16), sb_scr[c], NT_DIMS, preferred_element_type=F32)
        st_scr[...] = state_update(st, k, v.astype(BF16), dec_scr[1], cdec_f)
        mu = _head_sum(o, ones_bf16) * (1.0 / HEAD_DIM)
        d = o - mu
        var = _head_sum(d * d, ones_bf16) * (1.0 / HEAD_DIM)
        y = d * lax.rsqrt(var + EPS)
        o_ref[0, rows, :] = (y * _silu(g_ref[0, rows, :])).astype(o_ref.dtype)
        return carry

    lax.fori_loop(0, nc, sweep_fwd, 0, unroll=4)
    emit_final(0)


def _compact_states(s0):
    return jnp.swapaxes(s0, -1, -2).reshape(s0.shape[:-3] + (GROUP_W, HEAD_DIM))


def _uncompact_states(sc):
    bsz = sc.shape[0]
    return jnp.swapaxes(sc.reshape(bsz, 2, N_HEADS, HEAD_DIM, HEAD_DIM), -1, -2)


def _split3(x):
    h1 = x.astype(BF16)
    r1 = x - h1.astype(F32)
    h2 = r1.astype(BF16)
    return h1, h2, (r1 - h2.astype(F32)).astype(BF16)


def _expand_state(x, bmask):
    r = lax.broadcasted_iota(jnp.int32, (HEAD_DIM, GROUP_W), 0)
    c = lax.broadcasted_iota(jnp.int32, (HEAD_DIM, GROUP_W), 1) & (HEAD_DIM - 1)
    rep = jnp.where(r == c, 1.0, 0.0).astype(BF16)
    h1, h2, h3 = _split3(x)
    tiled = (jnp.dot(h1, rep, preferred_element_type=F32) + jnp.dot(h2, rep, preferred_element_type=F32)
             + jnp.dot(h3, rep, preferred_element_type=F32))
    return jnp.where(bmask, tiled, 0.0)


def _compact_state(st, bmask):
    r = lax.broadcasted_iota(jnp.int32, (GROUP_W, HEAD_DIM), 0) & (HEAD_DIM - 1)
    c = lax.broadcasted_iota(jnp.int32, (GROUP_W, HEAD_DIM), 1)
    fold = jnp.where(r == c, 1.0, 0.0).astype(BF16)
    h1, h2, h3 = _split3(jnp.where(bmask, st, 0.0))
    return (jnp.dot(h1, fold, preferred_element_type=F32) + jnp.dot(h2, fold, preferred_element_type=F32)
            + jnp.dot(h3, fold, preferred_element_type=F32))


def _state_specs(bsz, s0):
    spec = pl.BlockSpec((1, 2, GROUP_W, HEAD_DIM), lambda b: (b, 0, 0, 0))
    shape = jax.ShapeDtypeStruct((bsz, 2, GROUP_W, HEAD_DIM), F32)
    if s0 is None:
        return [], [], [spec], [shape]
    return [spec], [s0], [], []


RET_CHUNK = 256


def _retention(pc, log_gamma, s0=None, t=RET_CHUNK):
    bsz, slen, _ = pc.shape
    t = min(t, slen)
    lgl = jnp.repeat(log_gamma, HEAD_DIM, axis=1).reshape(2, 1, GROUP_W)
    lgc = jnp.repeat(log_gamma, t, axis=1).reshape(2, 1, N_HEADS * t)
    col = lambda j: pl.BlockSpec((1, slen, GROUP_W), lambda b: (b, 0, j))
    s_in_specs, s_args, s_out_specs, s_out_shape = _state_specs(bsz, s0)
    return pl.pallas_call(
        functools.partial(_ret_kernel, slen=slen, t=t, has_s0=s0 is not None),
        grid=(bsz,),
        in_specs=[col(0), col(1), col(2), col(3),
                  pl.BlockSpec((2, 1, GROUP_W), lambda b: (0, 0, 0)),
                  pl.BlockSpec((2, 1, N_HEADS * t), lambda b: (0, 0, 0))] + s_in_specs,
        out_specs=[pl.BlockSpec((1, slen, GROUP_W), lambda b: (b, 0, 0))] + s_out_specs,
        out_shape=[jax.ShapeDtypeStruct((bsz, slen, GROUP_W), BF16)] + s_out_shape,
        scratch_shapes=[pltpu.VMEM((slen // t, GROUP_W, GROUP_W), BF16), pltpu.VMEM((GROUP_W, GROUP_W), F32),
                        pltpu.VMEM((t, N_HEADS * t), F32), pltpu.VMEM((4, t, GROUP_W), F32)],
        compiler_params=_cparams(("arbitrary",)),
        name="retention",
    )(pc, pc, pc, pc, lgl, lgc, *s_args)


HGRN_DIRECT = 8
HGRN_BLOCKS = (128, 64, 32, 16)
HGRN_SAFE_EXP2 = 80.0


def _split2(x):
    hi = x.astype(BF16)
    return hi, (x - hi.astype(F32)).astype(BF16)


def _hgrn_blocks_levels(t):
    blocks = tuple(w for w in HGRN_BLOCKS if w <= t)
    levels = []
    s = t // 2
    while s >= HGRN_DIRECT:
        levels.append(s)
        s //= 2
    return blocks, levels


def _hgrn_kernel(q_ref, zf_ref, zb_ref, v_ref, g_ref, lb_ref, *rest, slen, t, has_s0):
    if has_s0:
        s0_ref, o_ref, o_scr, st_scr, b_scr, k_scr, mask_scr = rest
    else:
        o_ref, sfin_ref, o_scr, st_scr, b_scr, k_scr, mask_scr = rest
    nc = slen // t
    blocks, levels = _hgrn_blocks_levels(t)
    masks = _head_masks(t)
    bmask = _block_ones()
    ones_bf16 = jnp.where(bmask, 1.0, 0.0).astype(BF16)
    r_i = lax.broadcasted_iota(jnp.int32, (t, t), 0)
    c_i = lax.broadcasted_iota(jnp.int32, (t, t), 1)
    sub = lax.broadcasted_iota(jnp.int32, (t, GROUP_W), 0) & (HGRN_DIRECT - 1)

    @pl.when(pl.program_id(0) == 0)
    def _():
        r_w = lax.broadcasted_iota(jnp.int32, (t, N_HEADS * t), 0)
        c_w = lax.broadcasted_iota(jnp.int32, (t, N_HEADS * t), 1) & (t - 1)
        for li, s in enumerate(levels):
            shift = (2 * s).bit_length() - 1
            mask_scr[li] = jnp.where((r_w >> shift) == (c_w >> shift), 1.0, 0.0)
        for direction in range(2):
            for wi, w in enumerate(blocks):
                shift = w.bit_length() - 1
                causal = (c_w <= r_w) if direction == 0 else (c_w >= r_w)
                mask_scr[len(levels) + direction * len(blocks) + wi] = jnp.where(
                    ((r_w >> shift) == (c_w >> shift)) & causal, 1.0, 0.0)

    def block_roll(x, shift):
        x3 = x.reshape(t // HGRN_DIRECT, HGRN_DIRECT, GROUP_W)
        return pltpu.roll(x3, shift, 1).reshape(t, GROUP_W)

    def direction_fns(direction, z_ref):
        fwd = direction == 0
        lbd = jnp.maximum(lb_ref[direction], LB_FLOOR)
        tri = jnp.where((r_i >= c_i) if fwd else (r_i <= c_i), 1.0, 0.0).astype(BF16)
        near_mask_idx = {w: len(levels) + direction * len(blocks) + wi for wi, w in enumerate(blocks)}
        if has_s0:
            st_scr[direction] = _expand_state(s0_ref[0, direction], bmask)
        else:
            st_scr[direction] = jnp.zeros((GROUP_W, GROUP_W), F32)

        def gates(c, worst):
            rows = pl.ds(pl.multiple_of(c * t, t), t)
            z = z_ref[0, rows, :]
            a = jnp.exp(-jnp.abs(z))
            inv = 1.0 / (1.0 + a)
            pos = z >= 0
            sig = jnp.where(pos, inv, a * inv)
            nsig = jnp.where(pos, a * inv, inv)
            lf = jnp.log(lbd + (1.0 - lbd) * sig)
            k_scr[direction, rows, :] = (1.0 - lbd) * nsig
            h1, h2 = _split2(lf)
            b = (jnp.dot(tri, h1, preferred_element_type=F32) + jnp.dot(tri, h2, preferred_element_type=F32)) * LOG2E
            b_scr[direction, rows, :] = b
            worst = list(worst)
            for wi, w in enumerate(blocks):
                for gi in range(t // w):
                    first = b[gi * w:gi * w + 1]
                    last = b[(gi + 1) * w - 1:(gi + 1) * w]
                    worst[wi] = jnp.maximum(worst[wi], jnp.abs(first - last))
            return tuple(worst)

        def chunk(c, near_w):
            rows = pl.ds(pl.multiple_of(c * t, t), t)
            q = _silu(q_ref[0, rows, :])
            v = v_ref[0, rows, :]
            vb = v.astype(BF16)
            k = k_scr[direction, rows, :]
            b = b_scr[direction, rows, :]
            vs = _stack_heads(v, masks).astype(BF16)

            def level_scores(li, s):
                pa, pb = [], []
                zero = jnp.zeros((s, GROUP_W), F32)
                for gi in range(t // (2 * s)):
                    b_lo = b[gi * 2 * s:gi * 2 * s + s]
                    b_hi = b[gi * 2 * s + s:(gi + 1) * 2 * s]
                    if fwd:
                        anc = b_hi[0:1]
                        pa += [zero, jnp.exp2(b_hi - anc)]
                        pb += [jnp.exp2(anc - b_lo), zero]
                    else:
                        anc = b_lo[s - 1:s]
                        pa += [jnp.exp2(b_lo - anc), zero]
                        pb += [zero, jnp.exp2(anc - b_hi)]
                qa = (q * jnp.concatenate(pa, axis=0)).astype(BF16)
                kb = _stack_heads(k * jnp.concatenate(pb, axis=0), masks).astype(BF16)
                return lax.dot_general(qa, kb, NT_DIMS, preferred_element_type=F32) * mask_scr[li]

            def direct_pairs():
                prods, vals = [(q * k).astype(BF16)], [v]
                for dlt in range(1, HGRN_DIRECT):
                    sh = dlt if fwd else HGRN_DIRECT - dlt
                    ok = (sub >= dlt) if fwd else (sub + dlt <= HGRN_DIRECT - 1)
                    w = jnp.exp2(jnp.minimum(b - block_roll(b, sh), 0.0))
                    prods.append(jnp.where(ok, q * block_roll(k, sh) * w, 0.0).astype(BF16))
                    vals.append(block_roll(v, sh))
                hs = jnp.dot(jnp.concatenate(prods, axis=0), ones_bf16, preferred_element_type=F32)
                out = hs[0:t] * vals[0]
                for dlt in range(1, HGRN_DIRECT):
                    out = out + hs[dlt * t:(dlt + 1) * t] * vals[dlt]
                return out

            sc = None
            for li, s in enumerate(levels):
                if s >= max(near_w, HGRN_DIRECT):
                    term = level_scores(li, s)
                    sc = term if sc is None else sc + term

            if near_w:
                ea, eb = [], []
                for gi in range(t // near_w):
                    blk = b[gi * near_w:(gi + 1) * near_w]
                    anc = blk[0:1] if fwd else blk[near_w - 1:near_w]
                    ea.append(jnp.exp2(blk - anc))
                    eb.append(jnp.exp2(anc - blk))
                qa = (q * jnp.concatenate(ea, axis=0)).astype(BF16)
                kb = _stack_heads(k * jnp.concatenate(eb, axis=0), masks).astype(BF16)
                term = lax.dot_general(qa, kb, NT_DIMS, preferred_element_type=F32) * mask_scr[near_mask_idx[near_w]]
                sc = term if sc is None else sc + term
                o = jnp.dot(sc.astype(BF16), vs, preferred_element_type=F32)
            else:
                o = jnp.dot(sc.astype(BF16), vs, preferred_element_type=F32) + direct_pairs()

            st = st_scr[direction]
            o = o + lax.dot_general((q * jnp.exp2(b)).astype(BF16), st.astype(BF16), NT_DIMS,
                                    preferred_element_type=F32)
            bl = b[t - 1:t] if fwd else b[0:1]
            upd = lax.dot_general(vb, (k * jnp.exp2(bl - b)).astype(BF16), TN_DIMS,
                                  preferred_element_type=F32)
            st_scr[direction] = st * jnp.exp2(bl) + jnp.where(bmask, upd, 0.0)
            o_scr[direction, rows, :] = o

        return gates, chunk

    gates_f, chunk_f = direction_fns(0, zf_ref)
    gates_b, chunk_b = direction_fns(1, zb_ref)

    worst = lax.fori_loop(0, nc, lambda c, w: gates_b(c, gates_f(c, w)),
                          tuple(jnp.zeros((1, GROUP_W), F32) for _ in blocks), unroll=4)
    safe = [jnp.max(w) < HGRN_SAFE_EXP2 for w in worst]

    def sweep(ci, carry, *, near_w):
        chunk_f(ci, near_w)
        chunk_b(nc - 1 - ci, near_w)
        return carry

    taken = False
    for w, ok in zip(blocks, safe):
        cond = ok if taken is False else jnp.logical_and(jnp.logical_not(taken), ok)

        @pl.when(cond)
        def _(w=w):
            lax.fori_loop(0, nc, functools.partial(sweep, near_w=w), 0, unroll=4)

        taken = ok if taken is False else jnp.logical_or(taken, ok)

    @pl.when(jnp.logical_not(taken))
    def _():
        lax.fori_loop(0, nc, functools.partial(sweep, near_w=0), 0)

    if not has_s0:
        for direction in range(2):
            sfin_ref[0, direction] = _compact_state(st_scr[direction], bmask)

    def finish(c, carry):
        rows = pl.ds(pl.multiple_of(c * t, t), t)
        o = o_scr[0, rows, :] + o_scr[1, rows, :]
        ms = _head_sum(o * o, ones_bf16) * (1.0 / HEAD_DIM)
        y = o * lax.rsqrt(ms + EPS)
        o_ref[0, rows, :] = (y * _silu(g_ref[0, rows, :])).astype(o_ref.dtype)
        return carry

    lax.fori_loop(0, nc, finish, 0, unroll=4)


HGRN_CHUNK = 128


def _hgrn(pd, lb, s0=None, t=HGRN_CHUNK):
    bsz, slen, _ = pd.shape
    col = lambda j: pl.BlockSpec((1, slen, GROUP_W), lambda b: (b, 0, j))
    s_in_specs, s_args, s_out_specs, s_out_shape = _state_specs(bsz, s0)
    blocks, levels = _hgrn_blocks_levels(t)
    n_masks = len(levels) + 2 * len(blocks)
    return pl.pallas_call(
        functools.partial(_hgrn_kernel, slen=slen, t=t, has_s0=s0 is not None),
        grid=(bsz,),
        in_specs=[col(0), col(1), col(2), col(3), col(4),
                  pl.BlockSpec((2, 1, GROUP_W), lambda b: (0, 0, 0))] + s_in_specs,
        out_specs=[pl.BlockSpec((1, slen, GROUP_W), lambda b: (b, 0, 0))] + s_out_specs,
        out_shape=[jax.ShapeDtypeStruct((bsz, slen, GROUP_W), BF16)] + s_out_shape,
        scratch_shapes=[pltpu.VMEM((2, slen, GROUP_W), F32), pltpu.VMEM((2, GROUP_W, GROUP_W), F32),
                        pltpu.VMEM((2, slen, GROUP_W), F32), pltpu.VMEM((2, slen, GROUP_W), F32),
                        pltpu.VMEM((n_masks, t, N_HEADS * t), F32)],
        compiler_params=_cparams(("arbitrary",)),
        name="hgrn2",
    )(pd, pd, pd, pd, pd, lb.reshape(2, 1, GROUP_W), *s_args)


def _outproj_kernel(ma_ref, mb_ref, mc_ref, md_ref, w_ref, x_ref, gate_ref, g_ref, o_ref, *, layer, mod_row):
    g = GROUP_W
    y = jnp.dot(ma_ref[0], w_ref[0, 0:g, :], preferred_element_type=F32)
    y = y + jnp.dot(mb_ref[0], w_ref[0, g:2 * g, :], preferred_element_type=F32)
    y = y + jnp.dot(mc_ref[0], w_ref[0, 2 * g:3 * g, :], preferred_element_type=F32)
    y = y + jnp.dot(md_ref[0], w_ref[0, 3 * g:4 * g, :], preferred_element_type=F32)
    ms = jnp.mean(y * y, axis=-1, keepdims=True)
    r = y * lax.rsqrt(ms + EPS) * g_ref[layer:layer + 1, :]
    o_ref[0] = x_ref[0] + _mod_row(gate_ref, mod_row) * r


OUTPROJ_TM = 1024


def _out_proj(mixed, w_out_bf16, x, mods, layer, mod_row, g_post):
    bsz, slen, d = x.shape
    depth = g_post.shape[0]
    tm = min(OUTPROJ_TM, slen)
    mspec = pl.BlockSpec((1, tm, GROUP_W), lambda b, i: (b, i, 0))
    return pl.pallas_call(
        functools.partial(_outproj_kernel, layer=layer, mod_row=mod_row),
        grid=(bsz, slen // tm),
        in_specs=[mspec, mspec, mspec, mspec,
                  pl.BlockSpec((1, d, d), lambda b, i: (layer, 0, 0)),
                  pl.BlockSpec((1, tm, d), lambda b, i: (b, i, 0)),
                  _mod_spec(2, layer),
                  pl.BlockSpec((depth, d), lambda b, i: (0, 0))],
        out_specs=pl.BlockSpec((1, tm, d), lambda b, i: (b, i, 0)),
        out_shape=jax.ShapeDtypeStruct((bsz, slen, d), F32),
        compiler_params=_cparams(("arbitrary", "arbitrary")),
        name="out_proj",
    )(*mixed, w_out_bf16, x, mods, g_post)


def _context_layer(x, mods, layer, mod_row, g_pre, g_post, w_in, w_out, sink, log_gamma, lb):
    bsz, slen, _ = x.shape
    pa, pb, pc, pd, ak, av, bk, bv = _in_proj(x, mods, layer, mod_row, g_pre, w_in, rope=False, emit_kv=True)
    o_a, = _attention(_attn_operands(pa, 0, 2, 3, 2, n_kv=A_KV, qb=slen, kw=slen, back=0, sink=sink))
    o_b, = _attention(_attn_operands(pb, 0, 1, 2, 3, n_kv=N_HEADS, qb=slen, kw=slen, back=0))
    o_c, s_c = _retention(pc, log_gamma)
    o_d, s_d = _hgrn(pd, lb)
    x = _out_proj((o_a, o_b, o_c, o_d), w_out, x, mods, layer, mod_row, g_post)
    kv4 = lambda t: t.reshape(bsz, slen, -1, HEAD_DIM)
    return x, (kv4(ak), kv4(av), kv4(bk), kv4(bv), _uncompact_states(s_c), _uncompact_states(s_d))


def _latent_layer(x, mods, layer, ca_k, ca_v, cb_k, cb_v, st_c, st_d, g_pre, g_post, w_in, w_out, sink, win_tab,
                  na_tabs, log_gamma, lb):
    pa, pb, pc, pd = _in_proj(x, mods, layer, None, g_pre, w_in, rope=True)
    o_a, o_b = _attention(
        _attn_operands(pa, 0, 2, 3, 2, n_kv=A_KV, qb=WIN_QB, kw=WIN_KW, back=WIN_BACK,
                       ctx_k=ca_k, ctx_v=ca_v, table=win_tab, sink=sink, layer=layer),
        _attn_operands(pb, 0, 1, 2, 3, n_kv=N_HEADS, qb=NA_QB, kw=NA_KW, back=NA_QB,
                       ctx_k=cb_k, ctx_v=cb_v, table=na_tabs, layer=layer))
    o_c, = _retention(pc, log_gamma, st_c)
    o_d, = _hgrn(pd, lb, st_d)
    return _out_proj((o_a, o_b, o_c, o_d), w_out, x, mods, layer, None, g_post)


def kernel(x_prompt, x_sample, c, cache_win_k, cache_win_v, cache_na_k, cache_na_v, state_ret, state_hgrn,
           c_ctx, w_ada, b_ada, g_pre, g_post, w_in, w_out, attn_sink, na_rpb, ret_decay_logit, hgrn_lb_logit):
    depth = w_ada.shape[0]
    dec_b, dec_s, d = x_sample.shape
    p_lb = jax.nn.softmax(hgrn_lb_logit.astype(F32), axis=0)
    lower_bounds = jnp.cumsum(p_lb, axis=0) - p_lb[0:1]
    log_gammas = jax.nn.log_sigmoid(ret_decay_logit.astype(F32))
    w_in_b = w_in.astype(BF16)
    w_out_b = w_out.astype(BF16)

    cvecs = jnp.zeros((ADA_ROWS, d), F32).at[:dec_b].set(c).at[dec_b].set(c_ctx)
    mods = _adaln(cvecs, w_ada, b_ada)
    win_tab = _window_table(dec_s)
    na_tabs = _na_tables(na_rpb, dec_s)

    x = x_prompt
    outs = [[] for _ in range(6)]
    for l in range(depth):
        x, extra = _context_layer(x, mods, l, dec_b, g_pre, g_post, w_in_b, w_out_b,
                                  attn_sink[l], log_gammas[l], lower_bounds[l])
        for acc, e in zip(outs, extra):
            acc.append(e)
    y_prompt = x
    stacked = [jnp.stack(o, axis=1) for o in outs]

    past = cache_win_k.shape[2]
    ca_k = cache_win_k.reshape(dec_b, depth, past, A_KV * HEAD_DIM)
    ca_v = cache_win_v.reshape(dec_b, depth, past, A_KV * HEAD_DIM)
    cb_k = cache_na_k.reshape(dec_b, depth, past, GROUP_W)
    cb_v = cache_na_v.reshape(dec_b, depth, past, GROUP_W)
    st_c = _compact_states(state_ret)
    st_d = _compact_states(state_hgrn)
    x = x_sample
    for l in range(depth):
        x = _latent_layer(x, mods, l, ca_k, ca_v, cb_k, cb_v, st_c[:, l], st_d[:, l], g_pre, g_post,
                          w_in_b, w_out_b, attn_sink[l], win_tab, na_tabs, log_gammas[l], lower_bounds[l])
    return (y_prompt, x, *stacked)
```
